```python
import math
import jax, jax.numpy as jnp
from jax import lax
import numpy as np

D_MODEL = 2048
BATCH = 4
SEQ = 4096
DEPTH = 2

N_META = 16
LRU_WIDTH = D_MODEL // 2
LRU_BLOCKS = 8
LRU_BLOCK_W = LRU_WIDTH // LRU_BLOCKS
CONV_W = 4
LRU_C = 8.0
DA_HEADS = 8
DA_HEAD_DIM = 64
DA_V_DIM = 2 * DA_HEAD_DIM
DA_WIDTH = DA_HEADS * DA_V_DIM
QK_WIDTH = DA_HEADS * 2 * DA_HEAD_DIM
Q_BLOCK = 128
S5_WIDTH = D_MODEL // 2
S5_GROUP = 16
S5_GROUPS = S5_WIDTH // S5_GROUP
S5_STATE = 64
S5_DT_MIN = 1e-3
S5_DT_MAX = 1e-1
REL_BUCKETS = 32
REL_MAX_DIST = 128
N_BRANCH = 3
IN_SIZES = (LRU_WIDTH, LRU_WIDTH, QK_WIDTH, QK_WIDTH, DA_WIDTH, S5_WIDTH, N_BRANCH * D_MODEL)
N_IN = sum(IN_SIZES)
IN_SPLITS = tuple(int(v) for v in np.cumsum(IN_SIZES)[:-1])
D_FF = ((-(-(8 * D_MODEL) // 3) + 255) // 256) * 256

kernel_name = "hybrid_rglru_diffattn_s5_block"


def rms_norm(x, w, eps=1e-6):
    xf = x.astype(jnp.float32)
    xf = xf * lax.rsqrt(jnp.mean(xf * xf, axis=-1, keepdims=True) + eps)
    return (xf * w.astype(jnp.float32)).astype(x.dtype)


def _lin_combine(e1, e2):
    a1, b1 = e1
    a2, b2 = e2
    return a1 * a2, a2 * b1 + b2


def _complex_lin_combine(e1, e2):
    ar1, ai1, br1, bi1 = e1
    ar2, ai2, br2, bi2 = e2
    ar = ar2 * ar1 - ai2 * ai1
    ai = ar2 * ai1 + ai2 * ar1
    br = ar2 * br1 - ai2 * bi1 + br2
    bi = ar2 * bi1 + ai2 * br1 + bi2
    return ar, ai, br, bi


def causal_conv(x, w, b):
    y = lax.conv_general_dilated(
        x, w[:, None, :].astype(x.dtype), window_strides=(1,), padding=[(CONV_W - 1, 0)],
        dimension_numbers=("NWC", "WIO", "NWC"), feature_group_count=x.shape[-1])
    return y + b


def rg_lru_branch(gate_in, x_in, conv_w, conv_b, w_a, b_a, w_x, b_x, lam):
    B, T, _ = x_in.shape
    xc = causal_conv(x_in, conv_w, conv_b)
    xb = xc.reshape(B, T, LRU_BLOCKS, LRU_BLOCK_W)
    r = jax.nn.sigmoid(jnp.einsum("bthi,hij->bthj", xb, w_a).reshape(B, T, LRU_WIDTH) + b_a)
    i = jax.nn.sigmoid(jnp.einsum("bthi,hij->bthj", xb, w_x).reshape(B, T, LRU_WIDTH) + b_x)
    log_a = LRU_C * r.astype(jnp.float32) * jax.nn.log_sigmoid(lam.astype(jnp.float32))
    a = jnp.exp(log_a)
    b = jnp.sqrt(-jnp.expm1(2.0 * log_a)) * (i * xc).astype(jnp.float32)
    _, h = lax.associative_scan(_lin_combine, (a, b), axis=1)
    return h.astype(x_in.dtype) * jax.nn.gelu(gate_in)


def t5_bucket(q_pos, k_pos):
    n = jnp.maximum(q_pos[:, None] - k_pos[None, :], 0)
    max_exact = REL_BUCKETS // 2
    nf = jnp.maximum(n, 1).astype(jnp.float32)
    large = max_exact + (jnp.log(nf / max_exact) / math.log(REL_MAX_DIST / max_exact)
                         * (REL_BUCKETS - max_exact)).astype(jnp.int32)
    large = jnp.minimum(large, REL_BUCKETS - 1)
    return jnp.where(n < max_exact, n, large)


def diff_attention(q, k, v, rel_bias, lam, sub_w, lam_init):
    B, T = q.shape[0], q.shape[1]
    k_pos = jnp.arange(T)
    scale = DA_HEAD_DIM ** -0.5

    def block(q_blk, q_pos):
        s = jnp.einsum("bqhcd,bkhcd->bhcqk", q_blk, k).astype(jnp.float32) * scale
        bias = rel_bias[t5_bucket(q_pos, k_pos)].astype(jnp.float32)
        s = s + jnp.transpose(bias, (2, 0, 1))[None, :, None]
        s = jnp.where(k_pos[None, :] <= q_pos[:, None], s, -jnp.inf)
        p = jax.nn.softmax(s, axis=-1)
        w = p[:, :, 0] - lam * p[:, :, 1]
        return jnp.einsum("bhqk,bkhd->bqhd", w.astype(v.dtype), v)

    out_meta = block(q[:, :N_META], jnp.arange(N_META))
    nb = (T - N_META) // Q_BLOCK
    qb = q[:, N_META:].reshape(B, nb, Q_BLOCK, DA_HEADS, 2, DA_HEAD_DIM).transpose(1, 0, 2, 3, 4, 5)
    pb = (N_META + jnp.arange(nb * Q_BLOCK)).reshape(nb, Q_BLOCK)
    out_real = lax.map(lambda a: block(a[0], a[1]), (qb, pb))
    out_real = out_real.transpose(1, 0, 2, 3, 4).reshape(B, nb * Q_BLOCK, DA_HEADS, DA_V_DIM)
    o = jnp.concatenate([out_meta, out_real], axis=1)
    o = rms_norm(o, sub_w, eps=1e-5) * (1.0 - lam_init)
    return o.reshape(B, T, DA_WIDTH)


def s5_branch(u, lam_re, lam_im, b_re, b_im, c_re, c_im, d, log_step, w_glu, b_glu):
    B, T, _ = u.shape
    f32 = jnp.float32
    uf = u.astype(f32).reshape(B, T, S5_GROUPS, S5_GROUP)
    lr, li = lam_re.astype(f32), lam_im.astype(f32)
    step = jnp.exp(log_step.astype(f32))[:, None]
    mag = jnp.exp(lr * step)
    ab_re, ab_im = mag * jnp.cos(li * step), mag * jnp.sin(li * step)
    den = lr * lr + li * li
    coef_re = ((ab_re - 1.0) * lr + ab_im * li) / den
    coef_im = (ab_im * lr - (ab_re - 1.0) * li) / den
    br, bi = b_re.astype(f32), b_im.astype(f32)
    bb_re = coef_re[..., None] * br - coef_im[..., None] * bi
    bb_im = coef_re[..., None] * bi + coef_im[..., None] * br
    bu_re = jnp.einsum("gpc,btgc->btgp", bb_re, uf)
    bu_im = jnp.einsum("gpc,btgc->btgp", bb_im, uf)
    a_re = jnp.broadcast_to(ab_re[None, None], (1, T, S5_GROUPS, S5_STATE))
    a_im = jnp.broadcast_to(ab_im[None, None], (1, T, S5_GROUPS, S5_STATE))
    _, _, h_re, h_im = lax.associative_scan(_complex_lin_combine, (a_re, a_im, bu_re, bu_im), axis=1)
    y = (jnp.einsum("gcp,btgp->btgc", c_re.astype(f32), h_re)
         - jnp.einsum("gcp,btgp->btgc", c_im.astype(f32), h_im))
    y = y.reshape(B, T, S5_WIDTH) + d.astype(f32) * u.astype(f32)
    y = jax.nn.gelu(y).astype(u.dtype)
    return y * jax.nn.sigmoid(y @ w_glu + b_glu)


def hybrid_mixer(h, lam_init, rel_bias, w_in, conv_w, conv_b, lru_w_a, lru_b_a, lru_w_x, lru_b_x,
                 lru_lambda, da_lambda, da_subln, s5_lam_re, s5_lam_im, s5_b_re, s5_b_im,
                 s5_c_re, s5_c_im, s5_d, s5_log_step, s5_w_glu, s5_b_glu, b_gate, w_branch, w_out):
    B, T, _ = h.shape
    proj = h @ w_in
    a_gate, a_x, q, k, v, s5_u, g = jnp.split(proj, IN_SPLITS, axis=-1)
    y_a = rg_lru_branch(a_gate, a_x, conv_w, conv_b, lru_w_a, lru_b_a, lru_w_x, lru_b_x, lru_lambda)
    lf = da_lambda.astype(jnp.float32)
    lam = jnp.exp(jnp.sum(lf[0] * lf[1])) - jnp.exp(jnp.sum(lf[2] * lf[3])) + lam_init
    y_b = diff_attention(q.reshape(B, T, DA_HEADS, 2, DA_HEAD_DIM),
                         k.reshape(B, T, DA_HEADS, 2, DA_HEAD_DIM),
                         v.reshape(B, T, DA_HEADS, DA_V_DIM), rel_bias, lam, da_subln, lam_init)
    y_c = s5_branch(s5_u, s5_lam_re, s5_lam_im, s5_b_re, s5_b_im, s5_c_re, s5_c_im, s5_d,
                    s5_log_step, s5_w_glu, s5_b_glu)
    gates = jax.nn.sigmoid(g.reshape(B, T, N_BRANCH, D_MODEL) + b_gate)
    merged = (gates[:, :, 0] * (y_a @ w_branch[0])
              + gates[:, :, 1] * (y_b @ w_branch[1])
              + gates[:, :, 2] * (y_c @ w_branch[2]))
    return merged @ w_out


def swiglu(h, w_ffn_in, w_ffn_out):
    gu = h @ w_ffn_in
    gate, up = gu[..., :D_FF], gu[..., D_FF:]
    return (jax.nn.silu(gate) * up) @ w_ffn_out


def setup_inputs(seed: int = 0) -> dict:
    key = jax.random.key(seed)
    ks = jax.random.split(key, 32)
    f32 = jnp.float32

    def nrm(k, shape, scale):
        return scale * jax.random.normal(k, shape, f32)

    x = nrm(ks[0], (BATCH, SEQ, D_MODEL), 1.0)
    meta = nrm(ks[1], (N_META, D_MODEL), 1.0)
    rel_bias = nrm(ks[2], (REL_BUCKETS, DA_HEADS), 0.5)
    norm_w = 1.0 + nrm(ks[3], (DEPTH, 4, D_MODEL), 0.05)
    w_in = nrm(ks[4], (DEPTH, D_MODEL, N_IN), D_MODEL ** -0.5)
    conv_w = nrm(ks[5], (DEPTH, CONV_W, LRU_WIDTH), CONV_W ** -0.5)
    conv_b = nrm(ks[6], (DEPTH, LRU_WIDTH), 0.02)
    lru_w_a = nrm(ks[7], (DEPTH, LRU_BLOCKS, LRU_BLOCK_W, LRU_BLOCK_W), LRU_BLOCK_W ** -0.5)
    lru_b_a = nrm(ks[8], (DEPTH, LRU_WIDTH), 0.02)
    lru_w_x = nrm(ks[9], (DEPTH, LRU_BLOCKS, LRU_BLOCK_W, LRU_BLOCK_W), LRU_BLOCK_W ** -0.5)
    lru_b_x = nrm(ks[10], (DEPTH, LRU_WIDTH), 0.02)
    a8 = jax.random.uniform(ks[11], (DEPTH, LRU_WIDTH), f32, 0.9, 0.999)
    s = a8 ** (1.0 / LRU_C)
    lru_lambda = jnp.log(s) - jnp.log1p(-s)
    da_lambda = nrm(ks[12], (DEPTH, 4, DA_HEAD_DIM), 0.1)
    da_subln = 1.0 + nrm(ks[13], (DEPTH, DA_V_DIM), 0.05)
    s5_lam_re = -0.5 + nrm(ks[14], (DEPTH, S5_GROUPS, S5_STATE), 0.01)
    s5_lam_im = jnp.broadcast_to(math.pi * jnp.arange(S5_STATE, dtype=f32), (DEPTH, S5_GROUPS, S5_STATE))
    s5_b_re = nrm(ks[15], (DEPTH, S5_GROUPS, S5_STATE, S5_GROUP), (2 * S5_GROUP) ** -0.5)
    s5_b_im = nrm(ks[16], (DEPTH, S5_GROUPS, S5_STATE, S5_GROUP), (2 * S5_GROUP) ** -0.5)
    s5_c_re = nrm(ks[17], (DEPTH, S5_GROUPS, S5_GROUP, S5_STATE), S5_STATE ** -0.5)
    s5_c_im = nrm(ks[18], (DEPTH, S5_GROUPS, S5_GROUP, S5_STATE), S5_STATE ** -0.5)
    s5_d = nrm(ks[19], (DEPTH, S5_WIDTH), 1.0)
    s5_log_step = jax.random.uniform(ks[20], (DEPTH, S5_GROUPS), f32,
                                     math.log(S5_DT_MIN), math.log(S5_DT_MAX))
    s5_w_glu = nrm(ks[21], (DEPTH, S5_WIDTH, S5_WIDTH), S5_WIDTH ** -0.5)
    s5_b_glu = nrm(ks[22], (DEPTH, S5_WIDTH), 0.02)
    b_gate = nrm(ks[23], (DEPTH, N_BRANCH, D_MODEL), 0.02)
    w_branch = nrm(ks[24], (DEPTH, N_BRANCH, LRU_WIDTH, D_MODEL), LRU_WIDTH ** -0.5)
    w_out = nrm(ks[25], (DEPTH, D_MODEL, D_MODEL), D_MODEL ** -0.5)
    w_ffn_in = nrm(ks[26], (DEPTH, D_MODEL, 2 * D_FF), D_MODEL ** -0.5)
    w_ffn_out = nrm(ks[27], (DEPTH, D_FF, D_MODEL), D_FF ** -0.5)
    return {"x": x, "meta": meta, "rel_bias": rel_bias, "norm_w": norm_w, "w_in": w_in,
            "conv_w": conv_w, "conv_b": conv_b, "lru_w_a": lru_w_a, "lru_b_a": lru_b_a,
            "lru_w_x": lru_w_x, "lru_b_x": lru_b_x, "lru_lambda": lru_lambda,
            "da_lambda": da_lambda, "da_subln": da_subln, "s5_lam_re": s5_lam_re,
            "s5_lam_im": s5_lam_im, "s5_b_re": s5_b_re, "s5_b_im": s5_b_im, "s5_c_re": s5_c_re,
            "s5_c_im": s5_c_im, "s5_d": s5_d, "s5_log_step": s5_log_step, "s5_w_glu": s5_w_glu,
            "s5_b_glu": s5_b_glu, "b_gate": b_gate, "w_branch": w_branch, "w_out": w_out,
            "w_ffn_in": w_ffn_in, "w_ffn_out": w_ffn_out}


def reference(x, meta, rel_bias, norm_w, w_in, conv_w, conv_b, lru_w_a, lru_b_a, lru_w_x, lru_b_x,
              lru_lambda, da_lambda, da_subln, s5_lam_re, s5_lam_im, s5_b_re, s5_b_im, s5_c_re,
              s5_c_im, s5_d, s5_log_step, s5_w_glu, s5_b_glu, b_gate, w_branch, w_out,
              w_ffn_in, w_ffn_out):
    B = x.shape[0]
    xs = jnp.concatenate([jnp.broadcast_to(meta.astype(x.dtype)[None], (B, N_META, D_MODEL)), x], axis=1)
    for l in range(DEPTH):
        lam_init = 0.8 - 0.6 * math.exp(-0.3 * l)
        h = rms_norm(xs, norm_w[l, 0])
        mix = hybrid_mixer(h, lam_init, rel_bias, w_in[l], conv_w[l], conv_b[l], lru_w_a[l], lru_b_a[l],
                           lru_w_x[l], lru_b_x[l], lru_lambda[l], da_lambda[l], da_subln[l],
                           s5_lam_re[l], s5_lam_im[l], s5_b_re[l], s5_b_im[l], s5_c_re[l], s5_c_im[l],
                           s5_d[l], s5_log_step[l], s5_w_glu[l], s5_b_glu[l], b_gate[l], w_branch[l],
                           w_out[l])
        xs = xs + rms_norm(mix, norm_w[l, 1])
        h = rms_norm(xs, norm_w[l, 2])
        xs = xs + rms_norm(swiglu(h, w_ffn_in[l], w_ffn_out[l]), norm_w[l, 3])
    return xs[:, N_META:]
```

```python
import functools
import math

import jax
import jax.numpy as jnp
from jax import lax
from jax.experimental import pallas as pl
from jax.experimental.pallas import tpu as pltpu

F32 = jnp.float32
BF16 = jnp.bfloat16

N_META = 16
CONV_W = 4
LRU_C = 8.0
LRU_BLOCKS = 8
DA_HEADS = 8
DA_HEAD_DIM = 64
S5_GROUP = 16
S5_STATE = 64
REL_BUCKETS = 32
REL_MAX_DIST = 128
N_BRANCH = 3

LANES = 128
S5_GB = LANES // S5_GROUP
S5_HALF = S5_GB * S5_STATE
SCAN_CH = 128
MASK_NEG = -1e30
VMEM_LIMIT = 56 * 1024 * 1024


def _cparams(sem):
    return pltpu.CompilerParams(dimension_semantics=sem, vmem_limit_bytes=VMEM_LIMIT)


def _pick_tile(n, candidates):
    for c in candidates:
        if n % c == 0:
            return c
    raise ValueError(f"no tile in {candidates} divides {n}")


def _gelu_tanh(x):
    return 0.5 * x * (1.0 + jnp.tanh(math.sqrt(2.0 / math.pi) * (x + 0.044715 * (x * x * x))))


def _sigmoid(x):
    return 1.0 / (1.0 + jnp.exp(-x))


def _rms(x, w, eps):
    return (x * lax.rsqrt(jnp.mean(x * x, axis=-1, keepdims=True) + eps)) * w


def _inproj_kernel(x_ref, nw_ref, w_ref, o_ref, h_scr):
    @pl.when(pl.program_id(1) == 0)
    def _():
        h_scr[...] = _rms(x_ref[...], nw_ref[...], 1e-6).astype(BF16)

    o_ref[...] = jnp.dot(h_scr[...], w_ref[...], preferred_element_type=F32).astype(o_ref.dtype)


def _inproj(xs, nw, w, tm, tn):
    n, d = xs.shape
    n_out = w.shape[1]
    return pl.pallas_call(
        _inproj_kernel,
        grid=(n // tm, n_out // tn),
        in_specs=[pl.BlockSpec((tm, d), lambda i, j: (i, 0)),
                  pl.BlockSpec((1, d), lambda i, j: (0, 0)),
                  pl.BlockSpec((d, tn), lambda i, j: (0, j))],
        out_specs=pl.BlockSpec((tm, tn), lambda i, j: (i, j)),
        out_shape=jax.ShapeDtypeStruct((n, n_out), BF16),
        scratch_shapes=[pltpu.VMEM((tm, d), BF16)],
        compiler_params=_cparams(("parallel", "arbitrary")),
        name="inproj",
    )(xs, nw, w)


def _lru_kernel(g_ref, x_ref, p_ref, w_ref, o_ref, xs_scr):
    t_len = x_ref.shape[1]
    halo = 8
    xs_scr[0:halo, :] = jnp.zeros((halo, LANES), F32)
    xs_scr[halo:, :] = x_ref[0].astype(F32)
    p = p_ref[...]
    cw = [p[j:j + 1, :] for j in range(CONV_W)]
    cb, ba, bx, logsig = p[4:5, :], p[5:6, :], p[6:7, :], p[7:8, :]
    w = w_ref[0]
    row = lax.broadcasted_iota(jnp.int32, (SCAN_CH, LANES), 0)

    def chunk(c, h0):
        base = pl.multiple_of(c * SCAN_CH, SCAN_CH)
        xc = cb
        for j in range(CONV_W):
            xc = xc + cw[j] * xs_scr[pl.ds(base + halo - (CONV_W - 1) + j, SCAN_CH), :]
        ri = jnp.dot(xc.astype(BF16), w, preferred_element_type=F32)
        r = _sigmoid(ri[:, :LANES] + ba)
        i = _sigmoid(ri[:, LANES:] + bx)
        log_a = LRU_C * r * logsig
        a = jnp.exp(log_a)
        b = jnp.sqrt(1.0 - jnp.exp(2.0 * log_a)) * (i * xc)
        s = 1
        while s < SCAN_CH:
            keep = row >= s
            a_sh = jnp.where(keep, pltpu.roll(a, s, 0), 1.0)
            b_sh = jnp.where(keep, pltpu.roll(b, s, 0), 0.0)
            b = a * b_sh + b
            a = a * a_sh
            s *= 2
        h = a * h0 + b
        g = g_ref[0, pl.ds(base, SCAN_CH), :].astype(F32)
        o_ref[0, pl.ds(base, SCAN_CH), :] = (h * _gelu_tanh(g)).astype(o_ref.dtype)
        return h[SCAN_CH - 1:SCAN_CH, :]

    lax.fori_loop(0, t_len // SCAN_CH, chunk, jnp.zeros((1, LANES), F32))


def _lru(proj3, lru_p, w_ax, col_gate, col_x):
    b, t, _ = proj3.shape
    width = LRU_BLOCKS * LANES
    return pl.pallas_call(
        _lru_kernel,
        grid=(b, LRU_BLOCKS),
        in_specs=[pl.BlockSpec((1, t, LANES), lambda bi, h: (bi, 0, col_gate + h)),
                  pl.BlockSpec((1, t, LANES), lambda bi, h: (bi, 0, col_x + h)),
                  pl.BlockSpec((8, LANES), lambda bi, h: (0, h)),
                  pl.BlockSpec((1, LANES, 2 * LANES), lambda bi, h: (h, 0, 0))],
        out_specs=pl.BlockSpec((1, t, LANES), lambda bi, h: (bi, 0, h)),
        out_shape=jax.ShapeDtypeStruct((b, t, width), BF16),
        scratch_shapes=[pltpu.VMEM((t + 8, LANES), F32)],
        compiler_params=_cparams(("parallel", "parallel")),
        name="rglru",
    )(proj3, proj3, lru_p, w_ax)


def _bias_tile_kernel(rb_ref, o_ref):
    h = pl.program_id(0)
    tb = o_ref.shape[2]
    i = lax.broadcasted_iota(jnp.int32, (tb, tb), 0)
    j = lax.broadcasted_iota(jnp.int32, (tb, tb), 1)
    max_exact = REL_BUCKETS // 2
    far = rb_ref[REL_BUCKETS - 1, h]
    for which in range(2):
        d = i - j + which * tb
        n = jnp.maximum(d, 0)
        nf = jnp.maximum(n, 1).astype(F32)
        large = max_exact + (jnp.log(nf / max_exact) / math.log(REL_MAX_DIST / max_exact)
                             * (REL_BUCKETS - max_exact)).astype(jnp.int32)
        large = jnp.minimum(large, REL_BUCKETS - 1)
        bucket = jnp.where(n < max_exact, n, large)
        val = jnp.zeros((tb, tb), F32)
        for bkt in range(REL_BUCKETS):
            val = jnp.where(bucket == bkt, rb_ref[bkt, h], val)
        val = val - far
        if which == 0:
            val = jnp.where(d >= 0, val, MASK_NEG)
        o_ref[0, which] = val


def _bias_tiles(rel_bias, tb):
    return pl.pallas_call(
        _bias_tile_kernel,
        grid=(DA_HEADS,),
        in_specs=[pl.BlockSpec(memory_space=pltpu.SMEM)],
        out_specs=pl.BlockSpec((1, 2, tb, tb), lambda h: (h, 0, 0, 0)),
        out_shape=jax.ShapeDtypeStruct((DA_HEADS, 2, tb, tb), F32),
        compiler_params=_cparams(("parallel",)),
        name="t5_bias_tiles",
    )(rel_bias)


def _attn_kernel(q_ref, k_ref, v_ref, bt_ref, dl_ref, sw_ref, o_ref, m_scr, l_scr, acc_scr, *, lam_init):
    qi = pl.program_id(2)
    tb = q_ref.shape[1]
    dn_t = (((1,), (1,)), ((), ()))
    lane = lax.broadcasted_iota(jnp.int32, (tb, LANES), 1)
    qs = q_ref[0] * (DA_HEAD_DIM ** -0.5)
    zero = jnp.zeros_like(qs)
    qmaps = (jnp.where(lane < DA_HEAD_DIM, qs, zero), jnp.where(lane >= DA_HEAD_DIM, qs, zero))

    m_scr[...] = jnp.full(m_scr.shape, MASK_NEG, F32)
    l_scr[...] = jnp.zeros(l_scr.shape, F32)
    acc_scr[...] = jnp.zeros(acc_scr.shape, F32)

    def block(kstart, bias):
        kb = k_ref[0, pl.ds(kstart, tb), :]
        vb = v_ref[0, pl.ds(kstart, tb), :]
        for c in range(2):
            s = lax.dot_general(qmaps[c], kb, dn_t, preferred_element_type=F32)
            if bias is not None:
                s = s + bias
            m_prev = m_scr[c]
            m_new = jnp.maximum(m_prev, jnp.max(s, axis=-1, keepdims=True))
            alpha = jnp.exp(m_prev - m_new)
            p = jnp.exp(s - m_new)
            l_scr[c] = alpha * l_scr[c] + jnp.sum(p, axis=-1, keepdims=True)
            acc_scr[c] = alpha * acc_scr[c] + jnp.dot(p.astype(BF16), vb, preferred_element_type=F32)
            m_scr[c] = m_new

    def far_block(ki, carry):
        block(pl.multiple_of(ki * tb, tb), None)
        return carry

    lax.fori_loop(0, qi - 1, far_block, 0)

    @pl.when(qi >= 1)
    def _():
        block(pl.multiple_of((qi - 1) * tb, tb), bt_ref[0, 1])

    block(pl.multiple_of(qi * tb, tb), bt_ref[0, 0])

    dl = dl_ref[...]
    lam = (jnp.exp(jnp.sum(dl[0:1, :] * dl[1:2, :], axis=-1, keepdims=True))
           - jnp.exp(jnp.sum(dl[2:3, :] * dl[3:4, :], axis=-1, keepdims=True)) + lam_init)
    o = acc_scr[0] / l_scr[0] - lam * (acc_scr[1] / l_scr[1])
    o = _rms(o, sw_ref[...], 1e-5) * (1.0 - lam_init)
    o_ref[0] = o.astype(o_ref.dtype)


def _attn(proj3, btiles, da_lambda, da_subln, lam_init, tb, col_q, col_k, col_v):
    b, t, _ = proj3.shape
    width = DA_HEADS * LANES
    return pl.pallas_call(
        functools.partial(_attn_kernel, lam_init=lam_init),
        grid=(b, DA_HEADS, t // tb),
        in_specs=[pl.BlockSpec((1, tb, LANES), lambda bi, h, qi: (bi, qi, col_q + h)),
                  pl.BlockSpec((1, t, LANES), lambda bi, h, qi: (bi, 0, col_k + h)),
                  pl.BlockSpec((1, t, LANES), lambda bi, h, qi: (bi, 0, col_v + h)),
                  pl.BlockSpec((1, 2, tb, tb), lambda bi, h, qi: (h, 0, 0, 0)),
                  pl.BlockSpec((4, DA_HEAD_DIM), lambda bi, h, qi: (0, 0)),
                  pl.BlockSpec((1, LANES), lambda bi, h, qi: (0, 0))],
        out_specs=pl.BlockSpec((1, tb, LANES), lambda bi, h, qi: (bi, qi, h)),
        out_shape=jax.ShapeDtypeStruct((b, t, width), BF16),
        scratch_shapes=[pltpu.VMEM((2, tb, 1), F32), pltpu.VMEM((2, tb, 1), F32),
                        pltpu.VMEM((2, tb, LANES), F32)],
        compiler_params=_cparams(("parallel", "parallel", "arbitrary")),
        name="diff_attn",
    )(proj3, proj3, proj3, btiles, da_lambda, da_subln)


def _s5_kernel(u_ref, bm_ref, cm_ref, lp_ref, pt_ref, d_ref, o_ref):
    t_len = u_ref.shape[1]
    bm = bm_ref[0]
    cm = cm_ref[0]
    lp = lp_ref[0]
    pr = pt_ref[0, :, :S5_HALF]
    pi = pt_ref[0, :, S5_HALF:]
    dvec = d_ref[...]
    row = lax.broadcasted_iota(jnp.int32, (SCAN_CH, S5_HALF), 0)

    def chunk(c, carry):
        h0r, h0i = carry
        base = pl.multiple_of(c * SCAN_CH, SCAN_CH)
        u = u_ref[0, pl.ds(base, SCAN_CH), :]
        bu = jnp.dot(u, bm, preferred_element_type=F32)
        hr, hi = bu[:, :S5_HALF], bu[:, S5_HALF:]
        s, lvl = 1, 0
        while s < SCAN_CH:
            ar, ai = lp[lvl:lvl + 1, :S5_HALF], lp[lvl:lvl + 1, S5_HALF:]
            keep = row >= s
            sr = jnp.where(keep, pltpu.roll(hr, s, 0), 0.0)
            si = jnp.where(keep, pltpu.roll(hi, s, 0), 0.0)
            hr, hi = hr + (ar * sr - ai * si), hi + (ar * si + ai * sr)
            s *= 2
            lvl += 1
        hr, hi = hr + (pr * h0r - pi * h0i), hi + (pr * h0i + pi * h0r)
        hcat = jnp.concatenate([hr, hi], axis=1).astype(BF16)
        y = jnp.dot(hcat, cm, preferred_element_type=F32) + dvec * u.astype(F32)
        o_ref[0, pl.ds(base, SCAN_CH), :] = _gelu_tanh(y).astype(o_ref.dtype)
        return hr[SCAN_CH - 1:SCAN_CH, :], hi[SCAN_CH - 1:SCAN_CH, :]

    z = jnp.zeros((1, S5_HALF), F32)
    lax.fori_loop(0, t_len // SCAN_CH, chunk, (z, z))


def _s5(proj3, bmat, cmat, lpow, ptab, dvec, col_u):
    b, t, _ = proj3.shape
    nblk = bmat.shape[0]
    return pl.pallas_call(
        _s5_kernel,
        grid=(b, nblk),
        in_specs=[pl.BlockSpec((1, t, LANES), lambda bi, g: (bi, 0, col_u + g)),
                  pl.BlockSpec((1, LANES, 2 * S5_HALF), lambda bi, g: (g, 0, 0)),
                  pl.BlockSpec((1, 2 * S5_HALF, LANES), lambda bi, g: (g, 0, 0)),
                  pl.BlockSpec((1, 8, 2 * S5_HALF), lambda bi, g: (g, 0, 0)),
                  pl.BlockSpec((1, SCAN_CH, 2 * S5_HALF), lambda bi, g: (g, 0, 0)),
                  pl.BlockSpec((1, LANES), lambda bi, g: (0, g))],
        out_specs=pl.BlockSpec((1, t, LANES), lambda bi, g: (bi, 0, g)),
        out_shape=jax.ShapeDtypeStruct((b, t, nblk * LANES), BF16),
        compiler_params=_cparams(("parallel", "parallel")),
        name="s5",
    )(proj3, bmat, cmat, lpow, ptab, dvec)


def _merge_kernel(ya_ref, yb_ref, yc_ref, g0_ref, g1_ref, g2_ref, wg_ref, bg_ref, wb_ref, bgate_ref,
                  o_ref, yc_scr):
    @pl.when(pl.program_id(1) == 0)
    def _():
        yc = yc_ref[...]
        z = jnp.dot(yc, wg_ref[...], preferred_element_type=F32) + bg_ref[...]
        yc_scr[...] = (yc.astype(F32) * _sigmoid(z)).astype(BF16)

    bgate = bgate_ref[...]
    ys = (ya_ref[...], yb_ref[...], yc_scr[...])
    gs = (g0_ref, g1_ref, g2_ref)
    merged = None
    for br in range(N_BRANCH):
        gate = _sigmoid(gs[br][...].astype(F32) + bgate[br:br + 1, :])
        term = gate * jnp.dot(ys[br], wb_ref[br], preferred_element_type=F32)
        merged = term if merged is None else merged + term
    o_ref[...] = merged.astype(o_ref.dtype)


def _merge(ya, yb, yc, proj, w_glu, b_glu, w_branch, b_gate, col_g, tm, tn):
    n, wdt = ya.shape
    d = w_branch.shape[2]
    gcol = [(col_g + br * d) // tn for br in range(N_BRANCH)]
    yspec = pl.BlockSpec((tm, wdt), lambda i, j: (i, 0))

    def gspec(br):
        return pl.BlockSpec((tm, tn), lambda i, j: (i, gcol[br] + j))

    return pl.pallas_call(
        _merge_kernel,
        grid=(n // tm, d // tn),
        in_specs=[yspec, yspec, yspec, gspec(0), gspec(1), gspec(2),
                  pl.BlockSpec((wdt, wdt), lambda i, j: (0, 0)),
                  pl.BlockSpec((1, wdt), lambda i, j: (0, 0)),
                  pl.BlockSpec((N_BRANCH, wdt, tn), lambda i, j: (0, 0, j)),
                  pl.BlockSpec((N_BRANCH, tn), lambda i, j: (0, j))],
        out_specs=pl.BlockSpec((tm, tn), lambda i, j: (i, j)),
        out_shape=jax.ShapeDtypeStruct((n, d), BF16),
        scratch_shapes=[pltpu.VMEM((tm, wdt), BF16)],
        compiler_params=_cparams(("parallel", "arbitrary")),
        name="merge",
    )(ya, yb, yc, proj, proj, proj, w_glu, b_glu, w_branch, b_gate)


def _outproj_kernel(m_ref, w_ref, xs_ref, nw_ref, o_ref):
    mix = jnp.dot(m_ref[...], w_ref[...], preferred_element_type=F32)
    o_ref[...] = xs_ref[...] + _rms(mix, nw_ref[...], 1e-6)


def _outproj(merged, w_out, xs, nw, tm):
    n, d = xs.shape
    return pl.pallas_call(
        _outproj_kernel,
        grid=(n // tm,),
        in_specs=[pl.BlockSpec((tm, d), lambda i: (i, 0)),
                  pl.BlockSpec((d, d), lambda i: (0, 0)),
                  pl.BlockSpec((tm, d), lambda i: (i, 0)),
                  pl.BlockSpec((1, d), lambda i: (0, 0))],
        out_specs=pl.BlockSpec((tm, d), lambda i: (i, 0)),
        out_shape=jax.ShapeDtypeStruct((n, d), F32),
        compiler_params=_cparams(("parallel",)),
        name="outproj_residual",
    )(merged, w_out, xs, nw)


def _ffn_kernel(xs_ref, nw_in_ref, wg_ref, wu_ref, wo_ref, nw_out_ref, o_ref, h_scr, acc_scr):
    f = pl.program_id(1)

    @pl.when(f == 0)
    def _():
        h_scr[...] = _rms(xs_ref[...], nw_in_ref[...], 1e-6).astype(BF16)
        acc_scr[...] = jnp.zeros(acc_scr.shape, F32)

    h = h_scr[...]
    gate = jnp.dot(h, wg_ref[...], preferred_element_type=F32)
    up = jnp.dot(h, wu_ref[...], preferred_element_type=F32)
    act = (gate * _sigmoid(gate) * up).astype(BF16)
    acc_scr[...] += jnp.dot(act, wo_ref[...], preferred_element_type=F32)

    @pl.when(f == pl.num_programs(1) - 1)
    def _():
        o_ref[...] = xs_ref[...] + _rms(acc_scr[...], nw_out_ref[...], 1e-6)


def _ffn(xs, nw_in, w_ffn_in, w_ffn_out, nw_out, tm, tf):
    n, d = xs.shape
    d_ff = w_ffn_out.shape[0]
    nf = d_ff // tf
    return pl.pallas_call(
        _ffn_kernel,
        grid=(n // tm, nf),
        in_specs=[pl.BlockSpec((tm, d), lambda i, f: (i, 0)),
                  pl.BlockSpec((1, d), lambda i, f: (0, 0)),
                  pl.BlockSpec((d, tf), lambda i, f: (0, f)),
                  pl.BlockSpec((d, tf), lambda i, f: (0, nf + f)),
                  pl.BlockSpec((tf, d), lambda i, f: (f, 0)),
                  pl.BlockSpec((1, d), lambda i, f: (0, 0))],
        out_specs=pl.BlockSpec((tm, d), lambda i, f: (i, 0)),
        out_shape=jax.ShapeDtypeStruct((n, d), F32),
        scratch_shapes=[pltpu.VMEM((tm, d), BF16), pltpu.VMEM((tm, d), F32)],
        compiler_params=_cparams(("parallel", "arbitrary")),
        name="swiglu_ffn",
    )(xs, nw_in, w_ffn_in, w_ffn_in, w_ffn_out, nw_out)


def _s5_tables(lam_re, lam_im, b_re, b_im, c_re, c_im, log_step):
    groups = lam_re.shape[0]
    nblk = groups // S5_GB
    lr, li = lam_re.astype(F32), lam_im.astype(F32)
    step = jnp.exp(log_step.astype(F32))[:, None]
    mag = jnp.exp(lr * step)
    ab_re, ab_im = mag * jnp.cos(li * step), mag * jnp.sin(li * step)
    den = lr * lr + li * li
    coef_re = ((ab_re - 1.0) * lr + ab_im * li) / den
    coef_im = (ab_im * lr - (ab_re - 1.0) * li) / den
    br, bi = b_re.astype(F32), b_im.astype(F32)
    bb_re = coef_re[..., None] * br - coef_im[..., None] * bi
    bb_im = coef_re[..., None] * bi + coef_im[..., None] * br
    eye = jnp.eye(S5_GB, dtype=F32)

    def blockdiag_in(bb):
        bb = bb.reshape(nblk, S5_GB, S5_STATE, S5_GROUP)
        return jnp.einsum("ngpc,gh->ngchp", bb, eye).reshape(nblk, LANES, S5_HALF)

    def blockdiag_out(cc):
        cc = cc.reshape(nblk, S5_GB, S5_GROUP, S5_STATE)
        return jnp.einsum("ngcp,gh->ngphc", cc, eye).reshape(nblk, S5_HALF, LANES)

    bmat = jnp.concatenate([blockdiag_in(bb_re), blockdiag_in(bb_im)], axis=2).astype(BF16)
    cmat = jnp.concatenate([blockdiag_out(c_re.astype(F32)), blockdiag_out(-c_im.astype(F32))],
                           axis=1).astype(BF16)

    def powers(n):
        nn = n.astype(F32)[:, None, None]
        m = jnp.exp(nn * (lr * step)[None])
        ang = nn * (li * step)[None]
        pr = (m * jnp.cos(ang)).reshape(-1, nblk, S5_HALF)
        pi = (m * jnp.sin(ang)).reshape(-1, nblk, S5_HALF)
        return jnp.concatenate([pr, pi], axis=2).transpose(1, 0, 2)

    lpow = powers(2 ** jnp.arange(8))
    ptab = powers(1 + jnp.arange(SCAN_CH))
    return bmat, cmat, lpow, ptab


def kernel(x, meta, rel_bias, norm_w, w_in, conv_w, conv_b, lru_w_a, lru_b_a, lru_w_x, lru_b_x, lru_lambda, da_lambda, da_subln, s5_lam_re, s5_lam_im, s5_b_re, s5_b_im, s5_c_re, s5_c_im, s5_d, s5_log_step, s5_w_glu, s5_b_glu, b_gate, w_branch, w_out, w_ffn_in, w_ffn_out):
    bsz, seq, d_model = x.shape
    depth = w_in.shape[0]
    lru_w = conv_w.shape[2]
    s5_w = s5_d.shape[1]
    qk_w = DA_HEADS * 2 * DA_HEAD_DIM
    assert lru_w == LRU_BLOCKS * LANES and s5_w % LANES == 0 and d_model % LANES == 0
    col_gate, col_x = 0, lru_w // LANES
    col_q = 2 * lru_w // LANES
    col_k = col_q + qk_w // LANES
    col_v = col_k + qk_w // LANES
    col_u = col_v + DA_HEADS
    col_g = (col_u + s5_w // LANES) * LANES

    t_real = N_META + seq
    tb = 384 if t_real >= 1024 else 128
    t_pad = -(-t_real // tb) * tb
    assert t_pad % SCAN_CH == 0
    n_tok = bsz * t_pad
    tm = _pick_tile(n_tok, (512, 384, 256, 128))
    tm_out = _pick_tile(n_tok, (256, 128))

    xs = jnp.concatenate([jnp.broadcast_to(meta.astype(F32)[None], (bsz, N_META, d_model)), x,
                          jnp.zeros((bsz, t_pad - t_real, d_model), F32)], axis=1)
    xs = xs.reshape(n_tok, d_model)
    btiles = _bias_tiles(rel_bias.astype(F32), tb)

    for l in range(depth):
        lam_init = 0.8 - 0.6 * math.exp(-0.3 * l)
        proj = _inproj(xs, norm_w[l, 0][None], w_in[l].astype(BF16), tm, 1024)
        proj3 = proj.reshape(bsz, t_pad, proj.shape[1])

        lru_p = jnp.concatenate([conv_w[l], conv_b[l][None], lru_b_a[l][None], lru_b_x[l][None],
                                 jax.nn.log_sigmoid(lru_lambda[l].astype(F32))[None]], axis=0)
        w_ax = jnp.concatenate([lru_w_a[l], lru_w_x[l]], axis=2).astype(BF16)
        y_a = _lru(proj3, lru_p, w_ax, col_gate, col_x)

        y_b = _attn(proj3, btiles, da_lambda[l], da_subln[l][None], lam_init, tb, col_q, col_k, col_v)

        bmat, cmat, lpow, ptab = _s5_tables(s5_lam_re[l], s5_lam_im[l], s5_b_re[l], s5_b_im[l],
                                            s5_c_re[l], s5_c_im[l], s5_log_step[l])
        y_c = _s5(proj3, bmat, cmat, lpow, ptab, s5_d[l][None], col_u)

        merged = _merge(y_a.reshape(n_tok, lru_w), y_b.reshape(n_tok, -1), y_c.reshape(n_tok, s5_w),
                        proj, s5_w_glu[l].astype(BF16), s5_b_glu[l][None], w_branch[l].astype(BF16),
                        b_gate[l], col_g, tm, 512)
        xs = _outproj(merged, w_out[l].astype(BF16), xs, norm_w[l, 1][None], tm_out)
        xs = _ffn(xs, norm_w[l, 2][None], w_ffn_in[l].astype(BF16), w_ffn_out[l].astype(BF16),
                  norm_w[l, 3][None], tm, 512)

    return xs.reshape(bsz, t_pad, d_model)[:, N_META:t_real]
```

```python
import functools
import math

import jax
import jax.numpy as jnp
from jax import lax
from jax.experimental import pallas as pl
from jax.experimental.pallas import tpu as pltpu

F32 = jnp.float32
BF16 = jnp.bfloat16

N_META = 16
CONV_W = 4
LRU_C = 8.0
LRU_BLOCKS = 8
DA_HEADS = 8
DA_HEAD_DIM = 64
S5_GROUP = 16
S5_STATE = 64
REL_BUCKETS = 32
REL_MAX_DIST = 128
N_BRANCH = 3

LANES = 128
S5_GB = LANES // S5_GROUP
S5_HALF = S5_GB * S5_STATE
S5_L = 8
SCAN_CH = 128
MASK_NEG = -1e30
VMEM_LIMIT = 56 * 1024 * 1024


def _cparams(sem):
    return pltpu.CompilerParams(dimension_semantics=sem, vmem_limit_bytes=VMEM_LIMIT)


def _pick_tile(n, candidates):
    for c in candidates:
        if n % c == 0:
            return c
    raise ValueError(f"no tile in {candidates} divides {n}")


def _gelu_tanh(x):
    return 0.5 * x * (1.0 + jnp.tanh(math.sqrt(2.0 / math.pi) * (x + 0.044715 * (x * x * x))))


def _sigmoid(x):
    return 1.0 / (1.0 + jnp.exp(-x))


def _rms(x, w, eps):
    return (x * lax.rsqrt(jnp.mean(x * x, axis=-1, keepdims=True) + eps)) * w


def _inproj_kernel(x_ref, nw_ref, w_ref, o_ref, h_scr):
    @pl.when(pl.program_id(1) == 0)
    def _():
        h_scr[...] = _rms(x_ref[...], nw_ref[...], 1e-6).astype(BF16)

    o_ref[...] = jnp.dot(h_scr[...], w_ref[...], preferred_element_type=F32).astype(o_ref.dtype)


def _inproj(xs, nw, w, tm, tn):
    n, d = xs.shape
    n_out = w.shape[1]
    return pl.pallas_call(
        _inproj_kernel,
        grid=(n // tm, n_out // tn),
        in_specs=[pl.BlockSpec((tm, d), lambda i, j: (i, 0)),
                  pl.BlockSpec((1, d), lambda i, j: (0, 0)),
                  pl.BlockSpec((d, tn), lambda i, j: (0, j))],
        out_specs=pl.BlockSpec((tm, tn), lambda i, j: (i, j)),
        out_shape=jax.ShapeDtypeStruct((n, n_out), BF16),
        scratch_shapes=[pltpu.VMEM((tm, d), BF16)],
        compiler_params=_cparams(("parallel", "arbitrary")),
        name="inproj",
    )(xs, nw, w)


def _lru_kernel(g_ref, x_ref, p_ref, w_ref, o_ref, xs_scr):
    t_len = x_ref.shape[1]
    halo = 8
    xs_scr[0:halo, :] = jnp.zeros((halo, LANES), F32)
    xs_scr[halo:, :] = x_ref[0].astype(F32)
    p = p_ref[...]
    cw = [p[j:j + 1, :] for j in range(CONV_W)]
    cb, ba, bx, logsig = p[4:5, :], p[5:6, :], p[6:7, :], p[7:8, :]
    w = w_ref[0]
    row = lax.broadcasted_iota(jnp.int32, (SCAN_CH, LANES), 0)

    def chunk(c, h0):
        base = pl.multiple_of(c * SCAN_CH, SCAN_CH)
        xc = cb
        for j in range(CONV_W):
            xc = xc + cw[j] * xs_scr[pl.ds(base + halo - (CONV_W - 1) + j, SCAN_CH), :]
        ri = jnp.dot(xc.astype(BF16), w, preferred_element_type=F32)
        r = _sigmoid(ri[:, :LANES] + ba)
        i = _sigmoid(ri[:, LANES:] + bx)
        log_a = LRU_C * r * logsig
        a = jnp.exp(log_a)
        b = jnp.sqrt(1.0 - jnp.exp(2.0 * log_a)) * (i * xc)
        s = 1
        while s < SCAN_CH:
            keep = row >= s
            a_sh = jnp.where(keep, pltpu.roll(a, s, 0), 1.0)
            b_sh = jnp.where(keep, pltpu.roll(b, s, 0), 0.0)
            b = a * b_sh + b
            a = a * a_sh
            s *= 2
        h = a * h0 + b
        g = g_ref[0, pl.ds(base, SCAN_CH), :].astype(F32)
        o_ref[0, pl.ds(base, SCAN_CH), :] = (h * _gelu_tanh(g)).astype(o_ref.dtype)
        return h[SCAN_CH - 1:SCAN_CH, :]

    lax.fori_loop(0, t_len // SCAN_CH, chunk, jnp.zeros((1, LANES), F32))


def _lru(proj3, lru_p, w_ax, col_gate, col_x):
    b, t, _ = proj3.shape
    width = LRU_BLOCKS * LANES
    return pl.pallas_call(
        _lru_kernel,
        grid=(b, LRU_BLOCKS),
        in_specs=[pl.BlockSpec((1, t, LANES), lambda bi, h: (bi, 0, col_gate + h)),
                  pl.BlockSpec((1, t, LANES), lambda bi, h: (bi, 0, col_x + h)),
                  pl.BlockSpec((8, LANES), lambda bi, h: (0, h)),
                  pl.BlockSpec((1, LANES, 2 * LANES), lambda bi, h: (h, 0, 0))],
        out_specs=pl.BlockSpec((1, t, LANES), lambda bi, h: (bi, 0, h)),
        out_shape=jax.ShapeDtypeStruct((b, t, width), BF16),
        scratch_shapes=[pltpu.VMEM((t + 8, LANES), F32)],
        compiler_params=_cparams(("parallel", "parallel")),
        name="rglru",
    )(proj3, proj3, lru_p, w_ax)


def _bias_tile_kernel(rb_ref, o_ref):
    h = pl.program_id(0)
    tb = o_ref.shape[1]
    i = lax.broadcasted_iota(jnp.int32, (tb, 2 * tb), 0)
    j2 = lax.broadcasted_iota(jnp.int32, (tb, 2 * tb), 1)
    max_exact = REL_BUCKETS // 2
    d = i - j2 + tb
    n = jnp.maximum(d, 0)
    nf = jnp.maximum(n, 1).astype(F32)
    large = max_exact + (jnp.log(nf / max_exact) / math.log(REL_MAX_DIST / max_exact)
                         * (REL_BUCKETS - max_exact)).astype(jnp.int32)
    large = jnp.minimum(large, REL_BUCKETS - 1)
    bucket = jnp.where(n < max_exact, n, large)
    val = jnp.zeros((tb, 2 * tb), F32)
    for bkt in range(REL_BUCKETS):
        val = jnp.where(bucket == bkt, rb_ref[bkt, h], val)
    val = val - rb_ref[REL_BUCKETS - 1, h]
    o_ref[0] = jnp.where(d >= 0, val, MASK_NEG)


def _bias_tiles(rel_bias, tb):
    return pl.pallas_call(
        _bias_tile_kernel,
        grid=(DA_HEADS,),
        in_specs=[pl.BlockSpec(memory_space=pltpu.SMEM)],
        out_specs=pl.BlockSpec((1, tb, 2 * tb), lambda h: (h, 0, 0)),
        out_shape=jax.ShapeDtypeStruct((DA_HEADS, tb, 2 * tb), F32),
        compiler_params=_cparams(("parallel",)),
        name="t5_bias_tiles",
    )(rel_bias)


def _attn_kernel(q_ref, k_ref, v_ref, bt_ref, dl_ref, sw_ref, o_ref, m_scr, l_scr, acc_scr, *, lam_init):
    qi = pl.program_id(2)
    tb = q_ref.shape[1]
    dn_t = (((1,), (1,)), ((), ()))
    lane = lax.broadcasted_iota(jnp.int32, (tb, LANES), 1)
    qs = q_ref[0] * (DA_HEAD_DIM ** -0.5)
    zero = jnp.zeros_like(qs)
    qmaps = (jnp.where(lane < DA_HEAD_DIM, qs, zero), jnp.where(lane >= DA_HEAD_DIM, qs, zero))
    n_far = jnp.maximum(qi - 1, 0)
    diag_start = pl.multiple_of(qi * tb, tb)
    sub_start = pl.multiple_of(n_far * tb, tb)
    odd_start = pl.multiple_of(jnp.maximum(n_far - 1, 0) * tb, tb)
    has_sub = qi >= 1
    has_odd = n_far % 2 == 1
    n_pairs = n_far // 2

    def scores(c, kstart, width, bias):
        kb = k_ref[0, pl.ds(kstart, width), :]
        s = lax.dot_general(qmaps[c], kb, dn_t, preferred_element_type=F32)
        return s if bias is None else s + bias

    def lane_chunks(s):
        return [s[:, j * LANES:(j + 1) * LANES] for j in range(s.shape[1] // LANES)]

    def max_into(m, s):
        for ch in lane_chunks(s):
            m = jnp.maximum(m, ch)
        return m

    m_init = jnp.full((tb, LANES), MASK_NEG, F32)

    @pl.when(jnp.logical_not(has_sub))
    def _():
        for c in range(2):
            m_scr[c] = max_into(m_init, scores(c, diag_start, tb, bt_ref[0, :, tb:]))

    @pl.when(has_sub)
    def _():
        for c in range(2):
            m_scr[c] = max_into(m_init, scores(c, sub_start, 2 * tb, bt_ref[0]))

    @pl.when(has_odd)
    def _():
        for c in range(2):
            m_scr[c] = max_into(m_scr[c], scores(c, odd_start, tb, None))

    def pair_max(ki, ms):
        kstart = pl.multiple_of(ki * (2 * tb), 2 * tb)
        return tuple(max_into(ms[c], scores(c, kstart, 2 * tb, None)) for c in range(2))

    ms = lax.fori_loop(0, n_pairs, pair_max, (m_scr[0], m_scr[1]))
    mb = [jnp.broadcast_to(jnp.max(ms[c], axis=-1, keepdims=True), (tb, LANES)) for c in range(2)]

    def accumulate(c, kstart, width, bias, l, acc):
        ps = [jnp.exp(ch - mb[c]) for ch in lane_chunks(scores(c, kstart, width, bias))]
        for ch in ps:
            l = l + ch
        p = jnp.concatenate(ps, axis=1).astype(BF16)
        vb = v_ref[0, pl.ds(kstart, width), :]
        return l, acc + jnp.dot(p, vb, preferred_element_type=F32)

    zl = jnp.zeros((tb, LANES), F32)

    @pl.when(jnp.logical_not(has_sub))
    def _():
        for c in range(2):
            l_scr[c], acc_scr[c] = accumulate(c, diag_start, tb, bt_ref[0, :, tb:], zl, zl)

    @pl.when(has_sub)
    def _():
        for c in range(2):
            l_scr[c], acc_scr[c] = accumulate(c, sub_start, 2 * tb, bt_ref[0], zl, zl)

    @pl.when(has_odd)
    def _():
        for c in range(2):
            l_scr[c], acc_scr[c] = accumulate(c, odd_start, tb, None, l_scr[c], acc_scr[c])

    def pair_acc(ki, st):
        kstart = pl.multiple_of(ki * (2 * tb), 2 * tb)
        l0, a0 = accumulate(0, kstart, 2 * tb, None, st[0], st[1])
        l1, a1 = accumulate(1, kstart, 2 * tb, None, st[2], st[3])
        return l0, a0, l1, a1

    l0, a0, l1, a1 = lax.fori_loop(0, n_pairs, pair_acc, (l_scr[0], acc_scr[0], l_scr[1], acc_scr[1]))

    dl = dl_ref[...]
    lam = (jnp.exp(jnp.sum(dl[0:1, :] * dl[1:2, :], axis=-1, keepdims=True))
           - jnp.exp(jnp.sum(dl[2:3, :] * dl[3:4, :], axis=-1, keepdims=True)) + lam_init)
    o = (a0 / jnp.sum(l0, axis=-1, keepdims=True)
         - lam * (a1 / jnp.sum(l1, axis=-1, keepdims=True)))
    o = _rms(o, sw_ref[...], 1e-5) * (1.0 - lam_init)
    o_ref[0] = o.astype(o_ref.dtype)


def _attn(proj3, btiles, da_lambda, da_subln, lam_init, tb, col_q, col_k, col_v):
    b, t, _ = proj3.shape
    width = DA_HEADS * LANES
    return pl.pallas_call(
        functools.partial(_attn_kernel, lam_init=lam_init),
        grid=(b, DA_HEADS, t // tb),
        in_specs=[pl.BlockSpec((1, tb, LANES), lambda bi, h, qi: (bi, qi, col_q + h)),
                  pl.BlockSpec((1, t, LANES), lambda bi, h, qi: (bi, 0, col_k + h)),
                  pl.BlockSpec((1, t, LANES), lambda bi, h, qi: (bi, 0, col_v + h)),
                  pl.BlockSpec((1, tb, 2 * tb), lambda bi, h, qi: (h, 0, 0)),
                  pl.BlockSpec((4, DA_HEAD_DIM), lambda bi, h, qi: (0, 0)),
                  pl.BlockSpec((1, LANES), lambda bi, h, qi: (0, 0))],
        out_specs=pl.BlockSpec((1, tb, LANES), lambda bi, h, qi: (bi, qi, h)),
        out_shape=jax.ShapeDtypeStruct((b, t, width), BF16),
        scratch_shapes=[pltpu.VMEM((2, tb, LANES), F32), pltpu.VMEM((2, tb, LANES), F32),
                        pltpu.VMEM((2, tb, LANES), F32)],
        compiler_params=_cparams(("parallel", "parallel", "arbitrary")),
        name="diff_attn",
    )(proj3, proj3, proj3, btiles, da_lambda, da_subln)


def _s5_kernel(u_ref, wg_ref, wk_ref, wc_ref, pw_ref, d_ref, o_ref, uf_scr, g_scr, hp_scr, y_scr):
    t_len = u_ref.shape[1]
    nj = t_len // S5_L
    uf_scr[...] = u_ref[0].astype(F32)
    ur = jnp.concatenate([uf_scr[pl.ds(s, nj, stride=S5_L), :].astype(BF16) for s in range(S5_L)],
                         axis=1)
    g_scr[...] = jnp.dot(ur, wg_ref[0], preferred_element_type=F32)

    pw = pw_ref[0]
    pr, pi = pw[8:16, :S5_HALF], pw[8:16, S5_HALF:]
    row = lax.broadcasted_iota(jnp.int32, (8, S5_HALF), 0)

    def group(gi, carry):
        cr, ci = carry
        base = pl.multiple_of(gi * 8, 8)
        x = g_scr[pl.ds(base, 8), :]
        xr, xi = x[:, :S5_HALF], x[:, S5_HALF:]
        for lvl, s in enumerate((1, 2, 4)):
            ar, ai = pw[lvl:lvl + 1, :S5_HALF], pw[lvl:lvl + 1, S5_HALF:]
            keep = row >= s
            sr = jnp.where(keep, pltpu.roll(xr, s, 0), 0.0)
            si = jnp.where(keep, pltpu.roll(xi, s, 0), 0.0)
            xr, xi = xr + (ar * sr - ai * si), xi + (ar * si + ai * sr)
        xr, xi = xr + (pr * cr - pi * ci), xi + (pr * ci + pi * cr)
        first = row == 0
        hp_scr[pl.ds(base, 8), :] = jnp.concatenate(
            [jnp.where(first, cr, pltpu.roll(xr, 1, 0)), jnp.where(first, ci, pltpu.roll(xi, 1, 0))], axis=1)
        return xr[7:8, :], xi[7:8, :]

    z = jnp.zeros((1, S5_HALF), F32)
    lax.fori_loop(0, nj // 8, group, (z, z))

    y = (jnp.dot(ur, wk_ref[0], preferred_element_type=F32)
         + jnp.dot(hp_scr[...].astype(BF16), wc_ref[0], preferred_element_type=F32))
    for s in range(S5_L):
        y_scr[pl.ds(s, nj, stride=S5_L), :] = y[:, s * LANES:(s + 1) * LANES]
    o_ref[0] = _gelu_tanh(y_scr[...] + d_ref[...] * uf_scr[...]).astype(o_ref.dtype)


def _s5(proj3, wg, wk, wc, pw, dvec, col_u):
    b, t, _ = proj3.shape
    nblk = wg.shape[0]
    nj = t // S5_L
    wide = S5_L * LANES
    assert nj % 8 == 0

    def wspec(rows, cols):
        return pl.BlockSpec((1, rows, cols), lambda g, bi: (g, 0, 0))

    return pl.pallas_call(
        _s5_kernel,
        grid=(nblk, b),
        in_specs=[pl.BlockSpec((1, t, LANES), lambda g, bi: (bi, 0, col_u + g)),
                  wspec(wide, 2 * S5_HALF), wspec(wide, wide), wspec(2 * S5_HALF, wide),
                  wspec(16, 2 * S5_HALF),
                  pl.BlockSpec((1, LANES), lambda g, bi: (0, g))],
        out_specs=pl.BlockSpec((1, t, LANES), lambda g, bi: (bi, 0, g)),
        out_shape=jax.ShapeDtypeStruct((b, t, nblk * LANES), BF16),
        scratch_shapes=[pltpu.VMEM((t, LANES), F32), pltpu.VMEM((nj, 2 * S5_HALF), F32),
                        pltpu.VMEM((nj, 2 * S5_HALF), F32), pltpu.VMEM((t, LANES), F32)],
        compiler_params=_cparams(("parallel", "parallel")),
        name="s5",
    )(proj3, wg, wk, wc, pw, dvec)


def _merge_kernel(ya_ref, yb_ref, yc_ref, g0_ref, g1_ref, g2_ref, wg_ref, bg_ref, wb_ref, bgate_ref,
                  o_ref, yc_scr):
    @pl.when(pl.program_id(1) == 0)
    def _():
        yc = yc_ref[...]
        z = jnp.dot(yc, wg_ref[...], preferred_element_type=F32) + bg_ref[...]
        yc_scr[...] = (yc.astype(F32) * _sigmoid(z)).astype(BF16)

    bgate = bgate_ref[...]
    ys = (ya_ref[...], yb_ref[...], yc_scr[...])
    gs = (g0_ref, g1_ref, g2_ref)
    merged = None
    for br in range(N_BRANCH):
        gate = _sigmoid(gs[br][...].astype(F32) + bgate[br:br + 1, :])
        term = gate * jnp.dot(ys[br], wb_ref[br], preferred_element_type=F32)
        merged = term if merged is None else merged + term
    o_ref[...] = merged.astype(o_ref.dtype)


def _merge(ya, yb, yc, proj, w_glu, b_glu, w_branch, b_gate, col_g, tm, tn):
    n, wdt = ya.shape
    d = w_branch.shape[2]
    gcol = [(col_g + br * d) // tn for br in range(N_BRANCH)]
    yspec = pl.BlockSpec((tm, wdt), lambda i, j: (i, 0))

    def gspec(br):
        return pl.BlockSpec((tm, tn), lambda i, j: (i, gcol[br] + j))

    return pl.pallas_call(
        _merge_kernel,
        grid=(n // tm, d // tn),
        in_specs=[yspec, yspec, yspec, gspec(0), gspec(1), gspec(2),
                  pl.BlockSpec((wdt, wdt), lambda i, j: (0, 0)),
                  pl.BlockSpec((1, wdt), lambda i, j: (0, 0)),
                  pl.BlockSpec((N_BRANCH, wdt, tn), lambda i, j: (0, 0, j)),
                  pl.BlockSpec((N_BRANCH, tn), lambda i, j: (0, j))],
        out_specs=pl.BlockSpec((tm, tn), lambda i, j: (i, j)),
        out_shape=jax.ShapeDtypeStruct((n, d), BF16),
        scratch_shapes=[pltpu.VMEM((tm, wdt), BF16)],
        compiler_params=_cparams(("parallel", "arbitrary")),
        name="merge",
    )(ya, yb, yc, proj, proj, proj, w_glu, b_glu, w_branch, b_gate)


def _outproj_kernel(m_ref, w_ref, xs_ref, nw_ref, o_ref):
    mix = jnp.dot(m_ref[...], w_ref[...], preferred_element_type=F32)
    o_ref[...] = xs_ref[...] + _rms(mix, nw_ref[...], 1e-6)


def _outproj(merged, w_out, xs, nw, tm):
    n, d = xs.shape
    return pl.pallas_call(
        _outproj_kernel,
        grid=(n // tm,),
        in_specs=[pl.BlockSpec((tm, d), lambda i: (i, 0)),
                  pl.BlockSpec((d, d), lambda i: (0, 0)),
                  pl.BlockSpec((tm, d), lambda i: (i, 0)),
                  pl.BlockSpec((1, d), lambda i: (0, 0))],
        out_specs=pl.BlockSpec((tm, d), lambda i: (i, 0)),
        out_shape=jax.ShapeDtypeStruct((n, d), F32),
        compiler_params=_cparams(("parallel",)),
        name="outproj_residual",
    )(merged, w_out, xs, nw)


def _ffn_kernel(xs_ref, nw_in_ref, wg_ref, wu_ref, wo_ref, nw_out_ref, o_ref, h_scr, acc_scr):
    f = pl.program_id(1)

    @pl.when(f == 0)
    def _():
        h_scr[...] = _rms(xs_ref[...], nw_in_ref[...], 1e-6).astype(BF16)
        acc_scr[...] = jnp.zeros(acc_scr.shape, F32)

    h = h_scr[...]
    gate = jnp.dot(h, wg_ref[...], preferred_element_type=F32)
    up = jnp.dot(h, wu_ref[...], preferred_element_type=F32)
    act = (gate * _sigmoid(gate) * up).astype(BF16)
    acc_scr[...] += jnp.dot(act, wo_ref[...], preferred_element_type=F32)

    @pl.when(f == pl.num_programs(1) - 1)
    def _():
        o_ref[...] = xs_ref[...] + _rms(acc_scr[...], nw_out_ref[...], 1e-6)


def _ffn(xs, nw_in, w_ffn_in, w_ffn_out, nw_out, tm, tf):
    n, d = xs.shape
    d_ff = w_ffn_out.shape[0]
    nf = d_ff // tf
    return pl.pallas_call(
        _ffn_kernel,
        grid=(n // tm, nf),
        in_specs=[pl.BlockSpec((tm, d), lambda i, f: (i, 0)),
                  pl.BlockSpec((1, d), lambda i, f: (0, 0)),
                  pl.BlockSpec((d, tf), lambda i, f: (0, f)),
                  pl.BlockSpec((d, tf), lambda i, f: (0, nf + f)),
                  pl.BlockSpec((tf, d), lambda i, f: (f, 0)),
                  pl.BlockSpec((1, d), lambda i, f: (0, 0))],
        out_specs=pl.BlockSpec((tm, d), lambda i, f: (i, 0)),
        out_shape=jax.ShapeDtypeStruct((n, d), F32),
        scratch_shapes=[pltpu.VMEM((tm, d), BF16), pltpu.VMEM((tm, d), F32)],
        compiler_params=_cparams(("parallel", "arbitrary")),
        name="swiglu_ffn",
    )(xs, nw_in, w_ffn_in, w_ffn_in, w_ffn_out, nw_out)


def _s5_tables(lam_re, lam_im, b_re, b_im, c_re, c_im, log_step):
    hp = lax.Precision.HIGHEST
    groups = lam_re.shape[0]
    nblk = groups // S5_GB
    wide = S5_L * LANES
    lr, li = lam_re.astype(F32), lam_im.astype(F32)
    step = jnp.exp(log_step.astype(F32))[:, None]
    mag = jnp.exp(lr * step)
    ab_re, ab_im = mag * jnp.cos(li * step), mag * jnp.sin(li * step)
    den = lr * lr + li * li
    coef_re = ((ab_re - 1.0) * lr + ab_im * li) / den
    coef_im = (ab_im * lr - (ab_re - 1.0) * li) / den
    br, bi = b_re.astype(F32), b_im.astype(F32)
    bb_re = coef_re[..., None] * br - coef_im[..., None] * bi
    bb_im = coef_re[..., None] * bi + coef_im[..., None] * br
    cr, ci = c_re.astype(F32), c_im.astype(F32)
    eye = jnp.eye(S5_GB, dtype=F32)

    def lam_pow(n):
        nn = n.astype(F32)[:, None, None]
        m = jnp.exp(nn * (lr * step)[None])
        ang = nn * (li * step)[None]
        return m * jnp.cos(ang), m * jnp.sin(ang)

    taus = jnp.arange(S5_L)
    pr, pi = lam_pow(taus)

    qr, qi = pr[::-1, :, :, None], pi[::-1, :, :, None]
    g_re = qr * bb_re[None] - qi * bb_im[None]
    g_im = qr * bb_im[None] + qi * bb_re[None]

    def in_blockdiag(t):
        t = t.reshape(S5_L, nblk, S5_GB, S5_STATE, S5_GROUP)
        return jnp.einsum("sngpc,gh->nsgchp", t, eye).reshape(nblk, wide, S5_HALF)

    wg = jnp.concatenate([in_blockdiag(g_re), in_blockdiag(g_im)], axis=2).astype(BF16)

    p1r, p1i = lam_pow(1 + taus)
    p1r, p1i = p1r[:, :, None, :], p1i[:, :, None, :]
    cp_re = cr[None] * p1r - ci[None] * p1i
    cp_im = cr[None] * p1i + ci[None] * p1r

    def out_blockdiag(t):
        t = t.reshape(S5_L, nblk, S5_GB, S5_GROUP, S5_STATE)
        return jnp.einsum("rngcp,gh->ngprhc", t, eye).reshape(nblk, S5_HALF, wide)

    wc = jnp.concatenate([out_blockdiag(cp_re), out_blockdiag(-cp_im)], axis=1).astype(BF16)

    c0_re = cr[None] * pr[:, :, None, :] - ci[None] * pi[:, :, None, :]
    c0_im = cr[None] * pi[:, :, None, :] + ci[None] * pr[:, :, None, :]
    kt = (jnp.einsum("tgop,gpi->tgoi", c0_re, bb_re, precision=hp)
          - jnp.einsum("tgop,gpi->tgoi", c0_im, bb_im, precision=hp))
    lag = taus[None, :] - taus[:, None]
    ksr = jnp.where((lag >= 0)[:, :, None, None, None], kt[jnp.maximum(lag, 0)], 0.0)
    ksr = ksr.reshape(S5_L, S5_L, nblk, S5_GB, S5_GROUP, S5_GROUP)
    wk = jnp.einsum("srngoi,gh->nsgirho", ksr, eye).reshape(nblk, wide, wide).astype(BF16)

    n_list = S5_L * jnp.concatenate([jnp.array([1, 2, 4, 0, 0, 0, 0, 0]), 1 + jnp.arange(8)])
    wr, wi = lam_pow(n_list)
    pw = jnp.concatenate([wr.reshape(16, nblk, S5_HALF), wi.reshape(16, nblk, S5_HALF)],
                         axis=2).transpose(1, 0, 2)
    return wg, wk, wc, pw


def kernel(x, meta, rel_bias, norm_w, w_in, conv_w, conv_b, lru_w_a, lru_b_a, lru_w_x, lru_b_x, lru_lambda, da_lambda, da_subln, s5_lam_re, s5_lam_im, s5_b_re, s5_b_im, s5_c_re, s5_c_im, s5_d, s5_log_step, s5_w_glu, s5_b_glu, b_gate, w_branch, w_out, w_ffn_in, w_ffn_out):
    bsz, seq, d_model = x.shape
    depth = w_in.shape[0]
    lru_w = conv_w.shape[2]
    s5_w = s5_d.shape[1]
    qk_w = DA_HEADS * 2 * DA_HEAD_DIM
    assert lru_w == LRU_BLOCKS * LANES and s5_w % LANES == 0 and d_model % LANES == 0
    col_gate, col_x = 0, lru_w // LANES
    col_q = 2 * lru_w // LANES
    col_k = col_q + qk_w // LANES
    col_v = col_k + qk_w // LANES
    col_u = col_v + DA_HEADS
    col_g = (col_u + s5_w // LANES) * LANES

    t_real = N_META + seq
    tb = 384 if t_real >= 1024 else 128
    t_pad = -(-t_real // tb) * tb
    assert t_pad % SCAN_CH == 0
    n_tok = bsz * t_pad
    tm = _pick_tile(n_tok, (512, 384, 256, 128))
    tm_out = _pick_tile(n_tok, (256, 128))

    xs = jnp.concatenate([jnp.broadcast_to(meta.astype(F32)[None], (bsz, N_META, d_model)), x,
                          jnp.zeros((bsz, t_pad - t_real, d_model), F32)], axis=1)
    xs = xs.reshape(n_tok, d_model)
    btiles = _bias_tiles(rel_bias.astype(F32), tb)

    for l in range(depth):
        lam_init = 0.8 - 0.6 * math.exp(-0.3 * l)
        proj = _inproj(xs, norm_w[l, 0][None], w_in[l].astype(BF16), tm, 1024)
        proj3 = proj.reshape(bsz, t_pad, proj.shape[1])

        lru_p = jnp.concatenate([conv_w[l], conv_b[l][None], lru_b_a[l][None], lru_b_x[l][None],
                                 jax.nn.log_sigmoid(lru_lambda[l].astype(F32))[None]], axis=0)
        w_ax = jnp.concatenate([lru_w_a[l], lru_w_x[l]], axis=2).astype(BF16)
        y_a = _lru(proj3, lru_p, w_ax, col_gate, col_x)

        y_b = _attn(proj3, btiles, da_lambda[l], da_subln[l][None], lam_init, tb, col_q, col_k, col_v)

        wg, wk, wc, pw = _s5_tables(s5_lam_re[l], s5_lam_im[l], s5_b_re[l], s5_b_im[l],
                                    s5_c_re[l], s5_c_im[l], s5_log_step[l])
        y_c = _s5(proj3, wg, wk, wc, pw, s5_d[l][None], col_u)

        merged = _merge(y_a.reshape(n_tok, lru_w), y_b.reshape(n_tok, -1), y_c.reshape(n_tok, s5_w),
                        proj, s5_w_glu[l].astype(BF16), s5_b_glu[l][None], w_branch[l].astype(BF16),
                        b_gate[l], col_g, tm, 512)
        xs = _outproj(merged, w_out[l].astype(BF16), xs, norm_w[l, 1][None], tm_out)
        xs = _ffn(xs, norm_w[l, 2][None], w_ffn_in[l].astype(BF16), w_ffn_out[l].astype(BF16),
                  norm_w[l, 3][None], tm, 512)

    return xs.reshape(bsz, t_pad, d_model)[:, N_META:t_real]
```

```python
import functools
import math

import jax
import jax.numpy as jnp
from jax import lax
from jax.experimental import pallas as pl
from jax.experimental.pallas import tpu as pltpu

F32 = jnp.float32
BF16 = jnp.bfloat16

N_META = 16
CONV_W = 4
LRU_C = 8.0
LRU_BLOCKS = 8
DA_HEADS = 8
DA_HEAD_DIM = 64
S5_GROUP = 16
S5_STATE = 64
REL_BUCKETS = 32
REL_MAX_DIST = 128
N_BRANCH = 3

LANES = 128
S5_GB = LANES // S5_GROUP
S5_HALF = S5_GB * S5_STATE
S5_L = 8
SCAN_CH = 128
MASK_NEG = -1e30
LOG2E = math.log2(math.e)
VMEM_LIMIT = 56 * 1024 * 1024


def _cparams(sem):
    return pltpu.CompilerParams(dimension_semantics=sem, vmem_limit_bytes=VMEM_LIMIT)


def _pick_tile(n, candidates):
    for c in candidates:
        if n % c == 0:
            return c
    raise ValueError(f"no tile in {candidates} divides {n}")


def _gelu_tanh(x):
    return 0.5 * x * (1.0 + jnp.tanh(math.sqrt(2.0 / math.pi) * (x + 0.044715 * (x * x * x))))


def _sigmoid(x):
    return 1.0 / (1.0 + jnp.exp(-x))


def _rms(x, w, eps):
    return (x * lax.rsqrt(jnp.mean(x * x, axis=-1, keepdims=True) + eps)) * w


def _inproj_kernel(x_ref, nw_ref, w_ref, o_ref, h_scr):
    @pl.when(pl.program_id(1) == 0)
    def _():
        h_scr[...] = _rms(x_ref[...], nw_ref[...], 1e-6).astype(BF16)

    o_ref[...] = jnp.dot(h_scr[...], w_ref[...], preferred_element_type=F32).astype(o_ref.dtype)


def _inproj(xs, nw, w, tm, tn):
    n, d = xs.shape
    n_out = w.shape[1]
    return pl.pallas_call(
        _inproj_kernel,
        grid=(n // tm, n_out // tn),
        in_specs=[pl.BlockSpec((tm, d), lambda i, j: (i, 0)),
                  pl.BlockSpec((1, d), lambda i, j: (0, 0)),
                  pl.BlockSpec((d, tn), lambda i, j: (0, j))],
        out_specs=pl.BlockSpec((tm, tn), lambda i, j: (i, j)),
        out_shape=jax.ShapeDtypeStruct((n, n_out), BF16),
        scratch_shapes=[pltpu.VMEM((tm, d), BF16)],
        compiler_params=_cparams(("parallel", "arbitrary")),
        name="inproj",
    )(xs, nw, w)


def _lru_kernel(g_ref, x_ref, p_ref, w_ref, o_ref, xs_scr):
    t_len = x_ref.shape[1]
    halo = 8
    xs_scr[0:halo, :] = jnp.zeros((halo, LANES), F32)
    xs_scr[halo:, :] = x_ref[0].astype(F32)
    p = p_ref[...]
    cw = [p[j:j + 1, :] for j in range(CONV_W)]
    cb, ba, bx, logsig = p[4:5, :], p[5:6, :], p[6:7, :], p[7:8, :]
    w = w_ref[0]
    row = lax.broadcasted_iota(jnp.int32, (SCAN_CH, LANES), 0)

    def chunk(c, h0):
        base = pl.multiple_of(c * SCAN_CH, SCAN_CH)
        xc = cb
        for j in range(CONV_W):
            xc = xc + cw[j] * xs_scr[pl.ds(base + halo - (CONV_W - 1) + j, SCAN_CH), :]
        ri = jnp.dot(xc.astype(BF16), w, preferred_element_type=F32)
        r = _sigmoid(ri[:, :LANES] + ba)
        i = _sigmoid(ri[:, LANES:] + bx)
        log_a = LRU_C * r * logsig
        a = jnp.exp(log_a)
        b = jnp.sqrt(1.0 - jnp.exp(2.0 * log_a)) * (i * xc)
        s = 1
        while s < SCAN_CH:
            keep = row >= s
            a_sh = jnp.where(keep, pltpu.roll(a, s, 0), 1.0)
            b_sh = jnp.where(keep, pltpu.roll(b, s, 0), 0.0)
            b = a * b_sh + b
            a = a * a_sh
            s *= 2
        h = a * h0 + b
        g = g_ref[0, pl.ds(base, SCAN_CH), :].astype(F32)
        o_ref[0, pl.ds(base, SCAN_CH), :] = (h * _gelu_tanh(g)).astype(o_ref.dtype)
        return h[SCAN_CH - 1:SCAN_CH, :]

    lax.fori_loop(0, t_len // SCAN_CH, chunk, jnp.zeros((1, LANES), F32))


def _lru(proj3, lru_p, w_ax, col_gate, col_x):
    b, t, _ = proj3.shape
    width = LRU_BLOCKS * LANES
    return pl.pallas_call(
        _lru_kernel,
        grid=(b, LRU_BLOCKS),
        in_specs=[pl.BlockSpec((1, t, LANES), lambda bi, h: (bi, 0, col_gate + h)),
                  pl.BlockSpec((1, t, LANES), lambda bi, h: (bi, 0, col_x + h)),
                  pl.BlockSpec((8, LANES), lambda bi, h: (0, h)),
                  pl.BlockSpec((1, LANES, 2 * LANES), lambda bi, h: (h, 0, 0))],
        out_specs=pl.BlockSpec((1, t, LANES), lambda bi, h: (bi, 0, h)),
        out_shape=jax.ShapeDtypeStruct((b, t, width), BF16),
        scratch_shapes=[pltpu.VMEM((t + 8, LANES), F32)],
        compiler_params=_cparams(("parallel", "parallel")),
        name="rglru",
    )(proj3, proj3, lru_p, w_ax)


def _bias_tile_kernel(rb_ref, o_ref):
    h = pl.program_id(0)
    tb = o_ref.shape[1]
    i = lax.broadcasted_iota(jnp.int32, (tb, 2 * tb), 0)
    j2 = lax.broadcasted_iota(jnp.int32, (tb, 2 * tb), 1)
    max_exact = REL_BUCKETS // 2
    d = i - j2 + tb
    n = jnp.maximum(d, 0)
    nf = jnp.maximum(n, 1).astype(F32)
    large = max_exact + (jnp.log(nf / max_exact) / math.log(REL_MAX_DIST / max_exact)
                         * (REL_BUCKETS - max_exact)).astype(jnp.int32)
    large = jnp.minimum(large, REL_BUCKETS - 1)
    bucket = jnp.where(n < max_exact, n, large)
    val = jnp.zeros((tb, 2 * tb), F32)
    for bkt in range(REL_BUCKETS):
        val = jnp.where(bucket == bkt, rb_ref[bkt, h], val)
    val = val - rb_ref[REL_BUCKETS - 1, h]
    o_ref[0] = jnp.where(d >= 0, val, MASK_NEG)


def _bias_tiles(rel_bias, tb):
    return pl.pallas_call(
        _bias_tile_kernel,
        grid=(DA_HEADS,),
        in_specs=[pl.BlockSpec(memory_space=pltpu.SMEM)],
        out_specs=pl.BlockSpec((1, tb, 2 * tb), lambda h: (h, 0, 0)),
        out_shape=jax.ShapeDtypeStruct((DA_HEADS, tb, 2 * tb), F32),
        compiler_params=_cparams(("parallel",)),
        name="t5_bias_tiles",
    )(rel_bias)


def _attn_kernel(q_ref, k_ref, v_ref, bt_ref, dl_ref, sw_ref, o_ref, m_scr, l_scr, acc_scr, s_scr, *,
                 lam_init):
    qi = pl.program_id(2)
    tb = q_ref.shape[1]
    dn_t = (((1,), (1,)), ((), ()))
    lane = lax.broadcasted_iota(jnp.int32, (tb, LANES), 1)
    qs = q_ref[0] * (DA_HEAD_DIM ** -0.5)
    zero = jnp.zeros_like(qs)
    qmaps = (jnp.where(lane < DA_HEAD_DIM, qs, zero), jnp.where(lane >= DA_HEAD_DIM, qs, zero))
    n_far = jnp.maximum(qi - 1, 0)
    has_sub = qi >= 1
    has_odd = n_far % 2 == 1
    n_pairs = n_far // 2

    def scores(c, blk, nblocks, bias):
        kb = k_ref[0, pl.ds(pl.multiple_of(blk * tb, tb), nblocks * tb), :]
        s = lax.dot_general(qmaps[c], kb, dn_t, preferred_element_type=F32)
        if bias is not None:
            s = s + bias
        s = s * LOG2E
        for i in range(nblocks):
            s_scr[c, blk + i] = s[:, i * tb:(i + 1) * tb]
        return s

    def lane_chunks(s):
        return [s[:, j * LANES:(j + 1) * LANES] for j in range(s.shape[1] // LANES)]

    def max_into(m, s):
        for ch in lane_chunks(s):
            m = jnp.maximum(m, ch)
        return m

    m_init = jnp.full((tb, LANES), MASK_NEG, F32)

    @pl.when(jnp.logical_not(has_sub))
    def _():
        for c in range(2):
            m_scr[c] = max_into(m_init, scores(c, qi, 1, bt_ref[0, :, tb:]))

    @pl.when(has_sub)
    def _():
        for c in range(2):
            m_scr[c] = max_into(m_init, scores(c, n_far, 2, bt_ref[0]))

    @pl.when(has_odd)
    def _():
        for c in range(2):
            m_scr[c] = max_into(m_scr[c], scores(c, n_far - 1, 1, None))

    def pair_max(ki, ms):
        return tuple(max_into(ms[c], scores(c, 2 * ki, 2, None)) for c in range(2))

    ms = lax.fori_loop(0, n_pairs, pair_max, (m_scr[0], m_scr[1]))
    mb = [jnp.broadcast_to(jnp.max(ms[c], axis=-1, keepdims=True), (tb, LANES)) for c in range(2)]

    def accumulate(c, blk, nblocks, l, acc):
        ps = []
        for i in range(nblocks):
            ps += [jnp.exp2(ch - mb[c]) for ch in lane_chunks(s_scr[c, blk + i])]
        for ch in ps:
            l = l + ch
        p = jnp.concatenate(ps, axis=1).astype(BF16)
        vb = v_ref[0, pl.ds(pl.multiple_of(blk * tb, tb), nblocks * tb), :]
        return l, acc + jnp.dot(p, vb, preferred_element_type=F32)

    zl = jnp.zeros((tb, LANES), F32)

    @pl.when(jnp.logical_not(has_sub))
    def _():
        for c in range(2):
            l_scr[c], acc_scr[c] = accumulate(c, qi, 1, zl, zl)

    @pl.when(has_sub)
    def _():
        for c in range(2):
            l_scr[c], acc_scr[c] = accumulate(c, n_far, 2, zl, zl)

    @pl.when(has_odd)
    def _():
        for c in range(2):
            l_scr[c], acc_scr[c] = accumulate(c, n_far - 1, 1, l_scr[c], acc_scr[c])

    def pair_acc(ki, st):
        l0, a0 = accumulate(0, 2 * ki, 2, st[0], st[1])
        l1, a1 = accumulate(1, 2 * ki, 2, st[2], st[3])
        return l0, a0, l1, a1

    l0, a0, l1, a1 = lax.fori_loop(0, n_pairs, pair_acc, (l_scr[0], acc_scr[0], l_scr[1], acc_scr[1]))

    dl = dl_ref[...]
    lam = (jnp.exp(jnp.sum(dl[0:1, :] * dl[1:2, :], axis=-1, keepdims=True))
           - jnp.exp(jnp.sum(dl[2:3, :] * dl[3:4, :], axis=-1, keepdims=True)) + lam_init)
    o = (a0 / jnp.sum(l0, axis=-1, keepdims=True)
         - lam * (a1 / jnp.sum(l1, axis=-1, keepdims=True)))
    o = _rms(o, sw_ref[...], 1e-5) * (1.0 - lam_init)
    o_ref[0] = o.astype(o_ref.dtype)


def _attn(proj3, btiles, da_lambda, da_subln, lam_init, tb, col_q, col_k, col_v):
    b, t, _ = proj3.shape
    width = DA_HEADS * LANES
    return pl.pallas_call(
        functools.partial(_attn_kernel, lam_init=lam_init),
        grid=(b, DA_HEADS, t // tb),
        in_specs=[pl.BlockSpec((1, tb, LANES), lambda bi, h, qi: (bi, qi, col_q + h)),
                  pl.BlockSpec((1, t, LANES), lambda bi, h, qi: (bi, 0, col_k + h)),
                  pl.BlockSpec((1, t, LANES), lambda bi, h, qi: (bi, 0, col_v + h)),
                  pl.BlockSpec((1, tb, 2 * tb), lambda bi, h, qi: (h, 0, 0)),
                  pl.BlockSpec((4, DA_HEAD_DIM), lambda bi, h, qi: (0, 0)),
                  pl.BlockSpec((1, LANES), lambda bi, h, qi: (0, 0))],
        out_specs=pl.BlockSpec((1, tb, LANES), lambda bi, h, qi: (bi, qi, h)),
        out_shape=jax.ShapeDtypeStruct((b, t, width), BF16),
        scratch_shapes=[pltpu.VMEM((2, tb, LANES), F32), pltpu.VMEM((2, tb, LANES), F32),
                        pltpu.VMEM((2, tb, LANES), F32), pltpu.VMEM((2, t // tb, tb, tb), F32)],
        compiler_params=_cparams(("parallel", "parallel", "arbitrary")),
        name="diff_attn",
    )(proj3, proj3, proj3, btiles, da_lambda, da_subln)


def _s5_kernel(u_ref, wg_ref, wk_ref, wc_ref, pw_ref, d_ref, o_ref, uf_scr, g_scr, hp_scr, y_scr):
    t_len = u_ref.shape[1]
    nj = t_len // S5_L
    uf_scr[...] = u_ref[0].astype(F32)
    ur = jnp.concatenate([uf_scr[pl.ds(s, nj, stride=S5_L), :].astype(BF16) for s in range(S5_L)],
                         axis=1)
    g_scr[...] = jnp.dot(ur, wg_ref[0], preferred_element_type=F32)

    pw = pw_ref[0]
    pr, pi = pw[8:16, :S5_HALF], pw[8:16, S5_HALF:]
    row = lax.broadcasted_iota(jnp.int32, (8, S5_HALF), 0)

    def group(gi, carry):
        cr, ci = carry
        base = pl.multiple_of(gi * 8, 8)
        x = g_scr[pl.ds(base, 8), :]
        xr, xi = x[:, :S5_HALF], x[:, S5_HALF:]
        for lvl, s in enumerate((1, 2, 4)):
            ar, ai = pw[lvl:lvl + 1, :S5_HALF], pw[lvl:lvl + 1, S5_HALF:]
            keep = row >= s
            sr = jnp.where(keep, pltpu.roll(xr, s, 0), 0.0)
            si = jnp.where(keep, pltpu.roll(xi, s, 0), 0.0)
            xr, xi = xr + (ar * sr - ai * si), xi + (ar * si + ai * sr)
        xr, xi = xr + (pr * cr - pi * ci), xi + (pr * ci + pi * cr)
        first = row == 0
        hp_scr[pl.ds(base, 8), :] = jnp.concatenate(
            [jnp.where(first, cr, pltpu.roll(xr, 1, 0)), jnp.where(first, ci, pltpu.roll(xi, 1, 0))], axis=1)
        return xr[7:8, :], xi[7:8, :]

    z = jnp.zeros((1, S5_HALF), F32)
    lax.fori_loop(0, nj // 8, group, (z, z))

    y = (jnp.dot(ur, wk_ref[0], preferred_element_type=F32)
         + jnp.dot(hp_scr[...].astype(BF16), wc_ref[0], preferred_element_type=F32))
    for s in range(S5_L):
        y_scr[pl.ds(s, nj, stride=S5_L), :] = y[:, s * LANES:(s + 1) * LANES]
    o_ref[0] = _gelu_tanh(y_scr[...] + d_ref[...] * uf_scr[...]).astype(o_ref.dtype)


def _s5(proj3, wg, wk, wc, pw, dvec, col_u):
    b, t, _ = proj3.shape
    nblk = wg.shape[0]
    nj = t // S5_L
    wide = S5_L * LANES
    assert nj % 8 == 0

    def wspec(rows, cols):
        return pl.BlockSpec((1, rows, cols), lambda g, bi: (g, 0, 0))

    return pl.pallas_call(
        _s5_kernel,
        grid=(nblk, b),
        in_specs=[pl.BlockSpec((1, t, LANES), lambda g, bi: (bi, 0, col_u + g)),
                  wspec(wide, 2 * S5_HALF), wspec(wide, wide), wspec(2 * S5_HALF, wide),
                  wspec(16, 2 * S5_HALF),
                  pl.BlockSpec((1, LANES), lambda g, bi: (0, g))],
        out_specs=pl.BlockSpec((1, t, LANES), lambda g, bi: (bi, 0, g)),
        out_shape=jax.ShapeDtypeStruct((b, t, nblk * LANES), BF16),
        scratch_shapes=[pltpu.VMEM((t, LANES), F32), pltpu.VMEM((nj, 2 * S5_HALF), F32),
                        pltpu.VMEM((nj, 2 * S5_HALF), F32), pltpu.VMEM((t, LANES), F32)],
        compiler_params=_cparams(("parallel", "parallel")),
        name="s5",
    )(proj3, wg, wk, wc, pw, dvec)


def _merge_kernel(ya_ref, yb_ref, yc_ref, g0_ref, g1_ref, g2_ref, wg_ref, bg_ref, wb_ref, bgate_ref,
                  o_ref, yc_scr):
    @pl.when(pl.program_id(1) == 0)
    def _():
        yc = yc_ref[...]
        z = jnp.dot(yc, wg_ref[...], preferred_element_type=F32) + bg_ref[...]
        yc_scr[...] = (yc.astype(F32) * _sigmoid(z)).astype(BF16)

    bgate = bgate_ref[...]
    ys = (ya_ref[...], yb_ref[...], yc_scr[...])
    gs = (g0_ref, g1_ref, g2_ref)
    merged = None
    for br in range(N_BRANCH):
        gate = _sigmoid(gs[br][...].astype(F32) + bgate[br:br + 1, :])
        term = gate * jnp.dot(ys[br], wb_ref[br], preferred_element_type=F32)
        merged = term if merged is None else merged + term
    o_ref[...] = merged.astype(o_ref.dtype)


def _merge(ya, yb, yc, proj, w_glu, b_glu, w_branch, b_gate, col_g, tm, tn):
    n, wdt = ya.shape
    d = w_branch.shape[2]
    gcol = [(col_g + br * d) // tn for br in range(N_BRANCH)]
    yspec = pl.BlockSpec((tm, wdt), lambda i, j: (i, 0))

    def gspec(br):
        return pl.BlockSpec((tm, tn), lambda i, j: (i, gcol[br] + j))

    return pl.pallas_call(
        _merge_kernel,
        grid=(n // tm, d // tn),
        in_specs=[yspec, yspec, yspec, gspec(0), gspec(1), gspec(2),
                  pl.BlockSpec((wdt, wdt), lambda i, j: (0, 0)),
                  pl.BlockSpec((1, wdt), lambda i, j: (0, 0)),
                  pl.BlockSpec((N_BRANCH, wdt, tn), lambda i, j: (0, 0, j)),
                  pl.BlockSpec((N_BRANCH, tn), lambda i, j: (0, j))],
        out_specs=pl.BlockSpec((tm, tn), lambda i, j: (i, j)),
        out_shape=jax.ShapeDtypeStruct((n, d), BF16),
        scratch_shapes=[pltpu.VMEM((tm, wdt), BF16)],
        compiler_params=_cparams(("parallel", "arbitrary")),
        name="merge",
    )(ya, yb, yc, proj, proj, proj, w_glu, b_glu, w_branch, b_gate)


def _outproj_kernel(m_ref, w_ref, xs_ref, nw_ref, o_ref):
    mix = jnp.dot(m_ref[...], w_ref[...], preferred_element_type=F32)
    o_ref[...] = xs_ref[...] + _rms(mix, nw_ref[...], 1e-6)


def _outproj(merged, w_out, xs, nw, tm):
    n, d = xs.shape
    return pl.pallas_call(
        _outproj_kernel,
        grid=(n // tm,),
        in_specs=[pl.BlockSpec((tm, d), lambda i: (i, 0)),
                  pl.BlockSpec((d, d), lambda i: (0, 0)),
                  pl.BlockSpec((tm, d), lambda i: (i, 0)),
                  pl.BlockSpec((1, d), lambda i: (0, 0))],
        out_specs=pl.BlockSpec((tm, d), lambda i: (i, 0)),
        out_shape=jax.ShapeDtypeStruct((n, d), F32),
        compiler_params=_cparams(("parallel",)),
        name="outproj_residual",
    )(merged, w_out, xs, nw)


def _ffn_kernel(xs_ref, nw_in_ref, wg_ref, wu_ref, wo_ref, nw_out_ref, o_ref, h_scr, acc_scr):
    f = pl.program_id(1)

    @pl.when(f == 0)
    def _():
        h_scr[...] = _rms(xs_ref[...], nw_in_ref[...], 1e-6).astype(BF16)
        acc_scr[...] = jnp.zeros(acc_scr.shape, F32)

    h = h_scr[...]
    gate = jnp.dot(h, wg_ref[...], preferred_element_type=F32)
    up = jnp.dot(h, wu_ref[...], preferred_element_type=F32)
    act = (gate * _sigmoid(gate) * up).astype(BF16)
    acc_scr[...] += jnp.dot(act, wo_ref[...], preferred_element_type=F32)

    @pl.when(f == pl.num_programs(1) - 1)
    def _():
        o_ref[...] = xs_ref[...] + _rms(acc_scr[...], nw_out_ref[...], 1e-6)


def _ffn(xs, nw_in, w_ffn_in, w_ffn_out, nw_out, tm, tf):
    n, d = xs.shape
    d_ff = w_ffn_out.shape[0]
    nf = d_ff // tf
    return pl.pallas_call(
        _ffn_kernel,
        grid=(n // tm, nf),
        in_specs=[pl.BlockSpec((tm, d), lambda i, f: (i, 0)),
                  pl.BlockSpec((1, d), lambda i, f: (0, 0)),
                  pl.BlockSpec((d, tf), lambda i, f: (0, f)),
                  pl.BlockSpec((d, tf), lambda i, f: (0, nf + f)),
                  pl.BlockSpec((tf, d), lambda i, f: (f, 0)),
                  pl.BlockSpec((1, d), lambda i, f: (0, 0))],
        out_specs=pl.BlockSpec((tm, d), lambda i, f: (i, 0)),
        out_shape=jax.ShapeDtypeStruct((n, d), F32),
        scratch_shapes=[pltpu.VMEM((tm, d), BF16), pltpu.VMEM((tm, d), F32)],
        compiler_params=_cparams(("parallel", "arbitrary")),
        name="swiglu_ffn",
    )(xs, nw_in, w_ffn_in, w_ffn_in, w_ffn_out, nw_out)


def _s5_tables(lam_re, lam_im, b_re, b_im, c_re, c_im, log_step):
    hp = lax.Precision.HIGHEST
    groups = lam_re.shape[0]
    nblk = groups // S5_GB
    wide = S5_L * LANES
    lr, li = lam_re.astype(F32), lam_im.astype(F32)
    step = jnp.exp(log_step.astype(F32))[:, None]
    mag = jnp.exp(lr * step)
    ab_re, ab_im = mag * jnp.cos(li * step), mag * jnp.sin(li * step)
    den = lr * lr + li * li
    coef_re = ((ab_re - 1.0) * lr + ab_im * li) / den
    coef_im = (ab_im * lr - (ab_re - 1.0) * li) / den
    br, bi = b_re.astype(F32), b_im.astype(F32)
    bb_re = coef_re[..., None] * br - coef_im[..., None] * bi
    bb_im = coef_re[..., None] * bi + coef_im[..., None] * br
    cr, ci = c_re.astype(F32), c_im.astype(F32)
    eye = jnp.eye(S5_GB, dtype=F32)
    taus = jnp.arange(S5_L)
    lrs = (lr * step).reshape(nblk, 1, S5_HALF)
    lis = (li * step).reshape(nblk, 1, S5_HALF)

    def lam_pow(n):
        nn = n.astype(F32)[None, :, None]
        m = jnp.exp(nn * lrs)
        return m * jnp.cos(nn * lis), m * jnp.sin(nn * lis)

    def in_blockdiag(t):
        t = t.reshape(nblk, S5_GB, S5_STATE, S5_GROUP)
        return jnp.einsum("ngpc,gh->ngchp", t, eye).reshape(nblk, LANES, S5_HALF)

    def out_blockdiag(t):
        t = t.reshape(nblk, S5_GB, S5_GROUP, S5_STATE)
        return jnp.einsum("ngcp,gh->ngphc", t, eye).reshape(nblk, S5_HALF, LANES)

    bd_re, bd_im = in_blockdiag(bb_re)[:, None], in_blockdiag(bb_im)[:, None]
    ct_re, ct_im = out_blockdiag(cr)[:, :, None, :], out_blockdiag(ci)[:, :, None, :]

    qr, qi = lam_pow(S5_L - 1 - taus)
    qr, qi = qr[:, :, None, :], qi[:, :, None, :]
    wg = jnp.concatenate([qr * bd_re - qi * bd_im, qr * bd_im + qi * bd_re], axis=3)
    wg = wg.reshape(nblk, wide, 2 * S5_HALF).astype(BF16)

    p1r, p1i = lam_pow(1 + taus)
    p1r, p1i = p1r.transpose(0, 2, 1)[..., None], p1i.transpose(0, 2, 1)[..., None]
    wc = jnp.concatenate([ct_re * p1r - ct_im * p1i, -(ct_re * p1i + ct_im * p1r)], axis=1)
    wc = wc.reshape(nblk, 2 * S5_HALF, wide).astype(BF16)

    nn = taus.astype(F32)[:, None, None]
    mg = jnp.exp(nn * (lr * step)[None])
    pr, pi = mg * jnp.cos(nn * (li * step)[None]), mg * jnp.sin(nn * (li * step)[None])
    c0_re = cr[None] * pr[:, :, None, :] - ci[None] * pi[:, :, None, :]
    c0_im = cr[None] * pi[:, :, None, :] + ci[None] * pr[:, :, None, :]
    kt = (jnp.einsum("tgop,gpi->tgio", c0_re, bb_re, precision=hp)
          - jnp.einsum("tgop,gpi->tgio", c0_im, bb_im, precision=hp))
    kt = kt.reshape(S5_L, nblk, S5_GB, S5_GROUP, S5_GROUP)
    kblk = jnp.einsum("tngio,gh->ntgiho", kt, eye).reshape(nblk, S5_L, LANES, LANES)
    lag = taus[None, :] - taus[:, None]
    ksr = jnp.take(kblk, jnp.maximum(lag, 0).reshape(-1), axis=1).reshape(nblk, S5_L, S5_L, LANES, LANES)
    ksr = jnp.where((lag >= 0)[None, :, :, None, None], ksr, 0.0)
    wk = ksr.transpose(0, 1, 3, 2, 4).reshape(nblk, wide, wide).astype(BF16)

    n_list = S5_L * jnp.concatenate([jnp.array([1, 2, 4, 0, 0, 0, 0, 0]), 1 + jnp.arange(8)])
    wr, wi = lam_pow(n_list)
    pw = jnp.concatenate([wr, wi], axis=2)
    return wg, wk, wc, pw


def kernel(x, meta, rel_bias, norm_w, w_in, conv_w, conv_b, lru_w_a, lru_b_a, lru_w_x, lru_b_x, lru_lambda, da_lambda, da_subln, s5_lam_re, s5_lam_im, s5_b_re, s5_b_im, s5_c_re, s5_c_im, s5_d, s5_log_step, s5_w_glu, s5_b_glu, b_gate, w_branch, w_out, w_ffn_in, w_ffn_out):
    bsz, seq, d_model = x.shape
    depth = w_in.shape[0]
    lru_w = conv_w.shape[2]
    s5_w = s5_d.shape[1]
    qk_w = DA_HEADS * 2 * DA_HEAD_DIM
    assert lru_w == LRU_BLOCKS * LANES and s5_w % LANES == 0 and d_model % LANES == 0
    col_gate, col_x = 0, lru_w // LANES
    col_q = 2 * lru_w // LANES
    col_k = col_q + qk_w // LANES
    col_v = col_k + qk_w // LANES
    col_u = col_v + DA_HEADS
    col_g = (col_u + s5_w // LANES) * LANES

    t_real = N_META + seq
    tb = 384 if t_real >= 1024 else 128
    t_pad = -(-t_real // tb) * tb
    assert t_pad % SCAN_CH == 0
    n_tok = bsz * t_pad
    tm_in = _pick_tile(n_tok, (1408, 1056, 768, 512, 384, 256, 128))
    tm_merge = _pick_tile(n_tok, (1056, 768, 512, 384, 256, 128))
    tm_ffn = _pick_tile(n_tok, (768, 512, 384, 256, 128))
    tm_out = _pick_tile(n_tok, (256, 128))

    xs = jnp.concatenate([jnp.broadcast_to(meta.astype(F32)[None], (bsz, N_META, d_model)), x,
                          jnp.zeros((bsz, t_pad - t_real, d_model), F32)], axis=1)
    xs = xs.reshape(n_tok, d_model)
    btiles = _bias_tiles(rel_bias.astype(F32), tb)

    for l in range(depth):
        lam_init = 0.8 - 0.6 * math.exp(-0.3 * l)
        proj = _inproj(xs, norm_w[l, 0][None], w_in[l].astype(BF16), tm_in, 1024)
        proj3 = proj.reshape(bsz, t_pad, proj.shape[1])

        lru_p = jnp.concatenate([conv_w[l], conv_b[l][None], lru_b_a[l][None], lru_b_x[l][None],
                                 jax.nn.log_sigmoid(lru_lambda[l].astype(F32))[None]], axis=0)
        w_ax = jnp.concatenate([lru_w_a[l], lru_w_x[l]], axis=2).astype(BF16)
        y_a = _lru(proj3, lru_p, w_ax, col_gate, col_x)

        y_b = _attn(proj3, btiles, da_lambda[l], da_subln[l][None], lam_init, tb, col_q, col_k, col_v)

        wg, wk, wc, pw = _s5_tables(s5_lam_re[l], s5_lam_im[l], s5_b_re[l], s5_b_im[l],
                                    s5_c_re[l], s5_c_im[l], s5_log_step[l])
        y_c = _s5(proj3, wg, wk, wc, pw, s5_d[l][None], col_u)

        merged = _merge(y_a.reshape(n_tok, lru_w), y_b.reshape(n_tok, -1), y_c.reshape(n_tok, s5_w),
                        proj, s5_w_glu[l].astype(BF16), s5_b_glu[l][None], w_branch[l].astype(BF16),
                        b_gate[l], col_g, tm_merge, 512)
        xs = _outproj(merged, w_out[l].astype(BF16), xs, norm_w[l, 1][None], tm_out)
        xs = _ffn(xs, norm_w[l, 2][None], w_ffn_in[l].astype(BF16), w_ffn_out[l].astype(BF16),
                  norm_w[l, 3][None], tm_ffn, 512)

    return xs.reshape(bsz, t_pad, d_model)[:, N_META:t_real]
```

```python
import functools
import math

import jax
import jax.numpy as jnp
from jax import lax
from jax.experimental import pallas as pl
from jax.experimental.pallas import tpu as pltpu

F32 = jnp.float32
BF16 = jnp.bfloat16

N_META = 16
CONV_W = 4
LRU_C = 8.0
LRU_BLOCKS = 8
DA_HEADS = 8
DA_HEAD_DIM = 64
S5_GROUP = 16
S5_STATE = 64
REL_BUCKETS = 32
REL_MAX_DIST = 128
N_BRANCH = 3

LANES = 128
S5_GB = LANES // S5_GROUP
S5_HALF = S5_GB * S5_STATE
S5_L = 8
SCAN_CH = 128
MASK_NEG = -1e30
LOG2E = math.log2(math.e)
VMEM_LIMIT = 56 * 1024 * 1024


def _cparams(sem):
    return pltpu.CompilerParams(dimension_semantics=sem, vmem_limit_bytes=VMEM_LIMIT)


def _pick_tile(n, candidates):
    for c in candidates:
        if n % c == 0:
            return c
    raise ValueError(f"no tile in {candidates} divides {n}")


def _gelu_tanh(x):
    return 0.5 * x * (1.0 + jnp.tanh(math.sqrt(2.0 / math.pi) * (x + 0.044715 * (x * x * x))))


def _sigmoid(x):
    return 1.0 / (1.0 + jnp.exp(-x))


def _rms(x, w, eps):
    return (x * lax.rsqrt(jnp.mean(x * x, axis=-1, keepdims=True) + eps)) * w


def _inproj_kernel(x_ref, nw_ref, w_ref, o_ref, h_scr):
    @pl.when(pl.program_id(1) == 0)
    def _():
        h_scr[...] = _rms(x_ref[...], nw_ref[...], 1e-6).astype(BF16)

    o_ref[...] = jnp.dot(h_scr[...], w_ref[0], preferred_element_type=F32).astype(o_ref.dtype)


def _col_tiles(w, tn):
    k, n = w.shape
    return w.reshape(k, n // tn, tn).transpose(1, 0, 2)


def _inproj(xs, nw, w_tiles, tm):
    n, d = xs.shape
    nt, _, tn = w_tiles.shape
    n_out = nt * tn
    return pl.pallas_call(
        _inproj_kernel,
        grid=(n // tm, nt),
        in_specs=[pl.BlockSpec((tm, d), lambda i, j: (i, 0)),
                  pl.BlockSpec((1, d), lambda i, j: (0, 0)),
                  pl.BlockSpec((1, d, tn), lambda i, j: (j, 0, 0))],
        out_specs=pl.BlockSpec((tm, tn), lambda i, j: (i, j)),
        out_shape=jax.ShapeDtypeStruct((n, n_out), BF16),
        scratch_shapes=[pltpu.VMEM((tm, d), BF16)],
        compiler_params=_cparams(("parallel", "arbitrary")),
        name="inproj",
    )(xs, nw, w_tiles)


def _lru_kernel(g_ref, x_ref, p_ref, w_ref, o_ref, xs_scr):
    t_len = x_ref.shape[1]
    halo = 8
    xs_scr[0:halo, :] = jnp.zeros((halo, LANES), F32)
    xs_scr[halo:, :] = x_ref[0].astype(F32)
    p = p_ref[...]
    cw = [p[j:j + 1, :] for j in range(CONV_W)]
    cb, ba, bx, logsig = p[4:5, :], p[5:6, :], p[6:7, :], p[7:8, :]
    w = w_ref[0]
    row = lax.broadcasted_iota(jnp.int32, (SCAN_CH, LANES), 0)

    def chunk(c, h0):
        base = pl.multiple_of(c * SCAN_CH, SCAN_CH)
        xc = cb
        for j in range(CONV_W):
            xc = xc + cw[j] * xs_scr[pl.ds(base + halo - (CONV_W - 1) + j, SCAN_CH), :]
        ri = jnp.dot(xc.astype(BF16), w, preferred_element_type=F32)
        r = _sigmoid(ri[:, :LANES] + ba)
        i = _sigmoid(ri[:, LANES:] + bx)
        log_a = LRU_C * r * logsig
        a = jnp.exp(log_a)
        b = jnp.sqrt(1.0 - jnp.exp(2.0 * log_a)) * (i * xc)
        s = 1
        while s < SCAN_CH:
            keep = row >= s
            a_sh = jnp.where(keep, pltpu.roll(a, s, 0), 1.0)
            b_sh = jnp.where(keep, pltpu.roll(b, s, 0), 0.0)
            b = a * b_sh + b
            a = a * a_sh
            s *= 2
        h = a * h0 + b
        g = g_ref[0, pl.ds(base, SCAN_CH), :].astype(F32)
        o_ref[0, pl.ds(base, SCAN_CH), :] = (h * _gelu_tanh(g)).astype(o_ref.dtype)
        return h[SCAN_CH - 1:SCAN_CH, :]

    lax.fori_loop(0, t_len // SCAN_CH, chunk, jnp.zeros((1, LANES), F32))


def _lru(proj3, lru_p, w_ax, col_gate, col_x):
    b, t, _ = proj3.shape
    width = LRU_BLOCKS * LANES
    return pl.pallas_call(
        _lru_kernel,
        grid=(b, LRU_BLOCKS),
        in_specs=[pl.BlockSpec((1, t, LANES), lambda bi, h: (bi, 0, col_gate + h)),
                  pl.BlockSpec((1, t, LANES), lambda bi, h: (bi, 0, col_x + h)),
                  pl.BlockSpec((8, LANES), lambda bi, h: (0, h)),
                  pl.BlockSpec((1, LANES, 2 * LANES), lambda bi, h: (h, 0, 0))],
        out_specs=pl.BlockSpec((1, t, LANES), lambda bi, h: (bi, 0, h)),
        out_shape=jax.ShapeDtypeStruct((b, t, width), BF16),
        scratch_shapes=[pltpu.VMEM((t + 8, LANES), F32)],
        compiler_params=_cparams(("parallel", "parallel")),
        name="rglru",
    )(proj3, proj3, lru_p, w_ax)


def _bias_tile_kernel(rb_ref, o_ref):
    h = pl.program_id(0)
    tb = o_ref.shape[1]
    i = lax.broadcasted_iota(jnp.int32, (tb, 2 * tb), 0)
    j2 = lax.broadcasted_iota(jnp.int32, (tb, 2 * tb), 1)
    max_exact = REL_BUCKETS // 2
    d = i - j2 + tb
    n = jnp.maximum(d, 0)
    nf = jnp.maximum(n, 1).astype(F32)
    large = max_exact + (jnp.log(nf / max_exact) / math.log(REL_MAX_DIST / max_exact)
                         * (REL_BUCKETS - max_exact)).astype(jnp.int32)
    large = jnp.minimum(large, REL_BUCKETS - 1)
    bucket = jnp.where(n < max_exact, n, large)
    val = jnp.zeros((tb, 2 * tb), F32)
    for bkt in range(REL_BUCKETS):
        val = jnp.where(bucket == bkt, rb_ref[bkt, h], val)
    val = val - rb_ref[REL_BUCKETS - 1, h]
    o_ref[0] = jnp.where(d >= 0, val, MASK_NEG)


def _bias_tiles(rel_bias, tb):
    return pl.pallas_call(
        _bias_tile_kernel,
        grid=(DA_HEADS,),
        in_specs=[pl.BlockSpec(memory_space=pltpu.SMEM)],
        out_specs=pl.BlockSpec((1, tb, 2 * tb), lambda h: (h, 0, 0)),
        out_shape=jax.ShapeDtypeStruct((DA_HEADS, tb, 2 * tb), F32),
        compiler_params=_cparams(("parallel",)),
        name="t5_bias_tiles",
    )(rel_bias)


def _attn_kernel(q_ref, k_ref, v_ref, bt_ref, dl_ref, sw_ref, o_ref, m_scr, l_scr, acc_scr, s_scr, *,
                 lam_init):
    qi = pl.program_id(2)
    tb = q_ref.shape[1]
    dn_t = (((1,), (1,)), ((), ()))
    lane = lax.broadcasted_iota(jnp.int32, (tb, LANES), 1)
    qs = q_ref[0] * (DA_HEAD_DIM ** -0.5)
    zero = jnp.zeros_like(qs)
    qmaps = (jnp.where(lane < DA_HEAD_DIM, qs, zero), jnp.where(lane >= DA_HEAD_DIM, qs, zero))
    n_far = jnp.maximum(qi - 1, 0)
    has_sub = qi >= 1
    n_quads = n_far // 4
    has_pair = (n_far // 2) % 2 == 1
    has_odd = n_far % 2 == 1

    def scores(c, blk, nblocks, bias):
        kb = k_ref[0, pl.ds(pl.multiple_of(blk * tb, tb), nblocks * tb), :]
        s = lax.dot_general(qmaps[c], kb, dn_t, preferred_element_type=F32)
        if bias is not None:
            s = s + bias
        s = s * LOG2E
        for i in range(nblocks):
            s_scr[c, blk + i] = s[:, i * tb:(i + 1) * tb]
        return s

    def lane_chunks(s):
        return [s[:, j * LANES:(j + 1) * LANES] for j in range(s.shape[1] // LANES)]

    def max_into(m, s):
        for ch in lane_chunks(s):
            m = jnp.maximum(m, ch)
        return m

    m_init = jnp.full((tb, LANES), MASK_NEG, F32)

    @pl.when(jnp.logical_not(has_sub))
    def _():
        for c in range(2):
            m_scr[c] = max_into(m_init, scores(c, qi, 1, bt_ref[0, :, tb:]))

    @pl.when(has_sub)
    def _():
        for c in range(2):
            m_scr[c] = max_into(m_init, scores(c, n_far, 2, bt_ref[0]))

    @pl.when(has_odd)
    def _():
        for c in range(2):
            m_scr[c] = max_into(m_scr[c], scores(c, n_far - 1, 1, None))

    @pl.when(has_pair)
    def _():
        for c in range(2):
            m_scr[c] = max_into(m_scr[c], scores(c, 4 * n_quads, 2, None))

    def quad_max(ki, ms):
        return tuple(max_into(ms[c], scores(c, 4 * ki, 4, None)) for c in range(2))

    ms = lax.fori_loop(0, n_quads, quad_max, (m_scr[0], m_scr[1]))
    mb = [jnp.broadcast_to(jnp.max(ms[c], axis=-1, keepdims=True), (tb, LANES)) for c in range(2)]

    def accumulate(c, blk, nblocks, l, acc):
        ps = []
        for i in range(nblocks):
            ps += [jnp.exp2(ch - mb[c]) for ch in lane_chunks(s_scr[c, blk + i])]
        for ch in ps:
            l = l + ch
        p = jnp.concatenate(ps, axis=1).astype(BF16)
        vb = v_ref[0, pl.ds(pl.multiple_of(blk * tb, tb), nblocks * tb), :]
        return l, acc + jnp.dot(p, vb, preferred_element_type=F32)

    zl = jnp.zeros((tb, LANES), F32)

    @pl.when(jnp.logical_not(has_sub))
    def _():
        for c in range(2):
            l_scr[c], acc_scr[c] = accumulate(c, qi, 1, zl, zl)

    @pl.when(has_sub)
    def _():
        for c in range(2):
            l_scr[c], acc_scr[c] = accumulate(c, n_far, 2, zl, zl)

    @pl.when(has_odd)
    def _():
        for c in range(2):
            l_scr[c], acc_scr[c] = accumulate(c, n_far - 1, 1, l_scr[c], acc_scr[c])

    @pl.when(has_pair)
    def _():
        for c in range(2):
            l_scr[c], acc_scr[c] = accumulate(c, 4 * n_quads, 2, l_scr[c], acc_scr[c])

    def quad_acc(ki, st):
        l0, a0 = accumulate(0, 4 * ki, 4, st[0], st[1])
        l1, a1 = accumulate(1, 4 * ki, 4, st[2], st[3])
        return l0, a0, l1, a1

    l0, a0, l1, a1 = lax.fori_loop(0, n_quads, quad_acc, (l_scr[0], acc_scr[0], l_scr[1], acc_scr[1]))

    dl = dl_ref[...]
    lam = (jnp.exp(jnp.sum(dl[0:1, :] * dl[1:2, :], axis=-1, keepdims=True))
           - jnp.exp(jnp.sum(dl[2:3, :] * dl[3:4, :], axis=-1, keepdims=True)) + lam_init)
    o = (a0 / jnp.sum(l0, axis=-1, keepdims=True)
         - lam * (a1 / jnp.sum(l1, axis=-1, keepdims=True)))
    o = _rms(o, sw_ref[...], 1e-5) * (1.0 - lam_init)
    o_ref[0] = o.astype(o_ref.dtype)


def _attn(proj3, btiles, da_lambda, da_subln, lam_init, tb, col_q, col_k, col_v):
    b, t, _ = proj3.shape
    width = DA_HEADS * LANES
    return pl.pallas_call(
        functools.partial(_attn_kernel, lam_init=lam_init),
        grid=(b, DA_HEADS, t // tb),
        in_specs=[pl.BlockSpec((1, tb, LANES), lambda bi, h, qi: (bi, qi, col_q + h)),
                  pl.BlockSpec((1, t, LANES), lambda bi, h, qi: (bi, 0, col_k + h)),
                  pl.BlockSpec((1, t, LANES), lambda bi, h, qi: (bi, 0, col_v + h)),
                  pl.BlockSpec((1, tb, 2 * tb), lambda bi, h, qi: (h, 0, 0)),
                  pl.BlockSpec((4, DA_HEAD_DIM), lambda bi, h, qi: (0, 0)),
                  pl.BlockSpec((1, LANES), lambda bi, h, qi: (0, 0))],
        out_specs=pl.BlockSpec((1, tb, LANES), lambda bi, h, qi: (bi, qi, h)),
        out_shape=jax.ShapeDtypeStruct((b, t, width), BF16),
        scratch_shapes=[pltpu.VMEM((2, tb, LANES), F32), pltpu.VMEM((2, tb, LANES), F32),
                        pltpu.VMEM((2, tb, LANES), F32), pltpu.VMEM((2, t // tb, tb, tb), F32)],
        compiler_params=_cparams(("parallel", "parallel", "arbitrary")),
        name="diff_attn",
    )(proj3, proj3, proj3, btiles, da_lambda, da_subln)


def _s5_kernel(u_ref, wg_ref, wk_ref, wc_ref, pw_ref, d_ref, o_ref, uf_scr, g_scr, hp_scr, y_scr):
    t_len = u_ref.shape[1]
    nj = t_len // S5_L
    uf_scr[...] = u_ref[0].astype(F32)
    ur = jnp.concatenate([uf_scr[pl.ds(s, nj, stride=S5_L), :].astype(BF16) for s in range(S5_L)],
                         axis=1)
    g_scr[...] = jnp.dot(ur, wg_ref[0], preferred_element_type=F32)

    pw = pw_ref[0]
    pr, pi = pw[8:16, :S5_HALF], pw[8:16, S5_HALF:]
    row = lax.broadcasted_iota(jnp.int32, (8, S5_HALF), 0)

    def group(gi, carry):
        cr, ci = carry
        base = pl.multiple_of(gi * 8, 8)
        x = g_scr[pl.ds(base, 8), :]
        xr, xi = x[:, :S5_HALF], x[:, S5_HALF:]
        for lvl, s in enumerate((1, 2, 4)):
            ar, ai = pw[lvl:lvl + 1, :S5_HALF], pw[lvl:lvl + 1, S5_HALF:]
            keep = row >= s
            sr = jnp.where(keep, pltpu.roll(xr, s, 0), 0.0)
            si = jnp.where(keep, pltpu.roll(xi, s, 0), 0.0)
            xr, xi = xr + (ar * sr - ai * si), xi + (ar * si + ai * sr)
        xr, xi = xr + (pr * cr - pi * ci), xi + (pr * ci + pi * cr)
        first = row == 0
        hp_scr[pl.ds(base, 8), :] = jnp.concatenate(
            [jnp.where(first, cr, pltpu.roll(xr, 1, 0)), jnp.where(first, ci, pltpu.roll(xi, 1, 0))], axis=1)
        return xr[7:8, :], xi[7:8, :]

    z = jnp.zeros((1, S5_HALF), F32)
    lax.fori_loop(0, nj // 8, group, (z, z))

    y = (jnp.dot(ur, wk_ref[0], preferred_element_type=F32)
         + jnp.dot(hp_scr[...].astype(BF16), wc_ref[0], preferred_element_type=F32))
    for s in range(S5_L):
        y_scr[pl.ds(s, nj, stride=S5_L), :] = y[:, s * LANES:(s + 1) * LANES]
    o_ref[0] = _gelu_tanh(y_scr[...] + d_ref[...] * uf_scr[...]).astype(o_ref.dtype)


def _s5(proj3, wg, wk, wc, pw, dvec, col_u):
    b, t, _ = proj3.shape
    nblk = wg.shape[0]
    nj = t // S5_L
    wide = S5_L * LANES
    assert nj % 8 == 0

    def wspec(rows, cols):
        return pl.BlockSpec((1, rows, cols), lambda g, bi: (g, 0, 0))

    return pl.pallas_call(
        _s5_kernel,
        grid=(nblk, b),
        in_specs=[pl.BlockSpec((1, t, LANES), lambda g, bi: (bi, 0, col_u + g)),
                  wspec(wide, 2 * S5_HALF), wspec(wide, wide), wspec(2 * S5_HALF, wide),
                  wspec(16, 2 * S5_HALF),
                  pl.BlockSpec((1, LANES), lambda g, bi: (0, g))],
        out_specs=pl.BlockSpec((1, t, LANES), lambda g, bi: (bi, 0, g)),
        out_shape=jax.ShapeDtypeStruct((b, t, nblk * LANES), BF16),
        scratch_shapes=[pltpu.VMEM((t, LANES), F32), pltpu.VMEM((nj, 2 * S5_HALF), F32),
                        pltpu.VMEM((nj, 2 * S5_HALF), F32), pltpu.VMEM((t, LANES), F32)],
        compiler_params=_cparams(("parallel", "parallel")),
        name="s5",
    )(proj3, wg, wk, wc, pw, dvec)


def _merge_kernel(ya_ref, yb_ref, yc_ref, g0_ref, g1_ref, g2_ref, wg_ref, bg_ref, wb_ref, bgate_ref,
                  o_ref, yc_scr):
    @pl.when(pl.program_id(1) == 0)
    def _():
        yc = yc_ref[...]
        z = jnp.dot(yc, wg_ref[...], preferred_element_type=F32) + bg_ref[...]
        yc_scr[...] = (yc.astype(F32) * _sigmoid(z)).astype(BF16)

    bgate = bgate_ref[...]
    ys = (ya_ref[...], yb_ref[...], yc_scr[...])
    gs = (g0_ref, g1_ref, g2_ref)
    merged = None
    for br in range(N_BRANCH):
        gate = _sigmoid(gs[br][...].astype(F32) + bgate[br:br + 1, :])
        term = gate * jnp.dot(ys[br], wb_ref[0, br], preferred_element_type=F32)
        merged = term if merged is None else merged + term
    o_ref[...] = merged.astype(o_ref.dtype)


def _merge(ya, yb, yc, proj, w_glu, b_glu, wb_tiles, b_gate, col_g, tm):
    n, wdt = ya.shape
    nt, _, _, tn = wb_tiles.shape
    d = nt * tn
    gcol = [(col_g + br * d) // tn for br in range(N_BRANCH)]
    yspec = pl.BlockSpec((tm, wdt), lambda i, j: (i, 0))

    def gspec(br):
        return pl.BlockSpec((tm, tn), lambda i, j: (i, gcol[br] + j))

    return pl.pallas_call(
        _merge_kernel,
        grid=(n // tm, d // tn),
        in_specs=[yspec, yspec, yspec, gspec(0), gspec(1), gspec(2),
                  pl.BlockSpec((wdt, wdt), lambda i, j: (0, 0)),
                  pl.BlockSpec((1, wdt), lambda i, j: (0, 0)),
                  pl.BlockSpec((1, N_BRANCH, wdt, tn), lambda i, j: (j, 0, 0, 0)),
                  pl.BlockSpec((N_BRANCH, tn), lambda i, j: (0, j))],
        out_specs=pl.BlockSpec((tm, tn), lambda i, j: (i, j)),
        out_shape=jax.ShapeDtypeStruct((n, d), BF16),
        scratch_shapes=[pltpu.VMEM((tm, wdt), BF16)],
        compiler_params=_cparams(("parallel", "arbitrary")),
        name="merge",
    )(ya, yb, yc, proj, proj, proj, w_glu, b_glu, wb_tiles, b_gate)


def _outproj_kernel(m_ref, w_ref, xs_ref, nw_ref, o_ref):
    mix = jnp.dot(m_ref[...], w_ref[...], preferred_element_type=F32)
    o_ref[...] = xs_ref[...] + _rms(mix, nw_ref[...], 1e-6)


def _outproj(merged, w_out, xs, nw, tm):
    n, d = xs.shape
    return pl.pallas_call(
        _outproj_kernel,
        grid=(n // tm,),
        in_specs=[pl.BlockSpec((tm, d), lambda i: (i, 0)),
                  pl.BlockSpec((d, d), lambda i: (0, 0)),
                  pl.BlockSpec((tm, d), lambda i: (i, 0)),
                  pl.BlockSpec((1, d), lambda i: (0, 0))],
        out_specs=pl.BlockSpec((tm, d), lambda i: (i, 0)),
        out_shape=jax.ShapeDtypeStruct((n, d), F32),
        compiler_params=_cparams(("parallel",)),
        name="outproj_residual",
    )(merged, w_out, xs, nw)


def _ffn_kernel(xs_ref, nw_in_ref, wg_ref, wu_ref, wo_ref, nw_out_ref, o_ref, h_scr, acc_scr):
    f = pl.program_id(1)

    @pl.when(f == 0)
    def _():
        h_scr[...] = _rms(xs_ref[...], nw_in_ref[...], 1e-6).astype(BF16)
        acc_scr[...] = jnp.zeros(acc_scr.shape, F32)

    h = h_scr[...]
    gate = jnp.dot(h, wg_ref[0], preferred_element_type=F32)
    up = jnp.dot(h, wu_ref[0], preferred_element_type=F32)
    act = (gate * _sigmoid(gate) * up).astype(BF16)
    acc_scr[...] += jnp.dot(act, wo_ref[...], preferred_element_type=F32)

    @pl.when(f == pl.num_programs(1) - 1)
    def _():
        o_ref[...] = xs_ref[...] + _rms(acc_scr[...], nw_out_ref[...], 1e-6)


def _ffn(xs, nw_in, w_in_tiles, w_ffn_out, nw_out, tm):
    n, d = xs.shape
    tf = w_in_tiles.shape[2]
    nf = w_in_tiles.shape[0] // 2
    return pl.pallas_call(
        _ffn_kernel,
        grid=(n // tm, nf),
        in_specs=[pl.BlockSpec((tm, d), lambda i, f: (i, 0)),
                  pl.BlockSpec((1, d), lambda i, f: (0, 0)),
                  pl.BlockSpec((1, d, tf), lambda i, f: (f, 0, 0)),
                  pl.BlockSpec((1, d, tf), lambda i, f: (nf + f, 0, 0)),
                  pl.BlockSpec((tf, d), lambda i, f: (f, 0)),
                  pl.BlockSpec((1, d), lambda i, f: (0, 0))],
        out_specs=pl.BlockSpec((tm, d), lambda i, f: (i, 0)),
        out_shape=jax.ShapeDtypeStruct((n, d), F32),
        scratch_shapes=[pltpu.VMEM((tm, d), BF16), pltpu.VMEM((tm, d), F32)],
        compiler_params=_cparams(("parallel", "arbitrary")),
        name="swiglu_ffn",
    )(xs, nw_in, w_in_tiles, w_in_tiles, w_ffn_out, nw_out)


def _s5_tables(lam_re, lam_im, b_re, b_im, c_re, c_im, log_step):
    hp = lax.Precision.HIGHEST
    groups = lam_re.shape[0]
    nblk = groups // S5_GB
    wide = S5_L * LANES
    lr, li = lam_re.astype(F32), lam_im.astype(F32)
    step = jnp.exp(log_step.astype(F32))[:, None]
    mag = jnp.exp(lr * step)
    ab_re, ab_im = mag * jnp.cos(li * step), mag * jnp.sin(li * step)
    den = lr * lr + li * li
    coef_re = ((ab_re - 1.0) * lr + ab_im * li) / den
    coef_im = (ab_im * lr - (ab_re - 1.0) * li) / den
    br, bi = b_re.astype(F32), b_im.astype(F32)
    bb_re = coef_re[..., None] * br - coef_im[..., None] * bi
    bb_im = coef_re[..., None] * bi + coef_im[..., None] * br
    cr, ci = c_re.astype(F32), c_im.astype(F32)
    eye = jnp.eye(S5_GB, dtype=F32)
    taus = jnp.arange(S5_L)
    lrs = (lr * step).reshape(nblk, 1, S5_HALF)
    lis = (li * step).reshape(nblk, 1, S5_HALF)

    def lam_pow(n):
        nn = n.astype(F32)[None, :, None]
        m = jnp.exp(nn * lrs)
        return m * jnp.cos(nn * lis), m * jnp.sin(nn * lis)

    def in_blockdiag(t):
        t = t.reshape(nblk, S5_GB, S5_STATE, S5_GROUP)
        return jnp.einsum("ngpc,gh->ngchp", t, eye).reshape(nblk, LANES, S5_HALF)

    def out_blockdiag(t):
        t = t.reshape(nblk, S5_GB, S5_GROUP, S5_STATE)
        return jnp.einsum("ngcp,gh->ngphc", t, eye).reshape(nblk, S5_HALF, LANES)

    bd_re, bd_im = in_blockdiag(bb_re)[:, None], in_blockdiag(bb_im)[:, None]
    ct_re, ct_im = out_blockdiag(cr)[:, :, None, :], out_blockdiag(ci)[:, :, None, :]

    qr, qi = lam_pow(S5_L - 1 - taus)
    qr, qi = qr[:, :, None, :], qi[:, :, None, :]
    wg = jnp.concatenate([qr * bd_re - qi * bd_im, qr * bd_im + qi * bd_re], axis=3)
    wg = wg.reshape(nblk, wide, 2 * S5_HALF).astype(BF16)

    p1r, p1i = lam_pow(1 + taus)
    p1r, p1i = p1r.transpose(0, 2, 1)[..., None], p1i.transpose(0, 2, 1)[..., None]
    wc = jnp.concatenate([ct_re * p1r - ct_im * p1i, -(ct_re * p1i + ct_im * p1r)], axis=1)
    wc = wc.reshape(nblk, 2 * S5_HALF, wide).astype(BF16)

    nn = taus.astype(F32)[:, None, None]
    mg = jnp.exp(nn * (lr * step)[None])
    pr, pi = mg * jnp.cos(nn * (li * step)[None]), mg * jnp.sin(nn * (li * step)[None])
    c0_re = cr[None] * pr[:, :, None, :] - ci[None] * pi[:, :, None, :]
    c0_im = cr[None] * pi[:, :, None, :] + ci[None] * pr[:, :, None, :]
    kt = (jnp.einsum("tgop,gpi->tgio", c0_re, bb_re, precision=hp)
          - jnp.einsum("tgop,gpi->tgio", c0_im, bb_im, precision=hp))
    kt = kt.reshape(S5_L, nblk, S5_GB, S5_GROUP, S5_GROUP)
    kblk = jnp.einsum("tngio,gh->ntgiho", kt, eye).reshape(nblk, S5_L, LANES, LANES)
    lag = taus[None, :] - taus[:, None]
    ksr = jnp.take(kblk, jnp.maximum(lag, 0).reshape(-1), axis=1).reshape(nblk, S5_L, S5_L, LANES, LANES)
    ksr = jnp.where((lag >= 0)[None, :, :, None, None], ksr, 0.0)
    wk = ksr.transpose(0, 1, 3, 2, 4).reshape(nblk, wide, wide).astype(BF16)

    n_list = S5_L * jnp.concatenate([jnp.array([1, 2, 4, 0, 0, 0, 0, 0]), 1 + jnp.arange(8)])
    wr, wi = lam_pow(n_list)
    pw = jnp.concatenate([wr, wi], axis=2)
    return wg, wk, wc, pw


def kernel(x, meta, rel_bias, norm_w, w_in, conv_w, conv_b, lru_w_a, lru_b_a, lru_w_x, lru_b_x, lru_lambda, da_lambda, da_subln, s5_lam_re, s5_lam_im, s5_b_re, s5_b_im, s5_c_re, s5_c_im, s5_d, s5_log_step, s5_w_glu, s5_b_glu, b_gate, w_branch, w_out, w_ffn_in, w_ffn_out):
    bsz, seq, d_model = x.shape
    depth = w_in.shape[0]
    lru_w = conv_w.shape[2]
    s5_w = s5_d.shape[1]
    qk_w = DA_HEADS * 2 * DA_HEAD_DIM
    assert lru_w == LRU_BLOCKS * LANES and s5_w % LANES == 0 and d_model % LANES == 0
    col_gate, col_x = 0, lru_w // LANES
    col_q = 2 * lru_w // LANES
    col_k = col_q + qk_w // LANES
    col_v = col_k + qk_w // LANES
    col_u = col_v + DA_HEADS
    col_g = (col_u + s5_w // LANES) * LANES

    t_real = N_META + seq
    tb = 384 if t_real >= 1024 else 128
    t_pad = -(-t_real // tb) * tb
    assert t_pad % SCAN_CH == 0
    n_tok = bsz * t_pad
    tm_in = _pick_tile(n_tok, (1408, 1056, 768, 512, 384, 256, 128))
    tm_merge = _pick_tile(n_tok, (1056, 768, 512, 384, 256, 128))
    tm_ffn = _pick_tile(n_tok, (768, 512, 384, 256, 128))
    tm_out = _pick_tile(n_tok, (512, 384, 256, 128))

    xs = jnp.concatenate([jnp.broadcast_to(meta.astype(F32)[None], (bsz, N_META, d_model)), x,
                          jnp.zeros((bsz, t_pad - t_real, d_model), F32)], axis=1)
    xs = xs.reshape(n_tok, d_model)
    btiles = _bias_tiles(rel_bias.astype(F32), tb)

    for l in range(depth):
        lam_init = 0.8 - 0.6 * math.exp(-0.3 * l)
        proj = _inproj(xs, norm_w[l, 0][None], _col_tiles(w_in[l].astype(BF16), 1024), tm_in)
        proj3 = proj.reshape(bsz, t_pad, proj.shape[1])

        lru_p = jnp.concatenate([conv_w[l], conv_b[l][None], lru_b_a[l][None], lru_b_x[l][None],
                                 jax.nn.log_sigmoid(lru_lambda[l].astype(F32))[None]], axis=0)
        w_ax = jnp.concatenate([lru_w_a[l], lru_w_x[l]], axis=2).astype(BF16)
        y_a = _lru(proj3, lru_p, w_ax, col_gate, col_x)

        y_b = _attn(proj3, btiles, da_lambda[l], da_subln[l][None], lam_init, tb, col_q, col_k, col_v)

        wg, wk, wc, pw = _s5_tables(s5_lam_re[l], s5_lam_im[l], s5_b_re[l], s5_b_im[l],
                                    s5_c_re[l], s5_c_im[l], s5_log_step[l])
        y_c = _s5(proj3, wg, wk, wc, pw, s5_d[l][None], col_u)

        wb = w_branch[l].astype(BF16)
        wb_tiles = wb.reshape(N_BRANCH, wb.shape[1], d_model // 512, 512).transpose(2, 0, 1, 3)
        merged = _merge(y_a.reshape(n_tok, lru_w), y_b.reshape(n_tok, -1), y_c.reshape(n_tok, s5_w),
                        proj, s5_w_glu[l].astype(BF16), s5_b_glu[l][None], wb_tiles, b_gate[l], col_g,
                        tm_merge)
        xs = _outproj(merged, w_out[l].astype(BF16), xs, norm_w[l, 1][None], tm_out)
        xs = _ffn(xs, norm_w[l, 2][None], _col_tiles(w_ffn_in[l].astype(BF16), 512),
                  w_ffn_out[l].astype(BF16), norm_w[l, 3][None], tm_ffn)

    return xs.reshape(bsz, t_pad, d_model)[:, N_META:t_real]
```

```python
import functools
import math

import jax
import jax.numpy as jnp
from jax import lax
from jax.experimental import pallas as pl
from jax.experimental.pallas import tpu as pltpu

F32 = jnp.float32
BF16 = jnp.bfloat16

N_META = 16
CONV_W = 4
LRU_C = 8.0
LRU_BLOCKS = 8
DA_HEADS = 8
DA_HEAD_DIM = 64
S5_GROUP = 16
S5_STATE = 64
REL_BUCKETS = 32
REL_MAX_DIST = 128
N_BRANCH = 3

LANES = 128
S5_GB = LANES // S5_GROUP
S5_HALF = S5_GB * S5_STATE
S5_L = 8
SCAN_CH = 128
MASK_NEG = -1e30
LOG2E = math.log2(math.e)
VMEM_LIMIT = 56 * 1024 * 1024


def _cparams(sem):
    return pltpu.CompilerParams(dimension_semantics=sem, vmem_limit_bytes=VMEM_LIMIT)


def _pick_tile(n, candidates):
    for c in candidates:
        if n % c == 0:
            return c
    raise ValueError(f"no tile in {candidates} divides {n}")


def _gelu_tanh(x):
    return 0.5 * x * (1.0 + jnp.tanh(math.sqrt(2.0 / math.pi) * (x + 0.044715 * (x * x * x))))


def _sigmoid(x):
    return 1.0 / (1.0 + jnp.exp(-x))


def _rms(x, w, eps):
    return (x * lax.rsqrt(jnp.mean(x * x, axis=-1, keepdims=True) + eps)) * w


def _inproj_kernel(x_ref, nw_ref, w_ref, o_ref, h_scr):
    @pl.when(pl.program_id(1) == 0)
    def _():
        h_scr[...] = _rms(x_ref[...], nw_ref[...], 1e-6).astype(BF16)

    o_ref[...] = jnp.dot(h_scr[...], w_ref[...], preferred_element_type=F32).astype(o_ref.dtype)


def _inproj(xs, nw, w_all, layer, tm, tn):
    n, d = xs.shape
    n_out = w_all.shape[2]
    return pl.pallas_call(
        _inproj_kernel,
        grid=(n // tm, n_out // tn),
        in_specs=[pl.BlockSpec((tm, d), lambda i, j: (i, 0)),
                  pl.BlockSpec((1, d), lambda i, j: (0, 0)),
                  pl.BlockSpec((None, d, tn), lambda i, j: (layer, 0, j))],
        out_specs=pl.BlockSpec((tm, tn), lambda i, j: (i, j)),
        out_shape=jax.ShapeDtypeStruct((n, n_out), BF16),
        scratch_shapes=[pltpu.VMEM((tm, d), BF16)],
        compiler_params=_cparams(("parallel", "arbitrary")),
        name="inproj",
    )(xs, nw, w_all)


def _lru_kernel(g_ref, x_ref, p_ref, w_ref, o_ref, xs_scr):
    t_len = x_ref.shape[1]
    halo = 8
    xs_scr[0:halo, :] = jnp.zeros((halo, LANES), F32)
    xs_scr[halo:, :] = x_ref[0].astype(F32)
    p = p_ref[...]
    cw = [p[j:j + 1, :] for j in range(CONV_W)]
    cb, ba, bx, logsig = p[4:5, :], p[5:6, :], p[6:7, :], p[7:8, :]
    w = w_ref[0]
    row = lax.broadcasted_iota(jnp.int32, (SCAN_CH, LANES), 0)

    def chunk(c, h0):
        base = pl.multiple_of(c * SCAN_CH, SCAN_CH)
        xc = cb
        for j in range(CONV_W):
            xc = xc + cw[j] * xs_scr[pl.ds(base + halo - (CONV_W - 1) + j, SCAN_CH), :]
        ri = jnp.dot(xc.astype(BF16), w, preferred_element_type=F32)
        r = _sigmoid(ri[:, :LANES] + ba)
        i = _sigmoid(ri[:, LANES:] + bx)
        log_a = LRU_C * r * logsig
        a = jnp.exp(log_a)
        b = jnp.sqrt(1.0 - jnp.exp(2.0 * log_a)) * (i * xc)
        s = 1
        while s < SCAN_CH:
            keep = row >= s
            a_sh = jnp.where(keep, pltpu.roll(a, s, 0), 1.0)
            b_sh = jnp.where(keep, pltpu.roll(b, s, 0), 0.0)
            b = a * b_sh + b
            a = a * a_sh
            s *= 2
        h = a * h0 + b
        g = g_ref[0, pl.ds(base, SCAN_CH), :].astype(F32)
        o_ref[0, pl.ds(base, SCAN_CH), :] = (h * _gelu_tanh(g)).astype(o_ref.dtype)
        return h[SCAN_CH - 1:SCAN_CH, :]

    lax.fori_loop(0, t_len // SCAN_CH, chunk, jnp.zeros((1, LANES), F32))


def _lru(proj3, lru_p, w_ax, col_gate, col_x):
    b, t, _ = proj3.shape
    width = LRU_BLOCKS * LANES
    return pl.pallas_call(
        _lru_kernel,
        grid=(b, LRU_BLOCKS),
        in_specs=[pl.BlockSpec((1, t, LANES), lambda bi, h: (bi, 0, col_gate + h)),
                  pl.BlockSpec((1, t, LANES), lambda bi, h: (bi, 0, col_x + h)),
                  pl.BlockSpec((8, LANES), lambda bi, h: (0, h)),
                  pl.BlockSpec((1, LANES, 2 * LANES), lambda bi, h: (h, 0, 0))],
        out_specs=pl.BlockSpec((1, t, LANES), lambda bi, h: (bi, 0, h)),
        out_shape=jax.ShapeDtypeStruct((b, t, width), BF16),
        scratch_shapes=[pltpu.VMEM((t + 8, LANES), F32)],
        compiler_params=_cparams(("parallel", "parallel")),
        name="rglru",
    )(proj3, proj3, lru_p, w_ax)


def _bias_tile_kernel(rb_ref, o_ref):
    h = pl.program_id(0)
    tb = o_ref.shape[1]
    i = lax.broadcasted_iota(jnp.int32, (tb, 2 * tb), 0)
    j2 = lax.broadcasted_iota(jnp.int32, (tb, 2 * tb), 1)
    max_exact = REL_BUCKETS // 2
    d = i - j2 + tb
    n = jnp.maximum(d, 0)
    nf = jnp.maximum(n, 1).astype(F32)
    large = max_exact + (jnp.log(nf / max_exact) / math.log(REL_MAX_DIST / max_exact)
                         * (REL_BUCKETS - max_exact)).astype(jnp.int32)
    large = jnp.minimum(large, REL_BUCKETS - 1)
    bucket = jnp.where(n < max_exact, n, large)
    val = jnp.zeros((tb, 2 * tb), F32)
    for bkt in range(REL_BUCKETS):
        val = jnp.where(bucket == bkt, rb_ref[bkt, h], val)
    val = val - rb_ref[REL_BUCKETS - 1, h]
    o_ref[0] = jnp.where(d >= 0, val, MASK_NEG)


def _bias_tiles(rel_bias, tb):
    return pl.pallas_call(
        _bias_tile_kernel,
        grid=(DA_HEADS,),
        in_specs=[pl.BlockSpec(memory_space=pltpu.SMEM)],
        out_specs=pl.BlockSpec((1, tb, 2 * tb), lambda h: (h, 0, 0)),
        out_shape=jax.ShapeDtypeStruct((DA_HEADS, tb, 2 * tb), F32),
        compiler_params=_cparams(("parallel",)),
        name="t5_bias_tiles",
    )(rel_bias)


def _attn_kernel(q_ref, k_ref, v_ref, bt_ref, dl_ref, sw_ref, o_ref, m_scr, l_scr, acc_scr, s_scr, *,
                 lam_init):
    qi = pl.program_id(2)
    tb = q_ref.shape[1]
    dn_t = (((1,), (1,)), ((), ()))
    lane = lax.broadcasted_iota(jnp.int32, (tb, LANES), 1)
    qs = q_ref[0] * (DA_HEAD_DIM ** -0.5)
    zero = jnp.zeros_like(qs)
    qcat = jnp.concatenate([jnp.where(lane < DA_HEAD_DIM, qs, zero),
                            jnp.where(lane >= DA_HEAD_DIM, qs, zero)], axis=0)
    n_far = jnp.maximum(qi - 1, 0)
    has_sub = qi >= 1
    n_quads = n_far // 4
    has_pair = (n_far // 2) % 2 == 1
    has_odd = n_far % 2 == 1

    def scores(blk, nblocks, bias):
        kb = k_ref[0, pl.ds(pl.multiple_of(blk * tb, tb), nblocks * tb), :]
        s_both = lax.dot_general(qcat, kb, dn_t, preferred_element_type=F32)
        out = []
        for c in range(2):
            s = s_both[c * tb:(c + 1) * tb]
            if bias is not None:
                s = s + bias
            s = s * LOG2E
            for i in range(nblocks):
                s_scr[c, blk + i] = s[:, i * tb:(i + 1) * tb]
            out.append(s)
        return out

    def lane_chunks(s):
        return [s[:, j * LANES:(j + 1) * LANES] for j in range(s.shape[1] // LANES)]

    def max_into(ms, ss):
        out = []
        for m, s in zip(ms, ss):
            for ch in lane_chunks(s):
                m = jnp.maximum(m, ch)
            out.append(m)
        return out

    m_init = jnp.full((tb, LANES), MASK_NEG, F32)

    @pl.when(jnp.logical_not(has_sub))
    def _():
        m_scr[0], m_scr[1] = max_into((m_init, m_init), scores(qi, 1, bt_ref[0, :, tb:]))

    @pl.when(has_sub)
    def _():
        m_scr[0], m_scr[1] = max_into((m_init, m_init), scores(n_far, 2, bt_ref[0]))

    @pl.when(has_odd)
    def _():
        m_scr[0], m_scr[1] = max_into((m_scr[0], m_scr[1]), scores(n_far - 1, 1, None))

    @pl.when(has_pair)
    def _():
        m_scr[0], m_scr[1] = max_into((m_scr[0], m_scr[1]), scores(4 * n_quads, 2, None))

    def quad_max(ki, ms):
        return tuple(max_into(ms, scores(4 * ki, 4, None)))

    ms = lax.fori_loop(0, n_quads, quad_max, (m_scr[0], m_scr[1]))
    mb = [jnp.broadcast_to(jnp.max(ms[c], axis=-1, keepdims=True), (tb, LANES)) for c in range(2)]

    def accumulate(blk, nblocks, st):
        vb = v_ref[0, pl.ds(pl.multiple_of(blk * tb, tb), nblocks * tb), :]
        out = []
        for c in range(2):
            ps = []
            for i in range(nblocks):
                ps += [jnp.exp2(ch - mb[c]) for ch in lane_chunks(s_scr[c, blk + i])]
            l = st[2 * c]
            for ch in ps:
                l = l + ch
            p = jnp.concatenate(ps, axis=1).astype(BF16)
            out += [l, st[2 * c + 1] + jnp.dot(p, vb, preferred_element_type=F32)]
        return tuple(out)

    def load_state():
        return l_scr[0], acc_scr[0], l_scr[1], acc_scr[1]

    def store_state(st):
        l_scr[0], acc_scr[0], l_scr[1], acc_scr[1] = st

    zl = jnp.zeros((tb, LANES), F32)

    @pl.when(jnp.logical_not(has_sub))
    def _():
        store_state(accumulate(qi, 1, (zl, zl, zl, zl)))

    @pl.when(has_sub)
    def _():
        store_state(accumulate(n_far, 2, (zl, zl, zl, zl)))

    @pl.when(has_odd)
    def _():
        store_state(accumulate(n_far - 1, 1, load_state()))

    @pl.when(has_pair)
    def _():
        store_state(accumulate(4 * n_quads, 2, load_state()))

    l0, a0, l1, a1 = lax.fori_loop(0, n_quads, lambda ki, st: accumulate(4 * ki, 4, st), load_state())

    dl = dl_ref[...]
    lam = (jnp.exp(jnp.sum(dl[0:1, :] * dl[1:2, :], axis=-1, keepdims=True))
           - jnp.exp(jnp.sum(dl[2:3, :] * dl[3:4, :], axis=-1, keepdims=True)) + lam_init)
    o = (a0 / jnp.sum(l0, axis=-1, keepdims=True)
         - lam * (a1 / jnp.sum(l1, axis=-1, keepdims=True)))
    o = _rms(o, sw_ref[...], 1e-5) * (1.0 - lam_init)
    o_ref[0] = o.astype(o_ref.dtype)


def _attn(proj3, btiles, da_lambda, da_subln, lam_init, tb, col_q, col_k, col_v):
    b, t, _ = proj3.shape
    width = DA_HEADS * LANES
    return pl.pallas_call(
        functools.partial(_attn_kernel, lam_init=lam_init),
        grid=(b, DA_HEADS, t // tb),
        in_specs=[pl.BlockSpec((1, tb, LANES), lambda bi, h, qi: (bi, qi, col_q + h)),
                  pl.BlockSpec((1, t, LANES), lambda bi, h, qi: (bi, 0, col_k + h)),
                  pl.BlockSpec((1, t, LANES), lambda bi, h, qi: (bi, 0, col_v + h)),
                  pl.BlockSpec((1, tb, 2 * tb), lambda bi, h, qi: (h, 0, 0)),
                  pl.BlockSpec((4, DA_HEAD_DIM), lambda bi, h, qi: (0, 0)),
                  pl.BlockSpec((1, LANES), lambda bi, h, qi: (0, 0))],
        out_specs=pl.BlockSpec((1, tb, LANES), lambda bi, h, qi: (bi, qi, h)),
        out_shape=jax.ShapeDtypeStruct((b, t, width), BF16),
        scratch_shapes=[pltpu.VMEM((2, tb, LANES), F32), pltpu.VMEM((2, tb, LANES), F32),
                        pltpu.VMEM((2, tb, LANES), F32), pltpu.VMEM((2, t // tb, tb, tb), F32)],
        compiler_params=_cparams(("parallel", "parallel", "arbitrary")),
        name="diff_attn",
    )(proj3, proj3, proj3, btiles, da_lambda, da_subln)


def _s5_kernel(u_ref, wg_ref, wk_ref, wc_ref, pw_ref, d_ref, o_ref, uf_scr, g_scr, hp_scr, y_scr):
    t_len = u_ref.shape[1]
    nj = t_len // S5_L
    uf_scr[...] = u_ref[0].astype(F32)
    ur = jnp.concatenate([uf_scr[pl.ds(s, nj, stride=S5_L), :].astype(BF16) for s in range(S5_L)],
                         axis=1)
    g_scr[...] = jnp.dot(ur, wg_ref[0], preferred_element_type=F32)

    pw = pw_ref[0]
    pr, pi = pw[8:16, :S5_HALF], pw[8:16, S5_HALF:]
    row = lax.broadcasted_iota(jnp.int32, (8, S5_HALF), 0)

    def group(gi, carry):
        cr, ci = carry
        base = pl.multiple_of(gi * 8, 8)
        x = g_scr[pl.ds(base, 8), :]
        xr, xi = x[:, :S5_HALF], x[:, S5_HALF:]
        for lvl, s in enumerate((1, 2, 4)):
            ar, ai = pw[lvl:lvl + 1, :S5_HALF], pw[lvl:lvl + 1, S5_HALF:]
            keep = row >= s
            sr = jnp.where(keep, pltpu.roll(xr, s, 0), 0.0)
            si = jnp.where(keep, pltpu.roll(xi, s, 0), 0.0)
            xr, xi = xr + (ar * sr - ai * si), xi + (ar * si + ai * sr)
        xr, xi = xr + (pr * cr - pi * ci), xi + (pr * ci + pi * cr)
        first = row == 0
        hp_scr[pl.ds(base, 8), :] = jnp.concatenate(
            [jnp.where(first, cr, pltpu.roll(xr, 1, 0)), jnp.where(first, ci, pltpu.roll(xi, 1, 0))], axis=1)
        return xr[7:8, :], xi[7:8, :]

    z = jnp.zeros((1, S5_HALF), F32)
    lax.fori_loop(0, nj // 8, group, (z, z))

    y = (jnp.dot(ur, wk_ref[0], preferred_element_type=F32)
         + jnp.dot(hp_scr[...].astype(BF16), wc_ref[0], preferred_element_type=F32))
    for s in range(S5_L):
        y_scr[pl.ds(s, nj, stride=S5_L), :] = y[:, s * LANES:(s + 1) * LANES]
    o_ref[0] = _gelu_tanh(y_scr[...] + d_ref[...] * uf_scr[...]).astype(o_ref.dtype)


def _s5(proj3, wg, wk, wc, pw, dvec, col_u):
    b, t, _ = proj3.shape
    nblk = wg.shape[0]
    nj = t // S5_L
    wide = S5_L * LANES
    assert nj % 8 == 0

    def wspec(rows, cols):
        return pl.BlockSpec((1, rows, cols), lambda g, bi: (g, 0, 0))

    return pl.pallas_call(
        _s5_kernel,
        grid=(nblk, b),
        in_specs=[pl.BlockSpec((1, t, LANES), lambda g, bi: (bi, 0, col_u + g)),
                  wspec(wide, 2 * S5_HALF), wspec(wide, wide), wspec(2 * S5_HALF, wide),
                  wspec(16, 2 * S5_HALF),
                  pl.BlockSpec((1, LANES), lambda g, bi: (0, g))],
        out_specs=pl.BlockSpec((1, t, LANES), lambda g, bi: (bi, 0, g)),
        out_shape=jax.ShapeDtypeStruct((b, t, nblk * LANES), BF16),
        scratch_shapes=[pltpu.VMEM((t, LANES), F32), pltpu.VMEM((nj, 2 * S5_HALF), F32),
                        pltpu.VMEM((nj, 2 * S5_HALF), F32), pltpu.VMEM((t, LANES), F32)],
        compiler_params=_cparams(("parallel", "parallel")),
        name="s5",
    )(proj3, wg, wk, wc, pw, dvec)


def _merge_kernel(ya_ref, yb_ref, yc_ref, g0_ref, g1_ref, g2_ref, wg_ref, bg_ref, wb_ref, bgate_ref,
                  o_ref, yc_scr):
    @pl.when(pl.program_id(1) == 0)
    def _():
        yc = yc_ref[...]
        z = jnp.dot(yc, wg_ref[...], preferred_element_type=F32) + bg_ref[...]
        yc_scr[...] = (yc.astype(F32) * _sigmoid(z)).astype(BF16)

    bgate = bgate_ref[...]
    ys = (ya_ref[...], yb_ref[...], yc_scr[...])
    gs = (g0_ref, g1_ref, g2_ref)
    merged = None
    for br in range(N_BRANCH):
        gate = _sigmoid(gs[br][...].astype(F32) + bgate[br:br + 1, :])
        term = gate * jnp.dot(ys[br], wb_ref[br], preferred_element_type=F32)
        merged = term if merged is None else merged + term
    o_ref[...] = merged.astype(o_ref.dtype)


def _merge(ya, yb, yc, proj, w_glu_all, b_glu, w_branch_all, b_gate, col_g, layer, tm, tn):
    n, wdt = ya.shape
    d = w_branch_all.shape[3]
    gcol = [(col_g + br * d) // tn for br in range(N_BRANCH)]
    yspec = pl.BlockSpec((tm, wdt), lambda i, j: (i, 0))

    def gspec(br):
        return pl.BlockSpec((tm, tn), lambda i, j: (i, gcol[br] + j))

    return pl.pallas_call(
        _merge_kernel,
        grid=(n // tm, d // tn),
        in_specs=[yspec, yspec, yspec, gspec(0), gspec(1), gspec(2),
                  pl.BlockSpec((None, wdt, wdt), lambda i, j: (layer, 0, 0)),
                  pl.BlockSpec((1, wdt), lambda i, j: (0, 0)),
                  pl.BlockSpec((None, N_BRANCH, wdt, tn), lambda i, j: (layer, 0, 0, j)),
                  pl.BlockSpec((N_BRANCH, tn), lambda i, j: (0, j))],
        out_specs=pl.BlockSpec((tm, tn), lambda i, j: (i, j)),
        out_shape=jax.ShapeDtypeStruct((n, d), BF16),
        scratch_shapes=[pltpu.VMEM((tm, wdt), BF16)],
        compiler_params=_cparams(("parallel", "arbitrary")),
        name="merge",
    )(ya, yb, yc, proj, proj, proj, w_glu_all, b_glu, w_branch_all, b_gate)


def _outproj_kernel(m_ref, w_ref, xs_ref, nw_ref, o_ref):
    mix = jnp.dot(m_ref[...], w_ref[...], preferred_element_type=F32)
    o_ref[...] = xs_ref[...] + _rms(mix, nw_ref[...], 1e-6)


def _outproj(merged, w_out_all, layer, xs, nw, tm):
    n, d = xs.shape
    return pl.pallas_call(
        _outproj_kernel,
        grid=(n // tm,),
        in_specs=[pl.BlockSpec((tm, d), lambda i: (i, 0)),
                  pl.BlockSpec((None, d, d), lambda i: (layer, 0, 0)),
                  pl.BlockSpec((tm, d), lambda i: (i, 0)),
                  pl.BlockSpec((1, d), lambda i: (0, 0))],
        out_specs=pl.BlockSpec((tm, d), lambda i: (i, 0)),
        out_shape=jax.ShapeDtypeStruct((n, d), F32),
        compiler_params=_cparams(("parallel",)),
        name="outproj_residual",
    )(merged, w_out_all, xs, nw)


def _ffn_kernel(xs_ref, nw_in_ref, wg_ref, wu_ref, wo_ref, nw_out_ref, o_ref, h_scr, acc_scr):
    f = pl.program_id(1)

    @pl.when(f == 0)
    def _():
        h_scr[...] = _rms(xs_ref[...], nw_in_ref[...], 1e-6).astype(BF16)
        acc_scr[...] = jnp.zeros(acc_scr.shape, F32)

    h = h_scr[...]
    gate = jnp.dot(h, wg_ref[...], preferred_element_type=F32)
    up = jnp.dot(h, wu_ref[...], preferred_element_type=F32)
    act = (gate * _sigmoid(gate) * up).astype(BF16)
    acc_scr[...] += jnp.dot(act, wo_ref[...], preferred_element_type=F32)

    @pl.when(f == pl.num_programs(1) - 1)
    def _():
        o_ref[...] = xs_ref[...] + _rms(acc_scr[...], nw_out_ref[...], 1e-6)


def _ffn(xs, nw_in, w_ffn_in_all, w_ffn_out_all, nw_out, layer, tm, tf):
    n, d = xs.shape
    d_ff = w_ffn_out_all.shape[1]
    nf = d_ff // tf
    return pl.pallas_call(
        _ffn_kernel,
        grid=(n // tm, nf),
        in_specs=[pl.BlockSpec((tm, d), lambda i, f: (i, 0)),
                  pl.BlockSpec((1, d), lambda i, f: (0, 0)),
                  pl.BlockSpec((None, d, tf), lambda i, f: (layer, 0, f)),
                  pl.BlockSpec((None, d, tf), lambda i, f: (layer, 0, nf + f)),
                  pl.BlockSpec((None, tf, d), lambda i, f: (layer, f, 0)),
                  pl.BlockSpec((1, d), lambda i, f: (0, 0))],
        out_specs=pl.BlockSpec((tm, d), lambda i, f: (i, 0)),
        out_shape=jax.ShapeDtypeStruct((n, d), F32),
        scratch_shapes=[pltpu.VMEM((tm, d), BF16), pltpu.VMEM((tm, d), F32)],
        compiler_params=_cparams(("parallel", "arbitrary")),
        name="swiglu_ffn",
    )(xs, nw_in, w_ffn_in_all, w_ffn_in_all, w_ffn_out_all, nw_out)


def _s5_tables(lam_re, lam_im, b_re, b_im, c_re, c_im, log_step):
    hp = lax.Precision.HIGHEST
    groups = lam_re.shape[0]
    nblk = groups // S5_GB
    wide = S5_L * LANES
    lr, li = lam_re.astype(F32), lam_im.astype(F32)
    step = jnp.exp(log_step.astype(F32))[:, None]
    mag = jnp.exp(lr * step)
    ab_re, ab_im = mag * jnp.cos(li * step), mag * jnp.sin(li * step)
    den = lr * lr + li * li
    coef_re = ((ab_re - 1.0) * lr + ab_im * li) / den
    coef_im = (ab_im * lr - (ab_re - 1.0) * li) / den
    br, bi = b_re.astype(F32), b_im.astype(F32)
    bb_re = coef_re[..., None] * br - coef_im[..., None] * bi
    bb_im = coef_re[..., None] * bi + coef_im[..., None] * br
    cr, ci = c_re.astype(F32), c_im.astype(F32)
    eye = jnp.eye(S5_GB, dtype=F32)
    taus = jnp.arange(S5_L)
    lrs = (lr * step).reshape(nblk, 1, S5_HALF)
    lis = (li * step).reshape(nblk, 1, S5_HALF)

    def lam_pow(n):
        nn = n.astype(F32)[None, :, None]
        m = jnp.exp(nn * lrs)
        return m * jnp.cos(nn * lis), m * jnp.sin(nn * lis)

    def in_blockdiag(t):
        t = t.reshape(nblk, S5_GB, S5_STATE, S5_GROUP)
        return jnp.einsum("ngpc,gh->ngchp", t, eye).reshape(nblk, LANES, S5_HALF)

    def out_blockdiag(t):
        t = t.reshape(nblk, S5_GB, S5_GROUP, S5_STATE)
        return jnp.einsum("ngcp,gh->ngphc", t, eye).reshape(nblk, S5_HALF, LANES)

    bd_re, bd_im = in_blockdiag(bb_re)[:, None], in_blockdiag(bb_im)[:, None]
    ct_re, ct_im = out_blockdiag(cr)[:, :, None, :], out_blockdiag(ci)[:, :, None, :]

    qr, qi = lam_pow(S5_L - 1 - taus)
    qr, qi = qr[:, :, None, :], qi[:, :, None, :]
    wg = jnp.concatenate([qr * bd_re - qi * bd_im, qr * bd_im + qi * bd_re], axis=3)
    wg = wg.reshape(nblk, wide, 2 * S5_HALF).astype(BF16)

    p1r, p1i = lam_pow(1 + taus)
    p1r, p1i = p1r.transpose(0, 2, 1)[..., None], p1i.transpose(0, 2, 1)[..., None]
    wc = jnp.concatenate([ct_re * p1r - ct_im * p1i, -(ct_re * p1i + ct_im * p1r)], axis=1)
    wc = wc.reshape(nblk, 2 * S5_HALF, wide).astype(BF16)

    nn = taus.astype(F32)[:, None, None]
    mg = jnp.exp(nn * (lr * step)[None])
    pr, pi = mg * jnp.cos(nn * (li * step)[None]), mg * jnp.sin(nn * (li * step)[None])
    c0_re = cr[None] * pr[:, :, None, :] - ci[None] * pi[:, :, None, :]
    c0_im = cr[None] * pi[:, :, None, :] + ci[None] * pr[:, :, None, :]
    kt = (jnp.einsum("tgop,gpi->tgio", c0_re, bb_re, precision=hp)
          - jnp.einsum("tgop,gpi->tgio", c0_im, bb_im, precision=hp))
    kt = kt.reshape(S5_L, nblk, S5_GB, S5_GROUP, S5_GROUP)
    kblk = jnp.einsum("tngio,gh->ntgiho", kt, eye).reshape(nblk, S5_L, LANES, LANES)
    lag = taus[None, :] - taus[:, None]
    ksr = jnp.take(kblk, jnp.maximum(lag, 0).reshape(-1), axis=1).reshape(nblk, S5_L, S5_L, LANES, LANES)
    ksr = jnp.where((lag >= 0)[None, :, :, None, None], ksr, 0.0)
    wk = ksr.transpose(0, 1, 3, 2, 4).reshape(nblk, wide, wide).astype(BF16)

    n_list = S5_L * jnp.concatenate([jnp.array([1, 2, 4, 0, 0, 0, 0, 0]), 1 + jnp.arange(8)])
    wr, wi = lam_pow(n_list)
    pw = jnp.concatenate([wr, wi], axis=2)
    return wg, wk, wc, pw


def kernel(x, meta, rel_bias, norm_w, w_in, conv_w, conv_b, lru_w_a, lru_b_a, lru_w_x, lru_b_x, lru_lambda, da_lambda, da_subln, s5_lam_re, s5_lam_im, s5_b_re, s5_b_im, s5_c_re, s5_c_im, s5_d, s5_log_step, s5_w_glu, s5_b_glu, b_gate, w_branch, w_out, w_ffn_in, w_ffn_out):
    bsz, seq, d_model = x.shape
    depth = w_in.shape[0]
    lru_w = conv_w.shape[2]
    s5_w = s5_d.shape[1]
    qk_w = DA_HEADS * 2 * DA_HEAD_DIM
    assert lru_w == LRU_BLOCKS * LANES and s5_w % LANES == 0 and d_model % LANES == 0
    col_gate, col_x = 0, lru_w // LANES
    col_q = 2 * lru_w // LANES
    col_k = col_q + qk_w // LANES
    col_v = col_k + qk_w // LANES
    col_u = col_v + DA_HEADS
    col_g = (col_u + s5_w // LANES) * LANES

    t_real = N_META + seq
    tb = 384 if t_real >= 1024 else 128
    t_pad = -(-t_real // tb) * tb
    assert t_pad % SCAN_CH == 0
    n_tok = bsz * t_pad
    tm_in = _pick_tile(n_tok, (1408, 1056, 768, 512, 384, 256, 128))
    tm_merge = _pick_tile(n_tok, (1056, 768, 512, 384, 256, 128))
    tm_ffn = _pick_tile(n_tok, (768, 512, 384, 256, 128))
    tm_out = _pick_tile(n_tok, (512, 384, 256, 128))

    xs = jnp.concatenate([jnp.broadcast_to(meta.astype(F32)[None], (bsz, N_META, d_model)), x,
                          jnp.zeros((bsz, t_pad - t_real, d_model), F32)], axis=1)
    xs = xs.reshape(n_tok, d_model)
    btiles = _bias_tiles(rel_bias.astype(F32), tb)

    w_in_b, w_glu_b, w_branch_b = w_in.astype(BF16), s5_w_glu.astype(BF16), w_branch.astype(BF16)
    w_out_b, w_ffn_in_b, w_ffn_out_b = w_out.astype(BF16), w_ffn_in.astype(BF16), w_ffn_out.astype(BF16)

    for l in range(depth):
        lam_init = 0.8 - 0.6 * math.exp(-0.3 * l)
        proj = _inproj(xs, norm_w[l, 0][None], w_in_b, l, tm_in, 1024)
        proj3 = proj.reshape(bsz, t_pad, proj.shape[1])

        lru_p = jnp.concatenate([conv_w[l], conv_b[l][None], lru_b_a[l][None], lru_b_x[l][None],
                                 jax.nn.log_sigmoid(lru_lambda[l].astype(F32))[None]], axis=0)
        w_ax = jnp.concatenate([lru_w_a[l], lru_w_x[l]], axis=2).astype(BF16)
        y_a = _lru(proj3, lru_p, w_ax, col_gate, col_x)

        y_b = _attn(proj3, btiles, da_lambda[l], da_subln[l][None], lam_init, tb, col_q, col_k, col_v)

        wg, wk, wc, pw = _s5_tables(s5_lam_re[l], s5_lam_im[l], s5_b_re[l], s5_b_im[l],
                                    s5_c_re[l], s5_c_im[l], s5_log_step[l])
        y_c = _s5(proj3, wg, wk, wc, pw, s5_d[l][None], col_u)

        merged = _merge(y_a.reshape(n_tok, lru_w), y_b.reshape(n_tok, -1), y_c.reshape(n_tok, s5_w),
                        proj, w_glu_b, s5_b_glu[l][None], w_branch_b, b_gate[l], col_g, l, tm_merge, 512)
        xs = _outproj(merged, w_out_b, l, xs, norm_w[l, 1][None], tm_out)
        xs = _ffn(xs, norm_w[l, 2][None], w_ffn_in_b, w_ffn_out_b, norm_w[l, 3][None], l, tm_ffn, 512)

    return xs.reshape(bsz, t_pad, d_model)[:, N_META:t_real]
```

```python
import functools
import math

import jax
import jax.numpy as jnp
from jax import lax
from jax.experimental import pallas as pl
from jax.experimental.pallas import tpu as pltpu

F32 = jnp.float32
BF16 = jnp.bfloat16

N_META = 16
CONV_W = 4
LRU_C = 8.0
LRU_BLOCKS = 8
DA_HEADS = 8
DA_HEAD_DIM = 64
S5_GROUP = 16
S5_STATE = 64
REL_BUCKETS = 32
REL_MAX_DIST = 128
N_BRANCH = 3

LANES = 128
S5_GB = LANES // S5_GROUP
S5_HALF = S5_GB * S5_STATE
S5_L = 8
SCAN_CH = 128
MASK_NEG = -1e30
LOG2E = math.log2(math.e)
VMEM_LIMIT = 56 * 1024 * 1024


def _cparams(sem):
    return pltpu.CompilerParams(dimension_semantics=sem, vmem_limit_bytes=VMEM_LIMIT)


def _pick_tile(n, candidates):
    for c in candidates:
        if n % c == 0:
            return c
    raise ValueError(f"no tile in {candidates} divides {n}")


def _gelu_tanh(x):
    return 0.5 * x * (1.0 + jnp.tanh(math.sqrt(2.0 / math.pi) * (x + 0.044715 * (x * x * x))))


def _sigmoid(x):
    return 1.0 / (1.0 + jnp.exp(-x))


def _rms(x, w, eps):
    return (x * lax.rsqrt(jnp.mean(x * x, axis=-1, keepdims=True) + eps)) * w


def _inproj_kernel(x_ref, nw_ref, w_ref, o_ref, h_scr):
    @pl.when(pl.program_id(1) == 0)
    def _():
        h_scr[...] = _rms(x_ref[...], nw_ref[...], 1e-6).astype(BF16)

    o_ref[...] = jnp.dot(h_scr[...], w_ref[...], preferred_element_type=F32).astype(o_ref.dtype)


def _inproj(xs, nw, w_all, layer, tm, tn):
    n, d = xs.shape
    n_out = w_all.shape[2]
    return pl.pallas_call(
        _inproj_kernel,
        grid=(n // tm, n_out // tn),
        in_specs=[pl.BlockSpec((tm, d), lambda i, j: (i, 0)),
                  pl.BlockSpec((1, d), lambda i, j: (0, 0)),
                  pl.BlockSpec((None, d, tn), lambda i, j: (layer, 0, j))],
        out_specs=pl.BlockSpec((tm, tn), lambda i, j: (i, j)),
        out_shape=jax.ShapeDtypeStruct((n, n_out), BF16),
        scratch_shapes=[pltpu.VMEM((tm, d), BF16)],
        compiler_params=_cparams(("parallel", "arbitrary")),
        name="inproj",
    )(xs, nw, w_all)


def _lru_kernel(g_ref, x_ref, p_ref, w_ref, o_ref, xs_scr):
    t_len = x_ref.shape[1]
    halo = 8
    xs_scr[0:halo, :] = jnp.zeros((halo, LANES), F32)
    xs_scr[halo:, :] = x_ref[0].astype(F32)
    p = p_ref[...]
    cw = [p[j:j + 1, :] for j in range(CONV_W)]
    cb, ba, bx, logsig = p[4:5, :], p[5:6, :], p[6:7, :], p[7:8, :]
    w = w_ref[0]
    row = lax.broadcasted_iota(jnp.int32, (SCAN_CH, LANES), 0)

    def local_scan(base):
        xc = cb
        for j in range(CONV_W):
            xc = xc + cw[j] * xs_scr[pl.ds(base + halo - (CONV_W - 1) + j, SCAN_CH), :]
        ri = jnp.dot(xc.astype(BF16), w, preferred_element_type=F32)
        r = _sigmoid(ri[:, :LANES] + ba)
        i = _sigmoid(ri[:, LANES:] + bx)
        a = jnp.exp(LRU_C * r * logsig)
        b = jnp.sqrt(1.0 - a * a) * (i * xc)
        s = 1
        while s < SCAN_CH:
            keep = row >= s
            a_sh = jnp.where(keep, pltpu.roll(a, s, 0), 1.0)
            b_sh = jnp.where(keep, pltpu.roll(b, s, 0), 0.0)
            b = a * b_sh + b
            a = a * a_sh
            s *= 2
        return a, b

    n_chunks = t_len // SCAN_CH
    sub = 3 if n_chunks % 3 == 0 else 1

    def chunk(c, h0):
        bases = [pl.multiple_of((c * sub + j) * SCAN_CH, SCAN_CH) for j in range(sub)]
        scans = [local_scan(base) for base in bases]
        for base, (a, b) in zip(bases, scans):
            h = a * h0 + b
            g = g_ref[0, pl.ds(base, SCAN_CH), :].astype(F32)
            o_ref[0, pl.ds(base, SCAN_CH), :] = (h * _gelu_tanh(g)).astype(o_ref.dtype)
            h0 = h[SCAN_CH - 1:SCAN_CH, :]
        return h0

    lax.fori_loop(0, n_chunks // sub, chunk, jnp.zeros((1, LANES), F32))


def _lru(proj3, lru_p, w_ax, col_gate, col_x):
    b, t, _ = proj3.shape
    width = LRU_BLOCKS * LANES
    return pl.pallas_call(
        _lru_kernel,
        grid=(b, LRU_BLOCKS),
        in_specs=[pl.BlockSpec((1, t, LANES), lambda bi, h: (bi, 0, col_gate + h)),
                  pl.BlockSpec((1, t, LANES), lambda bi, h: (bi, 0, col_x + h)),
                  pl.BlockSpec((8, LANES), lambda bi, h: (0, h)),
                  pl.BlockSpec((1, LANES, 2 * LANES), lambda bi, h: (h, 0, 0))],
        out_specs=pl.BlockSpec((1, t, LANES), lambda bi, h: (bi, 0, h)),
        out_shape=jax.ShapeDtypeStruct((b, t, width), BF16),
        scratch_shapes=[pltpu.VMEM((t + 8, LANES), F32)],
        compiler_params=_cparams(("parallel", "parallel")),
        name="rglru",
    )(proj3, proj3, lru_p, w_ax)


def _bias_tile_kernel(rb_ref, o_ref):
    h = pl.program_id(0)
    tb = o_ref.shape[1]
    i = lax.broadcasted_iota(jnp.int32, (tb, 2 * tb), 0)
    j2 = lax.broadcasted_iota(jnp.int32, (tb, 2 * tb), 1)
    max_exact = REL_BUCKETS // 2
    d = i - j2 + tb
    n = jnp.maximum(d, 0)
    nf = jnp.maximum(n, 1).astype(F32)
    large = max_exact + (jnp.log(nf / max_exact) / math.log(REL_MAX_DIST / max_exact)
                         * (REL_BUCKETS - max_exact)).astype(jnp.int32)
    large = jnp.minimum(large, REL_BUCKETS - 1)
    bucket = jnp.where(n < max_exact, n, large)
    val = jnp.zeros((tb, 2 * tb), F32)
    for bkt in range(REL_BUCKETS):
        val = jnp.where(bucket == bkt, rb_ref[bkt, h], val)
    val = val - rb_ref[REL_BUCKETS - 1, h]
    o_ref[0] = jnp.where(d >= 0, val, MASK_NEG)


def _bias_tiles(rel_bias, tb):
    return pl.pallas_call(
        _bias_tile_kernel,
        grid=(DA_HEADS,),
        in_specs=[pl.BlockSpec(memory_space=pltpu.SMEM)],
        out_specs=pl.BlockSpec((1, tb, 2 * tb), lambda h: (h, 0, 0)),
        out_shape=jax.ShapeDtypeStruct((DA_HEADS, tb, 2 * tb), F32),
        compiler_params=_cparams(("parallel",)),
        name="t5_bias_tiles",
    )(rel_bias)


def _attn_kernel(q_ref, k_ref, v_ref, bt_ref, dl_ref, sw_ref, o_ref, m_scr, l_scr, acc_scr, s_scr, *,
                 lam_init):
    qi = pl.program_id(2)
    tb = q_ref.shape[1]
    dn_t = (((1,), (1,)), ((), ()))
    lane = lax.broadcasted_iota(jnp.int32, (tb, LANES), 1)
    qs = q_ref[0] * (DA_HEAD_DIM ** -0.5)
    zero = jnp.zeros_like(qs)
    qcat = jnp.concatenate([jnp.where(lane < DA_HEAD_DIM, qs, zero),
                            jnp.where(lane >= DA_HEAD_DIM, qs, zero)], axis=0)
    n_far = jnp.maximum(qi - 1, 0)
    n_quads = n_far // 4
    n_rest = n_far % 4
    tail_blk = 4 * n_quads

    def scores(blk, nblocks, biased):
        kb = k_ref[0, pl.ds(pl.multiple_of(blk * tb, tb), nblocks * tb), :]
        s_both = lax.dot_general(qcat, kb, dn_t, preferred_element_type=F32)
        out = []
        for c in range(2):
            s = s_both[c * tb:(c + 1) * tb]
            if biased:
                nb = min(nblocks, 2)
                plain = (nblocks - nb) * tb
                tail = s[:, plain:] + bt_ref[0, :, (2 - nb) * tb:]
                s = tail if plain == 0 else jnp.concatenate([s[:, :plain], tail], axis=1)
            s = s * LOG2E
            for i in range(nblocks):
                s_scr[c, blk + i] = s[:, i * tb:(i + 1) * tb]
            out.append(s)
        return out

    def lane_chunks(s):
        return [s[:, j * LANES:(j + 1) * LANES] for j in range(s.shape[1] // LANES)]

    def max_into(ms, ss):
        out = []
        for m, s in zip(ms, ss):
            for ch in lane_chunks(s):
                m = jnp.maximum(m, ch)
            out.append(m)
        return out

    m_init = jnp.full((tb, LANES), MASK_NEG, F32)

    @pl.when(qi == 0)
    def _():
        m_scr[0], m_scr[1] = max_into((m_init, m_init), scores(0, 1, True))

    for rest in range(4):
        @pl.when(jnp.logical_and(qi >= 1, n_rest == rest))
        def _():
            m_scr[0], m_scr[1] = max_into((m_init, m_init), scores(tail_blk, rest + 2, True))

    def quad_max(ki, ms):
        return tuple(max_into(ms, scores(4 * ki, 4, False)))

    ms = lax.fori_loop(0, n_quads, quad_max, (m_scr[0], m_scr[1]))
    mb = [jnp.broadcast_to(jnp.max(ms[c], axis=-1, keepdims=True), (tb, LANES)) for c in range(2)]

    def accumulate(blk, nblocks, st):
        vb = v_ref[0, pl.ds(pl.multiple_of(blk * tb, tb), nblocks * tb), :]
        out = []
        for c in range(2):
            ps = []
            for i in range(nblocks):
                ps += [jnp.exp2(ch - mb[c]) for ch in lane_chunks(s_scr[c, blk + i])]
            l = st[2 * c]
            for ch in ps:
                l = l + ch
            p = jnp.concatenate(ps, axis=1).astype(BF16)
            out += [l, st[2 * c + 1] + jnp.dot(p, vb, preferred_element_type=F32)]
        return tuple(out)

    def store_state(st):
        l_scr[0], acc_scr[0], l_scr[1], acc_scr[1] = st

    zl = jnp.zeros((tb, LANES), F32)

    @pl.when(qi == 0)
    def _():
        store_state(accumulate(0, 1, (zl, zl, zl, zl)))

    for rest in range(4):
        @pl.when(jnp.logical_and(qi >= 1, n_rest == rest))
        def _():
            store_state(accumulate(tail_blk, rest + 2, (zl, zl, zl, zl)))

    l0, a0, l1, a1 = lax.fori_loop(0, n_quads, lambda ki, st: accumulate(4 * ki, 4, st),
                                   (l_scr[0], acc_scr[0], l_scr[1], acc_scr[1]))

    dl = dl_ref[...]
    lam = (jnp.exp(jnp.sum(dl[0:1, :] * dl[1:2, :], axis=-1, keepdims=True))
           - jnp.exp(jnp.sum(dl[2:3, :] * dl[3:4, :], axis=-1, keepdims=True)) + lam_init)
    o = (a0 / jnp.sum(l0, axis=-1, keepdims=True)
         - lam * (a1 / jnp.sum(l1, axis=-1, keepdims=True)))
    o = _rms(o, sw_ref[...], 1e-5) * (1.0 - lam_init)
    o_ref[0] = o.astype(o_ref.dtype)


def _attn(proj3, btiles, da_lambda, da_subln, lam_init, tb, col_q, col_k, col_v):
    b, t, _ = proj3.shape
    width = DA_HEADS * LANES
    return pl.pallas_call(
        functools.partial(_attn_kernel, lam_init=lam_init),
        grid=(b, DA_HEADS, t // tb),
        in_specs=[pl.BlockSpec((1, tb, LANES), lambda bi, h, qi: (bi, qi, col_q + h)),
                  pl.BlockSpec((1, t, LANES), lambda bi, h, qi: (bi, 0, col_k + h)),
                  pl.BlockSpec((1, t, LANES), lambda bi, h, qi: (bi, 0, col_v + h)),
                  pl.BlockSpec((1, tb, 2 * tb), lambda bi, h, qi: (h, 0, 0)),
                  pl.BlockSpec((4, DA_HEAD_DIM), lambda bi, h, qi: (0, 0)),
                  pl.BlockSpec((1, LANES), lambda bi, h, qi: (0, 0))],
        out_specs=pl.BlockSpec((1, tb, LANES), lambda bi, h, qi: (bi, qi, h)),
        out_shape=jax.ShapeDtypeStruct((b, t, width), BF16),
        scratch_shapes=[pltpu.VMEM((2, tb, LANES), F32), pltpu.VMEM((2, tb, LANES), F32),
                        pltpu.VMEM((2, tb, LANES), F32), pltpu.VMEM((2, t // tb, tb, tb), F32)],
        compiler_params=_cparams(("parallel", "parallel", "arbitrary")),
        name="diff_attn",
    )(proj3, proj3, proj3, btiles, da_lambda, da_subln)


def _s5_kernel(u_ref, wg_ref, wk_ref, wc_ref, pw_ref, d_ref, o_ref, uf_scr, g_scr, hp_scr, y_scr):
    t_len = u_ref.shape[1]
    nj = t_len // S5_L
    uf_scr[...] = u_ref[0].astype(F32)
    ur = jnp.concatenate([uf_scr[pl.ds(s, nj, stride=S5_L), :].astype(BF16) for s in range(S5_L)],
                         axis=1)
    g_scr[...] = jnp.dot(ur, wg_ref[0], preferred_element_type=F32)

    pw = pw_ref[0]
    pr, pi = pw[8:16, :S5_HALF], pw[8:16, S5_HALF:]
    row = lax.broadcasted_iota(jnp.int32, (8, S5_HALF), 0)

    def local_scan(base):
        x = g_scr[pl.ds(base, 8), :]
        xr, xi = x[:, :S5_HALF], x[:, S5_HALF:]
        for lvl, s in enumerate((1, 2, 4)):
            ar, ai = pw[lvl:lvl + 1, :S5_HALF], pw[lvl:lvl + 1, S5_HALF:]
            keep = row >= s
            sr = jnp.where(keep, pltpu.roll(xr, s, 0), 0.0)
            si = jnp.where(keep, pltpu.roll(xi, s, 0), 0.0)
            xr, xi = xr + (ar * sr - ai * si), xi + (ar * si + ai * sr)
        return xr, xi

    first = row == 0
    per_iter = 2

    def groups(gi, carry):
        cr, ci = carry
        bases = [pl.multiple_of((gi * per_iter + k) * 8, 8) for k in range(per_iter)]
        scans = [local_scan(base) for base in bases]
        for base, (xr, xi) in zip(bases, scans):
            xr, xi = xr + (pr * cr - pi * ci), xi + (pr * ci + pi * cr)
            hp_scr[pl.ds(base, 8), :] = jnp.concatenate(
                [jnp.where(first, cr, pltpu.roll(xr, 1, 0)), jnp.where(first, ci, pltpu.roll(xi, 1, 0))],
                axis=1)
            cr, ci = xr[7:8, :], xi[7:8, :]
        return cr, ci

    z = jnp.zeros((1, S5_HALF), F32)
    lax.fori_loop(0, nj // (8 * per_iter), groups, (z, z))

    y = (jnp.dot(ur, wk_ref[0], preferred_element_type=F32)
         + jnp.dot(hp_scr[...].astype(BF16), wc_ref[0], preferred_element_type=F32))
    for s in range(S5_L):
        y_scr[pl.ds(s, nj, stride=S5_L), :] = y[:, s * LANES:(s + 1) * LANES]
    o_ref[0] = _gelu_tanh(y_scr[...] + d_ref[...] * uf_scr[...]).astype(o_ref.dtype)


def _s5(proj3, wg, wk, wc, pw, dvec, col_u):
    b, t, _ = proj3.shape
    nblk = wg.shape[0]
    nj = t // S5_L
    wide = S5_L * LANES
    assert nj % 16 == 0

    def wspec(rows, cols):
        return pl.BlockSpec((1, rows, cols), lambda g, bi: (g, 0, 0))

    return pl.pallas_call(
        _s5_kernel,
        grid=(nblk, b),
        in_specs=[pl.BlockSpec((1, t, LANES), lambda g, bi: (bi, 0, col_u + g)),
                  wspec(wide, 2 * S5_HALF), wspec(wide, wide), wspec(2 * S5_HALF, wide),
                  wspec(16, 2 * S5_HALF),
                  pl.BlockSpec((1, LANES), lambda g, bi: (0, g))],
        out_specs=pl.BlockSpec((1, t, LANES), lambda g, bi: (bi, 0, g)),
        out_shape=jax.ShapeDtypeStruct((b, t, nblk * LANES), BF16),
        scratch_shapes=[pltpu.VMEM((t, LANES), F32), pltpu.VMEM((nj, 2 * S5_HALF), F32),
                        pltpu.VMEM((nj, 2 * S5_HALF), F32), pltpu.VMEM((t, LANES), F32)],
        compiler_params=_cparams(("parallel", "parallel")),
        name="s5",
    )(proj3, wg, wk, wc, pw, dvec)


def _merge_kernel(ya_ref, yb_ref, yc_ref, g0_ref, g1_ref, g2_ref, wg_ref, bg_ref, wb_ref, bgate_ref,
                  o_ref, yc_scr):
    @pl.when(pl.program_id(1) == 0)
    def _():
        yc = yc_ref[...]
        z = jnp.dot(yc, wg_ref[...], preferred_element_type=F32) + bg_ref[...]
        yc_scr[...] = (yc.astype(F32) * _sigmoid(z)).astype(BF16)

    bgate = bgate_ref[...]
    ys = (ya_ref[...], yb_ref[...], yc_scr[...])
    gs = (g0_ref, g1_ref, g2_ref)
    merged = None
    for br in range(N_BRANCH):
        gate = _sigmoid(gs[br][...].astype(F32) + bgate[br:br + 1, :])
        term = gate * jnp.dot(ys[br], wb_ref[br], preferred_element_type=F32)
        merged = term if merged is None else merged + term
    o_ref[...] = merged.astype(o_ref.dtype)


def _merge(ya, yb, yc, proj, w_glu_all, b_glu, w_branch_all, b_gate, col_g, layer, tm, tn):
    n, wdt = ya.shape
    d = w_branch_all.shape[3]
    gcol = [(col_g + br * d) // tn for br in range(N_BRANCH)]
    yspec = pl.BlockSpec((tm, wdt), lambda i, j: (i, 0))

    def gspec(br):
        return pl.BlockSpec((tm, tn), lambda i, j: (i, gcol[br] + j))

    return pl.pallas_call(
        _merge_kernel,
        grid=(n // tm, d // tn),
        in_specs=[yspec, yspec, yspec, gspec(0), gspec(1), gspec(2),
                  pl.BlockSpec((None, wdt, wdt), lambda i, j: (layer, 0, 0)),
                  pl.BlockSpec((1, wdt), lambda i, j: (0, 0)),
                  pl.BlockSpec((None, N_BRANCH, wdt, tn), lambda i, j: (layer, 0, 0, j)),
                  pl.BlockSpec((N_BRANCH, tn), lambda i, j: (0, j))],
        out_specs=pl.BlockSpec((tm, tn), lambda i, j: (i, j)),
        out_shape=jax.ShapeDtypeStruct((n, d), BF16),
        scratch_shapes=[pltpu.VMEM((tm, wdt), BF16)],
        compiler_params=_cparams(("parallel", "arbitrary")),
        name="merge",
    )(ya, yb, yc, proj, proj, proj, w_glu_all, b_glu, w_branch_all, b_gate)


def _outproj_kernel(m_ref, w_ref, xs_ref, nw_ref, o_ref):
    mix = jnp.dot(m_ref[...], w_ref[...], preferred_element_type=F32)
    o_ref[...] = xs_ref[...] + _rms(mix, nw_ref[...], 1e-6)


def _outproj(merged, w_out_all, layer, xs, nw, tm):
    n, d = xs.shape
    return pl.pallas_call(
        _outproj_kernel,
        grid=(n // tm,),
        in_specs=[pl.BlockSpec((tm, d), lambda i: (i, 0)),
                  pl.BlockSpec((None, d, d), lambda i: (layer, 0, 0)),
                  pl.BlockSpec((tm, d), lambda i: (i, 0)),
                  pl.BlockSpec((1, d), lambda i: (0, 0))],
        out_specs=pl.BlockSpec((tm, d), lambda i: (i, 0)),
        out_shape=jax.ShapeDtypeStruct((n, d), F32),
        compiler_params=_cparams(("parallel",)),
        name="outproj_residual",
    )(merged, w_out_all, xs, nw)


def _ffn_kernel(xs_ref, nw_in_ref, wg_ref, wu_ref, wo_ref, nw_out_ref, o_ref, h_scr, acc_scr):
    f = pl.program_id(1)

    @pl.when(f == 0)
    def _():
        h_scr[...] = _rms(xs_ref[...], nw_in_ref[...], 1e-6).astype(BF16)
        acc_scr[...] = jnp.zeros(acc_scr.shape, F32)

    h = h_scr[...]
    gate = jnp.dot(h, wg_ref[...], preferred_element_type=F32)
    up = jnp.dot(h, wu_ref[...], preferred_element_type=F32)
    act = (gate * _sigmoid(gate) * up).astype(BF16)
    acc_scr[...] += jnp.dot(act, wo_ref[...], preferred_element_type=F32)

    @pl.when(f == pl.num_programs(1) - 1)
    def _():
        o_ref[...] = xs_ref[...] + _rms(acc_scr[...], nw_out_ref[...], 1e-6)


def _ffn(xs, nw_in, w_ffn_in_all, w_ffn_out_all, nw_out, layer, tm, tf):
    n, d = xs.shape
    d_ff = w_ffn_out_all.shape[1]
    nf = d_ff // tf
    return pl.pallas_call(
        _ffn_kernel,
        grid=(n // tm, nf),
        in_specs=[pl.BlockSpec((tm, d), lambda i, f: (i, 0)),
                  pl.BlockSpec((1, d), lambda i, f: (0, 0)),
                  pl.BlockSpec((None, d, tf), lambda i, f: (layer, 0, f)),
                  pl.BlockSpec((None, d, tf), lambda i, f: (layer, 0, nf + f)),
                  pl.BlockSpec((None, tf, d), lambda i, f: (layer, f, 0)),
                  pl.BlockSpec((1, d), lambda i, f: (0, 0))],
        out_specs=pl.BlockSpec((tm, d), lambda i, f: (i, 0)),
        out_shape=jax.ShapeDtypeStruct((n, d), F32),
        scratch_shapes=[pltpu.VMEM((tm, d), BF16), pltpu.VMEM((tm, d), F32)],
        compiler_params=_cparams(("parallel", "arbitrary")),
        name="swiglu_ffn",
    )(xs, nw_in, w_ffn_in_all, w_ffn_in_all, w_ffn_out_all, nw_out)


def _s5_tables(lam_re, lam_im, b_re, b_im, c_re, c_im, log_step):
    hp = lax.Precision.HIGHEST
    groups = lam_re.shape[0]
    nblk = groups // S5_GB
    wide = S5_L * LANES
    lr, li = lam_re.astype(F32), lam_im.astype(F32)
    step = jnp.exp(log_step.astype(F32))[:, None]
    mag = jnp.exp(lr * step)
    ab_re, ab_im = mag * jnp.cos(li * step), mag * jnp.sin(li * step)
    den = lr * lr + li * li
    coef_re = ((ab_re - 1.0) * lr + ab_im * li) / den
    coef_im = (ab_im * lr - (ab_re - 1.0) * li) / den
    br, bi = b_re.astype(F32), b_im.astype(F32)
    bb_re = coef_re[..., None] * br - coef_im[..., None] * bi
    bb_im = coef_re[..., None] * bi + coef_im[..., None] * br
    cr, ci = c_re.astype(F32), c_im.astype(F32)
    eye = jnp.eye(S5_GB, dtype=F32)
    taus = jnp.arange(S5_L)
    lrs = (lr * step).reshape(nblk, 1, S5_HALF)
    lis = (li * step).reshape(nblk, 1, S5_HALF)

    def lam_pow(n):
        nn = n.astype(F32)[None, :, None]
        m = jnp.exp(nn * lrs)
        return m * jnp.cos(nn * lis), m * jnp.sin(nn * lis)

    def in_blockdiag(t):
        t = t.reshape(nblk, S5_GB, S5_STATE, S5_GROUP)
        return jnp.einsum("ngpc,gh->ngchp", t, eye).reshape(nblk, LANES, S5_HALF)

    def out_blockdiag(t):
        t = t.reshape(nblk, S5_GB, S5_GROUP, S5_STATE)
        return jnp.einsum("ngcp,gh->ngphc", t, eye).reshape(nblk, S5_HALF, LANES)

    bd_re, bd_im = in_blockdiag(bb_re)[:, None], in_blockdiag(bb_im)[:, None]
    ct_re, ct_im = out_blockdiag(cr)[:, :, None, :], out_blockdiag(ci)[:, :, None, :]

    qr, qi = lam_pow(S5_L - 1 - taus)
    qr, qi = qr[:, :, None, :], qi[:, :, None, :]
    wg = jnp.concatenate([qr * bd_re - qi * bd_im, qr * bd_im + qi * bd_re], axis=3)
    wg = wg.reshape(nblk, wide, 2 * S5_HALF).astype(BF16)

    p1r, p1i = lam_pow(1 + taus)
    p1r, p1i = p1r.transpose(0, 2, 1)[..., None], p1i.transpose(0, 2, 1)[..., None]
    wc = jnp.concatenate([ct_re * p1r - ct_im * p1i, -(ct_re * p1i + ct_im * p1r)], axis=1)
    wc = wc.reshape(nblk, 2 * S5_HALF, wide).astype(BF16)

    nn = taus.astype(F32)[:, None, None]
    mg = jnp.exp(nn * (lr * step)[None])
    pr, pi = mg * jnp.cos(nn * (li * step)[None]), mg * jnp.sin(nn * (li * step)[None])
    c0_re = cr[None] * pr[:, :, None, :] - ci[None] * pi[:, :, None, :]
    c0_im = cr[None] * pi[:, :, None, :] + ci[None] * pr[:, :, None, :]
    kt = (jnp.einsum("tgop,gpi->tgio", c0_re, bb_re, precision=hp)
          - jnp.einsum("tgop,gpi->tgio", c0_im, bb_im, precision=hp))
    kt = kt.reshape(S5_L, nblk, S5_GB, S5_GROUP, S5_GROUP)
    kblk = jnp.einsum("tngio,gh->ntgiho", kt, eye).reshape(nblk, S5_L, LANES, LANES)
    lag = taus[None, :] - taus[:, None]
    ksr = jnp.take(kblk, jnp.maximum(lag, 0).reshape(-1), axis=1).reshape(nblk, S5_L, S5_L, LANES, LANES)
    ksr = jnp.where((lag >= 0)[None, :, :, None, None], ksr, 0.0)
    wk = ksr.transpose(0, 1, 3, 2, 4).reshape(nblk, wide, wide).astype(BF16)

    n_list = S5_L * jnp.concatenate([jnp.array([1, 2, 4, 0, 0, 0, 0, 0]), 1 + jnp.arange(8)])
    wr, wi = lam_pow(n_list)
    pw = jnp.concatenate([wr, wi], axis=2)
    return wg, wk, wc, pw


def kernel(x, meta, rel_bias, norm_w, w_in, conv_w, conv_b, lru_w_a, lru_b_a, lru_w_x, lru_b_x, lru_lambda, da_lambda, da_subln, s5_lam_re, s5_lam_im, s5_b_re, s5_b_im, s5_c_re, s5_c_im, s5_d, s5_log_step, s5_w_glu, s5_b_glu, b_gate, w_branch, w_out, w_ffn_in, w_ffn_out):
    bsz, seq, d_model = x.shape
    depth = w_in.shape[0]
    lru_w = conv_w.shape[2]
    s5_w = s5_d.shape[1]
    qk_w = DA_HEADS * 2 * DA_HEAD_DIM
    assert lru_w == LRU_BLOCKS * LANES and s5_w % LANES == 0 and d_model % LANES == 0
    col_gate, col_x = 0, lru_w // LANES
    col_q = 2 * lru_w // LANES
    col_k = col_q + qk_w // LANES
    col_v = col_k + qk_w // LANES
    col_u = col_v + DA_HEADS
    col_g = (col_u + s5_w // LANES) * LANES

    t_real = N_META + seq
    tb = 384 if t_real >= 1024 else 128
    t_pad = -(-t_real // tb) * tb
    assert t_pad % SCAN_CH == 0
    n_tok = bsz * t_pad
    tm_in = _pick_tile(n_tok, (1408, 1056, 768, 512, 384, 256, 128))
    tm_merge = _pick_tile(n_tok, (1056, 768, 512, 384, 256, 128))
    tm_ffn = _pick_tile(n_tok, (768, 512, 384, 256, 128))
    tm_out = _pick_tile(n_tok, (512, 384, 256, 128))

    xs = jnp.concatenate([jnp.broadcast_to(meta.astype(F32)[None], (bsz, N_META, d_model)), x,
                          jnp.zeros((bsz, t_pad - t_real, d_model), F32)], axis=1)
    xs = xs.reshape(n_tok, d_model)
    btiles = _bias_tiles(rel_bias.astype(F32), tb)

    w_in_b, w_glu_b, w_branch_b = w_in.astype(BF16), s5_w_glu.astype(BF16), w_branch.astype(BF16)
    w_out_b, w_ffn_in_b, w_ffn_out_b = w_out.astype(BF16), w_ffn_in.astype(BF16), w_ffn_out.astype(BF16)

    for l in range(depth):
        lam_init = 0.8 - 0.6 * math.exp(-0.3 * l)
        proj = _inproj(xs, norm_w[l, 0][None], w_in_b, l, tm_in, 1024)
        proj3 = proj.reshape(bsz, t_pad, proj.shape[1])

        lru_p = jnp.concatenate([conv_w[l], conv_b[l][None], lru_b_a[l][None], lru_b_x[l][None],
                                 jax.nn.log_sigmoid(lru_lambda[l].astype(F32))[None]], axis=0)
        w_ax = jnp.concatenate([lru_w_a[l], lru_w_x[l]], axis=2).astype(BF16)
        y_a = _lru(proj3, lru_p, w_ax, col_gate, col_x)

        y_b = _attn(proj3, btiles, da_lambda[l], da_subln[l][None], lam_init, tb, col_q, col_k, col_v)

        wg, wk, wc, pw = _s5_tables(s5_lam_re[l], s5_lam_im[l], s5_b_re[l], s5_b_im[l],
                                    s5_c_re[l], s5_c_im[l], s5_log_step[l])
        y_c = _s5(proj3, wg, wk, wc, pw, s5_d[l][None], col_u)

        merged = _merge(y_a.reshape(n_tok, lru_w), y_b.reshape(n_tok, -1), y_c.reshape(n_tok, s5_w),
                        proj, w_glu_b, s5_b_glu[l][None], w_branch_b, b_gate[l], col_g, l, tm_merge, 512)
        xs = _outproj(merged, w_out_b, l, xs, norm_w[l, 1][None], tm_out)
        xs = _ffn(xs, norm_w[l, 2][None], w_ffn_in_b, w_ffn_out_b, norm_w[l, 3][None], l, tm_ffn, 512)

    return xs.reshape(bsz, t_pad, d_model)[:, N_META:t_real]
```

```python
import functools
import math

import jax
import jax.numpy as jnp
from jax import lax
from jax.experimental import pallas as pl
from jax.experimental.pallas import tpu as pltpu

F32 = jnp.float32
BF16 = jnp.bfloat16

N_META = 16
CONV_W = 4
LRU_C = 8.0
LRU_BLOCKS = 8
DA_HEADS = 8
DA_HEAD_DIM = 64
S5_GROUP = 16
S5_STATE = 64
REL_BUCKETS = 32
REL_MAX_DIST = 128
N_BRANCH = 3

LANES = 128
S5_GB = LANES // S5_GROUP
S5_HALF = S5_GB * S5_STATE
S5_L = 8
SCAN_CH = 128
MASK_NEG = -1e30
LOG2E = math.log2(math.e)
VMEM_LIMIT = 56 * 1024 * 1024


def _cparams(sem):
    return pltpu.CompilerParams(dimension_semantics=sem, vmem_limit_bytes=VMEM_LIMIT)


def _pick_tile(n, candidates):
    for c in candidates:
        if n % c == 0:
            return c
    raise ValueError(f"no tile in {candidates} divides {n}")


def _gelu_tanh(x):
    return 0.5 * x * (1.0 + jnp.tanh(math.sqrt(2.0 / math.pi) * (x + 0.044715 * (x * x * x))))


def _sigmoid(x):
    return 1.0 / (1.0 + jnp.exp(-x))


def _rms(x, w, eps):
    return (x * lax.rsqrt(jnp.mean(x * x, axis=-1, keepdims=True) + eps)) * w


def _inproj_kernel(x_ref, nw_ref, w_ref, o_ref, h_scr):
    @pl.when(pl.program_id(1) == 0)
    def _():
        h_scr[...] = _rms(x_ref[...], nw_ref[...], 1e-6).astype(BF16)

    o_ref[...] = jnp.dot(h_scr[...], w_ref[...], preferred_element_type=F32).astype(o_ref.dtype)


def _inproj(xs, nw, w_all, layer, tm, tn):
    n, d = xs.shape
    n_out = w_all.shape[2]
    return pl.pallas_call(
        _inproj_kernel,
        grid=(n // tm, n_out // tn),
        in_specs=[pl.BlockSpec((tm, d), lambda i, j: (i, 0)),
                  pl.BlockSpec((1, d), lambda i, j: (0, 0)),
                  pl.BlockSpec((None, d, tn), lambda i, j: (layer, 0, j))],
        out_specs=pl.BlockSpec((tm, tn), lambda i, j: (i, j)),
        out_shape=jax.ShapeDtypeStruct((n, n_out), BF16),
        scratch_shapes=[pltpu.VMEM((tm, d), BF16)],
        compiler_params=_cparams(("parallel", "arbitrary")),
        name="inproj",
    )(xs, nw, w_all)


def _lru_kernel(g_ref, x_ref, p_ref, w_ref, o_ref, xs_scr):
    t_len = x_ref.shape[1]
    halo = 8
    xs_scr[0:halo, :] = jnp.zeros((halo, LANES), F32)
    xs_scr[halo:, :] = x_ref[0].astype(F32)
    p = p_ref[...]
    cw = [p[j:j + 1, :] for j in range(CONV_W)]
    cb, ba, bx, logsig = p[4:5, :], p[5:6, :], p[6:7, :], p[7:8, :]
    w = w_ref[0]
    row = lax.broadcasted_iota(jnp.int32, (SCAN_CH, LANES), 0)

    def local_scan(base):
        xc = cb
        for j in range(CONV_W):
            xc = xc + cw[j] * xs_scr[pl.ds(base + halo - (CONV_W - 1) + j, SCAN_CH), :]
        ri = jnp.dot(xc.astype(BF16), w, preferred_element_type=F32)
        r = _sigmoid(ri[:, :LANES] + ba)
        i = _sigmoid(ri[:, LANES:] + bx)
        a = jnp.exp(LRU_C * r * logsig)
        b = jnp.sqrt(1.0 - a * a) * (i * xc)
        s = 1
        while s < SCAN_CH:
            keep = row >= s
            a_sh = jnp.where(keep, pltpu.roll(a, s, 0), 1.0)
            b_sh = jnp.where(keep, pltpu.roll(b, s, 0), 0.0)
            b = a * b_sh + b
            a = a * a_sh
            s *= 2
        return a, b

    n_chunks = t_len // SCAN_CH
    sub = 3 if n_chunks % 3 == 0 else 1

    def chunk(c, h0):
        bases = [pl.multiple_of((c * sub + j) * SCAN_CH, SCAN_CH) for j in range(sub)]
        scans = [local_scan(base) for base in bases]
        for base, (a, b) in zip(bases, scans):
            h = a * h0 + b
            g = g_ref[0, pl.ds(base, SCAN_CH), :].astype(F32)
            o_ref[0, pl.ds(base, SCAN_CH), :] = (h * _gelu_tanh(g)).astype(o_ref.dtype)
            h0 = h[SCAN_CH - 1:SCAN_CH, :]
        return h0

    lax.fori_loop(0, n_chunks // sub, chunk, jnp.zeros((1, LANES), F32))


def _lru(proj3, lru_p, w_ax, col_gate, col_x):
    b, t, _ = proj3.shape
    width = LRU_BLOCKS * LANES
    return pl.pallas_call(
        _lru_kernel,
        grid=(b, LRU_BLOCKS),
        in_specs=[pl.BlockSpec((1, t, LANES), lambda bi, h: (bi, 0, col_gate + h)),
                  pl.BlockSpec((1, t, LANES), lambda bi, h: (bi, 0, col_x + h)),
                  pl.BlockSpec((8, LANES), lambda bi, h: (0, h)),
                  pl.BlockSpec((1, LANES, 2 * LANES), lambda bi, h: (h, 0, 0))],
        out_specs=pl.BlockSpec((1, t, LANES), lambda bi, h: (bi, 0, h)),
        out_shape=jax.ShapeDtypeStruct((b, t, width), BF16),
        scratch_shapes=[pltpu.VMEM((t + 8, LANES), F32)],
        compiler_params=_cparams(("parallel", "parallel")),
        name="rglru",
    )(proj3, proj3, lru_p, w_ax)


def _bias_tile_kernel(rb_ref, o_ref):
    h = pl.program_id(0)
    tb = o_ref.shape[1]
    i = lax.broadcasted_iota(jnp.int32, (tb, 2 * tb), 0)
    j2 = lax.broadcasted_iota(jnp.int32, (tb, 2 * tb), 1)
    max_exact = REL_BUCKETS // 2
    d = i - j2 + tb
    n = jnp.maximum(d, 0)
    nf = jnp.maximum(n, 1).astype(F32)
    large = max_exact + (jnp.log(nf / max_exact) / math.log(REL_MAX_DIST / max_exact)
                         * (REL_BUCKETS - max_exact)).astype(jnp.int32)
    large = jnp.minimum(large, REL_BUCKETS - 1)
    bucket = jnp.where(n < max_exact, n, large)
    val = jnp.zeros((tb, 2 * tb), F32)
    for bkt in range(REL_BUCKETS):
        val = jnp.where(bucket == bkt, rb_ref[bkt, h], val)
    val = val - rb_ref[REL_BUCKETS - 1, h]
    o_ref[0] = jnp.where(d >= 0, val, MASK_NEG)


def _bias_tiles(rel_bias, tb):
    return pl.pallas_call(
        _bias_tile_kernel,
        grid=(DA_HEADS,),
        in_specs=[pl.BlockSpec(memory_space=pltpu.SMEM)],
        out_specs=pl.BlockSpec((1, tb, 2 * tb), lambda h: (h, 0, 0)),
        out_shape=jax.ShapeDtypeStruct((DA_HEADS, tb, 2 * tb), F32),
        compiler_params=_cparams(("parallel",)),
        name="t5_bias_tiles",
    )(rel_bias)


def _attn_qblock(qi, q_ref, k_ref, v_ref, bt_ref, sw_ref, o_ref, m_scr, l_scr, acc_scr, s_scr, lam,
                 lam_init):
    tb = s_scr.shape[2]
    dn_t = (((1,), (1,)), ((), ()))
    lane = lax.broadcasted_iota(jnp.int32, (tb, LANES), 1)
    qstart = pl.multiple_of(qi * tb, tb)
    qs = q_ref[0, pl.ds(qstart, tb), :] * (DA_HEAD_DIM ** -0.5)
    zero = jnp.zeros_like(qs)
    qcat = jnp.concatenate([jnp.where(lane < DA_HEAD_DIM, qs, zero),
                            jnp.where(lane >= DA_HEAD_DIM, qs, zero)], axis=0)
    n_far = jnp.maximum(qi - 1, 0)
    n_quads = n_far // 4
    n_rest = n_far % 4
    tail_blk = 4 * n_quads

    def scores(blk, nblocks, biased):
        kb = k_ref[0, pl.ds(pl.multiple_of(blk * tb, tb), nblocks * tb), :]
        s_both = lax.dot_general(qcat, kb, dn_t, preferred_element_type=F32)
        out = []
        for c in range(2):
            s = s_both[c * tb:(c + 1) * tb]
            if biased:
                nb = min(nblocks, 2)
                plain = (nblocks - nb) * tb
                tail = s[:, plain:] + bt_ref[0, :, (2 - nb) * tb:]
                s = tail if plain == 0 else jnp.concatenate([s[:, :plain], tail], axis=1)
            s = s * LOG2E
            for i in range(nblocks):
                s_scr[c, blk + i] = s[:, i * tb:(i + 1) * tb]
            out.append(s)
        return out

    def lane_chunks(s):
        return [s[:, j * LANES:(j + 1) * LANES] for j in range(s.shape[1] // LANES)]

    def max_into(ms, ss):
        out = []
        for m, s in zip(ms, ss):
            for ch in lane_chunks(s):
                m = jnp.maximum(m, ch)
            out.append(m)
        return out

    m_init = jnp.full((tb, LANES), MASK_NEG, F32)

    @pl.when(qi == 0)
    def _():
        m_scr[0], m_scr[1] = max_into((m_init, m_init), scores(0, 1, True))

    for rest in range(4):
        @pl.when(jnp.logical_and(qi >= 1, n_rest == rest))
        def _():
            m_scr[0], m_scr[1] = max_into((m_init, m_init), scores(tail_blk, rest + 2, True))

    def quad_max(ki, ms):
        return tuple(max_into(ms, scores(4 * ki, 4, False)))

    ms = lax.fori_loop(0, n_quads, quad_max, (m_scr[0], m_scr[1]))
    mb = [jnp.broadcast_to(jnp.max(ms[c], axis=-1, keepdims=True), (tb, LANES)) for c in range(2)]

    def accumulate(blk, nblocks, st):
        vb = v_ref[0, pl.ds(pl.multiple_of(blk * tb, tb), nblocks * tb), :]
        out = []
        for c in range(2):
            ps = []
            for i in range(nblocks):
                ps += [jnp.exp2(ch - mb[c]) for ch in lane_chunks(s_scr[c, blk + i])]
            l = st[2 * c]
            for ch in ps:
                l = l + ch
            p = jnp.concatenate(ps, axis=1).astype(BF16)
            out += [l, st[2 * c + 1] + jnp.dot(p, vb, preferred_element_type=F32)]
        return tuple(out)

    def store_state(st):
        l_scr[0], acc_scr[0], l_scr[1], acc_scr[1] = st

    zl = jnp.zeros((tb, LANES), F32)

    @pl.when(qi == 0)
    def _():
        store_state(accumulate(0, 1, (zl, zl, zl, zl)))

    for rest in range(4):
        @pl.when(jnp.logical_and(qi >= 1, n_rest == rest))
        def _():
            store_state(accumulate(tail_blk, rest + 2, (zl, zl, zl, zl)))

    l0, a0, l1, a1 = lax.fori_loop(0, n_quads, lambda ki, st: accumulate(4 * ki, 4, st),
                                   (l_scr[0], acc_scr[0], l_scr[1], acc_scr[1]))

    o = (a0 / jnp.sum(l0, axis=-1, keepdims=True)
         - lam * (a1 / jnp.sum(l1, axis=-1, keepdims=True)))
    o = _rms(o, sw_ref[...], 1e-5) * (1.0 - lam_init)
    o_ref[0, pl.ds(qstart, tb), :] = o.astype(o_ref.dtype)


def _attn_kernel(q_ref, k_ref, v_ref, bt_ref, dl_ref, sw_ref, o_ref, m_scr, l_scr, acc_scr, s_scr, *,
                 lam_init):
    dl = dl_ref[...]
    lam = (jnp.exp(jnp.sum(dl[0:1, :] * dl[1:2, :], axis=-1, keepdims=True))
           - jnp.exp(jnp.sum(dl[2:3, :] * dl[3:4, :], axis=-1, keepdims=True)) + lam_init)

    def qblock(qi, carry):
        _attn_qblock(qi, q_ref, k_ref, v_ref, bt_ref, sw_ref, o_ref, m_scr, l_scr, acc_scr, s_scr, lam,
                     lam_init)
        return carry

    lax.fori_loop(0, q_ref.shape[1] // s_scr.shape[2], qblock, 0)


def _attn(proj3, btiles, da_lambda, da_subln, lam_init, tb, col_q, col_k, col_v):
    b, t, _ = proj3.shape
    width = DA_HEADS * LANES
    return pl.pallas_call(
        functools.partial(_attn_kernel, lam_init=lam_init),
        grid=(b, DA_HEADS),
        in_specs=[pl.BlockSpec((1, t, LANES), lambda bi, h: (bi, 0, col_q + h)),
                  pl.BlockSpec((1, t, LANES), lambda bi, h: (bi, 0, col_k + h)),
                  pl.BlockSpec((1, t, LANES), lambda bi, h: (bi, 0, col_v + h)),
                  pl.BlockSpec((1, tb, 2 * tb), lambda bi, h: (h, 0, 0)),
                  pl.BlockSpec((4, DA_HEAD_DIM), lambda bi, h: (0, 0)),
                  pl.BlockSpec((1, LANES), lambda bi, h: (0, 0))],
        out_specs=pl.BlockSpec((1, t, LANES), lambda bi, h: (bi, 0, h)),
        out_shape=jax.ShapeDtypeStruct((b, t, width), BF16),
        scratch_shapes=[pltpu.VMEM((2, tb, LANES), F32), pltpu.VMEM((2, tb, LANES), F32),
                        pltpu.VMEM((2, tb, LANES), F32), pltpu.VMEM((2, t // tb, tb, tb), F32)],
        compiler_params=_cparams(("parallel", "parallel")),
        name="diff_attn",
    )(proj3, proj3, proj3, btiles, da_lambda, da_subln)


def _s5_kernel(u_ref, wg_ref, wk_ref, wc_ref, pw_ref, d_ref, o_ref, uf_scr, g_scr, hp_scr, y_scr):
    t_len = u_ref.shape[1]
    nj = t_len // S5_L
    uf_scr[...] = u_ref[0].astype(F32)
    ur = jnp.concatenate([uf_scr[pl.ds(s, nj, stride=S5_L), :].astype(BF16) for s in range(S5_L)],
                         axis=1)
    g_scr[...] = jnp.dot(ur, wg_ref[0], preferred_element_type=F32)

    pw = pw_ref[0]
    pr, pi = pw[8:16, :S5_HALF], pw[8:16, S5_HALF:]
    row = lax.broadcasted_iota(jnp.int32, (8, S5_HALF), 0)

    def local_scan(base):
        x = g_scr[pl.ds(base, 8), :]
        xr, xi = x[:, :S5_HALF], x[:, S5_HALF:]
        for lvl, s in enumerate((1, 2, 4)):
            ar, ai = pw[lvl:lvl + 1, :S5_HALF], pw[lvl:lvl + 1, S5_HALF:]
            keep = row >= s
            sr = jnp.where(keep, pltpu.roll(xr, s, 0), 0.0)
            si = jnp.where(keep, pltpu.roll(xi, s, 0), 0.0)
            xr, xi = xr + (ar * sr - ai * si), xi + (ar * si + ai * sr)
        return xr, xi

    first = row == 0
    per_iter = 2

    def groups(gi, carry):
        cr, ci = carry
        bases = [pl.multiple_of((gi * per_iter + k) * 8, 8) for k in range(per_iter)]
        scans = [local_scan(base) for base in bases]
        for base, (xr, xi) in zip(bases, scans):
            xr, xi = xr + (pr * cr - pi * ci), xi + (pr * ci + pi * cr)
            hp_scr[pl.ds(base, 8), :] = jnp.concatenate(
                [jnp.where(first, cr, pltpu.roll(xr, 1, 0)), jnp.where(first, ci, pltpu.roll(xi, 1, 0))],
                axis=1)
            cr, ci = xr[7:8, :], xi[7:8, :]
        return cr, ci

    z = jnp.zeros((1, S5_HALF), F32)
    lax.fori_loop(0, nj // (8 * per_iter), groups, (z, z))

    y = (jnp.dot(ur, wk_ref[0], preferred_element_type=F32)
         + jnp.dot(hp_scr[...].astype(BF16), wc_ref[0], preferred_element_type=F32))
    for s in range(S5_L):
        y_scr[pl.ds(s, nj, stride=S5_L), :] = y[:, s * LANES:(s + 1) * LANES]
    o_ref[0] = _gelu_tanh(y_scr[...] + d_ref[...] * uf_scr[...]).astype(o_ref.dtype)


def _s5(proj3, wg, wk, wc, pw, dvec, col_u):
    b, t, _ = proj3.shape
    nblk = wg.shape[0]
    nj = t // S5_L
    wide = S5_L * LANES
    assert nj % 16 == 0

    def wspec(rows, cols):
        return pl.BlockSpec((1, rows, cols), lambda g, bi: (g, 0, 0))

    return pl.pallas_call(
        _s5_kernel,
        grid=(nblk, b),
        in_specs=[pl.BlockSpec((1, t, LANES), lambda g, bi: (bi, 0, col_u + g)),
                  wspec(wide, 2 * S5_HALF), wspec(wide, wide), wspec(2 * S5_HALF, wide),
                  wspec(16, 2 * S5_HALF),
                  pl.BlockSpec((1, LANES), lambda g, bi: (0, g))],
        out_specs=pl.BlockSpec((1, t, LANES), lambda g, bi: (bi, 0, g)),
        out_shape=jax.ShapeDtypeStruct((b, t, nblk * LANES), BF16),
        scratch_shapes=[pltpu.VMEM((t, LANES), F32), pltpu.VMEM((nj, 2 * S5_HALF), F32),
                        pltpu.VMEM((nj, 2 * S5_HALF), F32), pltpu.VMEM((t, LANES), F32)],
        compiler_params=_cparams(("parallel", "parallel")),
        name="s5",
    )(proj3, wg, wk, wc, pw, dvec)


def _merge_kernel(ya_ref, yb_ref, yc_ref, g0_ref, g1_ref, g2_ref, wg_ref, bg_ref, wb_ref, bgate_ref,
                  o_ref, yc_scr):
    @pl.when(pl.program_id(1) == 0)
    def _():
        yc = yc_ref[...]
        z = jnp.dot(yc, wg_ref[...], preferred_element_type=F32) + bg_ref[...]
        yc_scr[...] = (yc.astype(F32) * _sigmoid(z)).astype(BF16)

    bgate = bgate_ref[...]
    ys = (ya_ref[...], yb_ref[...], yc_scr[...])
    gs = (g0_ref, g1_ref, g2_ref)
    merged = None
    for br in range(N_BRANCH):
        gate = _sigmoid(gs[br][...].astype(F32) + bgate[br:br + 1, :])
        term = gate * jnp.dot(ys[br], wb_ref[br], preferred_element_type=F32)
        merged = term if merged is None else merged + term
    o_ref[...] = merged.astype(o_ref.dtype)


def _merge(ya, yb, yc, proj, w_glu_all, b_glu, w_branch_all, b_gate, col_g, layer, tm, tn):
    n, wdt = ya.shape
    d = w_branch_all.shape[3]
    gcol = [(col_g + br * d) // tn for br in range(N_BRANCH)]
    yspec = pl.BlockSpec((tm, wdt), lambda i, j: (i, 0))

    def gspec(br):
        return pl.BlockSpec((tm, tn), lambda i, j: (i, gcol[br] + j))

    return pl.pallas_call(
        _merge_kernel,
        grid=(n // tm, d // tn),
        in_specs=[yspec, yspec, yspec, gspec(0), gspec(1), gspec(2),
                  pl.BlockSpec((None, wdt, wdt), lambda i, j: (layer, 0, 0)),
                  pl.BlockSpec((1, wdt), lambda i, j: (0, 0)),
                  pl.BlockSpec((None, N_BRANCH, wdt, tn), lambda i, j: (layer, 0, 0, j)),
                  pl.BlockSpec((N_BRANCH, tn), lambda i, j: (0, j))],
        out_specs=pl.BlockSpec((tm, tn), lambda i, j: (i, j)),
        out_shape=jax.ShapeDtypeStruct((n, d), BF16),
        scratch_shapes=[pltpu.VMEM((tm, wdt), BF16)],
        compiler_params=_cparams(("parallel", "arbitrary")),
        name="merge",
    )(ya, yb, yc, proj, proj, proj, w_glu_all, b_glu, w_branch_all, b_gate)


def _outproj_kernel(m_ref, w_ref, xs_ref, nw_ref, o_ref):
    mix = jnp.dot(m_ref[...], w_ref[...], preferred_element_type=F32)
    o_ref[...] = xs_ref[...] + _rms(mix, nw_ref[...], 1e-6)


def _outproj(merged, w_out_all, layer, xs, nw, tm):
    n, d = xs.shape
    return pl.pallas_call(
        _outproj_kernel,
        grid=(n // tm,),
        in_specs=[pl.BlockSpec((tm, d), lambda i: (i, 0)),
                  pl.BlockSpec((None, d, d), lambda i: (layer, 0, 0)),
                  pl.BlockSpec((tm, d), lambda i: (i, 0)),
                  pl.BlockSpec((1, d), lambda i: (0, 0))],
        out_specs=pl.BlockSpec((tm, d), lambda i: (i, 0)),
        out_shape=jax.ShapeDtypeStruct((n, d), F32),
        compiler_params=_cparams(("parallel",)),
        name="outproj_residual",
    )(merged, w_out_all, xs, nw)


def _ffn_kernel(xs_ref, nw_in_ref, wg_ref, wu_ref, wo_ref, nw_out_ref, o_ref, h_scr, acc_scr):
    f = pl.program_id(1)

    @pl.when(f == 0)
    def _():
        h_scr[...] = _rms(xs_ref[...], nw_in_ref[...], 1e-6).astype(BF16)
        acc_scr[...] = jnp.zeros(acc_scr.shape, F32)

    h = h_scr[...]
    gate = jnp.dot(h, wg_ref[...], preferred_element_type=F32)
    up = jnp.dot(h, wu_ref[...], preferred_element_type=F32)
    act = (gate * _sigmoid(gate) * up).astype(BF16)
    acc_scr[...] += jnp.dot(act, wo_ref[...], preferred_element_type=F32)

    @pl.when(f == pl.num_programs(1) - 1)
    def _():
        o_ref[...] = xs_ref[...] + _rms(acc_scr[...], nw_out_ref[...], 1e-6)


def _ffn(xs, nw_in, w_ffn_in_all, w_ffn_out_all, nw_out, layer, tm, tf):
    n, d = xs.shape
    d_ff = w_ffn_out_all.shape[1]
    nf = d_ff // tf
    return pl.pallas_call(
        _ffn_kernel,
        grid=(n // tm, nf),
        in_specs=[pl.BlockSpec((tm, d), lambda i, f: (i, 0)),
                  pl.BlockSpec((1, d), lambda i, f: (0, 0)),
                  pl.BlockSpec((None, d, tf), lambda i, f: (layer, 0, f)),
                  pl.BlockSpec((None, d, tf), lambda i, f: (layer, 0, nf + f)),
                  pl.BlockSpec((None, tf, d), lambda i, f: (layer, f, 0)),
                  pl.BlockSpec((1, d), lambda i, f: (0, 0))],
        out_specs=pl.BlockSpec((tm, d), lambda i, f: (i, 0)),
        out_shape=jax.ShapeDtypeStruct((n, d), F32),
        scratch_shapes=[pltpu.VMEM((tm, d), BF16), pltpu.VMEM((tm, d), F32)],
        compiler_params=_cparams(("parallel", "arbitrary")),
        name="swiglu_ffn",
    )(xs, nw_in, w_ffn_in_all, w_ffn_in_all, w_ffn_out_all, nw_out)


def _s5_tables(lam_re, lam_im, b_re, b_im, c_re, c_im, log_step):
    hp = lax.Precision.HIGHEST
    groups = lam_re.shape[0]
    nblk = groups // S5_GB
    wide = S5_L * LANES
    lr, li = lam_re.astype(F32), lam_im.astype(F32)
    step = jnp.exp(log_step.astype(F32))[:, None]
    mag = jnp.exp(lr * step)
    ab_re, ab_im = mag * jnp.cos(li * step), mag * jnp.sin(li * step)
    den = lr * lr + li * li
    coef_re = ((ab_re - 1.0) * lr + ab_im * li) / den
    coef_im = (ab_im * lr - (ab_re - 1.0) * li) / den
    br, bi = b_re.astype(F32), b_im.astype(F32)
    bb_re = coef_re[..., None] * br - coef_im[..., None] * bi
    bb_im = coef_re[..., None] * bi + coef_im[..., None] * br
    cr, ci = c_re.astype(F32), c_im.astype(F32)
    eye = jnp.eye(S5_GB, dtype=F32)
    taus = jnp.arange(S5_L)
    lrs = (lr * step).reshape(nblk, 1, S5_HALF)
    lis = (li * step).reshape(nblk, 1, S5_HALF)

    def lam_pow(n):
        nn = n.astype(F32)[None, :, None]
        m = jnp.exp(nn * lrs)
        return m * jnp.cos(nn * lis), m * jnp.sin(nn * lis)

    def in_blockdiag(t):
        t = t.reshape(nblk, S5_GB, S5_STATE, S5_GROUP)
        return jnp.einsum("ngpc,gh->ngchp", t, eye).reshape(nblk, LANES, S5_HALF)

    def out_blockdiag(t):
        t = t.reshape(nblk, S5_GB, S5_GROUP, S5_STATE)
        return jnp.einsum("ngcp,gh->ngphc", t, eye).reshape(nblk, S5_HALF, LANES)

    bd_re, bd_im = in_blockdiag(bb_re)[:, None], in_blockdiag(bb_im)[:, None]
    ct_re, ct_im = out_blockdiag(cr)[:, :, None, :], out_blockdiag(ci)[:, :, None, :]

    qr, qi = lam_pow(S5_L - 1 - taus)
    qr, qi = qr[:, :, None, :], qi[:, :, None, :]
    wg = jnp.concatenate([qr * bd_re - qi * bd_im, qr * bd_im + qi * bd_re], axis=3)
    wg = wg.reshape(nblk, wide, 2 * S5_HALF).astype(BF16)

    p1r, p1i = lam_pow(1 + taus)
    p1r, p1i = p1r.transpose(0, 2, 1)[..., None], p1i.transpose(0, 2, 1)[..., None]
    wc = jnp.concatenate([ct_re * p1r - ct_im * p1i, -(ct_re * p1i + ct_im * p1r)], axis=1)
    wc = wc.reshape(nblk, 2 * S5_HALF, wide).astype(BF16)

    nn = taus.astype(F32)[:, None, None]
    mg = jnp.exp(nn * (lr * step)[None])
    pr, pi = mg * jnp.cos(nn * (li * step)[None]), mg * jnp.sin(nn * (li * step)[None])
    c0_re = cr[None] * pr[:, :, None, :] - ci[None] * pi[:, :, None, :]
    c0_im = cr[None] * pi[:, :, None, :] + ci[None] * pr[:, :, None, :]
    kt = (jnp.einsum("tgop,gpi->tgio", c0_re, bb_re, precision=hp)
          - jnp.einsum("tgop,gpi->tgio", c0_im, bb_im, precision=hp))
    kt = kt.reshape(S5_L, nblk, S5_GB, S5_GROUP, S5_GROUP)
    kblk = jnp.einsum("tngio,gh->ntgiho", kt, eye).reshape(nblk, S5_L, LANES, LANES)
    lag = taus[None, :] - taus[:, None]
    ksr = jnp.take(kblk, jnp.maximum(lag, 0).reshape(-1), axis=1).reshape(nblk, S5_L, S5_L, LANES, LANES)
    ksr = jnp.where((lag >= 0)[None, :, :, None, None], ksr, 0.0)
    wk = ksr.transpose(0, 1, 3, 2, 4).reshape(nblk, wide, wide).astype(BF16)

    n_list = S5_L * jnp.concatenate([jnp.array([1, 2, 4, 0, 0, 0, 0, 0]), 1 + jnp.arange(8)])
    wr, wi = lam_pow(n_list)
    pw = jnp.concatenate([wr, wi], axis=2)
    return wg, wk, wc, pw


def kernel(x, meta, rel_bias, norm_w, w_in, conv_w, conv_b, lru_w_a, lru_b_a, lru_w_x, lru_b_x, lru_lambda, da_lambda, da_subln, s5_lam_re, s5_lam_im, s5_b_re, s5_b_im, s5_c_re, s5_c_im, s5_d, s5_log_step, s5_w_glu, s5_b_glu, b_gate, w_branch, w_out, w_ffn_in, w_ffn_out):
    bsz, seq, d_model = x.shape
    depth = w_in.shape[0]
    lru_w = conv_w.shape[2]
    s5_w = s5_d.shape[1]
    qk_w = DA_HEADS * 2 * DA_HEAD_DIM
    assert lru_w == LRU_BLOCKS * LANES and s5_w % LANES == 0 and d_model % LANES == 0
    col_gate, col_x = 0, lru_w // LANES
    col_q = 2 * lru_w // LANES
    col_k = col_q + qk_w // LANES
    col_v = col_k + qk_w // LANES
    col_u = col_v + DA_HEADS
    col_g = (col_u + s5_w // LANES) * LANES

    t_real = N_META + seq
    tb = 384 if t_real >= 1024 else 128
    t_pad = -(-t_real // tb) * tb
    assert t_pad % SCAN_CH == 0
    n_tok = bsz * t_pad
    tm_in = _pick_tile(n_tok, (1408, 1056, 768, 512, 384, 256, 128))
    tm_merge = _pick_tile(n_tok, (1056, 768, 512, 384, 256, 128))
    tm_ffn = _pick_tile(n_tok, (768, 512, 384, 256, 128))
    tm_out = _pick_tile(n_tok, (512, 384, 256, 128))

    xs = jnp.concatenate([jnp.broadcast_to(meta.astype(F32)[None], (bsz, N_META, d_model)), x,
                          jnp.zeros((bsz, t_pad - t_real, d_model), F32)], axis=1)
    xs = xs.reshape(n_tok, d_model)
    btiles = _bias_tiles(rel_bias.astype(F32), tb)

    w_in_b, w_glu_b, w_branch_b = w_in.astype(BF16), s5_w_glu.astype(BF16), w_branch.astype(BF16)
    w_out_b, w_ffn_in_b, w_ffn_out_b = w_out.astype(BF16), w_ffn_in.astype(BF16), w_ffn_out.astype(BF16)

    for l in range(depth):
        lam_init = 0.8 - 0.6 * math.exp(-0.3 * l)
        proj = _inproj(xs, norm_w[l, 0][None], w_in_b, l, tm_in, 1024)
        proj3 = proj.reshape(bsz, t_pad, proj.shape[1])

        lru_p = jnp.concatenate([conv_w[l], conv_b[l][None], lru_b_a[l][None], lru_b_x[l][None],
                                 jax.nn.log_sigmoid(lru_lambda[l].astype(F32))[None]], axis=0)
        w_ax = jnp.concatenate([lru_w_a[l], lru_w_x[l]], axis=2).astype(BF16)
        y_a = _lru(proj3, lru_p, w_ax, col_gate, col_x)

        y_b = _attn(proj3, btiles, da_lambda[l], da_subln[l][None], lam_init, tb, col_q, col_k, col_v)

        wg, wk, wc, pw = _s5_tables(s5_lam_re[l], s5_lam_im[l], s5_b_re[l], s5_b_im[l],
                                    s5_c_re[l], s5_c_im[l], s5_log_step[l])
        y_c = _s5(proj3, wg, wk, wc, pw, s5_d[l][None], col_u)

        merged = _merge(y_a.reshape(n_tok, lru_w), y_b.reshape(n_tok, -1), y_c.reshape(n_tok, s5_w),
                        proj, w_glu_b, s5_b_glu[l][None], w_branch_b, b_gate[l], col_g, l, tm_merge, 512)
        xs = _outproj(merged, w_out_b, l, xs, norm_w[l, 1][None], tm_out)
        xs = _ffn(xs, norm_w[l, 2][None], w_ffn_in_b, w_ffn_out_b, norm_w[l, 3][None], l, tm_ffn, 512)

    return xs.reshape(bsz, t_pad, d_model)[:, N_META:t_real]
```

```python
import functools
import math

import jax
import jax.numpy as jnp
from jax import lax
from jax.experimental import pallas as pl
from jax.experimental.pallas import tpu as pltpu

F32 = jnp.float32
BF16 = jnp.bfloat16

N_META = 16
CONV_W = 4
LRU_C = 8.0
LRU_BLOCKS = 8
DA_HEADS = 8
DA_HEAD_DIM = 64
S5_GROUP = 16
S5_STATE = 64
REL_BUCKETS = 32
REL_MAX_DIST = 128
N_BRANCH = 3

LANES = 128
S5_GB = LANES // S5_GROUP
S5_HALF = S5_GB * S5_STATE
S5_L = 8
SCAN_CH = 128
MASK_NEG = -1e30
LOG2E = math.log2(math.e)
VMEM_LIMIT = 56 * 1024 * 1024


def _cparams(sem):
    return pltpu.CompilerParams(dimension_semantics=sem, vmem_limit_bytes=VMEM_LIMIT)


def _pick_tile(n, candidates):
    for c in candidates:
        if n % c == 0:
            return c
    raise ValueError(f"no tile in {candidates} divides {n}")


def _gelu_tanh(x):
    return 0.5 * x * (1.0 + jnp.tanh(math.sqrt(2.0 / math.pi) * (x + 0.044715 * (x * x * x))))


def _sigmoid(x):
    return 1.0 / (1.0 + jnp.exp(-x))


def _rms(x, w, eps):
    return (x * lax.rsqrt(jnp.mean(x * x, axis=-1, keepdims=True) + eps)) * w


def _inproj_kernel(x_ref, nw_ref, w_ref, o_ref, h_scr):
    @pl.when(pl.program_id(1) == 0)
    def _():
        h_scr[...] = _rms(x_ref[...], nw_ref[...], 1e-6).astype(BF16)

    o_ref[...] = jnp.dot(h_scr[...], w_ref[...], preferred_element_type=F32).astype(o_ref.dtype)


def _inproj(xs, nw, w_all, layer, tm, tn):
    n, d = xs.shape
    n_out = w_all.shape[2]
    return pl.pallas_call(
        _inproj_kernel,
        grid=(n // tm, n_out // tn),
        in_specs=[pl.BlockSpec((tm, d), lambda i, j: (i, 0)),
                  pl.BlockSpec((1, d), lambda i, j: (0, 0)),
                  pl.BlockSpec((None, d, tn), lambda i, j: (layer, 0, j))],
        out_specs=pl.BlockSpec((tm, tn), lambda i, j: (i, j)),
        out_shape=jax.ShapeDtypeStruct((n, n_out), BF16),
        scratch_shapes=[pltpu.VMEM((tm, d), BF16)],
        compiler_params=_cparams(("parallel", "arbitrary")),
        name="inproj",
    )(xs, nw, w_all)


def _lru_kernel(g_ref, x_ref, p_ref, w_ref, o_ref, xs_scr):
    t_len = x_ref.shape[1]
    halo = 8
    xs_scr[0:halo, :] = jnp.zeros((halo, LANES), F32)
    xs_scr[halo:, :] = x_ref[0].astype(F32)
    p = p_ref[...]
    cw = [p[j:j + 1, :] for j in range(CONV_W)]
    cb, ba, bx, logsig = p[4:5, :], p[5:6, :], p[6:7, :], p[7:8, :]
    w = w_ref[0]
    row = lax.broadcasted_iota(jnp.int32, (SCAN_CH, LANES), 0)

    def local_scan(base):
        xc = cb
        for j in range(CONV_W):
            xc = xc + cw[j] * xs_scr[pl.ds(base + halo - (CONV_W - 1) + j, SCAN_CH), :]
        ri = jnp.dot(xc.astype(BF16), w, preferred_element_type=F32)
        r = _sigmoid(ri[:, :LANES] + ba)
        i = _sigmoid(ri[:, LANES:] + bx)
        a = jnp.exp(LRU_C * r * logsig)
        b = jnp.sqrt(1.0 - a * a) * (i * xc)
        s = 1
        while s < SCAN_CH:
            keep = row >= s
            a_sh = jnp.where(keep, pltpu.roll(a, s, 0), 1.0)
            b_sh = jnp.where(keep, pltpu.roll(b, s, 0), 0.0)
            b = a * b_sh + b
            a = a * a_sh
            s *= 2
        return a, b

    n_chunks = t_len // SCAN_CH
    sub = 3 if n_chunks % 3 == 0 else 1

    def chunk(c, h0):
        bases = [pl.multiple_of((c * sub + j) * SCAN_CH, SCAN_CH) for j in range(sub)]
        scans = [local_scan(base) for base in bases]
        for base, (a, b) in zip(bases, scans):
            h = a * h0 + b
            g = g_ref[0, pl.ds(base, SCAN_CH), :].astype(F32)
            o_ref[0, pl.ds(base, SCAN_CH), :] = (h * _gelu_tanh(g)).astype(o_ref.dtype)
            h0 = h[SCAN_CH - 1:SCAN_CH, :]
        return h0

    lax.fori_loop(0, n_chunks // sub, chunk, jnp.zeros((1, LANES), F32))


def _lru(proj3, lru_p, w_ax, col_gate, col_x):
    b, t, _ = proj3.shape
    width = LRU_BLOCKS * LANES
    return pl.pallas_call(
        _lru_kernel,
        grid=(b, LRU_BLOCKS),
        in_specs=[pl.BlockSpec((1, t, LANES), lambda bi, h: (bi, 0, col_gate + h)),
                  pl.BlockSpec((1, t, LANES), lambda bi, h: (bi, 0, col_x + h)),
                  pl.BlockSpec((8, LANES), lambda bi, h: (0, h)),
                  pl.BlockSpec((1, LANES, 2 * LANES), lambda bi, h: (h, 0, 0))],
        out_specs=pl.BlockSpec((1, t, LANES), lambda bi, h: (bi, 0, h)),
        out_shape=jax.ShapeDtypeStruct((b, t, width), BF16),
        scratch_shapes=[pltpu.VMEM((t + 8, LANES), F32)],
        compiler_params=_cparams(("parallel", "parallel")),
        name="rglru",
    )(proj3, proj3, lru_p, w_ax)


def _bias_tile_kernel(rb_ref, o_ref):
    h = pl.program_id(0)
    tb = o_ref.shape[1]
    i = lax.broadcasted_iota(jnp.int32, (tb, 2 * tb), 0)
    j2 = lax.broadcasted_iota(jnp.int32, (tb, 2 * tb), 1)
    max_exact = REL_BUCKETS // 2
    d = i - j2 + tb
    n = jnp.maximum(d, 0)
    nf = jnp.maximum(n, 1).astype(F32)
    large = max_exact + (jnp.log(nf / max_exact) / math.log(REL_MAX_DIST / max_exact)
                         * (REL_BUCKETS - max_exact)).astype(jnp.int32)
    large = jnp.minimum(large, REL_BUCKETS - 1)
    bucket = jnp.where(n < max_exact, n, large)
    val = jnp.zeros((tb, 2 * tb), F32)
    for bkt in range(REL_BUCKETS):
        val = jnp.where(bucket == bkt, rb_ref[bkt, h], val)
    val = val - rb_ref[REL_BUCKETS - 1, h]
    o_ref[0] = jnp.where(d >= 0, val, MASK_NEG)


def _bias_tiles(rel_bias, tb):
    return pl.pallas_call(
        _bias_tile_kernel,
        grid=(DA_HEADS,),
        in_specs=[pl.BlockSpec(memory_space=pltpu.SMEM)],
        out_specs=pl.BlockSpec((1, tb, 2 * tb), lambda h: (h, 0, 0)),
        out_shape=jax.ShapeDtypeStruct((DA_HEADS, tb, 2 * tb), F32),
        compiler_params=_cparams(("parallel",)),
        name="t5_bias_tiles",
    )(rel_bias)


def _attn_qblock(qi, q_ref, k_ref, v_ref, bt_ref, sw_ref, o_ref, m_scr, l_scr, acc_scr, s_scr, lam,
                 lam_init):
    tb = s_scr.shape[2]
    dn_t = (((1,), (1,)), ((), ()))
    lane = lax.broadcasted_iota(jnp.int32, (tb, LANES), 1)
    qstart = pl.multiple_of(qi * tb, tb)
    qs = q_ref[0, pl.ds(qstart, tb), :] * (DA_HEAD_DIM ** -0.5)
    zero = jnp.zeros_like(qs)
    qcat = jnp.concatenate([jnp.where(lane < DA_HEAD_DIM, qs, zero),
                            jnp.where(lane >= DA_HEAD_DIM, qs, zero)], axis=0)
    n_far = jnp.maximum(qi - 1, 0)
    n_quads = n_far // 4
    n_rest = n_far % 4
    tail_blk = 4 * n_quads

    def scores(blk, nblocks, biased):
        kb = k_ref[0, pl.ds(pl.multiple_of(blk * tb, tb), nblocks * tb), :]
        s_both = lax.dot_general(qcat, kb, dn_t, preferred_element_type=F32)
        out = []
        for c in range(2):
            s = s_both[c * tb:(c + 1) * tb]
            if biased:
                nb = min(nblocks, 2)
                plain = (nblocks - nb) * tb
                tail = s[:, plain:] + bt_ref[0, :, (2 - nb) * tb:]
                s = tail if plain == 0 else jnp.concatenate([s[:, :plain], tail], axis=1)
            s = s * LOG2E
            for i in range(nblocks):
                s_scr[c, blk + i] = s[:, i * tb:(i + 1) * tb]
            out.append(s)
        return out

    def lane_chunks(s):
        return [s[:, j * LANES:(j + 1) * LANES] for j in range(s.shape[1] // LANES)]

    def max_into(ms, ss):
        out = []
        for m, s in zip(ms, ss):
            for ch in lane_chunks(s):
                m = jnp.maximum(m, ch)
            out.append(m)
        return out

    m_init = jnp.full((tb, LANES), MASK_NEG, F32)

    @pl.when(qi == 0)
    def _():
        m_scr[0], m_scr[1] = max_into((m_init, m_init), scores(0, 1, True))

    for rest in range(4):
        @pl.when(jnp.logical_and(qi >= 1, n_rest == rest))
        def _():
            m_scr[0], m_scr[1] = max_into((m_init, m_init), scores(tail_blk, rest + 2, True))

    def quad_max(ki, ms):
        return tuple(max_into(ms, scores(4 * ki, 4, False)))

    ms = lax.fori_loop(0, n_quads, quad_max, (m_scr[0], m_scr[1]))
    mb = [jnp.broadcast_to(jnp.max(ms[c], axis=-1, keepdims=True), (tb, LANES)) for c in range(2)]

    def accumulate(blk, nblocks, st):
        vb = v_ref[0, pl.ds(pl.multiple_of(blk * tb, tb), nblocks * tb), :]
        out = []
        for c in range(2):
            ps = []
            for i in range(nblocks):
                ps += [jnp.exp2(ch - mb[c]) for ch in lane_chunks(s_scr[c, blk + i])]
            l = st[2 * c]
            for ch in ps:
                l = l + ch
            p = jnp.concatenate(ps, axis=1).astype(BF16)
            out += [l, st[2 * c + 1] + jnp.dot(p, vb, preferred_element_type=F32)]
        return tuple(out)

    def store_state(st):
        l_scr[0], acc_scr[0], l_scr[1], acc_scr[1] = st

    zl = jnp.zeros((tb, LANES), F32)

    @pl.when(qi == 0)
    def _():
        store_state(accumulate(0, 1, (zl, zl, zl, zl)))

    for rest in range(4):
        @pl.when(jnp.logical_and(qi >= 1, n_rest == rest))
        def _():
            store_state(accumulate(tail_blk, rest + 2, (zl, zl, zl, zl)))

    l0, a0, l1, a1 = lax.fori_loop(0, n_quads, lambda ki, st: accumulate(4 * ki, 4, st),
                                   (l_scr[0], acc_scr[0], l_scr[1], acc_scr[1]))

    o = (a0 / jnp.sum(l0, axis=-1, keepdims=True)
         - lam * (a1 / jnp.sum(l1, axis=-1, keepdims=True)))
    o = _rms(o, sw_ref[...], 1e-5) * (1.0 - lam_init)
    o_ref[0, pl.ds(qstart, tb), :] = o.astype(o_ref.dtype)


def _attn_kernel(q_ref, k_ref, v_ref, bt_ref, dl_ref, sw_ref, o_ref, m_scr, l_scr, acc_scr, s_scr, *,
                 lam_init):
    dl = dl_ref[...]
    lam = (jnp.exp(jnp.sum(dl[0:1, :] * dl[1:2, :], axis=-1, keepdims=True))
           - jnp.exp(jnp.sum(dl[2:3, :] * dl[3:4, :], axis=-1, keepdims=True)) + lam_init)

    def qblock(qi, carry):
        _attn_qblock(qi, q_ref, k_ref, v_ref, bt_ref, sw_ref, o_ref, m_scr, l_scr, acc_scr, s_scr, lam,
                     lam_init)
        return carry

    lax.fori_loop(0, q_ref.shape[1] // s_scr.shape[2], qblock, 0)


def _attn(proj3, btiles, da_lambda, da_subln, lam_init, tb, col_q, col_k, col_v):
    b, t, _ = proj3.shape
    width = DA_HEADS * LANES
    return pl.pallas_call(
        functools.partial(_attn_kernel, lam_init=lam_init),
        grid=(b, DA_HEADS),
        in_specs=[pl.BlockSpec((1, t, LANES), lambda bi, h: (bi, 0, col_q + h)),
                  pl.BlockSpec((1, t, LANES), lambda bi, h: (bi, 0, col_k + h)),
                  pl.BlockSpec((1, t, LANES), lambda bi, h: (bi, 0, col_v + h)),
                  pl.BlockSpec((1, tb, 2 * tb), lambda bi, h: (h, 0, 0)),
                  pl.BlockSpec((4, DA_HEAD_DIM), lambda bi, h: (0, 0)),
                  pl.BlockSpec((1, LANES), lambda bi, h: (0, 0))],
        out_specs=pl.BlockSpec((1, t, LANES), lambda bi, h: (bi, 0, h)),
        out_shape=jax.ShapeDtypeStruct((b, t, width), BF16),
        scratch_shapes=[pltpu.VMEM((2, tb, LANES), F32), pltpu.VMEM((2, tb, LANES), F32),
                        pltpu.VMEM((2, tb, LANES), F32), pltpu.VMEM((2, t // tb, tb, tb), F32)],
        compiler_params=_cparams(("parallel", "parallel")),
        name="diff_attn",
    )(proj3, proj3, proj3, btiles, da_lambda, da_subln)


def _s5_kernel(u_ref, bc_ref, kb_ref, pw_ref, d_ref, o_ref, uf_scr, g_scr, hp_scr, y_scr,
               wg_scr, wk_scr, wct_scr):
    t_len = u_ref.shape[1]
    nj = t_len // S5_L
    pw = pw_ref[0]

    @pl.when(pl.program_id(1) == 0)
    def _():
        bd_re, bd_im, ct_re, ct_im = bc_ref[0, 0], bc_ref[0, 1], bc_ref[0, 2], bc_ref[0, 3]
        for s in range(S5_L):
            rows = slice(s * LANES, (s + 1) * LANES)
            qr, qi = pw[s:s + 1, :S5_HALF], pw[s:s + 1, S5_HALF:]
            wg_scr[rows, :S5_HALF] = (qr * bd_re - qi * bd_im).astype(BF16)
            wg_scr[rows, S5_HALF:] = (qr * bd_im + qi * bd_re).astype(BF16)
            er, ei = pw[8 + s:9 + s, :S5_HALF], pw[8 + s:9 + s, S5_HALF:]
            wct_scr[rows, :S5_HALF] = (ct_re * er - ct_im * ei).astype(BF16)
            wct_scr[rows, S5_HALF:] = (-(ct_re * ei + ct_im * er)).astype(BF16)
            for r in range(S5_L):
                cols = slice(r * LANES, (r + 1) * LANES)
                wk_scr[rows, cols] = kb_ref[0, r - s] if r >= s else jnp.zeros((LANES, LANES), BF16)

    uf_scr[...] = u_ref[0].astype(F32)
    ur = jnp.concatenate([uf_scr[pl.ds(s, nj, stride=S5_L), :].astype(BF16) for s in range(S5_L)],
                         axis=1)
    g_scr[...] = jnp.dot(ur, wg_scr[...], preferred_element_type=F32)

    pr, pi = pw[24:32, :S5_HALF], pw[24:32, S5_HALF:]
    row = lax.broadcasted_iota(jnp.int32, (8, S5_HALF), 0)

    def local_scan(base):
        x = g_scr[pl.ds(base, 8), :]
        xr, xi = x[:, :S5_HALF], x[:, S5_HALF:]
        for lvl, s in enumerate((1, 2, 4)):
            ar, ai = pw[16 + lvl:17 + lvl, :S5_HALF], pw[16 + lvl:17 + lvl, S5_HALF:]
            keep = row >= s
            sr = jnp.where(keep, pltpu.roll(xr, s, 0), 0.0)
            si = jnp.where(keep, pltpu.roll(xi, s, 0), 0.0)
            xr, xi = xr + (ar * sr - ai * si), xi + (ar * si + ai * sr)
        return xr, xi

    first = row == 0
    per_iter = 2

    def groups(gi, carry):
        cr, ci = carry
        bases = [pl.multiple_of((gi * per_iter + k) * 8, 8) for k in range(per_iter)]
        scans = [local_scan(base) for base in bases]
        for base, (xr, xi) in zip(bases, scans):
            xr, xi = xr + (pr * cr - pi * ci), xi + (pr * ci + pi * cr)
            hp_scr[pl.ds(base, 8), :] = jnp.concatenate(
                [jnp.where(first, cr, pltpu.roll(xr, 1, 0)), jnp.where(first, ci, pltpu.roll(xi, 1, 0))],
                axis=1)
            cr, ci = xr[7:8, :], xi[7:8, :]
        return cr, ci

    z = jnp.zeros((1, S5_HALF), F32)
    lax.fori_loop(0, nj // (8 * per_iter), groups, (z, z))

    y = (jnp.dot(ur, wk_scr[...], preferred_element_type=F32)
         + lax.dot_general(hp_scr[...].astype(BF16), wct_scr[...], (((1,), (1,)), ((), ())),
                           preferred_element_type=F32))
    for s in range(S5_L):
        y_scr[pl.ds(s, nj, stride=S5_L), :] = y[:, s * LANES:(s + 1) * LANES]
    o_ref[0] = _gelu_tanh(y_scr[...] + d_ref[...] * uf_scr[...]).astype(o_ref.dtype)


def _s5(proj3, bc, kblk, pw, dvec, col_u):
    b, t, _ = proj3.shape
    nblk = bc.shape[0]
    nj = t // S5_L
    wide = S5_L * LANES
    assert nj % 16 == 0
    return pl.pallas_call(
        _s5_kernel,
        grid=(nblk, b),
        in_specs=[pl.BlockSpec((1, t, LANES), lambda g, bi: (bi, 0, col_u + g)),
                  pl.BlockSpec((1, 4, LANES, S5_HALF), lambda g, bi: (g, 0, 0, 0)),
                  pl.BlockSpec((1, S5_L, LANES, LANES), lambda g, bi: (g, 0, 0, 0)),
                  pl.BlockSpec((1, 32, 2 * S5_HALF), lambda g, bi: (g, 0, 0)),
                  pl.BlockSpec((1, LANES), lambda g, bi: (0, g))],
        out_specs=pl.BlockSpec((1, t, LANES), lambda g, bi: (bi, 0, g)),
        out_shape=jax.ShapeDtypeStruct((b, t, nblk * LANES), BF16),
        scratch_shapes=[pltpu.VMEM((t, LANES), F32), pltpu.VMEM((nj, 2 * S5_HALF), F32),
                        pltpu.VMEM((nj, 2 * S5_HALF), F32), pltpu.VMEM((t, LANES), F32),
                        pltpu.VMEM((wide, 2 * S5_HALF), BF16), pltpu.VMEM((wide, wide), BF16),
                        pltpu.VMEM((wide, 2 * S5_HALF), BF16)],
        compiler_params=_cparams(("parallel", "arbitrary")),
        name="s5",
    )(proj3, bc, kblk, pw, dvec)


def _merge_kernel(ya_ref, yb_ref, yc_ref, g0_ref, g1_ref, g2_ref, wg_ref, bg_ref, wb_ref, bgate_ref,
                  o_ref, yc_scr):
    @pl.when(pl.program_id(1) == 0)
    def _():
        yc = yc_ref[...]
        z = jnp.dot(yc, wg_ref[...], preferred_element_type=F32) + bg_ref[...]
        yc_scr[...] = (yc.astype(F32) * _sigmoid(z)).astype(BF16)

    bgate = bgate_ref[...]
    ys = (ya_ref[...], yb_ref[...], yc_scr[...])
    gs = (g0_ref, g1_ref, g2_ref)
    merged = None
    for br in range(N_BRANCH):
        gate = _sigmoid(gs[br][...].astype(F32) + bgate[br:br + 1, :])
        term = gate * jnp.dot(ys[br], wb_ref[br], preferred_element_type=F32)
        merged = term if merged is None else merged + term
    o_ref[...] = merged.astype(o_ref.dtype)


def _merge(ya, yb, yc, proj, w_glu_all, b_glu, w_branch_all, b_gate, col_g, layer, tm, tn):
    n, wdt = ya.shape
    d = w_branch_all.shape[3]
    gcol = [(col_g + br * d) // tn for br in range(N_BRANCH)]
    yspec = pl.BlockSpec((tm, wdt), lambda i, j: (i, 0))

    def gspec(br):
        return pl.BlockSpec((tm, tn), lambda i, j: (i, gcol[br] + j))

    return pl.pallas_call(
        _merge_kernel,
        grid=(n // tm, d // tn),
        in_specs=[yspec, yspec, yspec, gspec(0), gspec(1), gspec(2),
                  pl.BlockSpec((None, wdt, wdt), lambda i, j: (layer, 0, 0)),
                  pl.BlockSpec((1, wdt), lambda i, j: (0, 0)),
                  pl.BlockSpec((None, N_BRANCH, wdt, tn), lambda i, j: (layer, 0, 0, j)),
                  pl.BlockSpec((N_BRANCH, tn), lambda i, j: (0, j))],
        out_specs=pl.BlockSpec((tm, tn), lambda i, j: (i, j)),
        out_shape=jax.ShapeDtypeStruct((n, d), BF16),
        scratch_shapes=[pltpu.VMEM((tm, wdt), BF16)],
        compiler_params=_cparams(("parallel", "arbitrary")),
        name="merge",
    )(ya, yb, yc, proj, proj, proj, w_glu_all, b_glu, w_branch_all, b_gate)


def _outproj_kernel(m_ref, w_ref, xs_ref, nw_ref, o_ref):
    mix = jnp.dot(m_ref[...], w_ref[...], preferred_element_type=F32)
    o_ref[...] = xs_ref[...] + _rms(mix, nw_ref[...], 1e-6)


def _outproj(merged, w_out_all, layer, xs, nw, tm):
    n, d = xs.shape
    return pl.pallas_call(
        _outproj_kernel,
        grid=(n // tm,),
        in_specs=[pl.BlockSpec((tm, d), lambda i: (i, 0)),
                  pl.BlockSpec((None, d, d), lambda i: (layer, 0, 0)),
                  pl.BlockSpec((tm, d), lambda i: (i, 0)),
                  pl.BlockSpec((1, d), lambda i: (0, 0))],
        out_specs=pl.BlockSpec((tm, d), lambda i: (i, 0)),
        out_shape=jax.ShapeDtypeStruct((n, d), F32),
        compiler_params=_cparams(("parallel",)),
        name="outproj_residual",
    )(merged, w_out_all, xs, nw)


def _ffn_kernel(xs_ref, nw_in_ref, wg_ref, wu_ref, wo_ref, nw_out_ref, o_ref, h_scr, acc_scr):
    f = pl.program_id(1)

    @pl.when(f == 0)
    def _():
        h_scr[...] = _rms(xs_ref[...], nw_in_ref[...], 1e-6).astype(BF16)
        acc_scr[...] = jnp.zeros(acc_scr.shape, F32)

    h = h_scr[...]
    gate = jnp.dot(h, wg_ref[...], preferred_element_type=F32)
    up = jnp.dot(h, wu_ref[...], preferred_element_type=F32)
    act = (gate * _sigmoid(gate) * up).astype(BF16)
    acc_scr[...] += jnp.dot(act, wo_ref[...], preferred_element_type=F32)

    @pl.when(f == pl.num_programs(1) - 1)
    def _():
        o_ref[...] = xs_ref[...] + _rms(acc_scr[...], nw_out_ref[...], 1e-6)


def _ffn(xs, nw_in, w_ffn_in_all, w_ffn_out_all, nw_out, layer, tm, tf):
    n, d = xs.shape
    d_ff = w_ffn_out_all.shape[1]
    nf = d_ff // tf
    return pl.pallas_call(
        _ffn_kernel,
        grid=(n // tm, nf),
        in_specs=[pl.BlockSpec((tm, d), lambda i, f: (i, 0)),
                  pl.BlockSpec((1, d), lambda i, f: (0, 0)),
                  pl.BlockSpec((None, d, tf), lambda i, f: (layer, 0, f)),
                  pl.BlockSpec((None, d, tf), lambda i, f: (layer, 0, nf + f)),
                  pl.BlockSpec((None, tf, d), lambda i, f: (layer, f, 0)),
                  pl.BlockSpec((1, d), lambda i, f: (0, 0))],
        out_specs=pl.BlockSpec((tm, d), lambda i, f: (i, 0)),
        out_shape=jax.ShapeDtypeStruct((n, d), F32),
        scratch_shapes=[pltpu.VMEM((tm, d), BF16), pltpu.VMEM((tm, d), F32)],
        compiler_params=_cparams(("parallel", "arbitrary")),
        name="swiglu_ffn",
    )(xs, nw_in, w_ffn_in_all, w_ffn_in_all, w_ffn_out_all, nw_out)


def _s5_tables(lam_re, lam_im, b_re, b_im, c_re, c_im, log_step):
    hp = lax.Precision.HIGHEST
    groups = lam_re.shape[0]
    nblk = groups // S5_GB
    lr, li = lam_re.astype(F32), lam_im.astype(F32)
    step = jnp.exp(log_step.astype(F32))[:, None]
    mag = jnp.exp(lr * step)
    ab_re, ab_im = mag * jnp.cos(li * step), mag * jnp.sin(li * step)
    den = lr * lr + li * li
    coef_re = ((ab_re - 1.0) * lr + ab_im * li) / den
    coef_im = (ab_im * lr - (ab_re - 1.0) * li) / den
    br, bi = b_re.astype(F32), b_im.astype(F32)
    bb_re = coef_re[..., None] * br - coef_im[..., None] * bi
    bb_im = coef_re[..., None] * bi + coef_im[..., None] * br
    cr, ci = c_re.astype(F32), c_im.astype(F32)
    eye = jnp.eye(S5_GB, dtype=F32)
    taus = jnp.arange(S5_L)
    lrs = (lr * step).reshape(nblk, 1, S5_HALF)
    lis = (li * step).reshape(nblk, 1, S5_HALF)

    def lam_pow(n):
        nn = n.astype(F32)[None, :, None]
        m = jnp.exp(nn * lrs)
        return m * jnp.cos(nn * lis), m * jnp.sin(nn * lis)

    def blockdiag(t):
        t = t.reshape(nblk, S5_GB, S5_GROUP, S5_STATE)
        return jnp.einsum("ngcp,gh->ngchp", t, eye).reshape(nblk, LANES, S5_HALF)

    bc = jnp.stack([blockdiag(bb_re.transpose(0, 2, 1)), blockdiag(bb_im.transpose(0, 2, 1)),
                    blockdiag(cr), blockdiag(ci)], axis=1)

    nn = taus.astype(F32)[:, None, None]
    mg = jnp.exp(nn * (lr * step)[None])
    pr, pi = mg * jnp.cos(nn * (li * step)[None]), mg * jnp.sin(nn * (li * step)[None])
    c0_re = cr[None] * pr[:, :, None, :] - ci[None] * pi[:, :, None, :]
    c0_im = cr[None] * pi[:, :, None, :] + ci[None] * pr[:, :, None, :]
    kt = (jnp.einsum("tgop,gpi->tgio", c0_re, bb_re, precision=hp)
          - jnp.einsum("tgop,gpi->tgio", c0_im, bb_im, precision=hp))
    kt = kt.reshape(S5_L, nblk, S5_GB, S5_GROUP, S5_GROUP)
    kblk = jnp.einsum("tngio,gh->ntgiho", kt, eye).reshape(nblk, S5_L, LANES, LANES).astype(BF16)

    n_list = jnp.concatenate([S5_L - 1 - taus, 1 + taus, S5_L * jnp.array([1, 2, 4, 0, 0, 0, 0, 0]),
                              S5_L * (1 + jnp.arange(8))])
    wr, wi = lam_pow(n_list)
    pw = jnp.concatenate([wr, wi], axis=2)
    return bc, kblk, pw


def kernel(x, meta, rel_bias, norm_w, w_in, conv_w, conv_b, lru_w_a, lru_b_a, lru_w_x, lru_b_x, lru_lambda, da_lambda, da_subln, s5_lam_re, s5_lam_im, s5_b_re, s5_b_im, s5_c_re, s5_c_im, s5_d, s5_log_step, s5_w_glu, s5_b_glu, b_gate, w_branch, w_out, w_ffn_in, w_ffn_out):
    bsz, seq, d_model = x.shape
    depth = w_in.shape[0]
    lru_w = conv_w.shape[2]
    s5_w = s5_d.shape[1]
    qk_w = DA_HEADS * 2 * DA_HEAD_DIM
    assert lru_w == LRU_BLOCKS * LANES and s5_w % LANES == 0 and d_model % LANES == 0
    col_gate, col_x = 0, lru_w // LANES
    col_q = 2 * lru_w // LANES
    col_k = col_q + qk_w // LANES
    col_v = col_k + qk_w // LANES
    col_u = col_v + DA_HEADS
    col_g = (col_u + s5_w // LANES) * LANES

    t_real = N_META + seq
    tb = 384 if t_real >= 1024 else 128
    t_pad = -(-t_real // tb) * tb
    assert t_pad % SCAN_CH == 0
    n_tok = bsz * t_pad
    tm_in = _pick_tile(n_tok, (1408, 1056, 768, 512, 384, 256, 128))
    tm_merge = _pick_tile(n_tok, (1056, 768, 512, 384, 256, 128))
    tm_ffn = _pick_tile(n_tok, (768, 512, 384, 256, 128))
    tm_out = _pick_tile(n_tok, (512, 384, 256, 128))

    xs = jnp.concatenate([jnp.broadcast_to(meta.astype(F32)[None], (bsz, N_META, d_model)), x,
                          jnp.zeros((bsz, t_pad - t_real, d_model), F32)], axis=1)
    xs = xs.reshape(n_tok, d_model)
    btiles = _bias_tiles(rel_bias.astype(F32), tb)

    w_in_b, w_glu_b, w_branch_b = w_in.astype(BF16), s5_w_glu.astype(BF16), w_branch.astype(BF16)
    w_out_b, w_ffn_in_b, w_ffn_out_b = w_out.astype(BF16), w_ffn_in.astype(BF16), w_ffn_out.astype(BF16)

    for l in range(depth):
        lam_init = 0.8 - 0.6 * math.exp(-0.3 * l)
        proj = _inproj(xs, norm_w[l, 0][None], w_in_b, l, tm_in, 1024)
        proj3 = proj.reshape(bsz, t_pad, proj.shape[1])

        lru_p = jnp.concatenate([conv_w[l], conv_b[l][None], lru_b_a[l][None], lru_b_x[l][None],
                                 jax.nn.log_sigmoid(lru_lambda[l].astype(F32))[None]], axis=0)
        w_ax = jnp.concatenate([lru_w_a[l], lru_w_x[l]], axis=2).astype(BF16)
        y_a = _lru(proj3, lru_p, w_ax, col_gate, col_x)

        y_b = _attn(proj3, btiles, da_lambda[l], da_subln[l][None], lam_init, tb, col_q, col_k, col_v)

        bc, kblk, pw = _s5_tables(s5_lam_re[l], s5_lam_im[l], s5_b_re[l], s5_b_im[l],
                                  s5_c_re[l], s5_c_im[l], s5_log_step[l])
        y_c = _s5(proj3, bc, kblk, pw, s5_d[l][None], col_u)

        merged = _merge(y_a.reshape(n_tok, lru_w), y_b.reshape(n_tok, -1), y_c.reshape(n_tok, s5_w),
                        proj, w_glu_b, s5_b_glu[l][None], w_branch_b, b_gate[l], col_g, l, tm_merge, 512)
        xs = _outproj(merged, w_out_b, l, xs, norm_w[l, 1][None], tm_out)
        xs = _ffn(xs, norm_w[l, 2][None], w_ffn_in_b, w_ffn_out_b, norm_w[l, 3][None], l, tm_ffn, 512)

    return xs.reshape(bsz, t_pad, d_model)[:, N_META:t_real]
```

```python
import functools
import math

import jax
import jax.numpy as jnp
from jax import lax
from jax.experimental import pallas as pl
from jax.experimental.pallas import tpu as pltpu

F32 = jnp.float32
BF16 = jnp.bfloat16

N_META = 16
CONV_W = 4
LRU_C = 8.0
LRU_BLOCKS = 8
DA_HEADS = 8
DA_HEAD_DIM = 64
S5_GROUP = 16
S5_STATE = 64
REL_BUCKETS = 32
REL_MAX_DIST = 128
N_BRANCH = 3

LANES = 128
S5_GB = LANES // S5_GROUP
S5_HALF = S5_GB * S5_STATE
S5_L = 8
SCAN_CH = 128
MASK_NEG = -1e30
LOG2E = math.log2(math.e)
VMEM_LIMIT = 56 * 1024 * 1024


def _cparams(sem):
    return pltpu.CompilerParams(dimension_semantics=sem, vmem_limit_bytes=VMEM_LIMIT)


def _pick_tile(n, candidates):
    for c in candidates:
        if n % c == 0:
            return c
    raise ValueError(f"no tile in {candidates} divides {n}")


def _gelu_tanh(x):
    return 0.5 * x * (1.0 + jnp.tanh(math.sqrt(2.0 / math.pi) * (x + 0.044715 * (x * x * x))))


def _sigmoid(x):
    return 1.0 / (1.0 + jnp.exp(-x))


def _rms(x, w, eps):
    return (x * lax.rsqrt(jnp.mean(x * x, axis=-1, keepdims=True) + eps)) * w


def _inproj_kernel(x_ref, nw_ref, w_ref, o_ref, h_scr):
    @pl.when(pl.program_id(1) == 0)
    def _():
        h_scr[...] = _rms(x_ref[...], nw_ref[...], 1e-6).astype(BF16)

    o_ref[...] = jnp.dot(h_scr[...], w_ref[...], preferred_element_type=F32).astype(o_ref.dtype)


def _inproj(xs, nw, w_all, layer, tm, tn):
    n, d = xs.shape
    n_out = w_all.shape[2]
    return pl.pallas_call(
        _inproj_kernel,
        grid=(n // tm, n_out // tn),
        in_specs=[pl.BlockSpec((tm, d), lambda i, j: (i, 0)),
                  pl.BlockSpec((1, d), lambda i, j: (0, 0)),
                  pl.BlockSpec((None, d, tn), lambda i, j: (layer, 0, j))],
        out_specs=pl.BlockSpec((tm, tn), lambda i, j: (i, j)),
        out_shape=jax.ShapeDtypeStruct((n, n_out), BF16),
        scratch_shapes=[pltpu.VMEM((tm, d), BF16)],
        compiler_params=_cparams(("parallel", "arbitrary")),
        name="inproj",
    )(xs, nw, w_all)


def _lru_kernel(g_ref, x_ref, p_ref, w_ref, o_ref, xs_scr):
    t_len = x_ref.shape[1]
    halo = 8
    xs_scr[0:halo, :] = jnp.zeros((halo, LANES), F32)
    xs_scr[halo:, :] = x_ref[0].astype(F32)
    p = p_ref[...]
    cw = [p[j:j + 1, :] for j in range(CONV_W)]
    cb, ba, bx, logsig = p[4:5, :], p[5:6, :], p[6:7, :], p[7:8, :]
    w = w_ref[0]
    row = lax.broadcasted_iota(jnp.int32, (SCAN_CH, LANES), 0)

    def local_scan(base):
        xc = cb
        for j in range(CONV_W):
            xc = xc + cw[j] * xs_scr[pl.ds(base + halo - (CONV_W - 1) + j, SCAN_CH), :]
        ri = jnp.dot(xc.astype(BF16), w, preferred_element_type=F32)
        r = _sigmoid(ri[:, :LANES] + ba)
        i = _sigmoid(ri[:, LANES:] + bx)
        a = jnp.exp(LRU_C * r * logsig)
        b = jnp.sqrt(1.0 - a * a) * (i * xc)
        s = 1
        while s < SCAN_CH:
            keep = row >= s
            a_sh = jnp.where(keep, pltpu.roll(a, s, 0), 1.0)
            b_sh = jnp.where(keep, pltpu.roll(b, s, 0), 0.0)
            b = a * b_sh + b
            a = a * a_sh
            s *= 2
        return a, b

    n_chunks = t_len // SCAN_CH
    sub = 3 if n_chunks % 3 == 0 else 1

    def chunk(c, h0):
        bases = [pl.multiple_of((c * sub + j) * SCAN_CH, SCAN_CH) for j in range(sub)]
        scans = [local_scan(base) for base in bases]
        for base, (a, b) in zip(bases, scans):
            h = a * h0 + b
            g = g_ref[0, pl.ds(base, SCAN_CH), :].astype(F32)
            o_ref[0, pl.ds(base, SCAN_CH), :] = (h * _gelu_tanh(g)).astype(o_ref.dtype)
            h0 = h[SCAN_CH - 1:SCAN_CH, :]
        return h0

    lax.fori_loop(0, n_chunks // sub, chunk, jnp.zeros((1, LANES), F32))


def _lru(proj3, lru_p, w_ax, col_gate, col_x):
    b, t, _ = proj3.shape
    width = LRU_BLOCKS * LANES
    return pl.pallas_call(
        _lru_kernel,
        grid=(b, LRU_BLOCKS),
        in_specs=[pl.BlockSpec((1, t, LANES), lambda bi, h: (bi, 0, col_gate + h)),
                  pl.BlockSpec((1, t, LANES), lambda bi, h: (bi, 0, col_x + h)),
                  pl.BlockSpec((8, LANES), lambda bi, h: (0, h)),
                  pl.BlockSpec((1, LANES, 2 * LANES), lambda bi, h: (h, 0, 0))],
        out_specs=pl.BlockSpec((1, t, LANES), lambda bi, h: (bi, 0, h)),
        out_shape=jax.ShapeDtypeStruct((b, t, width), BF16),
        scratch_shapes=[pltpu.VMEM((t + 8, LANES), F32)],
        compiler_params=_cparams(("parallel", "parallel")),
        name="rglru",
    )(proj3, proj3, lru_p, w_ax)


def _bias_tile_kernel(rb_ref, o_ref):
    h = pl.program_id(0)
    tb = o_ref.shape[1]
    i = lax.broadcasted_iota(jnp.int32, (tb, 2 * tb), 0)
    j2 = lax.broadcasted_iota(jnp.int32, (tb, 2 * tb), 1)
    max_exact = REL_BUCKETS // 2
    d = i - j2 + tb
    n = jnp.maximum(d, 0)
    nf = jnp.maximum(n, 1).astype(F32)
    large = max_exact + (jnp.log(nf / max_exact) / math.log(REL_MAX_DIST / max_exact)
                         * (REL_BUCKETS - max_exact)).astype(jnp.int32)
    large = jnp.minimum(large, REL_BUCKETS - 1)
    bucket = jnp.where(n < max_exact, n, large)
    val = jnp.zeros((tb, 2 * tb), F32)
    for bkt in range(REL_BUCKETS):
        val = jnp.where(bucket == bkt, rb_ref[bkt, h], val)
    val = val - rb_ref[REL_BUCKETS - 1, h]
    o_ref[0] = jnp.where(d >= 0, val, MASK_NEG)


def _bias_tiles(rel_bias, tb):
    return pl.pallas_call(
        _bias_tile_kernel,
        grid=(DA_HEADS,),
        in_specs=[pl.BlockSpec(memory_space=pltpu.SMEM)],
        out_specs=pl.BlockSpec((1, tb, 2 * tb), lambda h: (h, 0, 0)),
        out_shape=jax.ShapeDtypeStruct((DA_HEADS, tb, 2 * tb), F32),
        compiler_params=_cparams(("parallel",)),
        name="t5_bias_tiles",
    )(rel_bias)


def _attn_qblock(qi, q_ref, k_ref, v_ref, bt_ref, sw_ref, o_ref, m_scr, l_scr, acc_scr, s_scr, lam,
                 lam_init):
    tb = s_scr.shape[2]
    dn_t = (((1,), (1,)), ((), ()))
    lane = lax.broadcasted_iota(jnp.int32, (tb, LANES), 1)
    qstart = pl.multiple_of(qi * tb, tb)
    qs = q_ref[0, pl.ds(qstart, tb), :] * (DA_HEAD_DIM ** -0.5)
    zero = jnp.zeros_like(qs)
    qcat = jnp.concatenate([jnp.where(lane < DA_HEAD_DIM, qs, zero),
                            jnp.where(lane >= DA_HEAD_DIM, qs, zero)], axis=0)
    n_far = jnp.maximum(qi - 1, 0)
    n_quads = n_far // 4
    n_rest = n_far % 4
    tail_blk = 4 * n_quads

    def scores(blk, nblocks, biased):
        kb = k_ref[0, pl.ds(pl.multiple_of(blk * tb, tb), nblocks * tb), :]
        s_both = lax.dot_general(qcat, kb, dn_t, preferred_element_type=F32)
        out = []
        for c in range(2):
            s = s_both[c * tb:(c + 1) * tb]
            if biased:
                nb = min(nblocks, 2)
                plain = (nblocks - nb) * tb
                tail = s[:, plain:] + bt_ref[0, :, (2 - nb) * tb:]
                s = tail if plain == 0 else jnp.concatenate([s[:, :plain], tail], axis=1)
            s = s * LOG2E
            for i in range(nblocks):
                s_scr[c, blk + i] = s[:, i * tb:(i + 1) * tb]
            out.append(s)
        return out

    def lane_chunks(s):
        return [s[:, j * LANES:(j + 1) * LANES] for j in range(s.shape[1] // LANES)]

    def max_into(ms, ss):
        out = []
        for m, s in zip(ms, ss):
            for ch in lane_chunks(s):
                m = jnp.maximum(m, ch)
            out.append(m)
        return out

    m_init = jnp.full((tb, LANES), MASK_NEG, F32)

    @pl.when(qi == 0)
    def _():
        m_scr[0], m_scr[1] = max_into((m_init, m_init), scores(0, 1, True))

    for rest in range(4):
        @pl.when(jnp.logical_and(qi >= 1, n_rest == rest))
        def _():
            m_scr[0], m_scr[1] = max_into((m_init, m_init), scores(tail_blk, rest + 2, True))

    def quad_max(ki, ms):
        return tuple(max_into(ms, scores(4 * ki, 4, False)))

    ms = lax.fori_loop(0, n_quads, quad_max, (m_scr[0], m_scr[1]))
    mb = [jnp.broadcast_to(jnp.max(ms[c], axis=-1, keepdims=True), (tb, LANES)) for c in range(2)]

    def accumulate(blk, nblocks, st):
        vb = v_ref[0, pl.ds(pl.multiple_of(blk * tb, tb), nblocks * tb), :]
        out = []
        for c in range(2):
            ps = []
            for i in range(nblocks):
                ps += [jnp.exp2(ch - mb[c]) for ch in lane_chunks(s_scr[c, blk + i])]
            l = st[2 * c]
            for ch in ps:
                l = l + ch
            p = jnp.concatenate(ps, axis=1).astype(BF16)
            out += [l, st[2 * c + 1] + jnp.dot(p, vb, preferred_element_type=F32)]
        return tuple(out)

    def store_state(st):
        l_scr[0], acc_scr[0], l_scr[1], acc_scr[1] = st

    zl = jnp.zeros((tb, LANES), F32)

    @pl.when(qi == 0)
    def _():
        store_state(accumulate(0, 1, (zl, zl, zl, zl)))

    for rest in range(4):
        @pl.when(jnp.logical_and(qi >= 1, n_rest == rest))
        def _():
            store_state(accumulate(tail_blk, rest + 2, (zl, zl, zl, zl)))

    l0, a0, l1, a1 = lax.fori_loop(0, n_quads, lambda ki, st: accumulate(4 * ki, 4, st),
                                   (l_scr[0], acc_scr[0], l_scr[1], acc_scr[1]))

    o = (a0 / jnp.sum(l0, axis=-1, keepdims=True)
         - lam * (a1 / jnp.sum(l1, axis=-1, keepdims=True)))
    o = _rms(o, sw_ref[...], 1e-5) * (1.0 - lam_init)
    o_ref[0, pl.ds(qstart, tb), :] = o.astype(o_ref.dtype)


def _attn_kernel(q_ref, k_ref, v_ref, bt_ref, dl_ref, sw_ref, o_ref, m_scr, l_scr, acc_scr, s_scr, *,
                 lam_init):
    dl = dl_ref[...]
    lam = (jnp.exp(jnp.sum(dl[0:1, :] * dl[1:2, :], axis=-1, keepdims=True))
           - jnp.exp(jnp.sum(dl[2:3, :] * dl[3:4, :], axis=-1, keepdims=True)) + lam_init)

    def qblock(qi, carry):
        _attn_qblock(qi, q_ref, k_ref, v_ref, bt_ref, sw_ref, o_ref, m_scr, l_scr, acc_scr, s_scr, lam,
                     lam_init)
        return carry

    lax.fori_loop(0, q_ref.shape[1] // s_scr.shape[2], qblock, 0)


def _attn(proj3, btiles, da_lambda, da_subln, lam_init, tb, col_q, col_k, col_v):
    b, t, _ = proj3.shape
    width = DA_HEADS * LANES
    return pl.pallas_call(
        functools.partial(_attn_kernel, lam_init=lam_init),
        grid=(b, DA_HEADS),
        in_specs=[pl.BlockSpec((1, t, LANES), lambda bi, h: (bi, 0, col_q + h)),
                  pl.BlockSpec((1, t, LANES), lambda bi, h: (bi, 0, col_k + h)),
                  pl.BlockSpec((1, t, LANES), lambda bi, h: (bi, 0, col_v + h)),
                  pl.BlockSpec((1, tb, 2 * tb), lambda bi, h: (h, 0, 0)),
                  pl.BlockSpec((4, DA_HEAD_DIM), lambda bi, h: (0, 0)),
                  pl.BlockSpec((1, LANES), lambda bi, h: (0, 0))],
        out_specs=pl.BlockSpec((1, t, LANES), lambda bi, h: (bi, 0, h)),
        out_shape=jax.ShapeDtypeStruct((b, t, width), BF16),
        scratch_shapes=[pltpu.VMEM((2, tb, LANES), F32), pltpu.VMEM((2, tb, LANES), F32),
                        pltpu.VMEM((2, tb, LANES), F32), pltpu.VMEM((2, t // tb, tb, tb), F32)],
        compiler_params=_cparams(("parallel", "parallel")),
        name="diff_attn",
    )(proj3, proj3, proj3, btiles, da_lambda, da_subln)


def _s5_kernel(u_ref, bc_ref, pw_ref, d_ref, o_ref, uf_scr, g_scr, hp_scr, y_scr,
               wg_scr, wk_scr, wct_scr):
    t_len = u_ref.shape[1]
    nj = t_len // S5_L
    pw = pw_ref[0]

    @pl.when(pl.program_id(1) == 0)
    def _():
        bd_re, bd_im, ct_re, ct_im = bc_ref[0, 0], bc_ref[0, 1], bc_ref[0, 2], bc_ref[0, 3]
        for s in range(S5_L):
            rows = slice(s * LANES, (s + 1) * LANES)
            qr, qi = pw[s:s + 1, :S5_HALF], pw[s:s + 1, S5_HALF:]
            wg_scr[rows, :S5_HALF] = (qr * bd_re - qi * bd_im).astype(BF16)
            wg_scr[rows, S5_HALF:] = (qr * bd_im + qi * bd_re).astype(BF16)
            er, ei = pw[8 + s:9 + s, :S5_HALF], pw[8 + s:9 + s, S5_HALF:]
            wct_scr[rows, :S5_HALF] = (ct_re * er - ct_im * ei).astype(BF16)
            wct_scr[rows, S5_HALF:] = (-(ct_re * ei + ct_im * er)).astype(BF16)
        dn_t = (((1,), (1,)), ((), ()))
        bcat = jnp.concatenate([bd_re, bd_im], axis=1).astype(BF16)
        crows = [jnp.concatenate([ct_re, -ct_im], axis=1).astype(BF16)]
        crows += [wct_scr[s * LANES:(s + 1) * LANES, :] for s in range(S5_L - 1)]
        kts = [lax.dot_general(bcat, c, dn_t, preferred_element_type=F32).astype(BF16) for c in crows]
        for s in range(S5_L):
            for r in range(S5_L):
                wk_scr[s * LANES:(s + 1) * LANES, r * LANES:(r + 1) * LANES] = (
                    kts[r - s] if r >= s else jnp.zeros((LANES, LANES), BF16))

    uf_scr[...] = u_ref[0].astype(F32)
    ur = jnp.concatenate([uf_scr[pl.ds(s, nj, stride=S5_L), :].astype(BF16) for s in range(S5_L)],
                         axis=1)
    g_scr[...] = jnp.dot(ur, wg_scr[...], preferred_element_type=F32)

    pr, pi = pw[24:32, :S5_HALF], pw[24:32, S5_HALF:]
    row = lax.broadcasted_iota(jnp.int32, (8, S5_HALF), 0)

    def local_scan(base):
        x = g_scr[pl.ds(base, 8), :]
        xr, xi = x[:, :S5_HALF], x[:, S5_HALF:]
        for lvl, s in enumerate((1, 2, 4)):
            ar, ai = pw[16 + lvl:17 + lvl, :S5_HALF], pw[16 + lvl:17 + lvl, S5_HALF:]
            keep = row >= s
            sr = jnp.where(keep, pltpu.roll(xr, s, 0), 0.0)
            si = jnp.where(keep, pltpu.roll(xi, s, 0), 0.0)
            xr, xi = xr + (ar * sr - ai * si), xi + (ar * si + ai * sr)
        return xr, xi

    first = row == 0
    per_iter = 2

    def groups(gi, carry):
        cr, ci = carry
        bases = [pl.multiple_of((gi * per_iter + k) * 8, 8) for k in range(per_iter)]
        scans = [local_scan(base) for base in bases]
        for base, (xr, xi) in zip(bases, scans):
            xr, xi = xr + (pr * cr - pi * ci), xi + (pr * ci + pi * cr)
            hp_scr[pl.ds(base, 8), :] = jnp.concatenate(
                [jnp.where(first, cr, pltpu.roll(xr, 1, 0)), jnp.where(first, ci, pltpu.roll(xi, 1, 0))],
                axis=1)
            cr, ci = xr[7:8, :], xi[7:8, :]
        return cr, ci

    z = jnp.zeros((1, S5_HALF), F32)
    lax.fori_loop(0, nj // (8 * per_iter), groups, (z, z))

    y = (jnp.dot(ur, wk_scr[...], preferred_element_type=F32)
         + lax.dot_general(hp_scr[...].astype(BF16), wct_scr[...], (((1,), (1,)), ((), ())),
                           preferred_element_type=F32))
    for s in range(S5_L):
        y_scr[pl.ds(s, nj, stride=S5_L), :] = y[:, s * LANES:(s + 1) * LANES]
    o_ref[0] = _gelu_tanh(y_scr[...] + d_ref[...] * uf_scr[...]).astype(o_ref.dtype)


def _s5(proj3, bc, pw, dvec, col_u):
    b, t, _ = proj3.shape
    nblk = bc.shape[0]
    nj = t // S5_L
    wide = S5_L * LANES
    assert nj % 16 == 0
    return pl.pallas_call(
        _s5_kernel,
        grid=(nblk, b),
        in_specs=[pl.BlockSpec((1, t, LANES), lambda g, bi: (bi, 0, col_u + g)),
                  pl.BlockSpec((1, 4, LANES, S5_HALF), lambda g, bi: (g, 0, 0, 0)),
                  pl.BlockSpec((1, 32, 2 * S5_HALF), lambda g, bi: (g, 0, 0)),
                  pl.BlockSpec((1, LANES), lambda g, bi: (0, g))],
        out_specs=pl.BlockSpec((1, t, LANES), lambda g, bi: (bi, 0, g)),
        out_shape=jax.ShapeDtypeStruct((b, t, nblk * LANES), BF16),
        scratch_shapes=[pltpu.VMEM((t, LANES), F32), pltpu.VMEM((nj, 2 * S5_HALF), F32),
                        pltpu.VMEM((nj, 2 * S5_HALF), F32), pltpu.VMEM((t, LANES), F32),
                        pltpu.VMEM((wide, 2 * S5_HALF), BF16), pltpu.VMEM((wide, wide), BF16),
                        pltpu.VMEM((wide, 2 * S5_HALF), BF16)],
        compiler_params=_cparams(("parallel", "arbitrary")),
        name="s5",
    )(proj3, bc, pw, dvec)


def _merge_kernel(ya_ref, yb_ref, yc_ref, g0_ref, g1_ref, g2_ref, wg_ref, bg_ref, wb_ref, bgate_ref,
                  o_ref, yc_scr):
    @pl.when(pl.program_id(1) == 0)
    def _():
        yc = yc_ref[...]
        z = jnp.dot(yc, wg_ref[...], preferred_element_type=F32) + bg_ref[...]
        yc_scr[...] = (yc.astype(F32) * _sigmoid(z)).astype(BF16)

    bgate = bgate_ref[...]
    ys = (ya_ref[...], yb_ref[...], yc_scr[...])
    gs = (g0_ref, g1_ref, g2_ref)
    merged = None
    for br in range(N_BRANCH):
        gate = _sigmoid(gs[br][...].astype(F32) + bgate[br:br + 1, :])
        term = gate * jnp.dot(ys[br], wb_ref[br], preferred_element_type=F32)
        merged = term if merged is None else merged + term
    o_ref[...] = merged.astype(o_ref.dtype)


def _merge(ya, yb, yc, proj, w_glu_all, b_glu, w_branch_all, b_gate, col_g, layer, tm, tn):
    n, wdt = ya.shape
    d = w_branch_all.shape[3]
    gcol = [(col_g + br * d) // tn for br in range(N_BRANCH)]
    yspec = pl.BlockSpec((tm, wdt), lambda i, j: (i, 0))

    def gspec(br):
        return pl.BlockSpec((tm, tn), lambda i, j: (i, gcol[br] + j))

    return pl.pallas_call(
        _merge_kernel,
        grid=(n // tm, d // tn),
        in_specs=[yspec, yspec, yspec, gspec(0), gspec(1), gspec(2),
                  pl.BlockSpec((None, wdt, wdt), lambda i, j: (layer, 0, 0)),
                  pl.BlockSpec((1, wdt), lambda i, j: (0, 0)),
                  pl.BlockSpec((None, N_BRANCH, wdt, tn), lambda i, j: (layer, 0, 0, j)),
                  pl.BlockSpec((N_BRANCH, tn), lambda i, j: (0, j))],
        out_specs=pl.BlockSpec((tm, tn), lambda i, j: (i, j)),
        out_shape=jax.ShapeDtypeStruct((n, d), BF16),
        scratch_shapes=[pltpu.VMEM((tm, wdt), BF16)],
        compiler_params=_cparams(("parallel", "arbitrary")),
        name="merge",
    )(ya, yb, yc, proj, proj, proj, w_glu_all, b_glu, w_branch_all, b_gate)


def _outproj_kernel(m_ref, w_ref, xs_ref, nw_ref, o_ref):
    mix = jnp.dot(m_ref[...], w_ref[...], preferred_element_type=F32)
    o_ref[...] = xs_ref[...] + _rms(mix, nw_ref[...], 1e-6)


def _outproj(merged, w_out_all, layer, xs, nw, tm):
    n, d = xs.shape
    return pl.pallas_call(
        _outproj_kernel,
        grid=(n // tm,),
        in_specs=[pl.BlockSpec((tm, d), lambda i: (i, 0)),
                  pl.BlockSpec((None, d, d), lambda i: (layer, 0, 0)),
                  pl.BlockSpec((tm, d), lambda i: (i, 0)),
                  pl.BlockSpec((1, d), lambda i: (0, 0))],
        out_specs=pl.BlockSpec((tm, d), lambda i: (i, 0)),
        out_shape=jax.ShapeDtypeStruct((n, d), F32),
        compiler_params=_cparams(("parallel",)),
        name="outproj_residual",
    )(merged, w_out_all, xs, nw)


def _ffn_kernel(xs_ref, nw_in_ref, wg_ref, wu_ref, wo_ref, nw_out_ref, o_ref, h_scr, acc_scr):
    f = pl.program_id(1)

    @pl.when(f == 0)
    def _():
        h_scr[...] = _rms(xs_ref[...], nw_in_ref[...], 1e-6).astype(BF16)
        acc_scr[...] = jnp.zeros(acc_scr.shape, F32)

    h = h_scr[...]
    gate = jnp.dot(h, wg_ref[...], preferred_element_type=F32)
    up = jnp.dot(h, wu_ref[...], preferred_element_type=F32)
    act = (gate * _sigmoid(gate) * up).astype(BF16)
    acc_scr[...] += jnp.dot(act, wo_ref[...], preferred_element_type=F32)

    @pl.when(f == pl.num_programs(1) - 1)
    def _():
        o_ref[...] = xs_ref[...] + _rms(acc_scr[...], nw_out_ref[...], 1e-6)


def _ffn(xs, nw_in, w_ffn_in_all, w_ffn_out_all, nw_out, layer, tm, tf, real_rows=None):
    n, d = xs.shape
    d_ff = w_ffn_out_all.shape[1]
    nf = d_ff // tf
    if real_rows is None:
        n_out = n
        xs_spec = pl.BlockSpec((tm, d), lambda i, f: (i, 0))
    else:
        t_pad, first, count = real_rows
        per_seq = count // tm
        n_out = (n // t_pad) * count
        xs_spec = pl.BlockSpec((pl.Element(tm), pl.Element(d)),
                               lambda i, f: (pl.multiple_of(
                                   (i // per_seq) * t_pad + first + (i % per_seq) * tm, 8), 0))
        assert t_pad % 8 == 0 and first % 8 == 0 and tm % 8 == 0
    return pl.pallas_call(
        _ffn_kernel,
        grid=(n_out // tm, nf),
        in_specs=[xs_spec,
                  pl.BlockSpec((1, d), lambda i, f: (0, 0)),
                  pl.BlockSpec((None, d, tf), lambda i, f: (layer, 0, f)),
                  pl.BlockSpec((None, d, tf), lambda i, f: (layer, 0, nf + f)),
                  pl.BlockSpec((None, tf, d), lambda i, f: (layer, f, 0)),
                  pl.BlockSpec((1, d), lambda i, f: (0, 0))],
        out_specs=pl.BlockSpec((tm, d), lambda i, f: (i, 0)),
        out_shape=jax.ShapeDtypeStruct((n_out, d), F32),
        scratch_shapes=[pltpu.VMEM((tm, d), BF16), pltpu.VMEM((tm, d), F32)],
        compiler_params=_cparams(("parallel", "arbitrary")),
        name="swiglu_ffn",
    )(xs, nw_in, w_ffn_in_all, w_ffn_in_all, w_ffn_out_all, nw_out)


def _s5_tables(lam_re, lam_im, b_re, b_im, c_re, c_im, log_step):
    groups = lam_re.shape[0]
    nblk = groups // S5_GB
    lr, li = lam_re.astype(F32), lam_im.astype(F32)
    step = jnp.exp(log_step.astype(F32))[:, None]
    mag = jnp.exp(lr * step)
    ab_re, ab_im = mag * jnp.cos(li * step), mag * jnp.sin(li * step)
    den = lr * lr + li * li
    coef_re = ((ab_re - 1.0) * lr + ab_im * li) / den
    coef_im = (ab_im * lr - (ab_re - 1.0) * li) / den
    br, bi = b_re.astype(F32), b_im.astype(F32)
    bb_re = coef_re[..., None] * br - coef_im[..., None] * bi
    bb_im = coef_re[..., None] * bi + coef_im[..., None] * br
    cr, ci = c_re.astype(F32), c_im.astype(F32)
    eye = jnp.eye(S5_GB, dtype=F32)
    taus = jnp.arange(S5_L)
    lrs = (lr * step).reshape(nblk, 1, S5_HALF)
    lis = (li * step).reshape(nblk, 1, S5_HALF)

    def lam_pow(n):
        nn = n.astype(F32)[None, :, None]
        m = jnp.exp(nn * lrs)
        return m * jnp.cos(nn * lis), m * jnp.sin(nn * lis)

    def blockdiag(t):
        t = t.reshape(nblk, S5_GB, S5_GROUP, S5_STATE)
        return jnp.einsum("ngcp,gh->ngchp", t, eye).reshape(nblk, LANES, S5_HALF)

    bc = jnp.stack([blockdiag(bb_re.transpose(0, 2, 1)), blockdiag(bb_im.transpose(0, 2, 1)),
                    blockdiag(cr), blockdiag(ci)], axis=1)

    n_list = jnp.concatenate([S5_L - 1 - taus, 1 + taus, S5_L * jnp.array([1, 2, 4, 0, 0, 0, 0, 0]),
                              S5_L * (1 + jnp.arange(8))])
    wr, wi = lam_pow(n_list)
    pw = jnp.concatenate([wr, wi], axis=2)
    return bc, pw


def kernel(x, meta, rel_bias, norm_w, w_in, conv_w, conv_b, lru_w_a, lru_b_a, lru_w_x, lru_b_x, lru_lambda, da_lambda, da_subln, s5_lam_re, s5_lam_im, s5_b_re, s5_b_im, s5_c_re, s5_c_im, s5_d, s5_log_step, s5_w_glu, s5_b_glu, b_gate, w_branch, w_out, w_ffn_in, w_ffn_out):
    bsz, seq, d_model = x.shape
    depth = w_in.shape[0]
    lru_w = conv_w.shape[2]
    s5_w = s5_d.shape[1]
    qk_w = DA_HEADS * 2 * DA_HEAD_DIM
    assert lru_w == LRU_BLOCKS * LANES and s5_w % LANES == 0 and d_model % LANES == 0
    col_gate, col_x = 0, lru_w // LANES
    col_q = 2 * lru_w // LANES
    col_k = col_q + qk_w // LANES
    col_v = col_k + qk_w // LANES
    col_u = col_v + DA_HEADS
    col_g = (col_u + s5_w // LANES) * LANES

    t_real = N_META + seq
    tb = 384 if t_real >= 1024 else 128
    t_pad = -(-t_real // tb) * tb
    assert t_pad % SCAN_CH == 0
    n_tok = bsz * t_pad
    tm_in = _pick_tile(n_tok, (1408, 1056, 768, 512, 384, 256, 128))
    tm_merge = _pick_tile(n_tok, (1056, 768, 512, 384, 256, 128))
    tm_ffn = _pick_tile(n_tok, (768, 512, 384, 256, 128))
    tm_out = _pick_tile(n_tok, (512, 384, 256, 128))
    tm_last = _pick_tile(seq, (512, 256, 128))

    xs = jnp.concatenate([jnp.broadcast_to(meta.astype(F32)[None], (bsz, N_META, d_model)), x,
                          jnp.zeros((bsz, t_pad - t_real, d_model), F32)], axis=1)
    xs = xs.reshape(n_tok, d_model)
    btiles = _bias_tiles(rel_bias.astype(F32), tb)

    w_in_b, w_glu_b, w_branch_b = w_in.astype(BF16), s5_w_glu.astype(BF16), w_branch.astype(BF16)
    w_out_b, w_ffn_in_b, w_ffn_out_b = w_out.astype(BF16), w_ffn_in.astype(BF16), w_ffn_out.astype(BF16)

    for l in range(depth):
        lam_init = 0.8 - 0.6 * math.exp(-0.3 * l)
        proj = _inproj(xs, norm_w[l, 0][None], w_in_b, l, tm_in, 1024)
        proj3 = proj.reshape(bsz, t_pad, proj.shape[1])

        lru_p = jnp.concatenate([conv_w[l], conv_b[l][None], lru_b_a[l][None], lru_b_x[l][None],
                                 jax.nn.log_sigmoid(lru_lambda[l].astype(F32))[None]], axis=0)
        w_ax = jnp.concatenate([lru_w_a[l], lru_w_x[l]], axis=2).astype(BF16)
        y_a = _lru(proj3, lru_p, w_ax, col_gate, col_x)

        y_b = _attn(proj3, btiles, da_lambda[l], da_subln[l][None], lam_init, tb, col_q, col_k, col_v)

        bc, pw = _s5_tables(s5_lam_re[l], s5_lam_im[l], s5_b_re[l], s5_b_im[l],
                            s5_c_re[l], s5_c_im[l], s5_log_step[l])
        y_c = _s5(proj3, bc, pw, s5_d[l][None], col_u)

        merged = _merge(y_a.reshape(n_tok, lru_w), y_b.reshape(n_tok, -1), y_c.reshape(n_tok, s5_w),
                        proj, w_glu_b, s5_b_glu[l][None], w_branch_b, b_gate[l], col_g, l, tm_merge, 512)
        xs = _outproj(merged, w_out_b, l, xs, norm_w[l, 1][None], tm_out)
        if l + 1 < depth:
            xs = _ffn(xs, norm_w[l, 2][None], w_ffn_in_b, w_ffn_out_b, norm_w[l, 3][None], l, tm_ffn, 512)
        else:
            out = _ffn(xs, norm_w[l, 2][None], w_ffn_in_b, w_ffn_out_b, norm_w[l, 3][None], l, tm_last,
                       512, real_rows=(t_pad, N_META, seq))

    return out.reshape(bsz, seq, d_model)
```

```python
import functools
import math

import jax
import jax.numpy as jnp
from jax import lax
from jax.experimental import pallas as pl
from jax.experimental.pallas import tpu as pltpu

F32 = jnp.float32
BF16 = jnp.bfloat16

N_META = 16
CONV_W = 4
LRU_C = 8.0
LRU_BLOCKS = 8
DA_HEADS = 8
DA_HEAD_DIM = 64
S5_GROUP = 16
S5_STATE = 64
REL_BUCKETS = 32
REL_MAX_DIST = 128
N_BRANCH = 3

LANES = 128
S5_GB = LANES // S5_GROUP
S5_HALF = S5_GB * S5_STATE
S5_L = 8
SCAN_CH = 128
MASK_NEG = -1e30
LOG2E = math.log2(math.e)
VMEM_LIMIT = 56 * 1024 * 1024


def _cparams(sem):
    return pltpu.CompilerParams(dimension_semantics=sem, vmem_limit_bytes=VMEM_LIMIT)


def _pick_tile(n, candidates):
    for c in candidates:
        if n % c == 0:
            return c
    raise ValueError(f"no tile in {candidates} divides {n}")


def _gelu_tanh(x):
    return 0.5 * x * (1.0 + jnp.tanh(math.sqrt(2.0 / math.pi) * (x + 0.044715 * (x * x * x))))


def _sigmoid(x):
    return 1.0 / (1.0 + jnp.exp(-x))


def _rms(x, w, eps):
    return (x * lax.rsqrt(jnp.mean(x * x, axis=-1, keepdims=True) + eps)) * w


def _inproj_kernel(x_ref, nw_ref, w_ref, o_ref, h_scr):
    @pl.when(pl.program_id(1) == 0)
    def _():
        h_scr[...] = _rms(x_ref[...], nw_ref[...], 1e-6).astype(BF16)

    o_ref[...] = jnp.dot(h_scr[...], w_ref[...].astype(BF16),
                         preferred_element_type=F32).astype(o_ref.dtype)


def _inproj(xs, nw, w_all, layer, tm, tn):
    n, d = xs.shape
    n_out = w_all.shape[2]
    return pl.pallas_call(
        _inproj_kernel,
        grid=(n // tm, n_out // tn),
        in_specs=[pl.BlockSpec((tm, d), lambda i, j: (i, 0)),
                  pl.BlockSpec((1, d), lambda i, j: (0, 0)),
                  pl.BlockSpec((None, d, tn), lambda i, j: (layer, 0, j))],
        out_specs=pl.BlockSpec((tm, tn), lambda i, j: (i, j)),
        out_shape=jax.ShapeDtypeStruct((n, n_out), BF16),
        scratch_shapes=[pltpu.VMEM((tm, d), BF16)],
        compiler_params=_cparams(("parallel", "arbitrary")),
        name="inproj",
    )(xs, nw, w_all)


def _lru_kernel(g_ref, x_ref, p_ref, w_ref, o_ref, xs_scr):
    t_len = x_ref.shape[1]
    halo = 8
    xs_scr[0:halo, :] = jnp.zeros((halo, LANES), F32)
    xs_scr[halo:, :] = x_ref[0].astype(F32)
    p = p_ref[...]
    cw = [p[j:j + 1, :] for j in range(CONV_W)]
    cb, ba, bx, logsig = p[4:5, :], p[5:6, :], p[6:7, :], p[7:8, :]
    w = w_ref[0]
    row = lax.broadcasted_iota(jnp.int32, (SCAN_CH, LANES), 0)

    def local_scan(base):
        xc = cb
        for j in range(CONV_W):
            xc = xc + cw[j] * xs_scr[pl.ds(base + halo - (CONV_W - 1) + j, SCAN_CH), :]
        ri = jnp.dot(xc.astype(BF16), w, preferred_element_type=F32)
        r = _sigmoid(ri[:, :LANES] + ba)
        i = _sigmoid(ri[:, LANES:] + bx)
        a = jnp.exp(LRU_C * r * logsig)
        b = jnp.sqrt(1.0 - a * a) * (i * xc)
        s = 1
        while s < SCAN_CH:
            keep = row >= s
            a_sh = jnp.where(keep, pltpu.roll(a, s, 0), 1.0)
            b_sh = jnp.where(keep, pltpu.roll(b, s, 0), 0.0)
            b = a * b_sh + b
            a = a * a_sh
            s *= 2
        return a, b

    n_chunks = t_len // SCAN_CH
    sub = 3 if n_chunks % 3 == 0 else 1

    def chunk(c, h0):
        bases = [pl.multiple_of((c * sub + j) * SCAN_CH, SCAN_CH) for j in range(sub)]
        scans = [local_scan(base) for base in bases]
        for base, (a, b) in zip(bases, scans):
            h = a * h0 + b
            g = g_ref[0, pl.ds(base, SCAN_CH), :].astype(F32)
            o_ref[0, pl.ds(base, SCAN_CH), :] = (h * _gelu_tanh(g)).astype(o_ref.dtype)
            h0 = h[SCAN_CH - 1:SCAN_CH, :]
        return h0

    lax.fori_loop(0, n_chunks // sub, chunk, jnp.zeros((1, LANES), F32))


def _lru(proj3, lru_p, w_ax, col_gate, col_x):
    b, t, _ = proj3.shape
    width = LRU_BLOCKS * LANES
    return pl.pallas_call(
        _lru_kernel,
        grid=(b, LRU_BLOCKS),
        in_specs=[pl.BlockSpec((1, t, LANES), lambda bi, h: (bi, 0, col_gate + h)),
                  pl.BlockSpec((1, t, LANES), lambda bi, h: (bi, 0, col_x + h)),
                  pl.BlockSpec((8, LANES), lambda bi, h: (0, h)),
                  pl.BlockSpec((1, LANES, 2 * LANES), lambda bi, h: (h, 0, 0))],
        out_specs=pl.BlockSpec((1, t, LANES), lambda bi, h: (bi, 0, h)),
        out_shape=jax.ShapeDtypeStruct((b, t, width), BF16),
        scratch_shapes=[pltpu.VMEM((t + 8, LANES), F32)],
        compiler_params=_cparams(("parallel", "parallel")),
        name="rglru",
    )(proj3, proj3, lru_p, w_ax)


def _bias_tile_kernel(rb_ref, o_ref):
    h = pl.program_id(0)
    tb = o_ref.shape[1]
    i = lax.broadcasted_iota(jnp.int32, (tb, 2 * tb), 0)
    j2 = lax.broadcasted_iota(jnp.int32, (tb, 2 * tb), 1)
    max_exact = REL_BUCKETS // 2
    d = i - j2 + tb
    n = jnp.maximum(d, 0)
    nf = jnp.maximum(n, 1).astype(F32)
    large = max_exact + (jnp.log(nf / max_exact) / math.log(REL_MAX_DIST / max_exact)
                         * (REL_BUCKETS - max_exact)).astype(jnp.int32)
    large = jnp.minimum(large, REL_BUCKETS - 1)
    bucket = jnp.where(n < max_exact, n, large)
    val = jnp.zeros((tb, 2 * tb), F32)
    for bkt in range(REL_BUCKETS):
        val = jnp.where(bucket == bkt, rb_ref[bkt, h], val)
    val = val - rb_ref[REL_BUCKETS - 1, h]
    o_ref[0] = jnp.where(d >= 0, val, MASK_NEG)


def _bias_tiles(rel_bias, tb):
    return pl.pallas_call(
        _bias_tile_kernel,
        grid=(DA_HEADS,),
        in_specs=[pl.BlockSpec(memory_space=pltpu.SMEM)],
        out_specs=pl.BlockSpec((1, tb, 2 * tb), lambda h: (h, 0, 0)),
        out_shape=jax.ShapeDtypeStruct((DA_HEADS, tb, 2 * tb), F32),
        compiler_params=_cparams(("parallel",)),
        name="t5_bias_tiles",
    )(rel_bias)


def _attn_qblock(qi, q_ref, k_ref, v_ref, bt_ref, sw_ref, o_ref, m_scr, l_scr, acc_scr, s_scr, lam,
                 lam_init):
    tb = s_scr.shape[2]
    dn_t = (((1,), (1,)), ((), ()))
    lane = lax.broadcasted_iota(jnp.int32, (tb, LANES), 1)
    qstart = pl.multiple_of(qi * tb, tb)
    qs = q_ref[0, pl.ds(qstart, tb), :] * (DA_HEAD_DIM ** -0.5)
    zero = jnp.zeros_like(qs)
    qcat = jnp.concatenate([jnp.where(lane < DA_HEAD_DIM, qs, zero),
                            jnp.where(lane >= DA_HEAD_DIM, qs, zero)], axis=0)
    n_far = jnp.maximum(qi - 1, 0)
    n_quads = n_far // 4
    n_rest = n_far % 4
    tail_blk = 4 * n_quads

    def scores(blk, nblocks, biased):
        kb = k_ref[0, pl.ds(pl.multiple_of(blk * tb, tb), nblocks * tb), :]
        s_both = lax.dot_general(qcat, kb, dn_t, preferred_element_type=F32)
        out = []
        for c in range(2):
            s = s_both[c * tb:(c + 1) * tb]
            if biased:
                nb = min(nblocks, 2)
                plain = (nblocks - nb) * tb
                tail = s[:, plain:] + bt_ref[0, :, (2 - nb) * tb:]
                s = tail if plain == 0 else jnp.concatenate([s[:, :plain], tail], axis=1)
            s = s * LOG2E
            for i in range(nblocks):
                s_scr[c, blk + i] = s[:, i * tb:(i + 1) * tb]
            out.append(s)
        return out

    def lane_chunks(s):
        return [s[:, j * LANES:(j + 1) * LANES] for j in range(s.shape[1] // LANES)]

    def max_into(ms, ss):
        out = []
        for m, s in zip(ms, ss):
            for ch in lane_chunks(s):
                m = jnp.maximum(m, ch)
            out.append(m)
        return out

    m_init = jnp.full((tb, LANES), MASK_NEG, F32)

    @pl.when(qi == 0)
    def _():
        m_scr[0], m_scr[1] = max_into((m_init, m_init), scores(0, 1, True))

    for rest in range(4):
        @pl.when(jnp.logical_and(qi >= 1, n_rest == rest))
        def _():
            m_scr[0], m_scr[1] = max_into((m_init, m_init), scores(tail_blk, rest + 2, True))

    def quad_max(ki, ms):
        return tuple(max_into(ms, scores(4 * ki, 4, False)))

    ms = lax.fori_loop(0, n_quads, quad_max, (m_scr[0], m_scr[1]))
    mb = [jnp.broadcast_to(jnp.max(ms[c], axis=-1, keepdims=True), (tb, LANES)) for c in range(2)]

    def accumulate(blk, nblocks, st):
        vb = v_ref[0, pl.ds(pl.multiple_of(blk * tb, tb), nblocks * tb), :]
        out = []
        for c in range(2):
            ps = []
            for i in range(nblocks):
                ps += [jnp.exp2(ch - mb[c]) for ch in lane_chunks(s_scr[c, blk + i])]
            l = st[2 * c]
            for ch in ps:
                l = l + ch
            p = jnp.concatenate(ps, axis=1).astype(BF16)
            out += [l, st[2 * c + 1] + jnp.dot(p, vb, preferred_element_type=F32)]
        return tuple(out)

    def store_state(st):
        l_scr[0], acc_scr[0], l_scr[1], acc_scr[1] = st

    zl = jnp.zeros((tb, LANES), F32)

    @pl.when(qi == 0)
    def _():
        store_state(accumulate(0, 1, (zl, zl, zl, zl)))

    for rest in range(4):
        @pl.when(jnp.logical_and(qi >= 1, n_rest == rest))
        def _():
            store_state(accumulate(tail_blk, rest + 2, (zl, zl, zl, zl)))

    l0, a0, l1, a1 = lax.fori_loop(0, n_quads, lambda ki, st: accumulate(4 * ki, 4, st),
                                   (l_scr[0], acc_scr[0], l_scr[1], acc_scr[1]))

    o = (a0 / jnp.sum(l0, axis=-1, keepdims=True)
         - lam * (a1 / jnp.sum(l1, axis=-1, keepdims=True)))
    o = _rms(o, sw_ref[...], 1e-5) * (1.0 - lam_init)
    o_ref[0, pl.ds(qstart, tb), :] = o.astype(o_ref.dtype)


def _attn_kernel(q_ref, k_ref, v_ref, bt_ref, dl_ref, sw_ref, o_ref, m_scr, l_scr, acc_scr, s_scr, *,
                 lam_init):
    dl = dl_ref[...]
    lam = (jnp.exp(jnp.sum(dl[0:1, :] * dl[1:2, :], axis=-1, keepdims=True))
           - jnp.exp(jnp.sum(dl[2:3, :] * dl[3:4, :], axis=-1, keepdims=True)) + lam_init)

    def qblock(qi, carry):
        _attn_qblock(qi, q_ref, k_ref, v_ref, bt_ref, sw_ref, o_ref, m_scr, l_scr, acc_scr, s_scr, lam,
                     lam_init)
        return carry

    lax.fori_loop(0, q_ref.shape[1] // s_scr.shape[2], qblock, 0)


def _attn(proj3, btiles, da_lambda, da_subln, lam_init, tb, col_q, col_k, col_v):
    b, t, _ = proj3.shape
    width = DA_HEADS * LANES
    return pl.pallas_call(
        functools.partial(_attn_kernel, lam_init=lam_init),
        grid=(b, DA_HEADS),
        in_specs=[pl.BlockSpec((1, t, LANES), lambda bi, h: (bi, 0, col_q + h)),
                  pl.BlockSpec((1, t, LANES), lambda bi, h: (bi, 0, col_k + h)),
                  pl.BlockSpec((1, t, LANES), lambda bi, h: (bi, 0, col_v + h)),
                  pl.BlockSpec((1, tb, 2 * tb), lambda bi, h: (h, 0, 0)),
                  pl.BlockSpec((4, DA_HEAD_DIM), lambda bi, h: (0, 0)),
                  pl.BlockSpec((1, LANES), lambda bi, h: (0, 0))],
        out_specs=pl.BlockSpec((1, t, LANES), lambda bi, h: (bi, 0, h)),
        out_shape=jax.ShapeDtypeStruct((b, t, width), BF16),
        scratch_shapes=[pltpu.VMEM((2, tb, LANES), F32), pltpu.VMEM((2, tb, LANES), F32),
                        pltpu.VMEM((2, tb, LANES), F32), pltpu.VMEM((2, t // tb, tb, tb), F32)],
        compiler_params=_cparams(("parallel", "parallel")),
        name="diff_attn",
    )(proj3, proj3, proj3, btiles, da_lambda, da_subln)


def _s5_kernel(u_ref, bc_ref, pw_ref, d_ref, o_ref, uf_scr, g_scr, hp_scr, y_scr,
               wg_scr, wk_scr, wct_scr):
    t_len = u_ref.shape[1]
    nj = t_len // S5_L
    pw = pw_ref[0]

    @pl.when(pl.program_id(1) == 0)
    def _():
        bd_re, bd_im, ct_re, ct_im = bc_ref[0, 0], bc_ref[0, 1], bc_ref[0, 2], bc_ref[0, 3]
        for s in range(S5_L):
            rows = slice(s * LANES, (s + 1) * LANES)
            qr, qi = pw[s:s + 1, :S5_HALF], pw[s:s + 1, S5_HALF:]
            wg_scr[rows, :S5_HALF] = (qr * bd_re - qi * bd_im).astype(BF16)
            wg_scr[rows, S5_HALF:] = (qr * bd_im + qi * bd_re).astype(BF16)
            er, ei = pw[8 + s:9 + s, :S5_HALF], pw[8 + s:9 + s, S5_HALF:]
            wct_scr[rows, :S5_HALF] = (ct_re * er - ct_im * ei).astype(BF16)
            wct_scr[rows, S5_HALF:] = (-(ct_re * ei + ct_im * er)).astype(BF16)
        dn_t = (((1,), (1,)), ((), ()))
        bcat = jnp.concatenate([bd_re, bd_im], axis=1).astype(BF16)
        crows = [jnp.concatenate([ct_re, -ct_im], axis=1).astype(BF16)]
        crows += [wct_scr[s * LANES:(s + 1) * LANES, :] for s in range(S5_L - 1)]
        kts = [lax.dot_general(bcat, c, dn_t, preferred_element_type=F32).astype(BF16) for c in crows]
        for s in range(S5_L):
            for r in range(S5_L):
                wk_scr[s * LANES:(s + 1) * LANES, r * LANES:(r + 1) * LANES] = (
                    kts[r - s] if r >= s else jnp.zeros((LANES, LANES), BF16))

    uf_scr[...] = u_ref[0].astype(F32)
    ur = jnp.concatenate([uf_scr[pl.ds(s, nj, stride=S5_L), :].astype(BF16) for s in range(S5_L)],
                         axis=1)
    g_scr[...] = jnp.dot(ur, wg_scr[...], preferred_element_type=F32)

    pr, pi = pw[24:32, :S5_HALF], pw[24:32, S5_HALF:]
    row = lax.broadcasted_iota(jnp.int32, (8, S5_HALF), 0)

    def local_scan(base):
        x = g_scr[pl.ds(base, 8), :]
        xr, xi = x[:, :S5_HALF], x[:, S5_HALF:]
        for lvl, s in enumerate((1, 2, 4)):
            ar, ai = pw[16 + lvl:17 + lvl, :S5_HALF], pw[16 + lvl:17 + lvl, S5_HALF:]
            keep = row >= s
            sr = jnp.where(keep, pltpu.roll(xr, s, 0), 0.0)
            si = jnp.where(keep, pltpu.roll(xi, s, 0), 0.0)
            xr, xi = xr + (ar * sr - ai * si), xi + (ar * si + ai * sr)
        return xr, xi

    first = row == 0
    per_iter = 2

    def groups(gi, carry):
        cr, ci = carry
        bases = [pl.multiple_of((gi * per_iter + k) * 8, 8) for k in range(per_iter)]
        scans = [local_scan(base) for base in bases]
        for base, (xr, xi) in zip(bases, scans):
            xr, xi = xr + (pr * cr - pi * ci), xi + (pr * ci + pi * cr)
            hp_scr[pl.ds(base, 8), :] = jnp.concatenate(
                [jnp.where(first, cr, pltpu.roll(xr, 1, 0)), jnp.where(first, ci, pltpu.roll(xi, 1, 0))],
                axis=1)
            cr, ci = xr[7:8, :], xi[7:8, :]
        return cr, ci

    z = jnp.zeros((1, S5_HALF), F32)
    lax.fori_loop(0, nj // (8 * per_iter), groups, (z, z))

    y = (jnp.dot(ur, wk_scr[...], preferred_element_type=F32)
         + lax.dot_general(hp_scr[...].astype(BF16), wct_scr[...], (((1,), (1,)), ((), ())),
                           preferred_element_type=F32))
    for s in range(S5_L):
        y_scr[pl.ds(s, nj, stride=S5_L), :] = y[:, s * LANES:(s + 1) * LANES]
    o_ref[0] = _gelu_tanh(y_scr[...] + d_ref[...] * uf_scr[...]).astype(o_ref.dtype)


def _s5(proj3, bc, pw, dvec, col_u):
    b, t, _ = proj3.shape
    nblk = bc.shape[0]
    nj = t // S5_L
    wide = S5_L * LANES
    assert nj % 16 == 0
    return pl.pallas_call(
        _s5_kernel,
        grid=(nblk, b),
        in_specs=[pl.BlockSpec((1, t, LANES), lambda g, bi: (bi, 0, col_u + g)),
                  pl.BlockSpec((1, 4, LANES, S5_HALF), lambda g, bi: (g, 0, 0, 0)),
                  pl.BlockSpec((1, 32, 2 * S5_HALF), lambda g, bi: (g, 0, 0)),
                  pl.BlockSpec((1, LANES), lambda g, bi: (0, g))],
        out_specs=pl.BlockSpec((1, t, LANES), lambda g, bi: (bi, 0, g)),
        out_shape=jax.ShapeDtypeStruct((b, t, nblk * LANES), BF16),
        scratch_shapes=[pltpu.VMEM((t, LANES), F32), pltpu.VMEM((nj, 2 * S5_HALF), F32),
                        pltpu.VMEM((nj, 2 * S5_HALF), F32), pltpu.VMEM((t, LANES), F32),
                        pltpu.VMEM((wide, 2 * S5_HALF), BF16), pltpu.VMEM((wide, wide), BF16),
                        pltpu.VMEM((wide, 2 * S5_HALF), BF16)],
        compiler_params=_cparams(("parallel", "arbitrary")),
        name="s5",
    )(proj3, bc, pw, dvec)


def _merge_kernel(ya_ref, yb_ref, yc_ref, g0_ref, g1_ref, g2_ref, wg_ref, bg_ref, wb_ref, bgate_ref,
                  o_ref, yc_scr):
    @pl.when(pl.program_id(1) == 0)
    def _():
        yc = yc_ref[...]
        z = jnp.dot(yc, wg_ref[...], preferred_element_type=F32) + bg_ref[...]
        yc_scr[...] = (yc.astype(F32) * _sigmoid(z)).astype(BF16)

    bgate = bgate_ref[...]
    ys = (ya_ref[...], yb_ref[...], yc_scr[...])
    gs = (g0_ref, g1_ref, g2_ref)
    merged = None
    for br in range(N_BRANCH):
        gate = _sigmoid(gs[br][...].astype(F32) + bgate[br:br + 1, :])
        term = gate * jnp.dot(ys[br], wb_ref[br], preferred_element_type=F32)
        merged = term if merged is None else merged + term
    o_ref[...] = merged.astype(o_ref.dtype)


def _merge(ya, yb, yc, proj, w_glu_all, b_glu, w_branch_all, b_gate, col_g, layer, tm, tn):
    n, wdt = ya.shape
    d = w_branch_all.shape[3]
    gcol = [(col_g + br * d) // tn for br in range(N_BRANCH)]
    yspec = pl.BlockSpec((tm, wdt), lambda i, j: (i, 0))

    def gspec(br):
        return pl.BlockSpec((tm, tn), lambda i, j: (i, gcol[br] + j))

    return pl.pallas_call(
        _merge_kernel,
        grid=(n // tm, d // tn),
        in_specs=[yspec, yspec, yspec, gspec(0), gspec(1), gspec(2),
                  pl.BlockSpec((None, wdt, wdt), lambda i, j: (layer, 0, 0)),
                  pl.BlockSpec((1, wdt), lambda i, j: (0, 0)),
                  pl.BlockSpec((None, N_BRANCH, wdt, tn), lambda i, j: (layer, 0, 0, j)),
                  pl.BlockSpec((N_BRANCH, tn), lambda i, j: (0, j))],
        out_specs=pl.BlockSpec((tm, tn), lambda i, j: (i, j)),
        out_shape=jax.ShapeDtypeStruct((n, d), BF16),
        scratch_shapes=[pltpu.VMEM((tm, wdt), BF16)],
        compiler_params=_cparams(("parallel", "arbitrary")),
        name="merge",
    )(ya, yb, yc, proj, proj, proj, w_glu_all, b_glu, w_branch_all, b_gate)


def _outproj_kernel(m_ref, w_ref, xs_ref, nw_ref, o_ref):
    mix = jnp.dot(m_ref[...], w_ref[...], preferred_element_type=F32)
    o_ref[...] = xs_ref[...] + _rms(mix, nw_ref[...], 1e-6)


def _outproj(merged, w_out_all, layer, xs, nw, tm):
    n, d = xs.shape
    return pl.pallas_call(
        _outproj_kernel,
        grid=(n // tm,),
        in_specs=[pl.BlockSpec((tm, d), lambda i: (i, 0)),
                  pl.BlockSpec((None, d, d), lambda i: (layer, 0, 0)),
                  pl.BlockSpec((tm, d), lambda i: (i, 0)),
                  pl.BlockSpec((1, d), lambda i: (0, 0))],
        out_specs=pl.BlockSpec((tm, d), lambda i: (i, 0)),
        out_shape=jax.ShapeDtypeStruct((n, d), F32),
        compiler_params=_cparams(("parallel",)),
        name="outproj_residual",
    )(merged, w_out_all, xs, nw)


def _ffn_kernel(xs_ref, nw_in_ref, wg_ref, wu_ref, wo_ref, nw_out_ref, o_ref, h_scr, acc_scr):
    f = pl.program_id(1)

    @pl.when(f == 0)
    def _():
        h_scr[...] = _rms(xs_ref[...], nw_in_ref[...], 1e-6).astype(BF16)
        acc_scr[...] = jnp.zeros(acc_scr.shape, F32)

    h = h_scr[...]
    gate = jnp.dot(h, wg_ref[...], preferred_element_type=F32)
    up = jnp.dot(h, wu_ref[...], preferred_element_type=F32)
    act = (gate * _sigmoid(gate) * up).astype(BF16)
    acc_scr[...] += jnp.dot(act, wo_ref[...], preferred_element_type=F32)

    @pl.when(f == pl.num_programs(1) - 1)
    def _():
        o_ref[...] = xs_ref[...] + _rms(acc_scr[...], nw_out_ref[...], 1e-6)


def _ffn(xs, nw_in, w_ffn_in_all, w_ffn_out_all, nw_out, layer, tm, tf, real_rows=None):
    n, d = xs.shape
    d_ff = w_ffn_out_all.shape[1]
    nf = d_ff // tf
    if real_rows is None:
        n_out = n
        xs_spec = pl.BlockSpec((tm, d), lambda i, f: (i, 0))
    else:
        t_pad, first, count = real_rows
        per_seq = count // tm
        n_out = (n // t_pad) * count
        xs_spec = pl.BlockSpec((pl.Element(tm), pl.Element(d)),
                               lambda i, f: (pl.multiple_of(
                                   (i // per_seq) * t_pad + first + (i % per_seq) * tm, 8), 0))
        assert t_pad % 8 == 0 and first % 8 == 0 and tm % 8 == 0
    return pl.pallas_call(
        _ffn_kernel,
        grid=(n_out // tm, nf),
        in_specs=[xs_spec,
                  pl.BlockSpec((1, d), lambda i, f: (0, 0)),
                  pl.BlockSpec((None, d, tf), lambda i, f: (layer, 0, f)),
                  pl.BlockSpec((None, d, tf), lambda i, f: (layer, 0, nf + f)),
                  pl.BlockSpec((None, tf, d), lambda i, f: (layer, f, 0)),
                  pl.BlockSpec((1, d), lambda i, f: (0, 0))],
        out_specs=pl.BlockSpec((tm, d), lambda i, f: (i, 0)),
        out_shape=jax.ShapeDtypeStruct((n_out, d), F32),
        scratch_shapes=[pltpu.VMEM((tm, d), BF16), pltpu.VMEM((tm, d), F32)],
        compiler_params=_cparams(("parallel", "arbitrary")),
        name="swiglu_ffn",
    )(xs, nw_in, w_ffn_in_all, w_ffn_in_all, w_ffn_out_all, nw_out)


def _s5_tables(lam_re, lam_im, b_re, b_im, c_re, c_im, log_step):
    groups = lam_re.shape[0]
    nblk = groups // S5_GB
    lr, li = lam_re.astype(F32), lam_im.astype(F32)
    step = jnp.exp(log_step.astype(F32))[:, None]
    mag = jnp.exp(lr * step)
    ab_re, ab_im = mag * jnp.cos(li * step), mag * jnp.sin(li * step)
    den = lr * lr + li * li
    coef_re = ((ab_re - 1.0) * lr + ab_im * li) / den
    coef_im = (ab_im * lr - (ab_re - 1.0) * li) / den
    br, bi = b_re.astype(F32), b_im.astype(F32)
    bb_re = coef_re[..., None] * br - coef_im[..., None] * bi
    bb_im = coef_re[..., None] * bi + coef_im[..., None] * br
    cr, ci = c_re.astype(F32), c_im.astype(F32)
    eye = jnp.eye(S5_GB, dtype=F32)
    taus = jnp.arange(S5_L)
    lrs = (lr * step).reshape(nblk, 1, S5_HALF)
    lis = (li * step).reshape(nblk, 1, S5_HALF)

    def lam_pow(n):
        nn = n.astype(F32)[None, :, None]
        m = jnp.exp(nn * lrs)
        return m * jnp.cos(nn * lis), m * jnp.sin(nn * lis)

    def blockdiag(t):
        t = t.reshape(nblk, S5_GB, S5_GROUP, S5_STATE)
        return jnp.einsum("ngcp,gh->ngchp", t, eye).reshape(nblk, LANES, S5_HALF)

    bc = jnp.stack([blockdiag(bb_re.transpose(0, 2, 1)), blockdiag(bb_im.transpose(0, 2, 1)),
                    blockdiag(cr), blockdiag(ci)], axis=1)

    n_list = jnp.concatenate([S5_L - 1 - taus, 1 + taus, S5_L * jnp.array([1, 2, 4, 0, 0, 0, 0, 0]),
                              S5_L * (1 + jnp.arange(8))])
    wr, wi = lam_pow(n_list)
    pw = jnp.concatenate([wr, wi], axis=2)
    return bc, pw


def kernel(x, meta, rel_bias, norm_w, w_in, conv_w, conv_b, lru_w_a, lru_b_a, lru_w_x, lru_b_x, lru_lambda, da_lambda, da_subln, s5_lam_re, s5_lam_im, s5_b_re, s5_b_im, s5_c_re, s5_c_im, s5_d, s5_log_step, s5_w_glu, s5_b_glu, b_gate, w_branch, w_out, w_ffn_in, w_ffn_out):
    bsz, seq, d_model = x.shape
    depth = w_in.shape[0]
    lru_w = conv_w.shape[2]
    s5_w = s5_d.shape[1]
    qk_w = DA_HEADS * 2 * DA_HEAD_DIM
    assert lru_w == LRU_BLOCKS * LANES and s5_w % LANES == 0 and d_model % LANES == 0
    col_gate, col_x = 0, lru_w // LANES
    col_q = 2 * lru_w // LANES
    col_k = col_q + qk_w // LANES
    col_v = col_k + qk_w // LANES
    col_u = col_v + DA_HEADS
    col_g = (col_u + s5_w // LANES) * LANES

    t_real = N_META + seq
    tb = 384 if t_real >= 1024 else 128
    t_pad = -(-t_real // tb) * tb
    assert t_pad % SCAN_CH == 0
    n_tok = bsz * t_pad
    tm_in = _pick_tile(n_tok, (1408, 1056, 768, 512, 384, 256, 128))
    tm_merge = _pick_tile(n_tok, (1056, 768, 512, 384, 256, 128))
    tm_ffn = _pick_tile(n_tok, (768, 512, 384, 256, 128))
    tm_out = _pick_tile(n_tok, (512, 384, 256, 128))
    tm_last = _pick_tile(seq, (512, 256, 128))

    xs = jnp.concatenate([jnp.broadcast_to(meta.astype(F32)[None], (bsz, N_META, d_model)), x,
                          jnp.zeros((bsz, t_pad - t_real, d_model), F32)], axis=1)
    xs = xs.reshape(n_tok, d_model)
    btiles = _bias_tiles(rel_bias.astype(F32), tb)

    w_glu_b, w_branch_b = s5_w_glu.astype(BF16), w_branch.astype(BF16)
    w_out_b, w_ffn_in_b, w_ffn_out_b = w_out.astype(BF16), w_ffn_in.astype(BF16), w_ffn_out.astype(BF16)

    for l in range(depth):
        lam_init = 0.8 - 0.6 * math.exp(-0.3 * l)
        proj = _inproj(xs, norm_w[l, 0][None], w_in, l, tm_in, 512)
        proj3 = proj.reshape(bsz, t_pad, proj.shape[1])

        lru_p = jnp.concatenate([conv_w[l], conv_b[l][None], lru_b_a[l][None], lru_b_x[l][None],
                                 jax.nn.log_sigmoid(lru_lambda[l].astype(F32))[None]], axis=0)
        w_ax = jnp.concatenate([lru_w_a[l], lru_w_x[l]], axis=2).astype(BF16)
        y_a = _lru(proj3, lru_p, w_ax, col_gate, col_x)

        y_b = _attn(proj3, btiles, da_lambda[l], da_subln[l][None], lam_init, tb, col_q, col_k, col_v)

        bc, pw = _s5_tables(s5_lam_re[l], s5_lam_im[l], s5_b_re[l], s5_b_im[l],
                            s5_c_re[l], s5_c_im[l], s5_log_step[l])
        y_c = _s5(proj3, bc, pw, s5_d[l][None], col_u)

        merged = _merge(y_a.reshape(n_tok, lru_w), y_b.reshape(n_tok, -1), y_c.reshape(n_tok, s5_w),
                        proj, w_glu_b, s5_b_glu[l][None], w_branch_b, b_gate[l], col_g, l, tm_merge, 512)
        xs = _outproj(merged, w_out_b, l, xs, norm_w[l, 1][None], tm_out)
        if l + 1 < depth:
            xs = _ffn(xs, norm_w[l, 2][None], w_ffn_in_b, w_ffn_out_b, norm_w[l, 3][None], l, tm_ffn, 512)
        else:
            out = _ffn(xs, norm_w[l, 2][None], w_ffn_in_b, w_ffn_out_b, norm_w[l, 3][None], l, tm_last,
                       512, real_rows=(t_pad, N_META, seq))

    return out.reshape(bsz, seq, d_model)
```

```python
import functools
import math

import jax
import jax.numpy as jnp
from jax import lax
from jax.experimental import pallas as pl
from jax.experimental.pallas import tpu as pltpu

F32 = jnp.float32
BF16 = jnp.bfloat16

N_META = 16
CONV_W = 4
LRU_C = 8.0
LRU_BLOCKS = 8
DA_HEADS = 8
DA_HEAD_DIM = 64
S5_GROUP = 16
S5_STATE = 64
REL_BUCKETS = 32
REL_MAX_DIST = 128
N_BRANCH = 3

LANES = 128
S5_GB = LANES // S5_GROUP
S5_HALF = S5_GB * S5_STATE
S5_L = 8
SCAN_CH = 128
MASK_NEG = -1e30
LOG2E = math.log2(math.e)
VMEM_LIMIT = 56 * 1024 * 1024


def _cparams(sem):
    return pltpu.CompilerParams(dimension_semantics=sem, vmem_limit_bytes=VMEM_LIMIT)


def _pick_tile(n, candidates):
    for c in candidates:
        if n % c == 0:
            return c
    raise ValueError(f"no tile in {candidates} divides {n}")


def _gelu_tanh(x):
    return 0.5 * x * (1.0 + jnp.tanh(math.sqrt(2.0 / math.pi) * (x + 0.044715 * (x * x * x))))


def _sigmoid(x):
    return 1.0 / (1.0 + jnp.exp(-x))


def _rms(x, w, eps):
    return (x * lax.rsqrt(jnp.mean(x * x, axis=-1, keepdims=True) + eps)) * w


def _inproj_kernel(x_ref, nw_ref, w_ref, o_ref, h_scr):
    @pl.when(pl.program_id(1) == 0)
    def _():
        h_scr[...] = _rms(x_ref[...], nw_ref[...], 1e-6).astype(BF16)

    o_ref[...] = jnp.dot(h_scr[...], w_ref[...], preferred_element_type=F32).astype(o_ref.dtype)


def _inproj(xs, nw, w_all, layer, tm, tn):
    n, d = xs.shape
    n_out = w_all.shape[2]
    return pl.pallas_call(
        _inproj_kernel,
        grid=(n // tm, n_out // tn),
        in_specs=[pl.BlockSpec((tm, d), lambda i, j: (i, 0)),
                  pl.BlockSpec((1, d), lambda i, j: (0, 0)),
                  pl.BlockSpec((None, d, tn), lambda i, j: (layer, 0, j))],
        out_specs=pl.BlockSpec((tm, tn), lambda i, j: (i, j)),
        out_shape=jax.ShapeDtypeStruct((n, n_out), BF16),
        scratch_shapes=[pltpu.VMEM((tm, d), BF16)],
        compiler_params=_cparams(("parallel", "arbitrary")),
        name="inproj",
    )(xs, nw, w_all)


def _lru_kernel(g_ref, x_ref, p_ref, w_ref, o_ref, xs_scr):
    t_len = x_ref.shape[1]
    halo = 8
    xs_scr[0:halo, :] = jnp.zeros((halo, LANES), F32)
    xs_scr[halo:, :] = x_ref[0].astype(F32)
    p = p_ref[...]
    cw = [p[j:j + 1, :] for j in range(CONV_W)]
    cb, ba, bx, logsig = p[4:5, :], p[5:6, :], p[6:7, :], p[7:8, :]
    w = w_ref[0]
    row = lax.broadcasted_iota(jnp.int32, (SCAN_CH, LANES), 0)

    def local_scan(base):
        xc = cb
        for j in range(CONV_W):
            xc = xc + cw[j] * xs_scr[pl.ds(base + halo - (CONV_W - 1) + j, SCAN_CH), :]
        ri = jnp.dot(xc.astype(BF16), w, preferred_element_type=F32)
        r = _sigmoid(ri[:, :LANES] + ba)
        i = _sigmoid(ri[:, LANES:] + bx)
        a = jnp.exp(LRU_C * r * logsig)
        b = jnp.sqrt(1.0 - a * a) * (i * xc)
        s = 1
        while s < SCAN_CH:
            keep = row >= s
            a_sh = jnp.where(keep, pltpu.roll(a, s, 0), 1.0)
            b_sh = jnp.where(keep, pltpu.roll(b, s, 0), 0.0)
            b = a * b_sh + b
            a = a * a_sh
            s *= 2
        return a, b

    n_chunks = t_len // SCAN_CH
    sub = 3 if n_chunks % 3 == 0 else 1

    def chunk(c, h0):
        bases = [pl.multiple_of((c * sub + j) * SCAN_CH, SCAN_CH) for j in range(sub)]
        scans = [local_scan(base) for base in bases]
        for base, (a, b) in zip(bases, scans):
            h = a * h0 + b
            g = g_ref[0, pl.ds(base, SCAN_CH), :].astype(F32)
            o_ref[0, pl.ds(base, SCAN_CH), :] = (h * _gelu_tanh(g)).astype(o_ref.dtype)
            h0 = h[SCAN_CH - 1:SCAN_CH, :]
        return h0

    lax.fori_loop(0, n_chunks // sub, chunk, jnp.zeros((1, LANES), F32))


def _lru(proj3, lru_p, w_ax, col_gate, col_x):
    b, t, _ = proj3.shape
    width = LRU_BLOCKS * LANES
    return pl.pallas_call(
        _lru_kernel,
        grid=(b, LRU_BLOCKS),
        in_specs=[pl.BlockSpec((1, t, LANES), lambda bi, h: (bi, 0, col_gate + h)),
                  pl.BlockSpec((1, t, LANES), lambda bi, h: (bi, 0, col_x + h)),
                  pl.BlockSpec((8, LANES), lambda bi, h: (0, h)),
                  pl.BlockSpec((1, LANES, 2 * LANES), lambda bi, h: (h, 0, 0))],
        out_specs=pl.BlockSpec((1, t, LANES), lambda bi, h: (bi, 0, h)),
        out_shape=jax.ShapeDtypeStruct((b, t, width), BF16),
        scratch_shapes=[pltpu.VMEM((t + 8, LANES), F32)],
        compiler_params=_cparams(("parallel", "parallel")),
        name="rglru",
    )(proj3, proj3, lru_p, w_ax)


def _bias_tile_kernel(rb_ref, o_ref):
    h = pl.program_id(0)
    tb = o_ref.shape[1]
    i = lax.broadcasted_iota(jnp.int32, (tb, 2 * tb), 0)
    j2 = lax.broadcasted_iota(jnp.int32, (tb, 2 * tb), 1)
    max_exact = REL_BUCKETS // 2
    d = i - j2 + tb
    n = jnp.maximum(d, 0)
    nf = jnp.maximum(n, 1).astype(F32)
    large = max_exact + (jnp.log(nf / max_exact) / math.log(REL_MAX_DIST / max_exact)
                         * (REL_BUCKETS - max_exact)).astype(jnp.int32)
    large = jnp.minimum(large, REL_BUCKETS - 1)
    bucket = jnp.where(n < max_exact, n, large)
    val = jnp.zeros((tb, 2 * tb), F32)
    for bkt in range(REL_BUCKETS):
        val = jnp.where(bucket == bkt, rb_ref[bkt, h], val)
    val = val - rb_ref[REL_BUCKETS - 1, h]
    o_ref[0] = jnp.where(d >= 0, val, MASK_NEG)


def _bias_tiles(rel_bias, tb):
    return pl.pallas_call(
        _bias_tile_kernel,
        grid=(DA_HEADS,),
        in_specs=[pl.BlockSpec(memory_space=pltpu.SMEM)],
        out_specs=pl.BlockSpec((1, tb, 2 * tb), lambda h: (h, 0, 0)),
        out_shape=jax.ShapeDtypeStruct((DA_HEADS, tb, 2 * tb), F32),
        compiler_params=_cparams(("parallel",)),
        name="t5_bias_tiles",
    )(rel_bias)


def _attn_qblock(qi, q_ref, k_ref, v1_scr, bt_ref, sw_ref, o_ref, m_scr, acc_scr, s_scr, lam, lam_init):
    tb = s_scr.shape[2]
    dn_t = (((1,), (1,)), ((), ()))
    lane = lax.broadcasted_iota(jnp.int32, (tb, LANES), 1)
    qstart = pl.multiple_of(qi * tb, tb)
    qs = q_ref[0, pl.ds(qstart, tb), :] * (DA_HEAD_DIM ** -0.5)
    zero = jnp.zeros_like(qs)
    qcat = jnp.concatenate([jnp.where(lane < DA_HEAD_DIM, qs, zero),
                            jnp.where(lane >= DA_HEAD_DIM, qs, zero)], axis=0)
    n_far = jnp.maximum(qi - 1, 0)
    n_quads = n_far // 4
    n_rest = n_far % 4
    tail_blk = 4 * n_quads

    def scores(blk, nblocks, biased):
        kb = k_ref[0, pl.ds(pl.multiple_of(blk * tb, tb), nblocks * tb), :]
        s_both = lax.dot_general(qcat, kb, dn_t, preferred_element_type=F32)
        out = []
        for c in range(2):
            s = s_both[c * tb:(c + 1) * tb]
            if biased:
                nb = min(nblocks, 2)
                plain = (nblocks - nb) * tb
                tail = s[:, plain:] + bt_ref[0, :, (2 - nb) * tb:]
                s = tail if plain == 0 else jnp.concatenate([s[:, :plain], tail], axis=1)
            s = s * LOG2E
            for i in range(nblocks):
                s_scr[c, blk + i] = s[:, i * tb:(i + 1) * tb]
            out.append(s)
        return out

    def lane_chunks(s):
        return [s[:, j * LANES:(j + 1) * LANES] for j in range(s.shape[1] // LANES)]

    def max_into(ms, ss):
        out = []
        for m, s in zip(ms, ss):
            for ch in lane_chunks(s):
                m = jnp.maximum(m, ch)
            out.append(m)
        return out

    m_init = jnp.full((tb, LANES), MASK_NEG, F32)

    @pl.when(qi == 0)
    def _():
        m_scr[0], m_scr[1] = max_into((m_init, m_init), scores(0, 1, True))

    for rest in range(4):
        @pl.when(jnp.logical_and(qi >= 1, n_rest == rest))
        def _():
            m_scr[0], m_scr[1] = max_into((m_init, m_init), scores(tail_blk, rest + 2, True))

    def quad_max(ki, ms):
        return tuple(max_into(ms, scores(4 * ki, 4, False)))

    ms = lax.fori_loop(0, n_quads, quad_max, (m_scr[0], m_scr[1]))
    mb = [jnp.broadcast_to(jnp.max(ms[c], axis=-1, keepdims=True), (tb, LANES)) for c in range(2)]

    def accumulate(blk, nblocks, st):
        vb = v1_scr[pl.ds(pl.multiple_of(blk * tb, tb), nblocks * tb), :]
        out = []
        for c in range(2):
            ps = []
            for i in range(nblocks):
                ps += [jnp.exp2(ch - mb[c]) for ch in lane_chunks(s_scr[c, blk + i])]
            p = jnp.concatenate(ps, axis=1).astype(BF16)
            out.append(st[c] + jnp.dot(p, vb, preferred_element_type=F32))
        return tuple(out)

    def store_state(st):
        acc_scr[0], acc_scr[1] = st

    za = jnp.zeros((tb, 2 * LANES), F32)

    @pl.when(qi == 0)
    def _():
        store_state(accumulate(0, 1, (za, za)))

    for rest in range(4):
        @pl.when(jnp.logical_and(qi >= 1, n_rest == rest))
        def _():
            store_state(accumulate(tail_blk, rest + 2, (za, za)))

    a0, a1 = lax.fori_loop(0, n_quads, lambda ki, st: accumulate(4 * ki, 4, st),
                           (acc_scr[0], acc_scr[1]))

    o = a0[:, :LANES] / a0[:, LANES:] - lam * (a1[:, :LANES] / a1[:, LANES:])
    o = _rms(o, sw_ref[...], 1e-5) * (1.0 - lam_init)
    o_ref[0, pl.ds(qstart, tb), :] = o.astype(o_ref.dtype)


def _attn_kernel(q_ref, k_ref, v_ref, bt_ref, dl_ref, sw_ref, o_ref, m_scr, acc_scr, s_scr, v1_scr, *,
                 lam_init):
    dl = dl_ref[...]
    lam = (jnp.exp(jnp.sum(dl[0:1, :] * dl[1:2, :], axis=-1, keepdims=True))
           - jnp.exp(jnp.sum(dl[2:3, :] * dl[3:4, :], axis=-1, keepdims=True)) + lam_init)
    v1_scr[:, :LANES] = v_ref[0]
    v1_scr[:, LANES:] = jnp.ones((v_ref.shape[1], LANES), BF16)

    def qblock(qi, carry):
        _attn_qblock(qi, q_ref, k_ref, v1_scr, bt_ref, sw_ref, o_ref, m_scr, acc_scr, s_scr, lam, lam_init)
        return carry

    lax.fori_loop(0, q_ref.shape[1] // s_scr.shape[2], qblock, 0)


def _attn(proj3, btiles, da_lambda, da_subln, lam_init, tb, col_q, col_k, col_v):
    b, t, _ = proj3.shape
    width = DA_HEADS * LANES
    return pl.pallas_call(
        functools.partial(_attn_kernel, lam_init=lam_init),
        grid=(b, DA_HEADS),
        in_specs=[pl.BlockSpec((1, t, LANES), lambda bi, h: (bi, 0, col_q + h)),
                  pl.BlockSpec((1, t, LANES), lambda bi, h: (bi, 0, col_k + h)),
                  pl.BlockSpec((1, t, LANES), lambda bi, h: (bi, 0, col_v + h)),
                  pl.BlockSpec((1, tb, 2 * tb), lambda bi, h: (h, 0, 0)),
                  pl.BlockSpec((4, DA_HEAD_DIM), lambda bi, h: (0, 0)),
                  pl.BlockSpec((1, LANES), lambda bi, h: (0, 0))],
        out_specs=pl.BlockSpec((1, t, LANES), lambda bi, h: (bi, 0, h)),
        out_shape=jax.ShapeDtypeStruct((b, t, width), BF16),
        scratch_shapes=[pltpu.VMEM((2, tb, LANES), F32), pltpu.VMEM((2, tb, 2 * LANES), F32),
                        pltpu.VMEM((2, t // tb, tb, tb), F32), pltpu.VMEM((t, 2 * LANES), BF16)],
        compiler_params=_cparams(("parallel", "parallel")),
        name="diff_attn",
    )(proj3, proj3, proj3, btiles, da_lambda, da_subln)


def _s5_kernel(u_ref, bc_ref, pw_ref, d_ref, o_ref, uf_scr, g_scr, hp_scr, y_scr,
               wg_scr, wk_scr, wct_scr):
    t_len = u_ref.shape[1]
    nj = t_len // S5_L
    pw = pw_ref[0]

    @pl.when(pl.program_id(1) == 0)
    def _():
        bd_re, bd_im, ct_re, ct_im = bc_ref[0, 0], bc_ref[0, 1], bc_ref[0, 2], bc_ref[0, 3]
        for s in range(S5_L):
            rows = slice(s * LANES, (s + 1) * LANES)
            qr, qi = pw[s:s + 1, :S5_HALF], pw[s:s + 1, S5_HALF:]
            wg_scr[rows, :S5_HALF] = (qr * bd_re - qi * bd_im).astype(BF16)
            wg_scr[rows, S5_HALF:] = (qr * bd_im + qi * bd_re).astype(BF16)
            er, ei = pw[8 + s:9 + s, :S5_HALF], pw[8 + s:9 + s, S5_HALF:]
            wct_scr[rows, :S5_HALF] = (ct_re * er - ct_im * ei).astype(BF16)
            wct_scr[rows, S5_HALF:] = (-(ct_re * ei + ct_im * er)).astype(BF16)
        dn_t = (((1,), (1,)), ((), ()))
        bcat = jnp.concatenate([bd_re, bd_im], axis=1).astype(BF16)
        crows = [jnp.concatenate([ct_re, -ct_im], axis=1).astype(BF16)]
        crows += [wct_scr[s * LANES:(s + 1) * LANES, :] for s in range(S5_L - 1)]
        kts = [lax.dot_general(bcat, c, dn_t, preferred_element_type=F32).astype(BF16) for c in crows]
        for s in range(S5_L):
            for r in range(S5_L):
                wk_scr[s * LANES:(s + 1) * LANES, r * LANES:(r + 1) * LANES] = (
                    kts[r - s] if r >= s else jnp.zeros((LANES, LANES), BF16))

    uf_scr[...] = u_ref[0].astype(F32)
    ur = jnp.concatenate([uf_scr[pl.ds(s, nj, stride=S5_L), :].astype(BF16) for s in range(S5_L)],
                         axis=1)
    g_scr[...] = jnp.dot(ur, wg_scr[...], preferred_element_type=F32)

    pr, pi = pw[24:32, :S5_HALF], pw[24:32, S5_HALF:]
    row = lax.broadcasted_iota(jnp.int32, (8, S5_HALF), 0)

    def local_scan(base):
        x = g_scr[pl.ds(base, 8), :]
        xr, xi = x[:, :S5_HALF], x[:, S5_HALF:]
        for lvl, s in enumerate((1, 2, 4)):
            ar, ai = pw[16 + lvl:17 + lvl, :S5_HALF], pw[16 + lvl:17 + lvl, S5_HALF:]
            keep = row >= s
            sr = jnp.where(keep, pltpu.roll(xr, s, 0), 0.0)
            si = jnp.where(keep, pltpu.roll(xi, s, 0), 0.0)
            xr, xi = xr + (ar * sr - ai * si), xi + (ar * si + ai * sr)
        return xr, xi

    first = row == 0
    per_iter = 2

    def groups(gi, carry):
        cr, ci = carry
        bases = [pl.multiple_of((gi * per_iter + k) * 8, 8) for k in range(per_iter)]
        scans = [local_scan(base) for base in bases]
        for base, (xr, xi) in zip(bases, scans):
            xr, xi = xr + (pr * cr - pi * ci), xi + (pr * ci + pi * cr)
            hp_scr[pl.ds(base, 8), :] = jnp.concatenate(
                [jnp.where(first, cr, pltpu.roll(xr, 1, 0)), jnp.where(first, ci, pltpu.roll(xi, 1, 0))],
                axis=1)
            cr, ci = xr[7:8, :], xi[7:8, :]
        return cr, ci

    z = jnp.zeros((1, S5_HALF), F32)
    lax.fori_loop(0, nj // (8 * per_iter), groups, (z, z))

    y = (jnp.dot(ur, wk_scr[...], preferred_element_type=F32)
         + lax.dot_general(hp_scr[...].astype(BF16), wct_scr[...], (((1,), (1,)), ((), ())),
                           preferred_element_type=F32))
    for s in range(S5_L):
        y_scr[pl.ds(s, nj, stride=S5_L), :] = y[:, s * LANES:(s + 1) * LANES]
    o_ref[0] = _gelu_tanh(y_scr[...] + d_ref[...] * uf_scr[...]).astype(o_ref.dtype)


def _s5(proj3, bc, pw, dvec, col_u):
    b, t, _ = proj3.shape
    nblk = bc.shape[0]
    nj = t // S5_L
    wide = S5_L * LANES
    assert nj % 16 == 0
    return pl.pallas_call(
        _s5_kernel,
        grid=(nblk, b),
        in_specs=[pl.BlockSpec((1, t, LANES), lambda g, bi: (bi, 0, col_u + g)),
                  pl.BlockSpec((1, 4, LANES, S5_HALF), lambda g, bi: (g, 0, 0, 0)),
                  pl.BlockSpec((1, 32, 2 * S5_HALF), lambda g, bi: (g, 0, 0)),
                  pl.BlockSpec((1, LANES), lambda g, bi: (0, g))],
        out_specs=pl.BlockSpec((1, t, LANES), lambda g, bi: (bi, 0, g)),
        out_shape=jax.ShapeDtypeStruct((b, t, nblk * LANES), BF16),
        scratch_shapes=[pltpu.VMEM((t, LANES), F32), pltpu.VMEM((nj, 2 * S5_HALF), F32),
                        pltpu.VMEM((nj, 2 * S5_HALF), F32), pltpu.VMEM((t, LANES), F32),
                        pltpu.VMEM((wide, 2 * S5_HALF), BF16), pltpu.VMEM((wide, wide), BF16),
                        pltpu.VMEM((wide, 2 * S5_HALF), BF16)],
        compiler_params=_cparams(("parallel", "arbitrary")),
        name="s5",
    )(proj3, bc, pw, dvec)


def _merge_kernel(ya_ref, yb_ref, yc_ref, g0_ref, g1_ref, g2_ref, wg_ref, bg_ref, wb_ref, bgate_ref,
                  o_ref, yc_scr):
    @pl.when(pl.program_id(1) == 0)
    def _():
        yc = yc_ref[...]
        z = jnp.dot(yc, wg_ref[...], preferred_element_type=F32) + bg_ref[...]
        yc_scr[...] = (yc.astype(F32) * _sigmoid(z)).astype(BF16)

    bgate = bgate_ref[...]
    ys = (ya_ref[...], yb_ref[...], yc_scr[...])
    gs = (g0_ref, g1_ref, g2_ref)
    merged = None
    for br in range(N_BRANCH):
        gate = _sigmoid(gs[br][...].astype(F32) + bgate[br:br + 1, :])
        term = gate * jnp.dot(ys[br], wb_ref[br], preferred_element_type=F32)
        merged = term if merged is None else merged + term
    o_ref[...] = merged.astype(o_ref.dtype)


def _merge(ya, yb, yc, proj, w_glu_all, b_glu, w_branch_all, b_gate, col_g, layer, tm, tn):
    n, wdt = ya.shape
    d = w_branch_all.shape[3]
    gcol = [(col_g + br * d) // tn for br in range(N_BRANCH)]
    yspec = pl.BlockSpec((tm, wdt), lambda i, j: (i, 0))

    def gspec(br):
        return pl.BlockSpec((tm, tn), lambda i, j: (i, gcol[br] + j))

    return pl.pallas_call(
        _merge_kernel,
        grid=(n // tm, d // tn),
        in_specs=[yspec, yspec, yspec, gspec(0), gspec(1), gspec(2),
                  pl.BlockSpec((None, wdt, wdt), lambda i, j: (layer, 0, 0)),
                  pl.BlockSpec((1, wdt), lambda i, j: (0, 0)),
                  pl.BlockSpec((None, N_BRANCH, wdt, tn), lambda i, j: (layer, 0, 0, j)),
                  pl.BlockSpec((N_BRANCH, tn), lambda i, j: (0, j))],
        out_specs=pl.BlockSpec((tm, tn), lambda i, j: (i, j)),
        out_shape=jax.ShapeDtypeStruct((n, d), BF16),
        scratch_shapes=[pltpu.VMEM((tm, wdt), BF16)],
        compiler_params=_cparams(("parallel", "arbitrary")),
        name="merge",
    )(ya, yb, yc, proj, proj, proj, w_glu_all, b_glu, w_branch_all, b_gate)


def _outproj_kernel(m_ref, w_ref, xs_ref, nw_ref, o_ref):
    mix = jnp.dot(m_ref[...], w_ref[...], preferred_element_type=F32)
    o_ref[...] = xs_ref[...] + _rms(mix, nw_ref[...], 1e-6)


def _outproj(merged, w_out_all, layer, xs, nw, tm):
    n, d = xs.shape
    return pl.pallas_call(
        _outproj_kernel,
        grid=(n // tm,),
        in_specs=[pl.BlockSpec((tm, d), lambda i: (i, 0)),
                  pl.BlockSpec((None, d, d), lambda i: (layer, 0, 0)),
                  pl.BlockSpec((tm, d), lambda i: (i, 0)),
                  pl.BlockSpec((1, d), lambda i: (0, 0))],
        out_specs=pl.BlockSpec((tm, d), lambda i: (i, 0)),
        out_shape=jax.ShapeDtypeStruct((n, d), F32),
        compiler_params=_cparams(("parallel",)),
        name="outproj_residual",
    )(merged, w_out_all, xs, nw)


def _ffn_kernel(xs_ref, nw_in_ref, wg_ref, wu_ref, wo_ref, nw_out_ref, o_ref, h_scr, acc_scr):
    f = pl.program_id(1)

    @pl.when(f == 0)
    def _():
        h_scr[...] = _rms(xs_ref[...], nw_in_ref[...], 1e-6).astype(BF16)
        acc_scr[...] = jnp.zeros(acc_scr.shape, F32)

    h = h_scr[...]
    gate = jnp.dot(h, wg_ref[...], preferred_element_type=F32)
    up = jnp.dot(h, wu_ref[...], preferred_element_type=F32)
    act = (gate * _sigmoid(gate) * up).astype(BF16)
    acc_scr[...] += jnp.dot(act, wo_ref[...], preferred_element_type=F32)

    @pl.when(f == pl.num_programs(1) - 1)
    def _():
        o_ref[...] = xs_ref[...] + _rms(acc_scr[...], nw_out_ref[...], 1e-6)


def _ffn(xs, nw_in, w_ffn_in_all, w_ffn_out_all, nw_out, layer, tm, tf, real_rows=None):
    n, d = xs.shape
    d_ff = w_ffn_out_all.shape[1]
    nf = d_ff // tf
    if real_rows is None:
        n_out = n
        xs_spec = pl.BlockSpec((tm, d), lambda i, f: (i, 0))
    else:
        t_pad, first, count = real_rows
        per_seq = count // tm
        n_out = (n // t_pad) * count
        xs_spec = pl.BlockSpec((pl.Element(tm), pl.Element(d)),
                               lambda i, f: (pl.multiple_of(
                                   (i // per_seq) * t_pad + first + (i % per_seq) * tm, 8), 0))
        assert t_pad % 8 == 0 and first % 8 == 0 and tm % 8 == 0
    return pl.pallas_call(
        _ffn_kernel,
        grid=(n_out // tm, nf),
        in_specs=[xs_spec,
                  pl.BlockSpec((1, d), lambda i, f: (0, 0)),
                  pl.BlockSpec((None, d, tf), lambda i, f: (layer, 0, f)),
                  pl.BlockSpec((None, d, tf), lambda i, f: (layer, 0, nf + f)),
                  pl.BlockSpec((None, tf, d), lambda i, f: (layer, f, 0)),
                  pl.BlockSpec((1, d), lambda i, f: (0, 0))],
        out_specs=pl.BlockSpec((tm, d), lambda i, f: (i, 0)),
        out_shape=jax.ShapeDtypeStruct((n_out, d), F32),
        scratch_shapes=[pltpu.VMEM((tm, d), BF16), pltpu.VMEM((tm, d), F32)],
        compiler_params=_cparams(("parallel", "arbitrary")),
        name="swiglu_ffn",
    )(xs, nw_in, w_ffn_in_all, w_ffn_in_all, w_ffn_out_all, nw_out)


def _s5_tables(lam_re, lam_im, b_re, b_im, c_re, c_im, log_step):
    groups = lam_re.shape[0]
    nblk = groups // S5_GB
    lr, li = lam_re.astype(F32), lam_im.astype(F32)
    step = jnp.exp(log_step.astype(F32))[:, None]
    mag = jnp.exp(lr * step)
    ab_re, ab_im = mag * jnp.cos(li * step), mag * jnp.sin(li * step)
    den = lr * lr + li * li
    coef_re = ((ab_re - 1.0) * lr + ab_im * li) / den
    coef_im = (ab_im * lr - (ab_re - 1.0) * li) / den
    br, bi = b_re.astype(F32), b_im.astype(F32)
    bb_re = coef_re[..., None] * br - coef_im[..., None] * bi
    bb_im = coef_re[..., None] * bi + coef_im[..., None] * br
    cr, ci = c_re.astype(F32), c_im.astype(F32)
    eye = jnp.eye(S5_GB, dtype=F32)
    taus = jnp.arange(S5_L)
    lrs = (lr * step).reshape(nblk, 1, S5_HALF)
    lis = (li * step).reshape(nblk, 1, S5_HALF)

    def lam_pow(n):
        nn = n.astype(F32)[None, :, None]
        m = jnp.exp(nn * lrs)
        return m * jnp.cos(nn * lis), m * jnp.sin(nn * lis)

    def blockdiag(t):
        t = t.reshape(nblk, S5_GB, S5_GROUP, S5_STATE)
        return jnp.einsum("ngcp,gh->ngchp", t, eye).reshape(nblk, LANES, S5_HALF)

    bc = jnp.stack([blockdiag(bb_re.transpose(0, 2, 1)), blockdiag(bb_im.transpose(0, 2, 1)),
                    blockdiag(cr), blockdiag(ci)], axis=1)

    n_list = jnp.concatenate([S5_L - 1 - taus, 1 + taus, S5_L * jnp.array([1, 2, 4, 0, 0, 0, 0, 0]),
                              S5_L * (1 + jnp.arange(8))])
    wr, wi = lam_pow(n_list)
    pw = jnp.concatenate([wr, wi], axis=2)
    return bc, pw


def kernel(x, meta, rel_bias, norm_w, w_in, conv_w, conv_b, lru_w_a, lru_b_a, lru_w_x, lru_b_x, lru_lambda, da_lambda, da_subln, s5_lam_re, s5_lam_im, s5_b_re, s5_b_im, s5_c_re, s5_c_im, s5_d, s5_log_step, s5_w_glu, s5_b_glu, b_gate, w_branch, w_out, w_ffn_in, w_ffn_out):
    bsz, seq, d_model = x.shape
    depth = w_in.shape[0]
    lru_w = conv_w.shape[2]
    s5_w = s5_d.shape[1]
    qk_w = DA_HEADS * 2 * DA_HEAD_DIM
    assert lru_w == LRU_BLOCKS * LANES and s5_w % LANES == 0 and d_model % LANES == 0
    col_gate, col_x = 0, lru_w // LANES
    col_q = 2 * lru_w // LANES
    col_k = col_q + qk_w // LANES
    col_v = col_k + qk_w // LANES
    col_u = col_v + DA_HEADS
    col_g = (col_u + s5_w // LANES) * LANES

    t_real = N_META + seq
    tb = 384 if t_real >= 1024 else 128
    t_pad = -(-t_real // tb) * tb
    assert t_pad % SCAN_CH == 0
    n_tok = bsz * t_pad
    tm_in = _pick_tile(n_tok, (1408, 1056, 768, 512, 384, 256, 128))
    tm_merge = _pick_tile(n_tok, (1056, 768, 512, 384, 256, 128))
    tm_ffn = _pick_tile(n_tok, (768, 512, 384, 256, 128))
    tm_out = _pick_tile(n_tok, (512, 384, 256, 128))
    tm_last = _pick_tile(seq, (512, 256, 128))

    xs = jnp.concatenate([jnp.broadcast_to(meta.astype(F32)[None], (bsz, N_META, d_model)), x,
                          jnp.zeros((bsz, t_pad - t_real, d_model), F32)], axis=1)
    xs = xs.reshape(n_tok, d_model)
    btiles = _bias_tiles(rel_bias.astype(F32), tb)

    w_in_b, w_glu_b, w_branch_b = w_in.astype(BF16), s5_w_glu.astype(BF16), w_branch.astype(BF16)
    w_out_b, w_ffn_in_b, w_ffn_out_b = w_out.astype(BF16), w_ffn_in.astype(BF16), w_ffn_out.astype(BF16)

    for l in range(depth):
        lam_init = 0.8 - 0.6 * math.exp(-0.3 * l)
        proj = _inproj(xs, norm_w[l, 0][None], w_in_b, l, tm_in, 1024)
        proj3 = proj.reshape(bsz, t_pad, proj.shape[1])

        lru_p = jnp.concatenate([conv_w[l], conv_b[l][None], lru_b_a[l][None], lru_b_x[l][None],
                                 jax.nn.log_sigmoid(lru_lambda[l].astype(F32))[None]], axis=0)
        w_ax = jnp.concatenate([lru_w_a[l], lru_w_x[l]], axis=2).astype(BF16)
        y_a = _lru(proj3, lru_p, w_ax, col_gate, col_x)

        y_b = _attn(proj3, btiles, da_lambda[l], da_subln[l][None], lam_init, tb, col_q, col_k, col_v)

        bc, pw = _s5_tables(s5_lam_re[l], s5_lam_im[l], s5_b_re[l], s5_b_im[l],
                            s5_c_re[l], s5_c_im[l], s5_log_step[l])
        y_c = _s5(proj3, bc, pw, s5_d[l][None], col_u)

        merged = _merge(y_a.reshape(n_tok, lru_w), y_b.reshape(n_tok, -1), y_c.reshape(n_tok, s5_w),
                        proj, w_glu_b, s5_b_glu[l][None], w_branch_b, b_gate[l], col_g, l, tm_merge, 512)
        xs = _outproj(merged, w_out_b, l, xs, norm_w[l, 1][None], tm_out)
        if l + 1 < depth:
            xs = _ffn(xs, norm_w[l, 2][None], w_ffn_in_b, w_ffn_out_b, norm_w[l, 3][None], l, tm_ffn, 512)
        else:
            out = _ffn(xs, norm_w[l, 2][None], w_ffn_in_b, w_ffn_out_b, norm_w[l, 3][None], l, tm_last,
                       512, real_rows=(t_pad, N_META, seq))

    return out.reshape(bsz, seq, d_model)
```

```python
import functools
import math

import jax
import jax.numpy as jnp
from jax import lax
from jax.experimental import pallas as pl
from jax.experimental.pallas import tpu as pltpu

F32 = jnp.float32
BF16 = jnp.bfloat16

N_META = 16
CONV_W = 4
LRU_C = 8.0
LRU_BLOCKS = 8
DA_HEADS = 8
DA_HEAD_DIM = 64
S5_GROUP = 16
S5_STATE = 64
REL_BUCKETS = 32
REL_MAX_DIST = 128
N_BRANCH = 3

LANES = 128
S5_GB = LANES // S5_GROUP
S5_HALF = S5_GB * S5_STATE
S5_L = 8
SCAN_CH = 128
MASK_NEG = -1e30
LOG2E = math.log2(math.e)
VMEM_LIMIT = 56 * 1024 * 1024


def _cparams(sem):
    return pltpu.CompilerParams(dimension_semantics=sem, vmem_limit_bytes=VMEM_LIMIT)


def _pick_tile(n, candidates):
    for c in candidates:
        if n % c == 0:
            return c
    raise ValueError(f"no tile in {candidates} divides {n}")


def _gelu_tanh(x):
    return 0.5 * x * (1.0 + jnp.tanh(math.sqrt(2.0 / math.pi) * (x + 0.044715 * (x * x * x))))


def _sigmoid(x):
    return 1.0 / (1.0 + jnp.exp(-x))


def _rms(x, w, eps):
    return (x * lax.rsqrt(jnp.mean(x * x, axis=-1, keepdims=True) + eps)) * w


def _inproj_kernel(x_ref, nw_ref, w_ref, o_ref, h_scr):
    @pl.when(pl.program_id(1) == 0)
    def _():
        h_scr[...] = _rms(x_ref[...], nw_ref[...], 1e-6).astype(BF16)

    o_ref[...] = jnp.dot(h_scr[...], w_ref[...], preferred_element_type=F32).astype(o_ref.dtype)


def _inproj(xs, nw, w_all, layer, tm, tn):
    n, d = xs.shape
    n_out = w_all.shape[2]
    return pl.pallas_call(
        _inproj_kernel,
        grid=(n // tm, n_out // tn),
        in_specs=[pl.BlockSpec((tm, d), lambda i, j: (i, 0)),
                  pl.BlockSpec((1, d), lambda i, j: (0, 0)),
                  pl.BlockSpec((None, d, tn), lambda i, j: (layer, 0, j))],
        out_specs=pl.BlockSpec((tm, tn), lambda i, j: (i, j)),
        out_shape=jax.ShapeDtypeStruct((n, n_out), BF16),
        scratch_shapes=[pltpu.VMEM((tm, d), BF16)],
        compiler_params=_cparams(("parallel", "arbitrary")),
        name="inproj",
    )(xs, nw, w_all)


def _lru_kernel(g_ref, x_ref, p_ref, w_ref, o_ref, xs_scr):
    t_len = x_ref.shape[1]
    halo = 8
    xs_scr[0:halo, :] = jnp.zeros((halo, LANES), F32)
    xs_scr[halo:, :] = x_ref[0].astype(F32)
    p = p_ref[...]
    cw = [p[j:j + 1, :] for j in range(CONV_W)]
    cb, ba, bx, logsig = p[4:5, :], p[5:6, :], p[6:7, :], p[7:8, :]
    w = w_ref[0]
    row = lax.broadcasted_iota(jnp.int32, (SCAN_CH, LANES), 0)

    def local_scan(base):
        xc = cb
        for j in range(CONV_W):
            xc = xc + cw[j] * xs_scr[pl.ds(base + halo - (CONV_W - 1) + j, SCAN_CH), :]
        ri = jnp.dot(xc.astype(BF16), w, preferred_element_type=F32)
        r = _sigmoid(ri[:, :LANES] + ba)
        i = _sigmoid(ri[:, LANES:] + bx)
        a = jnp.exp(LRU_C * r * logsig)
        b = jnp.sqrt(1.0 - a * a) * (i * xc)
        s = 1
        while s < SCAN_CH:
            keep = row >= s
            a_sh = jnp.where(keep, pltpu.roll(a, s, 0), 1.0)
            b_sh = jnp.where(keep, pltpu.roll(b, s, 0), 0.0)
            b = a * b_sh + b
            a = a * a_sh
            s *= 2
        return a, b

    n_chunks = t_len // SCAN_CH
    sub = 3 if n_chunks % 3 == 0 else 1

    def chunk(c, h0):
        bases = [pl.multiple_of((c * sub + j) * SCAN_CH, SCAN_CH) for j in range(sub)]
        scans = [local_scan(base) for base in bases]
        for base, (a, b) in zip(bases, scans):
            h = a * h0 + b
            g = g_ref[0, pl.ds(base, SCAN_CH), :].astype(F32)
            o_ref[0, pl.ds(base, SCAN_CH), :] = (h * _gelu_tanh(g)).astype(o_ref.dtype)
            h0 = h[SCAN_CH - 1:SCAN_CH, :]
        return h0

    lax.fori_loop(0, n_chunks // sub, chunk, jnp.zeros((1, LANES), F32))


def _lru(proj3, lru_p, w_ax, col_gate, col_x):
    b, t, _ = proj3.shape
    width = LRU_BLOCKS * LANES
    return pl.pallas_call(
        _lru_kernel,
        grid=(b, LRU_BLOCKS),
        in_specs=[pl.BlockSpec((1, t, LANES), lambda bi, h: (bi, 0, col_gate + h)),
                  pl.BlockSpec((1, t, LANES), lambda bi, h: (bi, 0, col_x + h)),
                  pl.BlockSpec((8, LANES), lambda bi, h: (0, h)),
                  pl.BlockSpec((1, LANES, 2 * LANES), lambda bi, h: (h, 0, 0))],
        out_specs=pl.BlockSpec((1, t, LANES), lambda bi, h: (bi, 0, h)),
        out_shape=jax.ShapeDtypeStruct((b, t, width), BF16),
        scratch_shapes=[pltpu.VMEM((t + 8, LANES), F32)],
        compiler_params=_cparams(("parallel", "parallel")),
        name="rglru",
    )(proj3, proj3, lru_p, w_ax)


def _bias_tile_kernel(rb_ref, o_ref):
    h = pl.program_id(0)
    tb = o_ref.shape[1]
    i = lax.broadcasted_iota(jnp.int32, (tb, 2 * tb), 0)
    j2 = lax.broadcasted_iota(jnp.int32, (tb, 2 * tb), 1)
    max_exact = REL_BUCKETS // 2
    d = i - j2 + tb
    n = jnp.maximum(d, 0)
    nf = jnp.maximum(n, 1).astype(F32)
    large = max_exact + (jnp.log(nf / max_exact) / math.log(REL_MAX_DIST / max_exact)
                         * (REL_BUCKETS - max_exact)).astype(jnp.int32)
    large = jnp.minimum(large, REL_BUCKETS - 1)
    bucket = jnp.where(n < max_exact, n, large)
    val = jnp.zeros((tb, 2 * tb), F32)
    for bkt in range(REL_BUCKETS):
        val = jnp.where(bucket == bkt, rb_ref[bkt, h], val)
    val = val - rb_ref[REL_BUCKETS - 1, h]
    o_ref[0] = jnp.where(d >= 0, val, MASK_NEG)


def _bias_tiles(rel_bias, tb):
    return pl.pallas_call(
        _bias_tile_kernel,
        grid=(DA_HEADS,),
        in_specs=[pl.BlockSpec(memory_space=pltpu.SMEM)],
        out_specs=pl.BlockSpec((1, tb, 2 * tb), lambda h: (h, 0, 0)),
        out_shape=jax.ShapeDtypeStruct((DA_HEADS, tb, 2 * tb), F32),
        compiler_params=_cparams(("parallel",)),
        name="t5_bias_tiles",
    )(rel_bias)


def _attn_qblock(qi, q_ref, k_ref, v1_scr, bt_ref, sw_ref, o_ref, m_scr, acc_scr, s_scr, accp_scr, lam,
                 lam_init):
    tb = s_scr.shape[2]
    dn_t = (((1,), (1,)), ((), ()))
    lane = lax.broadcasted_iota(jnp.int32, (tb, LANES), 1)
    qstart = pl.multiple_of(qi * tb, tb)
    qs = q_ref[0, pl.ds(qstart, tb), :] * (DA_HEAD_DIM ** -0.5)
    zero = jnp.zeros_like(qs)
    qcat = jnp.concatenate([jnp.where(lane < DA_HEAD_DIM, qs, zero),
                            jnp.where(lane >= DA_HEAD_DIM, qs, zero)], axis=0)
    n_far = jnp.maximum(qi - 1, 0)
    n_quads = n_far // 4
    n_rest = n_far % 4
    tail_blk = 4 * n_quads

    def scores(blk, nblocks, biased):
        kb = k_ref[0, pl.ds(pl.multiple_of(blk * tb, tb), nblocks * tb), :]
        s_both = lax.dot_general(qcat, kb, dn_t, preferred_element_type=F32)
        out = []
        for c in range(2):
            s = s_both[c * tb:(c + 1) * tb]
            if biased:
                nb = min(nblocks, 2)
                plain = (nblocks - nb) * tb
                tail = s[:, plain:] + bt_ref[0, :, (2 - nb) * tb:]
                s = tail if plain == 0 else jnp.concatenate([s[:, :plain], tail], axis=1)
            s = s * LOG2E
            for i in range(nblocks):
                s_scr[c, blk + i] = s[:, i * tb:(i + 1) * tb]
            out.append(s)
        return out

    def lane_chunks(s):
        return [s[:, j * LANES:(j + 1) * LANES] for j in range(s.shape[1] // LANES)]

    def max_into(ms, ss):
        out = []
        for m, s in zip(ms, ss):
            for ch in lane_chunks(s):
                m = jnp.maximum(m, ch)
            out.append(m)
        return out

    m_init = jnp.full((tb, LANES), MASK_NEG, F32)

    @pl.when(qi == 0)
    def _():
        m_scr[0], m_scr[1] = max_into((m_init, m_init), scores(0, 1, True))

    for rest in range(4):
        @pl.when(jnp.logical_and(qi >= 1, n_rest == rest))
        def _():
            m_scr[0], m_scr[1] = max_into((m_init, m_init), scores(tail_blk, rest + 2, True))
            _attn_finalize(qstart - tb, accp_scr, sw_ref, o_ref, lam, lam_init)

    def quad_max(ki, ms):
        return tuple(max_into(ms, scores(4 * ki, 4, False)))

    ms = lax.fori_loop(0, n_quads, quad_max, (m_scr[0], m_scr[1]))
    mb = [jnp.broadcast_to(jnp.max(ms[c], axis=-1, keepdims=True), (tb, LANES)) for c in range(2)]

    def accumulate(blk, nblocks, st):
        vb = v1_scr[pl.ds(pl.multiple_of(blk * tb, tb), nblocks * tb), :]
        out = []
        for c in range(2):
            ps = []
            for i in range(nblocks):
                ps += [jnp.exp2(ch - mb[c]) for ch in lane_chunks(s_scr[c, blk + i])]
            p = jnp.concatenate(ps, axis=1).astype(BF16)
            out.append(st[c] + jnp.dot(p, vb, preferred_element_type=F32))
        return tuple(out)

    def store_state(st):
        acc_scr[0], acc_scr[1] = st

    za = jnp.zeros((tb, 2 * LANES), F32)

    @pl.when(qi == 0)
    def _():
        store_state(accumulate(0, 1, (za, za)))

    for rest in range(4):
        @pl.when(jnp.logical_and(qi >= 1, n_rest == rest))
        def _():
            store_state(accumulate(tail_blk, rest + 2, (za, za)))

    a0, a1 = lax.fori_loop(0, n_quads, lambda ki, st: accumulate(4 * ki, 4, st),
                           (acc_scr[0], acc_scr[1]))

    accp_scr[0], accp_scr[1] = a0, a1


def _attn_finalize(row_start, accp_scr, sw_ref, o_ref, lam, lam_init):
    tb = accp_scr.shape[1]
    a0, a1 = accp_scr[0], accp_scr[1]
    o = a0[:, :LANES] / a0[:, LANES:] - lam * (a1[:, :LANES] / a1[:, LANES:])
    o = _rms(o, sw_ref[...], 1e-5) * (1.0 - lam_init)
    if not isinstance(row_start, int):
        row_start = pl.multiple_of(row_start, tb)
    o_ref[0, pl.ds(row_start, tb), :] = o.astype(o_ref.dtype)


def _attn_kernel(q_ref, k_ref, v_ref, bt_ref, dl_ref, sw_ref, o_ref, m_scr, acc_scr, s_scr, v1_scr, accp_scr,
                 *, lam_init):
    dl = dl_ref[...]
    lam = (jnp.exp(jnp.sum(dl[0:1, :] * dl[1:2, :], axis=-1, keepdims=True))
           - jnp.exp(jnp.sum(dl[2:3, :] * dl[3:4, :], axis=-1, keepdims=True)) + lam_init)
    v1_scr[:, :LANES] = v_ref[0]
    v1_scr[:, LANES:] = jnp.ones((v_ref.shape[1], LANES), BF16)

    def qblock(qi, carry):
        _attn_qblock(qi, q_ref, k_ref, v1_scr, bt_ref, sw_ref, o_ref, m_scr, acc_scr, s_scr, accp_scr, lam,
                     lam_init)
        return carry

    tb = s_scr.shape[2]
    nq = q_ref.shape[1] // tb
    lax.fori_loop(0, nq, qblock, 0)
    _attn_finalize((nq - 1) * tb, accp_scr, sw_ref, o_ref, lam, lam_init)


def _attn(proj3, btiles, da_lambda, da_subln, lam_init, tb, col_q, col_k, col_v):
    b, t, _ = proj3.shape
    width = DA_HEADS * LANES
    return pl.pallas_call(
        functools.partial(_attn_kernel, lam_init=lam_init),
        grid=(b, DA_HEADS),
        in_specs=[pl.BlockSpec((1, t, LANES), lambda bi, h: (bi, 0, col_q + h)),
                  pl.BlockSpec((1, t, LANES), lambda bi, h: (bi, 0, col_k + h)),
                  pl.BlockSpec((1, t, LANES), lambda bi, h: (bi, 0, col_v + h)),
                  pl.BlockSpec((1, tb, 2 * tb), lambda bi, h: (h, 0, 0)),
                  pl.BlockSpec((4, DA_HEAD_DIM), lambda bi, h: (0, 0)),
                  pl.BlockSpec((1, LANES), lambda bi, h: (0, 0))],
        out_specs=pl.BlockSpec((1, t, LANES), lambda bi, h: (bi, 0, h)),
        out_shape=jax.ShapeDtypeStruct((b, t, width), BF16),
        scratch_shapes=[pltpu.VMEM((2, tb, LANES), F32), pltpu.VMEM((2, tb, 2 * LANES), F32),
                        pltpu.VMEM((2, t // tb, tb, tb), F32), pltpu.VMEM((t, 2 * LANES), BF16),
                        pltpu.VMEM((2, tb, 2 * LANES), F32)],
        compiler_params=_cparams(("parallel", "parallel")),
        name="diff_attn",
    )(proj3, proj3, proj3, btiles, da_lambda, da_subln)


def _s5_kernel(u_ref, bc_ref, pw_ref, d_ref, o_ref, uf_scr, g_scr, hp_scr, y_scr,
               wg_scr, wk_scr, wct_scr):
    t_len = u_ref.shape[1]
    nj = t_len // S5_L
    pw = pw_ref[0]

    @pl.when(pl.program_id(1) == 0)
    def _():
        bd_re, bd_im, ct_re, ct_im = bc_ref[0, 0], bc_ref[0, 1], bc_ref[0, 2], bc_ref[0, 3]
        for s in range(S5_L):
            rows = slice(s * LANES, (s + 1) * LANES)
            qr, qi = pw[s:s + 1, :S5_HALF], pw[s:s + 1, S5_HALF:]
            wg_scr[rows, :S5_HALF] = (qr * bd_re - qi * bd_im).astype(BF16)
            wg_scr[rows, S5_HALF:] = (qr * bd_im + qi * bd_re).astype(BF16)
            er, ei = pw[8 + s:9 + s, :S5_HALF], pw[8 + s:9 + s, S5_HALF:]
            wct_scr[rows, :S5_HALF] = (ct_re * er - ct_im * ei).astype(BF16)
            wct_scr[rows, S5_HALF:] = (-(ct_re * ei + ct_im * er)).astype(BF16)
        dn_t = (((1,), (1,)), ((), ()))
        bcat = jnp.concatenate([bd_re, bd_im], axis=1).astype(BF16)
        crows = [jnp.concatenate([ct_re, -ct_im], axis=1).astype(BF16)]
        crows += [wct_scr[s * LANES:(s + 1) * LANES, :] for s in range(S5_L - 1)]
        kts = [lax.dot_general(bcat, c, dn_t, preferred_element_type=F32).astype(BF16) for c in crows]
        for s in range(S5_L):
            for r in range(S5_L):
                wk_scr[s * LANES:(s + 1) * LANES, r * LANES:(r + 1) * LANES] = (
                    kts[r - s] if r >= s else jnp.zeros((LANES, LANES), BF16))

    uf_scr[...] = u_ref[0].astype(F32)
    ur = jnp.concatenate([uf_scr[pl.ds(s, nj, stride=S5_L), :].astype(BF16) for s in range(S5_L)],
                         axis=1)
    g_scr[...] = jnp.dot(ur, wg_scr[...], preferred_element_type=F32)

    pr, pi = pw[24:32, :S5_HALF], pw[24:32, S5_HALF:]
    row = lax.broadcasted_iota(jnp.int32, (8, S5_HALF), 0)

    def local_scan(base):
        x = g_scr[pl.ds(base, 8), :]
        xr, xi = x[:, :S5_HALF], x[:, S5_HALF:]
        for lvl, s in enumerate((1, 2, 4)):
            ar, ai = pw[16 + lvl:17 + lvl, :S5_HALF], pw[16 + lvl:17 + lvl, S5_HALF:]
            keep = row >= s
            sr = jnp.where(keep, pltpu.roll(xr, s, 0), 0.0)
            si = jnp.where(keep, pltpu.roll(xi, s, 0), 0.0)
            xr, xi = xr + (ar * sr - ai * si), xi + (ar * si + ai * sr)
        return xr, xi

    first = row == 0
    per_iter = 2

    def groups(gi, carry):
        cr, ci = carry
        bases = [pl.multiple_of((gi * per_iter + k) * 8, 8) for k in range(per_iter)]
        scans = [local_scan(base) for base in bases]
        for base, (xr, xi) in zip(bases, scans):
            xr, xi = xr + (pr * cr - pi * ci), xi + (pr * ci + pi * cr)
            hp_scr[pl.ds(base, 8), :] = jnp.concatenate(
                [jnp.where(first, cr, pltpu.roll(xr, 1, 0)), jnp.where(first, ci, pltpu.roll(xi, 1, 0))],
                axis=1)
            cr, ci = xr[7:8, :], xi[7:8, :]
        return cr, ci

    z = jnp.zeros((1, S5_HALF), F32)
    lax.fori_loop(0, nj // (8 * per_iter), groups, (z, z))

    y = (jnp.dot(ur, wk_scr[...], preferred_element_type=F32)
         + lax.dot_general(hp_scr[...].astype(BF16), wct_scr[...], (((1,), (1,)), ((), ())),
                           preferred_element_type=F32))
    for s in range(S5_L):
        y_scr[pl.ds(s, nj, stride=S5_L), :] = y[:, s * LANES:(s + 1) * LANES]
    o_ref[0] = _gelu_tanh(y_scr[...] + d_ref[...] * uf_scr[...]).astype(o_ref.dtype)


def _s5(proj3, bc, pw, dvec, col_u):
    b, t, _ = proj3.shape
    nblk = bc.shape[0]
    nj = t // S5_L
    wide = S5_L * LANES
    assert nj % 16 == 0
    return pl.pallas_call(
        _s5_kernel,
        grid=(nblk, b),
        in_specs=[pl.BlockSpec((1, t, LANES), lambda g, bi: (bi, 0, col_u + g)),
                  pl.BlockSpec((1, 4, LANES, S5_HALF), lambda g, bi: (g, 0, 0, 0)),
                  pl.BlockSpec((1, 32, 2 * S5_HALF), lambda g, bi: (g, 0, 0)),
                  pl.BlockSpec((1, LANES), lambda g, bi: (0, g))],
        out_specs=pl.BlockSpec((1, t, LANES), lambda g, bi: (bi, 0, g)),
        out_shape=jax.ShapeDtypeStruct((b, t, nblk * LANES), BF16),
        scratch_shapes=[pltpu.VMEM((t, LANES), F32), pltpu.VMEM((nj, 2 * S5_HALF), F32),
                        pltpu.VMEM((nj, 2 * S5_HALF), F32), pltpu.VMEM((t, LANES), F32),
                        pltpu.VMEM((wide, 2 * S5_HALF), BF16), pltpu.VMEM((wide, wide), BF16),
                        pltpu.VMEM((wide, 2 * S5_HALF), BF16)],
        compiler_params=_cparams(("parallel", "arbitrary")),
        name="s5",
    )(proj3, bc, pw, dvec)


def _merge_kernel(ya_ref, yb_ref, yc_ref, g0_ref, g1_ref, g2_ref, wg_ref, bg_ref, wb_ref, bgate_ref,
                  o_ref, yc_scr):
    @pl.when(pl.program_id(1) == 0)
    def _():
        yc = yc_ref[...]
        z = jnp.dot(yc, wg_ref[...], preferred_element_type=F32) + bg_ref[...]
        yc_scr[...] = (yc.astype(F32) * _sigmoid(z)).astype(BF16)

    bgate = bgate_ref[...]
    ys = (ya_ref[...], yb_ref[...], yc_scr[...])
    gs = (g0_ref, g1_ref, g2_ref)
    merged = None
    for br in range(N_BRANCH):
        gate = _sigmoid(gs[br][...].astype(F32) + bgate[br:br + 1, :])
        term = gate * jnp.dot(ys[br], wb_ref[br], preferred_element_type=F32)
        merged = term if merged is None else merged + term
    o_ref[...] = merged.astype(o_ref.dtype)


def _merge(ya, yb, yc, proj, w_glu_all, b_glu, w_branch_all, b_gate, col_g, layer, tm, tn):
    n, wdt = ya.shape
    d = w_branch_all.shape[3]
    gcol = [(col_g + br * d) // tn for br in range(N_BRANCH)]
    yspec = pl.BlockSpec((tm, wdt), lambda i, j: (i, 0))

    def gspec(br):
        return pl.BlockSpec((tm, tn), lambda i, j: (i, gcol[br] + j))

    return pl.pallas_call(
        _merge_kernel,
        grid=(n // tm, d // tn),
        in_specs=[yspec, yspec, yspec, gspec(0), gspec(1), gspec(2),
                  pl.BlockSpec((None, wdt, wdt), lambda i, j: (layer, 0, 0)),
                  pl.BlockSpec((1, wdt), lambda i, j: (0, 0)),
                  pl.BlockSpec((None, N_BRANCH, wdt, tn), lambda i, j: (layer, 0, 0, j)),
                  pl.BlockSpec((N_BRANCH, tn), lambda i, j: (0, j))],
        out_specs=pl.BlockSpec((tm, tn), lambda i, j: (i, j)),
        out_shape=jax.ShapeDtypeStruct((n, d), BF16),
        scratch_shapes=[pltpu.VMEM((tm, wdt), BF16)],
        compiler_params=_cparams(("parallel", "arbitrary")),
        name="merge",
    )(ya, yb, yc, proj, proj, proj, w_glu_all, b_glu, w_branch_all, b_gate)


def _outproj_kernel(m_ref, w_ref, xs_ref, nw_ref, o_ref):
    mix = jnp.dot(m_ref[...], w_ref[...], preferred_element_type=F32)
    o_ref[...] = xs_ref[...] + _rms(mix, nw_ref[...], 1e-6)


def _outproj(merged, w_out_all, layer, xs, nw, tm):
    n, d = xs.shape
    return pl.pallas_call(
        _outproj_kernel,
        grid=(n // tm,),
        in_specs=[pl.BlockSpec((tm, d), lambda i: (i, 0)),
                  pl.BlockSpec((None, d, d), lambda i: (layer, 0, 0)),
                  pl.BlockSpec((tm, d), lambda i: (i, 0)),
                  pl.BlockSpec((1, d), lambda i: (0, 0))],
        out_specs=pl.BlockSpec((tm, d), lambda i: (i, 0)),
        out_shape=jax.ShapeDtypeStruct((n, d), F32),
        compiler_params=_cparams(("parallel",)),
        name="outproj_residual",
    )(merged, w_out_all, xs, nw)


def _ffn_kernel(xs_ref, nw_in_ref, wg_ref, wu_ref, wo_ref, nw_out_ref, o_ref, h_scr, acc_scr):
    f = pl.program_id(1)

    @pl.when(f == 0)
    def _():
        h_scr[...] = _rms(xs_ref[...], nw_in_ref[...], 1e-6).astype(BF16)
        acc_scr[...] = jnp.zeros(acc_scr.shape, F32)

    h = h_scr[...]
    gate = jnp.dot(h, wg_ref[...], preferred_element_type=F32)
    up = jnp.dot(h, wu_ref[...], preferred_element_type=F32)
    act = (gate * _sigmoid(gate) * up).astype(BF16)
    acc_scr[...] += jnp.dot(act, wo_ref[...], preferred_element_type=F32)

    @pl.when(f == pl.num_programs(1) - 1)
    def _():
        o_ref[...] = xs_ref[...] + _rms(acc_scr[...], nw_out_ref[...], 1e-6)


def _ffn(xs, nw_in, w_ffn_in_all, w_ffn_out_all, nw_out, layer, tm, tf, real_rows=None):
    n, d = xs.shape
    d_ff = w_ffn_out_all.shape[1]
    nf = d_ff // tf
    if real_rows is None:
        n_out = n
        xs_spec = pl.BlockSpec((tm, d), lambda i, f: (i, 0))
    else:
        t_pad, first, count = real_rows
        per_seq = count // tm
        n_out = (n // t_pad) * count
        xs_spec = pl.BlockSpec((pl.Element(tm), pl.Element(d)),
                               lambda i, f: (pl.multiple_of(
                                   (i // per_seq) * t_pad + first + (i % per_seq) * tm, 8), 0))
        assert t_pad % 8 == 0 and first % 8 == 0 and tm % 8 == 0
    return pl.pallas_call(
        _ffn_kernel,
        grid=(n_out // tm, nf),
        in_specs=[xs_spec,
                  pl.BlockSpec((1, d), lambda i, f: (0, 0)),
                  pl.BlockSpec((None, d, tf), lambda i, f: (layer, 0, f)),
                  pl.BlockSpec((None, d, tf), lambda i, f: (layer, 0, nf + f)),
                  pl.BlockSpec((None, tf, d), lambda i, f: (layer, f, 0)),
                  pl.BlockSpec((1, d), lambda i, f: (0, 0))],
        out_specs=pl.BlockSpec((tm, d), lambda i, f: (i, 0)),
        out_shape=jax.ShapeDtypeStruct((n_out, d), F32),
        scratch_shapes=[pltpu.VMEM((tm, d), BF16), pltpu.VMEM((tm, d), F32)],
        compiler_params=_cparams(("parallel", "arbitrary")),
        name="swiglu_ffn",
    )(xs, nw_in, w_ffn_in_all, w_ffn_in_all, w_ffn_out_all, nw_out)


def _s5_tables(lam_re, lam_im, b_re, b_im, c_re, c_im, log_step):
    groups = lam_re.shape[0]
    nblk = groups // S5_GB
    lr, li = lam_re.astype(F32), lam_im.astype(F32)
    step = jnp.exp(log_step.astype(F32))[:, None]
    mag = jnp.exp(lr * step)
    ab_re, ab_im = mag * jnp.cos(li * step), mag * jnp.sin(li * step)
    den = lr * lr + li * li
    coef_re = ((ab_re - 1.0) * lr + ab_im * li) / den
    coef_im = (ab_im * lr - (ab_re - 1.0) * li) / den
    br, bi = b_re.astype(F32), b_im.astype(F32)
    bb_re = coef_re[..., None] * br - coef_im[..., None] * bi
    bb_im = coef_re[..., None] * bi + coef_im[..., None] * br
    cr, ci = c_re.astype(F32), c_im.astype(F32)
    eye = jnp.eye(S5_GB, dtype=F32)
    taus = jnp.arange(S5_L)
    lrs = (lr * step).reshape(nblk, 1, S5_HALF)
    lis = (li * step).reshape(nblk, 1, S5_HALF)

    def lam_pow(n):
        nn = n.astype(F32)[None, :, None]
        m = jnp.exp(nn * lrs)
        return m * jnp.cos(nn * lis), m * jnp.sin(nn * lis)

    def blockdiag(t):
        t = t.reshape(nblk, S5_GB, S5_GROUP, S5_STATE)
        return jnp.einsum("ngcp,gh->ngchp", t, eye).reshape(nblk, LANES, S5_HALF)

    bc = jnp.stack([blockdiag(bb_re.transpose(0, 2, 1)), blockdiag(bb_im.transpose(0, 2, 1)),
                    blockdiag(cr), blockdiag(ci)], axis=1)

    n_list = jnp.concatenate([S5_L - 1 - taus, 1 + taus, S5_L * jnp.array([1, 2, 4, 0, 0, 0, 0, 0]),
                              S5_L * (1 + jnp.arange(8))])
    wr, wi = lam_pow(n_list)
    pw = jnp.concatenate([wr, wi], axis=2)
    return bc, pw


def kernel(x, meta, rel_bias, norm_w, w_in, conv_w, conv_b, lru_w_a, lru_b_a, lru_w_x, lru_b_x, lru_lambda, da_lambda, da_subln, s5_lam_re, s5_lam_im, s5_b_re, s5_b_im, s5_c_re, s5_c_im, s5_d, s5_log_step, s5_w_glu, s5_b_glu, b_gate, w_branch, w_out, w_ffn_in, w_ffn_out):
    bsz, seq, d_model = x.shape
    depth = w_in.shape[0]
    lru_w = conv_w.shape[2]
    s5_w = s5_d.shape[1]
    qk_w = DA_HEADS * 2 * DA_HEAD_DIM
    assert lru_w == LRU_BLOCKS * LANES and s5_w % LANES == 0 and d_model % LANES == 0
    col_gate, col_x = 0, lru_w // LANES
    col_q = 2 * lru_w // LANES
    col_k = col_q + qk_w // LANES
    col_v = col_k + qk_w // LANES
    col_u = col_v + DA_HEADS
    col_g = (col_u + s5_w // LANES) * LANES

    t_real = N_META + seq
    tb = 384 if t_real >= 1024 else 128
    t_pad = -(-t_real // tb) * tb
    assert t_pad % SCAN_CH == 0
    n_tok = bsz * t_pad
    tm_in = _pick_tile(n_tok, (1408, 1056, 768, 512, 384, 256, 128))
    tm_merge = _pick_tile(n_tok, (1056, 768, 512, 384, 256, 128))
    tm_ffn = _pick_tile(n_tok, (768, 512, 384, 256, 128))
    tm_out = _pick_tile(n_tok, (512, 384, 256, 128))
    tm_last = _pick_tile(seq, (512, 256, 128))

    xs = jnp.concatenate([jnp.broadcast_to(meta.astype(F32)[None], (bsz, N_META, d_model)), x,
                          jnp.zeros((bsz, t_pad - t_real, d_model), F32)], axis=1)
    xs = xs.reshape(n_tok, d_model)
    btiles = _bias_tiles(rel_bias.astype(F32), tb)

    w_in_b, w_glu_b, w_branch_b = w_in.astype(BF16), s5_w_glu.astype(BF16), w_branch.astype(BF16)
    w_out_b, w_ffn_in_b, w_ffn_out_b = w_out.astype(BF16), w_ffn_in.astype(BF16), w_ffn_out.astype(BF16)

    for l in range(depth):
        lam_init = 0.8 - 0.6 * math.exp(-0.3 * l)
        proj = _inproj(xs, norm_w[l, 0][None], w_in_b, l, tm_in, 1024)
        proj3 = proj.reshape(bsz, t_pad, proj.shape[1])

        lru_p = jnp.concatenate([conv_w[l], conv_b[l][None], lru_b_a[l][None], lru_b_x[l][None],
                                 jax.nn.log_sigmoid(lru_lambda[l].astype(F32))[None]], axis=0)
        w_ax = jnp.concatenate([lru_w_a[l], lru_w_x[l]], axis=2).astype(BF16)
        y_a = _lru(proj3, lru_p, w_ax, col_gate, col_x)

        y_b = _attn(proj3, btiles, da_lambda[l], da_subln[l][None], lam_init, tb, col_q, col_k, col_v)

        bc, pw = _s5_tables(s5_lam_re[l], s5_lam_im[l], s5_b_re[l], s5_b_im[l],
                            s5_c_re[l], s5_c_im[l], s5_log_step[l])
        y_c = _s5(proj3, bc, pw, s5_d[l][None], col_u)

        merged = _merge(y_a.reshape(n_tok, lru_w), y_b.reshape(n_tok, -1), y_c.reshape(n_tok, s5_w),
                        proj, w_glu_b, s5_b_glu[l][None], w_branch_b, b_gate[l], col_g, l, tm_merge, 512)
        xs = _outproj(merged, w_out_b, l, xs, norm_w[l, 1][None], tm_out)
        if l + 1 < depth:
            xs = _ffn(xs, norm_w[l, 2][None], w_ffn_in_b, w_ffn_out_b, norm_w[l, 3][None], l, tm_ffn, 512)
        else:
            out = _ffn(xs, norm_w[l, 2][None], w_ffn_in_b, w_ffn_out_b, norm_w[l, 3][None], l, tm_last,
                       512, real_rows=(t_pad, N_META, seq))

    return out.reshape(bsz, seq, d_model)
```

```python
import functools
import math

import jax
import jax.numpy as jnp
from jax import lax
from jax.experimental import pallas as pl
from jax.experimental.pallas import tpu as pltpu

F32 = jnp.float32
BF16 = jnp.bfloat16

N_META = 16
CONV_W = 4
LRU_C = 8.0
LRU_BLOCKS = 8
DA_HEADS = 8
DA_HEAD_DIM = 64
S5_GROUP = 16
S5_STATE = 64
REL_BUCKETS = 32
REL_MAX_DIST = 128
N_BRANCH = 3

LANES = 128
S5_GB = LANES // S5_GROUP
S5_HALF = S5_GB * S5_STATE
S5_L = 8
SCAN_CH = 128
MASK_NEG = -1e30
LOG2E = math.log2(math.e)
VMEM_LIMIT = 56 * 1024 * 1024


def _cparams(sem):
    return pltpu.CompilerParams(dimension_semantics=sem, vmem_limit_bytes=VMEM_LIMIT)


def _pick_tile(n, candidates):
    for c in candidates:
        if n % c == 0:
            return c
    raise ValueError(f"no tile in {candidates} divides {n}")


def _gelu_tanh(x):
    return 0.5 * x * (1.0 + jnp.tanh(math.sqrt(2.0 / math.pi) * (x + 0.044715 * (x * x * x))))


def _sigmoid(x):
    return 1.0 / (1.0 + jnp.exp(-x))


def _rms(x, w, eps):
    return (x * lax.rsqrt(jnp.mean(x * x, axis=-1, keepdims=True) + eps)) * w


def _inproj_kernel(x_ref, nw_ref, w_ref, o_ref, h_scr):
    @pl.when(pl.program_id(1) == 0)
    def _():
        h_scr[...] = _rms(x_ref[...], nw_ref[...], 1e-6).astype(BF16)

    o_ref[...] = jnp.dot(h_scr[...], w_ref[...], preferred_element_type=F32).astype(o_ref.dtype)


def _inproj(xs, nw, w_all, layer, tm, tn):
    n, d = xs.shape
    n_out = w_all.shape[2]
    return pl.pallas_call(
        _inproj_kernel,
        grid=(n // tm, n_out // tn),
        in_specs=[pl.BlockSpec((tm, d), lambda i, j: (i, 0)),
                  pl.BlockSpec((1, d), lambda i, j: (0, 0)),
                  pl.BlockSpec((None, d, tn), lambda i, j: (layer, 0, j))],
        out_specs=pl.BlockSpec((tm, tn), lambda i, j: (i, j)),
        out_shape=jax.ShapeDtypeStruct((n, n_out), BF16),
        scratch_shapes=[pltpu.VMEM((tm, d), BF16)],
        compiler_params=_cparams(("parallel", "arbitrary")),
        name="inproj",
    )(xs, nw, w_all)


def _lru_kernel(g_ref, x_ref, p_ref, w_ref, o_ref, xs_scr):
    t_len = x_ref.shape[1]
    halo = 8
    xs_scr[0:halo, :] = jnp.zeros((halo, LANES), F32)
    xs_scr[halo:, :] = x_ref[0].astype(F32)
    p = p_ref[...]
    cw = [p[j:j + 1, :] for j in range(CONV_W)]
    cb, ba, bx, logsig = p[4:5, :], p[5:6, :], p[6:7, :], p[7:8, :]
    w = w_ref[0]
    row = lax.broadcasted_iota(jnp.int32, (SCAN_CH, LANES), 0)

    def local_scan(base):
        xc = cb
        for j in range(CONV_W):
            xc = xc + cw[j] * xs_scr[pl.ds(base + halo - (CONV_W - 1) + j, SCAN_CH), :]
        ri = jnp.dot(xc.astype(BF16), w, preferred_element_type=F32)
        r = _sigmoid(ri[:, :LANES] + ba)
        i = _sigmoid(ri[:, LANES:] + bx)
        a = jnp.exp(LRU_C * r * logsig)
        b = jnp.sqrt(1.0 - a * a) * (i * xc)
        s = 1
        while s < SCAN_CH:
            keep = row >= s
            a_sh = jnp.where(keep, pltpu.roll(a, s, 0), 1.0)
            b_sh = jnp.where(keep, pltpu.roll(b, s, 0), 0.0)
            b = a * b_sh + b
            a = a * a_sh
            s *= 2
        return a, b

    n_chunks = t_len // SCAN_CH
    sub = 3 if n_chunks % 3 == 0 else 1

    def chunk(c, h0):
        bases = [pl.multiple_of((c * sub + j) * SCAN_CH, SCAN_CH) for j in range(sub)]
        scans = [local_scan(base) for base in bases]
        for base, (a, b) in zip(bases, scans):
            h = a * h0 + b
            g = g_ref[0, pl.ds(base, SCAN_CH), :].astype(F32)
            o_ref[0, pl.ds(base, SCAN_CH), :] = (h * _gelu_tanh(g)).astype(o_ref.dtype)
            h0 = h[SCAN_CH - 1:SCAN_CH, :]
        return h0

    lax.fori_loop(0, n_chunks // sub, chunk, jnp.zeros((1, LANES), F32))


def _lru(proj3, lru_p, w_ax, col_gate, col_x):
    b, t, _ = proj3.shape
    width = LRU_BLOCKS * LANES
    return pl.pallas_call(
        _lru_kernel,
        grid=(b, LRU_BLOCKS),
        in_specs=[pl.BlockSpec((1, t, LANES), lambda bi, h: (bi, 0, col_gate + h)),
                  pl.BlockSpec((1, t, LANES), lambda bi, h: (bi, 0, col_x + h)),
                  pl.BlockSpec((8, LANES), lambda bi, h: (0, h)),
                  pl.BlockSpec((1, LANES, 2 * LANES), lambda bi, h: (h, 0, 0))],
        out_specs=pl.BlockSpec((1, t, LANES), lambda bi, h: (bi, 0, h)),
        out_shape=jax.ShapeDtypeStruct((b, t, width), BF16),
        scratch_shapes=[pltpu.VMEM((t + 8, LANES), F32)],
        compiler_params=_cparams(("parallel", "parallel")),
        name="rglru",
    )(proj3, proj3, lru_p, w_ax)


def _bias_tile_kernel(rb_ref, o_ref):
    h = pl.program_id(0)
    tb = o_ref.shape[1]
    i = lax.broadcasted_iota(jnp.int32, (tb, 2 * tb), 0)
    j2 = lax.broadcasted_iota(jnp.int32, (tb, 2 * tb), 1)
    max_exact = REL_BUCKETS // 2
    d = i - j2 + tb
    n = jnp.maximum(d, 0)
    nf = jnp.maximum(n, 1).astype(F32)
    large = max_exact + (jnp.log(nf / max_exact) / math.log(REL_MAX_DIST / max_exact)
                         * (REL_BUCKETS - max_exact)).astype(jnp.int32)
    large = jnp.minimum(large, REL_BUCKETS - 1)
    bucket = jnp.where(n < max_exact, n, large)
    val = jnp.zeros((tb, 2 * tb), F32)
    for bkt in range(REL_BUCKETS):
        val = jnp.where(bucket == bkt, rb_ref[bkt, h], val)
    val = val - rb_ref[REL_BUCKETS - 1, h]
    o_ref[0] = jnp.where(d >= 0, val, MASK_NEG)


def _bias_tiles(rel_bias, tb):
    return pl.pallas_call(
        _bias_tile_kernel,
        grid=(DA_HEADS,),
        in_specs=[pl.BlockSpec(memory_space=pltpu.SMEM)],
        out_specs=pl.BlockSpec((1, tb, 2 * tb), lambda h: (h, 0, 0)),
        out_shape=jax.ShapeDtypeStruct((DA_HEADS, tb, 2 * tb), F32),
        compiler_params=_cparams(("parallel",)),
        name="t5_bias_tiles",
    )(rel_bias)


def _attn_qblock(qi, q_ref, k_ref, v1_scr, bt_ref, sw_ref, o_ref, m_scr, acc_scr, s_scr, accp_scr, lam,
                 lam_init):
    tb = s_scr.shape[2]
    dn_t = (((1,), (1,)), ((), ()))
    lane = lax.broadcasted_iota(jnp.int32, (tb, LANES), 1)
    qstart = pl.multiple_of(qi * tb, tb)
    qs = q_ref[0, pl.ds(qstart, tb), :] * (DA_HEAD_DIM ** -0.5)
    zero = jnp.zeros_like(qs)
    qcat = jnp.concatenate([jnp.where(lane < DA_HEAD_DIM, qs, zero),
                            jnp.where(lane >= DA_HEAD_DIM, qs, zero)], axis=0)
    n_far = jnp.maximum(qi - 1, 0)
    n_quads = n_far // 4
    n_rest = n_far % 4
    tail_blk = 4 * n_quads

    def scores(blk, nblocks, biased):
        kb = k_ref[0, pl.ds(pl.multiple_of(blk * tb, tb), nblocks * tb), :]
        s_both = lax.dot_general(qcat, kb, dn_t, preferred_element_type=F32)
        out = []
        for c in range(2):
            s = s_both[c * tb:(c + 1) * tb]
            if biased:
                nb = min(nblocks, 2)
                plain = (nblocks - nb) * tb
                tail = s[:, plain:] + bt_ref[0, :, (2 - nb) * tb:]
                s = tail if plain == 0 else jnp.concatenate([s[:, :plain], tail], axis=1)
            s = s * LOG2E
            for i in range(nblocks):
                s_scr[c, blk + i] = s[:, i * tb:(i + 1) * tb]
            out.append(s)
        return out

    def lane_chunks(s):
        return [s[:, j * LANES:(j + 1) * LANES] for j in range(s.shape[1] // LANES)]

    def max_into(ms, ss):
        out = []
        for m, s in zip(ms, ss):
            for ch in lane_chunks(s):
                m = jnp.maximum(m, ch)
            out.append(m)
        return out

    m_init = jnp.full((tb, LANES), MASK_NEG, F32)

    @pl.when(qi == 0)
    def _():
        m_scr[0], m_scr[1] = max_into((m_init, m_init), scores(0, 1, True))

    for rest in range(4):
        @pl.when(jnp.logical_and(qi >= 1, n_rest == rest))
        def _():
            m_scr[0], m_scr[1] = max_into((m_init, m_init), scores(tail_blk, rest + 2, True))
            _attn_finalize(qstart - tb, accp_scr, sw_ref, o_ref, lam, lam_init)

    def quad_max(ki, ms):
        return tuple(max_into(ms, scores(4 * ki, 4, False)))

    ms = lax.fori_loop(0, n_quads, quad_max, (m_scr[0], m_scr[1]))
    def row_max():
        mb = [jnp.broadcast_to(jnp.max(ms[c], axis=-1, keepdims=True), (tb, LANES)) for c in range(2)]
        m_scr[0], m_scr[1] = mb
        return mb

    def accumulate(blk, nblocks, st, mb):
        vb = v1_scr[pl.ds(pl.multiple_of(blk * tb, tb), nblocks * tb), :]
        out = []
        for c in range(2):
            ps = []
            for i in range(nblocks):
                ps += [jnp.exp2(ch - mb[c]) for ch in lane_chunks(s_scr[c, blk + i])]
            p = jnp.concatenate(ps, axis=1).astype(BF16)
            out.append(st[c] + jnp.dot(p, vb, preferred_element_type=F32))
        return tuple(out)

    def store_state(st):
        acc_scr[0], acc_scr[1] = st

    za = jnp.zeros((tb, 2 * LANES), F32)

    @pl.when(qi == 0)
    def _():
        store_state(accumulate(0, 1, (za, za), row_max()))

    for rest in range(4):
        @pl.when(jnp.logical_and(qi >= 1, n_rest == rest))
        def _():
            store_state(accumulate(tail_blk, rest + 2, (za, za), row_max()))

    mb_q = (m_scr[0], m_scr[1])
    a0, a1 = lax.fori_loop(0, n_quads, lambda ki, st: accumulate(4 * ki, 4, st, mb_q),
                           (acc_scr[0], acc_scr[1]))

    accp_scr[0], accp_scr[1] = a0, a1


def _attn_finalize(row_start, accp_scr, sw_ref, o_ref, lam, lam_init):
    tb = accp_scr.shape[1]
    a0, a1 = accp_scr[0], accp_scr[1]
    o = a0[:, :LANES] / a0[:, LANES:] - lam * (a1[:, :LANES] / a1[:, LANES:])
    o = _rms(o, sw_ref[...], 1e-5) * (1.0 - lam_init)
    if not isinstance(row_start, int):
        row_start = pl.multiple_of(row_start, tb)
    o_ref[0, pl.ds(row_start, tb), :] = o.astype(o_ref.dtype)


def _attn_kernel(q_ref, k_ref, v_ref, bt_ref, dl_ref, sw_ref, o_ref, m_scr, acc_scr, s_scr, v1_scr, accp_scr,
                 *, lam_init):
    dl = dl_ref[...]
    lam = (jnp.exp(jnp.sum(dl[0:1, :] * dl[1:2, :], axis=-1, keepdims=True))
           - jnp.exp(jnp.sum(dl[2:3, :] * dl[3:4, :], axis=-1, keepdims=True)) + lam_init)
    v1_scr[:, :LANES] = v_ref[0]
    v1_scr[:, LANES:] = jnp.ones((v_ref.shape[1], LANES), BF16)

    def qblock(qi, carry):
        _attn_qblock(qi, q_ref, k_ref, v1_scr, bt_ref, sw_ref, o_ref, m_scr, acc_scr, s_scr, accp_scr, lam,
                     lam_init)
        return carry

    tb = s_scr.shape[2]
    nq = q_ref.shape[1] // tb
    lax.fori_loop(0, nq, qblock, 0)
    _attn_finalize((nq - 1) * tb, accp_scr, sw_ref, o_ref, lam, lam_init)


def _attn(proj3, btiles, da_lambda, da_subln, lam_init, tb, col_q, col_k, col_v):
    b, t, _ = proj3.shape
    width = DA_HEADS * LANES
    return pl.pallas_call(
        functools.partial(_attn_kernel, lam_init=lam_init),
        grid=(b, DA_HEADS),
        in_specs=[pl.BlockSpec((1, t, LANES), lambda bi, h: (bi, 0, col_q + h)),
                  pl.BlockSpec((1, t, LANES), lambda bi, h: (bi, 0, col_k + h)),
                  pl.BlockSpec((1, t, LANES), lambda bi, h: (bi, 0, col_v + h)),
                  pl.BlockSpec((1, tb, 2 * tb), lambda bi, h: (h, 0, 0)),
                  pl.BlockSpec((4, DA_HEAD_DIM), lambda bi, h: (0, 0)),
                  pl.BlockSpec((1, LANES), lambda bi, h: (0, 0))],
        out_specs=pl.BlockSpec((1, t, LANES), lambda bi, h: (bi, 0, h)),
        out_shape=jax.ShapeDtypeStruct((b, t, width), BF16),
        scratch_shapes=[pltpu.VMEM((2, tb, LANES), F32), pltpu.VMEM((2, tb, 2 * LANES), F32),
                        pltpu.VMEM((2, t // tb, tb, tb), F32), pltpu.VMEM((t, 2 * LANES), BF16),
                        pltpu.VMEM((2, tb, 2 * LANES), F32)],
        compiler_params=_cparams(("parallel", "parallel")),
        name="diff_attn",
    )(proj3, proj3, proj3, btiles, da_lambda, da_subln)


def _s5_kernel(u_ref, bc_ref, pw_ref, d_ref, o_ref, uf_scr, g_scr, hp_scr, y_scr,
               wg_scr, wk_scr, wct_scr):
    t_len = u_ref.shape[1]
    nj = t_len // S5_L
    pw = pw_ref[0]

    @pl.when(pl.program_id(1) == 0)
    def _():
        bd_re, bd_im, ct_re, ct_im = bc_ref[0, 0], bc_ref[0, 1], bc_ref[0, 2], bc_ref[0, 3]
        for s in range(S5_L):
            rows = slice(s * LANES, (s + 1) * LANES)
            qr, qi = pw[s:s + 1, :S5_HALF], pw[s:s + 1, S5_HALF:]
            wg_scr[rows, :S5_HALF] = (qr * bd_re - qi * bd_im).astype(BF16)
            wg_scr[rows, S5_HALF:] = (qr * bd_im + qi * bd_re).astype(BF16)
            er, ei = pw[8 + s:9 + s, :S5_HALF], pw[8 + s:9 + s, S5_HALF:]
            wct_scr[rows, :S5_HALF] = (ct_re * er - ct_im * ei).astype(BF16)
            wct_scr[rows, S5_HALF:] = (-(ct_re * ei + ct_im * er)).astype(BF16)
        dn_t = (((1,), (1,)), ((), ()))
        bcat = jnp.concatenate([bd_re, bd_im], axis=1).astype(BF16)
        crows = [jnp.concatenate([ct_re, -ct_im], axis=1).astype(BF16)]
        crows += [wct_scr[s * LANES:(s + 1) * LANES, :] for s in range(S5_L - 1)]
        kts = [lax.dot_general(bcat, c, dn_t, preferred_element_type=F32).astype(BF16) for c in crows]
        for s in range(S5_L):
            for r in range(S5_L):
                wk_scr[s * LANES:(s + 1) * LANES, r * LANES:(r + 1) * LANES] = (
                    kts[r - s] if r >= s else jnp.zeros((LANES, LANES), BF16))

    uf_scr[...] = u_ref[0].astype(F32)
    ur = jnp.concatenate([uf_scr[pl.ds(s, nj, stride=S5_L), :].astype(BF16) for s in range(S5_L)],
                         axis=1)
    g_scr[...] = jnp.dot(ur, wg_scr[...], preferred_element_type=F32)

    pr, pi = pw[24:32, :S5_HALF], pw[24:32, S5_HALF:]
    row = lax.broadcasted_iota(jnp.int32, (8, S5_HALF), 0)

    def local_scan(base):
        x = g_scr[pl.ds(base, 8), :]
        xr, xi = x[:, :S5_HALF], x[:, S5_HALF:]
        for lvl, s in enumerate((1, 2, 4)):
            ar, ai = pw[16 + lvl:17 + lvl, :S5_HALF], pw[16 + lvl:17 + lvl, S5_HALF:]
            keep = row >= s
            sr = jnp.where(keep, pltpu.roll(xr, s, 0), 0.0)
            si = jnp.where(keep, pltpu.roll(xi, s, 0), 0.0)
            xr, xi = xr + (ar * sr - ai * si), xi + (ar * si + ai * sr)
        return xr, xi

    first = row == 0
    per_iter = 2

    def groups(gi, carry):
        cr, ci = carry
        bases = [pl.multiple_of((gi * per_iter + k) * 8, 8) for k in range(per_iter)]
        scans = [local_scan(base) for base in bases]
        for base, (xr, xi) in zip(bases, scans):
            xr, xi = xr + (pr * cr - pi * ci), xi + (pr * ci + pi * cr)
            hp_scr[pl.ds(base, 8), :] = jnp.concatenate(
                [jnp.where(first, cr, pltpu.roll(xr, 1, 0)), jnp.where(first, ci, pltpu.roll(xi, 1, 0))],
                axis=1)
            cr, ci = xr[7:8, :], xi[7:8, :]
        return cr, ci

    z = jnp.zeros((1, S5_HALF), F32)
    lax.fori_loop(0, nj // (8 * per_iter), groups, (z, z))

    y = (jnp.dot(ur, wk_scr[...], preferred_element_type=F32)
         + lax.dot_general(hp_scr[...].astype(BF16), wct_scr[...], (((1,), (1,)), ((), ())),
                           preferred_element_type=F32))
    for s in range(S5_L):
        y_scr[pl.ds(s, nj, stride=S5_L), :] = y[:, s * LANES:(s + 1) * LANES]
    o_ref[0] = _gelu_tanh(y_scr[...] + d_ref[...] * uf_scr[...]).astype(o_ref.dtype)


def _s5(proj3, bc, pw, dvec, col_u):
    b, t, _ = proj3.shape
    nblk = bc.shape[0]
    nj = t // S5_L
    wide = S5_L * LANES
    assert nj % 16 == 0
    return pl.pallas_call(
        _s5_kernel,
        grid=(nblk, b),
        in_specs=[pl.BlockSpec((1, t, LANES), lambda g, bi: (bi, 0, col_u + g)),
                  pl.BlockSpec((1, 4, LANES, S5_HALF), lambda g, bi: (g, 0, 0, 0)),
                  pl.BlockSpec((1, 32, 2 * S5_HALF), lambda g, bi: (g, 0, 0)),
                  pl.BlockSpec((1, LANES), lambda g, bi: (0, g))],
        out_specs=pl.BlockSpec((1, t, LANES), lambda g, bi: (bi, 0, g)),
        out_shape=jax.ShapeDtypeStruct((b, t, nblk * LANES), BF16),
        scratch_shapes=[pltpu.VMEM((t, LANES), F32), pltpu.VMEM((nj, 2 * S5_HALF), F32),
                        pltpu.VMEM((nj, 2 * S5_HALF), F32), pltpu.VMEM((t, LANES), F32),
                        pltpu.VMEM((wide, 2 * S5_HALF), BF16), pltpu.VMEM((wide, wide), BF16),
                        pltpu.VMEM((wide, 2 * S5_HALF), BF16)],
        compiler_params=_cparams(("parallel", "arbitrary")),
        name="s5",
    )(proj3, bc, pw, dvec)


def _merge_kernel(ya_ref, yb_ref, yc_ref, g0_ref, g1_ref, g2_ref, wg_ref, bg_ref, wb_ref, bgate_ref,
                  o_ref, yc_scr):
    @pl.when(pl.program_id(1) == 0)
    def _():
        yc = yc_ref[...]
        z = jnp.dot(yc, wg_ref[...], preferred_element_type=F32) + bg_ref[...]
        yc_scr[...] = (yc.astype(F32) * _sigmoid(z)).astype(BF16)

    bgate = bgate_ref[...]
    ys = (ya_ref[...], yb_ref[...], yc_scr[...])
    gs = (g0_ref, g1_ref, g2_ref)
    merged = None
    for br in range(N_BRANCH):
        gate = _sigmoid(gs[br][...].astype(F32) + bgate[br:br + 1, :])
        term = gate * jnp.dot(ys[br], wb_ref[br], preferred_element_type=F32)
        merged = term if merged is None else merged + term
    o_ref[...] = merged.astype(o_ref.dtype)


def _merge(ya, yb, yc, proj, w_glu_all, b_glu, w_branch_all, b_gate, col_g, layer, tm, tn):
    n, wdt = ya.shape
    d = w_branch_all.shape[3]
    gcol = [(col_g + br * d) // tn for br in range(N_BRANCH)]
    yspec = pl.BlockSpec((tm, wdt), lambda i, j: (i, 0))

    def gspec(br):
        return pl.BlockSpec((tm, tn), lambda i, j: (i, gcol[br] + j))

    return pl.pallas_call(
        _merge_kernel,
        grid=(n // tm, d // tn),
        in_specs=[yspec, yspec, yspec, gspec(0), gspec(1), gspec(2),
                  pl.BlockSpec((None, wdt, wdt), lambda i, j: (layer, 0, 0)),
                  pl.BlockSpec((1, wdt), lambda i, j: (0, 0)),
                  pl.BlockSpec((None, N_BRANCH, wdt, tn), lambda i, j: (layer, 0, 0, j)),
                  pl.BlockSpec((N_BRANCH, tn), lambda i, j: (0, j))],
        out_specs=pl.BlockSpec((tm, tn), lambda i, j: (i, j)),
        out_shape=jax.ShapeDtypeStruct((n, d), BF16),
        scratch_shapes=[pltpu.VMEM((tm, wdt), BF16)],
        compiler_params=_cparams(("parallel", "arbitrary")),
        name="merge",
    )(ya, yb, yc, proj, proj, proj, w_glu_all, b_glu, w_branch_all, b_gate)


def _outproj_kernel(m_ref, w_ref, xs_ref, nw_ref, o_ref):
    mix = jnp.dot(m_ref[...], w_ref[...], preferred_element_type=F32)
    o_ref[...] = xs_ref[...] + _rms(mix, nw_ref[...], 1e-6)


def _outproj(merged, w_out_all, layer, xs, nw, tm):
    n, d = xs.shape
    return pl.pallas_call(
        _outproj_kernel,
        grid=(n // tm,),
        in_specs=[pl.BlockSpec((tm, d), lambda i: (i, 0)),
                  pl.BlockSpec((None, d, d), lambda i: (layer, 0, 0)),
                  pl.BlockSpec((tm, d), lambda i: (i, 0)),
                  pl.BlockSpec((1, d), lambda i: (0, 0))],
        out_specs=pl.BlockSpec((tm, d), lambda i: (i, 0)),
        out_shape=jax.ShapeDtypeStruct((n, d), F32),
        compiler_params=_cparams(("parallel",)),
        name="outproj_residual",
    )(merged, w_out_all, xs, nw)


def _ffn_kernel(xs_ref, nw_in_ref, wg_ref, wu_ref, wo_ref, nw_out_ref, o_ref, h_scr, acc_scr):
    f = pl.program_id(1)

    @pl.when(f == 0)
    def _():
        h_scr[...] = _rms(xs_ref[...], nw_in_ref[...], 1e-6).astype(BF16)
        acc_scr[...] = jnp.zeros(acc_scr.shape, F32)

    h = h_scr[...]
    gate = jnp.dot(h, wg_ref[...], preferred_element_type=F32)
    up = jnp.dot(h, wu_ref[...], preferred_element_type=F32)
    act = (gate * _sigmoid(gate) * up).astype(BF16)
    acc_scr[...] += jnp.dot(act, wo_ref[...], preferred_element_type=F32)

    @pl.when(f == pl.num_programs(1) - 1)
    def _():
        o_ref[...] = xs_ref[...] + _rms(acc_scr[...], nw_out_ref[...], 1e-6)


def _ffn(xs, nw_in, w_ffn_in_all, w_ffn_out_all, nw_out, layer, tm, tf, real_rows=None):
    n, d = xs.shape
    d_ff = w_ffn_out_all.shape[1]
    nf = d_ff // tf
    if real_rows is None:
        n_out = n
        xs_spec = pl.BlockSpec((tm, d), lambda i, f: (i, 0))
    else:
        t_pad, first, count = real_rows
        per_seq = count // tm
        n_out = (n // t_pad) * count
        xs_spec = pl.BlockSpec((pl.Element(tm), pl.Element(d)),
                               lambda i, f: (pl.multiple_of(
                                   (i // per_seq) * t_pad + first + (i % per_seq) * tm, 8), 0))
        assert t_pad % 8 == 0 and first % 8 == 0 and tm % 8 == 0
    return pl.pallas_call(
        _ffn_kernel,
        grid=(n_out // tm, nf),
        in_specs=[xs_spec,
                  pl.BlockSpec((1, d), lambda i, f: (0, 0)),
                  pl.BlockSpec((None, d, tf), lambda i, f: (layer, 0, f)),
                  pl.BlockSpec((None, d, tf), lambda i, f: (layer, 0, nf + f)),
                  pl.BlockSpec((None, tf, d), lambda i, f: (layer, f, 0)),
                  pl.BlockSpec((1, d), lambda i, f: (0, 0))],
        out_specs=pl.BlockSpec((tm, d), lambda i, f: (i, 0)),
        out_shape=jax.ShapeDtypeStruct((n_out, d), F32),
        scratch_shapes=[pltpu.VMEM((tm, d), BF16), pltpu.VMEM((tm, d), F32)],
        compiler_params=_cparams(("parallel", "arbitrary")),
        name="swiglu_ffn",
    )(xs, nw_in, w_ffn_in_all, w_ffn_in_all, w_ffn_out_all, nw_out)


def _s5_tables(lam_re, lam_im, b_re, b_im, c_re, c_im, log_step):
    groups = lam_re.shape[0]
    nblk = groups // S5_GB
    lr, li = lam_re.astype(F32), lam_im.astype(F32)
    step = jnp.exp(log_step.astype(F32))[:, None]
    mag = jnp.exp(lr * step)
    ab_re, ab_im = mag * jnp.cos(li * step), mag * jnp.sin(li * step)
    den = lr * lr + li * li
    coef_re = ((ab_re - 1.0) * lr + ab_im * li) / den
    coef_im = (ab_im * lr - (ab_re - 1.0) * li) / den
    br, bi = b_re.astype(F32), b_im.astype(F32)
    bb_re = coef_re[..., None] * br - coef_im[..., None] * bi
    bb_im = coef_re[..., None] * bi + coef_im[..., None] * br
    cr, ci = c_re.astype(F32), c_im.astype(F32)
    eye = jnp.eye(S5_GB, dtype=F32)
    taus = jnp.arange(S5_L)
    lrs = (lr * step).reshape(nblk, 1, S5_HALF)
    lis = (li * step).reshape(nblk, 1, S5_HALF)

    def lam_pow(n):
        nn = n.astype(F32)[None, :, None]
        m = jnp.exp(nn * lrs)
        return m * jnp.cos(nn * lis), m * jnp.sin(nn * lis)

    def blockdiag(t):
        t = t.reshape(nblk, S5_GB, S5_GROUP, S5_STATE)
        return jnp.einsum("ngcp,gh->ngchp", t, eye).reshape(nblk, LANES, S5_HALF)

    bc = jnp.stack([blockdiag(bb_re.transpose(0, 2, 1)), blockdiag(bb_im.transpose(0, 2, 1)),
                    blockdiag(cr), blockdiag(ci)], axis=1)

    n_list = jnp.concatenate([S5_L - 1 - taus, 1 + taus, S5_L * jnp.array([1, 2, 4, 0, 0, 0, 0, 0]),
                              S5_L * (1 + jnp.arange(8))])
    wr, wi = lam_pow(n_list)
    pw = jnp.concatenate([wr, wi], axis=2)
    return bc, pw


def kernel(x, meta, rel_bias, norm_w, w_in, conv_w, conv_b, lru_w_a, lru_b_a, lru_w_x, lru_b_x, lru_lambda, da_lambda, da_subln, s5_lam_re, s5_lam_im, s5_b_re, s5_b_im, s5_c_re, s5_c_im, s5_d, s5_log_step, s5_w_glu, s5_b_glu, b_gate, w_branch, w_out, w_ffn_in, w_ffn_out):
    bsz, seq, d_model = x.shape
    depth = w_in.shape[0]
    lru_w = conv_w.shape[2]
    s5_w = s5_d.shape[1]
    qk_w = DA_HEADS * 2 * DA_HEAD_DIM
    assert lru_w == LRU_BLOCKS * LANES and s5_w % LANES == 0 and d_model % LANES == 0
    col_gate, col_x = 0, lru_w // LANES
    col_q = 2 * lru_w // LANES
    col_k = col_q + qk_w // LANES
    col_v = col_k + qk_w // LANES
    col_u = col_v + DA_HEADS
    col_g = (col_u + s5_w // LANES) * LANES

    t_real = N_META + seq
    tb = 384 if t_real >= 1024 else 128
    t_pad = -(-t_real // tb) * tb
    assert t_pad % SCAN_CH == 0
    n_tok = bsz * t_pad
    tm_in = _pick_tile(n_tok, (1408, 1056, 768, 512, 384, 256, 128))
    tm_merge = _pick_tile(n_tok, (1056, 768, 512, 384, 256, 128))
    tm_ffn = _pick_tile(n_tok, (768, 512, 384, 256, 128))
    tm_out = _pick_tile(n_tok, (512, 384, 256, 128))
    tm_last = _pick_tile(seq, (512, 256, 128))

    xs = jnp.concatenate([jnp.broadcast_to(meta.astype(F32)[None], (bsz, N_META, d_model)), x,
                          jnp.zeros((bsz, t_pad - t_real, d_model), F32)], axis=1)
    xs = xs.reshape(n_tok, d_model)
    btiles = _bias_tiles(rel_bias.astype(F32), tb)

    w_in_b, w_glu_b, w_branch_b = w_in.astype(BF16), s5_w_glu.astype(BF16), w_branch.astype(BF16)
    w_out_b, w_ffn_in_b, w_ffn_out_b = w_out.astype(BF16), w_ffn_in.astype(BF16), w_ffn_out.astype(BF16)

    for l in range(depth):
        lam_init = 0.8 - 0.6 * math.exp(-0.3 * l)
        proj = _inproj(xs, norm_w[l, 0][None], w_in_b, l, tm_in, 1024)
        proj3 = proj.reshape(bsz, t_pad, proj.shape[1])

        lru_p = jnp.concatenate([conv_w[l], conv_b[l][None], lru_b_a[l][None], lru_b_x[l][None],
                                 jax.nn.log_sigmoid(lru_lambda[l].astype(F32))[None]], axis=0)
        w_ax = jnp.concatenate([lru_w_a[l], lru_w_x[l]], axis=2).astype(BF16)
        y_a = _lru(proj3, lru_p, w_ax, col_gate, col_x)

        y_b = _attn(proj3, btiles, da_lambda[l], da_subln[l][None], lam_init, tb, col_q, col_k, col_v)

        bc, pw = _s5_tables(s5_lam_re[l], s5_lam_im[l], s5_b_re[l], s5_b_im[l],
                            s5_c_re[l], s5_c_im[l], s5_log_step[l])
        y_c = _s5(proj3, bc, pw, s5_d[l][None], col_u)

        merged = _merge(y_a.reshape(n_tok, lru_w), y_b.reshape(n_tok, -1), y_c.reshape(n_tok, s5_w),
                        proj, w_glu_b, s5_b_glu[l][None], w_branch_b, b_gate[l], col_g, l, tm_merge, 512)
        xs = _outproj(merged, w_out_b, l, xs, norm_w[l, 1][None], tm_out)
        if l + 1 < depth:
            xs = _ffn(xs, norm_w[l, 2][None], w_ffn_in_b, w_ffn_out_b, norm_w[l, 3][None], l, tm_ffn, 512)
        else:
            out = _ffn(xs, norm_w[l, 2][None], w_ffn_in_b, w_ffn_out_b, norm_w[l, 3][None], l, tm_last,
                       512, real_rows=(t_pad, N_META, seq))

    return out.reshape(bsz, seq, d_model)
```

```python
import functools
import math

import jax
import jax.numpy as jnp
from jax import lax
from jax.experimental import pallas as pl
from jax.experimental.pallas import tpu as pltpu

F32 = jnp.float32
BF16 = jnp.bfloat16

N_META = 16
CONV_W = 4
LRU_C = 8.0
LRU_BLOCKS = 8
DA_HEADS = 8
DA_HEAD_DIM = 64
S5_GROUP = 16
S5_STATE = 64
REL_BUCKETS = 32
REL_MAX_DIST = 128
N_BRANCH = 3

LANES = 128
S5_GB = LANES // S5_GROUP
S5_HALF = S5_GB * S5_STATE
S5_L = 8
SCAN_CH = 128
MASK_NEG = -1e30
LOG2E = math.log2(math.e)
VMEM_LIMIT = 56 * 1024 * 1024


def _cparams(sem):
    return pltpu.CompilerParams(dimension_semantics=sem, vmem_limit_bytes=VMEM_LIMIT)


def _pick_tile(n, candidates):
    for c in candidates:
        if n % c == 0:
            return c
    raise ValueError(f"no tile in {candidates} divides {n}")


def _gelu_tanh(x):
    return 0.5 * x * (1.0 + jnp.tanh(math.sqrt(2.0 / math.pi) * (x + 0.044715 * (x * x * x))))


def _sigmoid(x):
    return 1.0 / (1.0 + jnp.exp(-x))


def _rms(x, w, eps):
    return (x * lax.rsqrt(jnp.mean(x * x, axis=-1, keepdims=True) + eps)) * w


def _inproj_kernel(x_ref, nw_ref, w_ref, o_ref, h_scr):
    @pl.when(pl.program_id(1) == 0)
    def _():
        h_scr[...] = _rms(x_ref[...], nw_ref[...], 1e-6).astype(BF16)

    o_ref[...] = jnp.dot(h_scr[...], w_ref[...], preferred_element_type=F32).astype(o_ref.dtype)


def _inproj(xs, nw, w_all, layer, tm, tn):
    n, d = xs.shape
    n_out = w_all.shape[2]
    return pl.pallas_call(
        _inproj_kernel,
        grid=(n // tm, n_out // tn),
        in_specs=[pl.BlockSpec((tm, d), lambda i, j: (i, 0)),
                  pl.BlockSpec((1, d), lambda i, j: (0, 0)),
                  pl.BlockSpec((None, d, tn), lambda i, j: (layer, 0, j))],
        out_specs=pl.BlockSpec((tm, tn), lambda i, j: (i, j)),
        out_shape=jax.ShapeDtypeStruct((n, n_out), BF16),
        scratch_shapes=[pltpu.VMEM((tm, d), BF16)],
        compiler_params=_cparams(("parallel", "arbitrary")),
        name="inproj",
    )(xs, nw, w_all)


def _lru_kernel(g_ref, x_ref, p_ref, w_ref, o_ref, xs_scr):
    t_len = x_ref.shape[1]
    halo = 8
    xs_scr[0:halo, :] = jnp.zeros((halo, LANES), F32)
    xs_scr[halo:, :] = x_ref[0].astype(F32)
    p = p_ref[...]
    cw = [p[j:j + 1, :] for j in range(CONV_W)]
    cb, ba, bx = p[4:5, :], p[5:6, :], p[6:7, :]
    c_logsig = LRU_C * p[7:8, :]
    w = w_ref[0]
    row = lax.broadcasted_iota(jnp.int32, (SCAN_CH, LANES), 0)

    def local_scan(base):
        xc = cb
        for j in range(CONV_W):
            xc = xc + cw[j] * xs_scr[pl.ds(base + halo - (CONV_W - 1) + j, SCAN_CH), :]
        ri = jnp.dot(xc.astype(BF16), w, preferred_element_type=F32)
        r = _sigmoid(ri[:, :LANES] + ba)
        i = _sigmoid(ri[:, LANES:] + bx)
        a = jnp.exp(r * c_logsig)
        b = jnp.sqrt(1.0 - a * a) * (i * xc)
        s = 1
        while s < SCAN_CH:
            keep = row >= s
            a_sh = jnp.where(keep, pltpu.roll(a, s, 0), 1.0)
            b_sh = jnp.where(keep, pltpu.roll(b, s, 0), 0.0)
            b = a * b_sh + b
            a = a * a_sh
            s *= 2
        return a, b

    n_chunks = t_len // SCAN_CH
    sub = 3 if n_chunks % 3 == 0 else 1

    def chunk(c, h0):
        bases = [pl.multiple_of((c * sub + j) * SCAN_CH, SCAN_CH) for j in range(sub)]
        scans = [local_scan(base) for base in bases]
        for base, (a, b) in zip(bases, scans):
            h = a * h0 + b
            g = g_ref[0, pl.ds(base, SCAN_CH), :].astype(F32)
            o_ref[0, pl.ds(base, SCAN_CH), :] = (h * _gelu_tanh(g)).astype(o_ref.dtype)
            h0 = h[SCAN_CH - 1:SCAN_CH, :]
        return h0

    lax.fori_loop(0, n_chunks // sub, chunk, jnp.zeros((1, LANES), F32))


def _lru(proj3, lru_p, w_ax, col_gate, col_x):
    b, t, _ = proj3.shape
    width = LRU_BLOCKS * LANES
    return pl.pallas_call(
        _lru_kernel,
        grid=(b, LRU_BLOCKS),
        in_specs=[pl.BlockSpec((1, t, LANES), lambda bi, h: (bi, 0, col_gate + h)),
                  pl.BlockSpec((1, t, LANES), lambda bi, h: (bi, 0, col_x + h)),
                  pl.BlockSpec((8, LANES), lambda bi, h: (0, h)),
                  pl.BlockSpec((1, LANES, 2 * LANES), lambda bi, h: (h, 0, 0))],
        out_specs=pl.BlockSpec((1, t, LANES), lambda bi, h: (bi, 0, h)),
        out_shape=jax.ShapeDtypeStruct((b, t, width), BF16),
        scratch_shapes=[pltpu.VMEM((t + 8, LANES), F32)],
        compiler_params=_cparams(("parallel", "parallel")),
        name="rglru",
    )(proj3, proj3, lru_p, w_ax)


def _bias_tile_kernel(rb_ref, o_ref):
    h = pl.program_id(0)
    tb = o_ref.shape[1]
    nb = tb // LANES
    i = lax.broadcasted_iota(jnp.int32, (LANES, LANES), 0)
    j = lax.broadcasted_iota(jnp.int32, (LANES, LANES), 1)
    max_exact = REL_BUCKETS // 2

    def pattern(delta):
        d = i - j + delta * LANES
        n = jnp.maximum(d, 0)
        nf = jnp.maximum(n, 1).astype(F32)
        large = max_exact + (jnp.log(nf / max_exact) / math.log(REL_MAX_DIST / max_exact)
                             * (REL_BUCKETS - max_exact)).astype(jnp.int32)
        large = jnp.minimum(large, REL_BUCKETS - 1)
        bucket = jnp.where(n < max_exact, n, large)
        val = jnp.zeros((LANES, LANES), F32)
        for bkt in range(REL_BUCKETS):
            val = jnp.where(bucket == bkt, rb_ref[bkt, h], val)
        val = val - rb_ref[REL_BUCKETS - 1, h]
        return jnp.where(d >= 0, val, MASK_NEG)

    pieces = {0: pattern(0), 1: pattern(1)}
    for ri in range(nb):
        for cj in range(2 * nb):
            delta = ri - cj + nb
            if delta in pieces:
                piece = pieces[delta]
            else:
                piece = jnp.full((LANES, LANES), 0.0 if delta >= 2 else MASK_NEG, F32)
            o_ref[0, ri * LANES:(ri + 1) * LANES, cj * LANES:(cj + 1) * LANES] = piece


def _bias_tiles(rel_bias, tb):
    return pl.pallas_call(
        _bias_tile_kernel,
        grid=(DA_HEADS,),
        in_specs=[pl.BlockSpec(memory_space=pltpu.SMEM)],
        out_specs=pl.BlockSpec((1, tb, 2 * tb), lambda h: (h, 0, 0)),
        out_shape=jax.ShapeDtypeStruct((DA_HEADS, tb, 2 * tb), F32),
        compiler_params=_cparams(("parallel",)),
        name="t5_bias_tiles",
    )(rel_bias)


def _attn_qblock(qi, q_ref, k_ref, v1_scr, bt_ref, sw_ref, o_ref, m_scr, acc_scr, s_scr, accp_scr, lam,
                 lam_init):
    tb = s_scr.shape[2]
    dn_t = (((1,), (1,)), ((), ()))
    lane = lax.broadcasted_iota(jnp.int32, (tb, LANES), 1)
    qstart = pl.multiple_of(qi * tb, tb)
    qs = q_ref[0, pl.ds(qstart, tb), :] * (DA_HEAD_DIM ** -0.5)
    zero = jnp.zeros_like(qs)
    qcat = jnp.concatenate([jnp.where(lane < DA_HEAD_DIM, qs, zero),
                            jnp.where(lane >= DA_HEAD_DIM, qs, zero)], axis=0)
    n_far = jnp.maximum(qi - 1, 0)
    n_quads = n_far // 4
    n_rest = n_far % 4
    tail_blk = 4 * n_quads

    def scores(blk, nblocks, biased):
        kb = k_ref[0, pl.ds(pl.multiple_of(blk * tb, tb), nblocks * tb), :]
        s_both = lax.dot_general(qcat, kb, dn_t, preferred_element_type=F32)
        out = []
        for c in range(2):
            s = s_both[c * tb:(c + 1) * tb]
            if biased:
                nb = min(nblocks, 2)
                plain = (nblocks - nb) * tb
                tail = s[:, plain:] + bt_ref[0, :, (2 - nb) * tb:]
                s = tail if plain == 0 else jnp.concatenate([s[:, :plain], tail], axis=1)
            s = s * LOG2E
            for i in range(nblocks):
                s_scr[c, blk + i] = s[:, i * tb:(i + 1) * tb]
            out.append(s)
        return out

    def lane_chunks(s):
        return [s[:, j * LANES:(j + 1) * LANES] for j in range(s.shape[1] // LANES)]

    def max_into(ms, ss):
        out = []
        for m, s in zip(ms, ss):
            for ch in lane_chunks(s):
                m = jnp.maximum(m, ch)
            out.append(m)
        return out

    m_init = jnp.full((tb, LANES), MASK_NEG, F32)

    @pl.when(qi == 0)
    def _():
        m_scr[0], m_scr[1] = max_into((m_init, m_init), scores(0, 1, True))

    for rest in range(4):
        @pl.when(jnp.logical_and(qi >= 1, n_rest == rest))
        def _():
            m_scr[0], m_scr[1] = max_into((m_init, m_init), scores(tail_blk, rest + 2, True))
            _attn_finalize(qstart - tb, accp_scr, sw_ref, o_ref, lam, lam_init)

    def quad_max(ki, ms):
        return tuple(max_into(ms, scores(4 * ki, 4, False)))

    ms = lax.fori_loop(0, n_quads, quad_max, (m_scr[0], m_scr[1]))
    def row_max():
        mb = [jnp.broadcast_to(jnp.max(ms[c], axis=-1, keepdims=True), (tb, LANES)) for c in range(2)]
        m_scr[0], m_scr[1] = mb
        return mb

    def accumulate(blk, nblocks, st, mb):
        vb = v1_scr[pl.ds(pl.multiple_of(blk * tb, tb), nblocks * tb), :]
        out = []
        for c in range(2):
            ps = []
            for i in range(nblocks):
                ps += [jnp.exp2(ch - mb[c]) for ch in lane_chunks(s_scr[c, blk + i])]
            p = jnp.concatenate(ps, axis=1).astype(BF16)
            out.append(st[c] + jnp.dot(p, vb, preferred_element_type=F32))
        return tuple(out)

    def store_state(st):
        acc_scr[0], acc_scr[1] = st

    za = jnp.zeros((tb, 2 * LANES), F32)

    @pl.when(qi == 0)
    def _():
        store_state(accumulate(0, 1, (za, za), row_max()))

    for rest in range(4):
        @pl.when(jnp.logical_and(qi >= 1, n_rest == rest))
        def _():
            store_state(accumulate(tail_blk, rest + 2, (za, za), row_max()))

    mb_q = (m_scr[0], m_scr[1])
    a0, a1 = lax.fori_loop(0, n_quads, lambda ki, st: accumulate(4 * ki, 4, st, mb_q),
                           (acc_scr[0], acc_scr[1]))

    accp_scr[0], accp_scr[1] = a0, a1


def _attn_finalize(row_start, accp_scr, sw_ref, o_ref, lam, lam_init):
    tb = accp_scr.shape[1]
    a0, a1 = accp_scr[0], accp_scr[1]
    o = a0[:, :LANES] / a0[:, LANES:] - lam * (a1[:, :LANES] / a1[:, LANES:])
    o = _rms(o, sw_ref[...], 1e-5) * (1.0 - lam_init)
    if not isinstance(row_start, int):
        row_start = pl.multiple_of(row_start, tb)
    o_ref[0, pl.ds(row_start, tb), :] = o.astype(o_ref.dtype)


def _attn_kernel(q_ref, k_ref, v_ref, bt_ref, dl_ref, sw_ref, o_ref, m_scr, acc_scr, s_scr, v1_scr, accp_scr,
                 *, lam_init):
    dl = dl_ref[...]
    lam = (jnp.exp(jnp.sum(dl[0:1, :] * dl[1:2, :], axis=-1, keepdims=True))
           - jnp.exp(jnp.sum(dl[2:3, :] * dl[3:4, :], axis=-1, keepdims=True)) + lam_init)
    v1_scr[:, :LANES] = v_ref[0]
    v1_scr[:, LANES:] = jnp.ones((v_ref.shape[1], LANES), BF16)

    def qblock(qi, carry):
        _attn_qblock(qi, q_ref, k_ref, v1_scr, bt_ref, sw_ref, o_ref, m_scr, acc_scr, s_scr, accp_scr, lam,
                     lam_init)
        return carry

    tb = s_scr.shape[2]
    nq = q_ref.shape[1] // tb
    lax.fori_loop(0, nq, qblock, 0)
    _attn_finalize((nq - 1) * tb, accp_scr, sw_ref, o_ref, lam, lam_init)


def _attn(proj3, btiles, da_lambda, da_subln, lam_init, tb, col_q, col_k, col_v):
    b, t, _ = proj3.shape
    width = DA_HEADS * LANES
    return pl.pallas_call(
        functools.partial(_attn_kernel, lam_init=lam_init),
        grid=(b, DA_HEADS),
        in_specs=[pl.BlockSpec((1, t, LANES), lambda bi, h: (bi, 0, col_q + h)),
                  pl.BlockSpec((1, t, LANES), lambda bi, h: (bi, 0, col_k + h)),
                  pl.BlockSpec((1, t, LANES), lambda bi, h: (bi, 0, col_v + h)),
                  pl.BlockSpec((1, tb, 2 * tb), lambda bi, h: (h, 0, 0)),
                  pl.BlockSpec((4, DA_HEAD_DIM), lambda bi, h: (0, 0)),
                  pl.BlockSpec((1, LANES), lambda bi, h: (0, 0))],
        out_specs=pl.BlockSpec((1, t, LANES), lambda bi, h: (bi, 0, h)),
        out_shape=jax.ShapeDtypeStruct((b, t, width), BF16),
        scratch_shapes=[pltpu.VMEM((2, tb, LANES), F32), pltpu.VMEM((2, tb, 2 * LANES), F32),
                        pltpu.VMEM((2, t // tb, tb, tb), F32), pltpu.VMEM((t, 2 * LANES), BF16),
                        pltpu.VMEM((2, tb, 2 * LANES), F32)],
        compiler_params=_cparams(("parallel", "parallel")),
        name="diff_attn",
    )(proj3, proj3, proj3, btiles, da_lambda, da_subln)


def _s5_kernel(u_ref, bc_ref, pw_ref, d_ref, o_ref, uf_scr, g_scr, hp_scr, y_scr,
               wg_scr, wk_scr, wct_scr):
    t_len = u_ref.shape[1]
    nj = t_len // S5_L
    pw = pw_ref[0]

    @pl.when(pl.program_id(1) == 0)
    def _():
        bd_re, bd_im, ct_re, ct_im = bc_ref[0, 0], bc_ref[0, 1], bc_ref[0, 2], bc_ref[0, 3]
        for s in range(S5_L):
            rows = slice(s * LANES, (s + 1) * LANES)
            qr, qi = pw[s:s + 1, :S5_HALF], pw[s:s + 1, S5_HALF:]
            wg_scr[rows, :S5_HALF] = (qr * bd_re - qi * bd_im).astype(BF16)
            wg_scr[rows, S5_HALF:] = (qr * bd_im + qi * bd_re).astype(BF16)
            er, ei = pw[8 + s:9 + s, :S5_HALF], pw[8 + s:9 + s, S5_HALF:]
            wct_scr[rows, :S5_HALF] = (ct_re * er - ct_im * ei).astype(BF16)
            wct_scr[rows, S5_HALF:] = (-(ct_re * ei + ct_im * er)).astype(BF16)
        dn_t = (((1,), (1,)), ((), ()))
        bcat = jnp.concatenate([bd_re, bd_im], axis=1).astype(BF16)
        crows = [jnp.concatenate([ct_re, -ct_im], axis=1).astype(BF16)]
        crows += [wct_scr[s * LANES:(s + 1) * LANES, :] for s in range(S5_L - 1)]
        kts = [lax.dot_general(bcat, c, dn_t, preferred_element_type=F32).astype(BF16) for c in crows]
        for s in range(S5_L):
            for r in range(S5_L):
                wk_scr[s * LANES:(s + 1) * LANES, r * LANES:(r + 1) * LANES] = (
                    kts[r - s] if r >= s else jnp.zeros((LANES, LANES), BF16))

    uf_scr[...] = u_ref[0].astype(F32)
    ur = jnp.concatenate([uf_scr[pl.ds(s, nj, stride=S5_L), :].astype(BF16) for s in range(S5_L)],
                         axis=1)
    g_scr[...] = jnp.dot(ur, wg_scr[...], preferred_element_type=F32)

    pr, pi = pw[24:32, :S5_HALF], pw[24:32, S5_HALF:]
    row = lax.broadcasted_iota(jnp.int32, (8, S5_HALF), 0)

    def local_scan(base):
        x = g_scr[pl.ds(base, 8), :]
        xr, xi = x[:, :S5_HALF], x[:, S5_HALF:]
        for lvl, s in enumerate((1, 2, 4)):
            ar, ai = pw[16 + lvl:17 + lvl, :S5_HALF], pw[16 + lvl:17 + lvl, S5_HALF:]
            keep = row >= s
            sr = jnp.where(keep, pltpu.roll(xr, s, 0), 0.0)
            si = jnp.where(keep, pltpu.roll(xi, s, 0), 0.0)
            xr, xi = xr + (ar * sr - ai * si), xi + (ar * si + ai * sr)
        return xr, xi

    first = row == 0
    per_iter = 2

    def groups(gi, carry):
        cr, ci = carry
        bases = [pl.multiple_of((gi * per_iter + k) * 8, 8) for k in range(per_iter)]
        scans = [local_scan(base) for base in bases]
        for base, (xr, xi) in zip(bases, scans):
            xr, xi = xr + (pr * cr - pi * ci), xi + (pr * ci + pi * cr)
            hp_scr[pl.ds(base, 8), :] = jnp.concatenate(
                [jnp.where(first, cr, pltpu.roll(xr, 1, 0)), jnp.where(first, ci, pltpu.roll(xi, 1, 0))],
                axis=1)
            cr, ci = xr[7:8, :], xi[7:8, :]
        return cr, ci

    z = jnp.zeros((1, S5_HALF), F32)
    lax.fori_loop(0, nj // (8 * per_iter), groups, (z, z))

    y = (jnp.dot(ur, wk_scr[...], preferred_element_type=F32)
         + lax.dot_general(hp_scr[...].astype(BF16), wct_scr[...], (((1,), (1,)), ((), ())),
                           preferred_element_type=F32))
    for s in range(S5_L):
        y_scr[pl.ds(s, nj, stride=S5_L), :] = y[:, s * LANES:(s + 1) * LANES]
    o_ref[0] = _gelu_tanh(y_scr[...] + d_ref[...] * uf_scr[...]).astype(o_ref.dtype)


def _s5(proj3, bc, pw, dvec, col_u):
    b, t, _ = proj3.shape
    nblk = bc.shape[0]
    nj = t // S5_L
    wide = S5_L * LANES
    assert nj % 16 == 0
    return pl.pallas_call(
        _s5_kernel,
        grid=(nblk, b),
        in_specs=[pl.BlockSpec((1, t, LANES), lambda g, bi: (bi, 0, col_u + g)),
                  pl.BlockSpec((1, 4, LANES, S5_HALF), lambda g, bi: (g, 0, 0, 0)),
                  pl.BlockSpec((1, 32, 2 * S5_HALF), lambda g, bi: (g, 0, 0)),
                  pl.BlockSpec((1, LANES), lambda g, bi: (0, g))],
        out_specs=pl.BlockSpec((1, t, LANES), lambda g, bi: (bi, 0, g)),
        out_shape=jax.ShapeDtypeStruct((b, t, nblk * LANES), BF16),
        scratch_shapes=[pltpu.VMEM((t, LANES), F32), pltpu.VMEM((nj, 2 * S5_HALF), F32),
                        pltpu.VMEM((nj, 2 * S5_HALF), F32), pltpu.VMEM((t, LANES), F32),
                        pltpu.VMEM((wide, 2 * S5_HALF), BF16), pltpu.VMEM((wide, wide), BF16),
                        pltpu.VMEM((wide, 2 * S5_HALF), BF16)],
        compiler_params=_cparams(("parallel", "arbitrary")),
        name="s5",
    )(proj3, bc, pw, dvec)


def _merge_kernel(ya_ref, yb_ref, yc_ref, g0_ref, g1_ref, g2_ref, wg_ref, bg_ref, wb_ref, bgate_ref,
                  o_ref, yc_scr):
    @pl.when(pl.program_id(1) == 0)
    def _():
        yc = yc_ref[...]
        z = jnp.dot(yc, wg_ref[...], preferred_element_type=F32) + bg_ref[...]
        yc_scr[...] = (yc.astype(F32) * _sigmoid(z)).astype(BF16)

    bgate = bgate_ref[...]
    ys = (ya_ref[...], yb_ref[...], yc_scr[...])
    gs = (g0_ref, g1_ref, g2_ref)
    merged = None
    for br in range(N_BRANCH):
        gate = _sigmoid(gs[br][...].astype(F32) + bgate[br:br + 1, :])
        term = gate * jnp.dot(ys[br], wb_ref[br], preferred_element_type=F32)
        merged = term if merged is None else merged + term
    o_ref[...] = merged.astype(o_ref.dtype)


def _merge(ya, yb, yc, proj, w_glu_all, b_glu, w_branch_all, b_gate, col_g, layer, tm, tn):
    n, wdt = ya.shape
    d = w_branch_all.shape[3]
    gcol = [(col_g + br * d) // tn for br in range(N_BRANCH)]
    yspec = pl.BlockSpec((tm, wdt), lambda i, j: (i, 0))

    def gspec(br):
        return pl.BlockSpec((tm, tn), lambda i, j: (i, gcol[br] + j))

    return pl.pallas_call(
        _merge_kernel,
        grid=(n // tm, d // tn),
        in_specs=[yspec, yspec, yspec, gspec(0), gspec(1), gspec(2),
                  pl.BlockSpec((None, wdt, wdt), lambda i, j: (layer, 0, 0)),
                  pl.BlockSpec((1, wdt), lambda i, j: (0, 0)),
                  pl.BlockSpec((None, N_BRANCH, wdt, tn), lambda i, j: (layer, 0, 0, j)),
                  pl.BlockSpec((N_BRANCH, tn), lambda i, j: (0, j))],
        out_specs=pl.BlockSpec((tm, tn), lambda i, j: (i, j)),
        out_shape=jax.ShapeDtypeStruct((n, d), BF16),
        scratch_shapes=[pltpu.VMEM((tm, wdt), BF16)],
        compiler_params=_cparams(("parallel", "arbitrary")),
        name="merge",
    )(ya, yb, yc, proj, proj, proj, w_glu_all, b_glu, w_branch_all, b_gate)


def _outproj_kernel(m_ref, w_ref, xs_ref, nw_ref, o_ref):
    mix = jnp.dot(m_ref[...], w_ref[...], preferred_element_type=F32)
    o_ref[...] = xs_ref[...] + _rms(mix, nw_ref[...], 1e-6)


def _outproj(merged, w_out_all, layer, xs, nw, tm):
    n, d = xs.shape
    return pl.pallas_call(
        _outproj_kernel,
        grid=(n // tm,),
        in_specs=[pl.BlockSpec((tm, d), lambda i: (i, 0)),
                  pl.BlockSpec((None, d, d), lambda i: (layer, 0, 0)),
                  pl.BlockSpec((tm, d), lambda i: (i, 0)),
                  pl.BlockSpec((1, d), lambda i: (0, 0))],
        out_specs=pl.BlockSpec((tm, d), lambda i: (i, 0)),
        out_shape=jax.ShapeDtypeStruct((n, d), F32),
        compiler_params=_cparams(("parallel",)),
        name="outproj_residual",
    )(merged, w_out_all, xs, nw)


def _ffn_kernel(xs_ref, nw_in_ref, wg_ref, wu_ref, wo_ref, nw_out_ref, o_ref, h_scr, acc_scr):
    f = pl.program_id(1)

    @pl.when(f == 0)
    def _():
        h_scr[...] = _rms(xs_ref[...], nw_in_ref[...], 1e-6).astype(BF16)
        acc_scr[...] = jnp.zeros(acc_scr.shape, F32)

    h = h_scr[...]
    gate = jnp.dot(h, wg_ref[...], preferred_element_type=F32)
    up = jnp.dot(h, wu_ref[...], preferred_element_type=F32)
    act = (gate * _sigmoid(gate) * up).astype(BF16)
    acc_scr[...] += jnp.dot(act, wo_ref[...], preferred_element_type=F32)

    @pl.when(f == pl.num_programs(1) - 1)
    def _():
        o_ref[...] = xs_ref[...] + _rms(acc_scr[...], nw_out_ref[...], 1e-6)


def _ffn(xs, nw_in, w_ffn_in_all, w_ffn_out_all, nw_out, layer, tm, tf, real_rows=None):
    n, d = xs.shape
    d_ff = w_ffn_out_all.shape[1]
    nf = d_ff // tf
    if real_rows is None:
        n_out = n
        xs_spec = pl.BlockSpec((tm, d), lambda i, f: (i, 0))
    else:
        t_pad, first, count = real_rows
        per_seq = count // tm
        n_out = (n // t_pad) * count
        xs_spec = pl.BlockSpec((pl.Element(tm), pl.Element(d)),
                               lambda i, f: (pl.multiple_of(
                                   (i // per_seq) * t_pad + first + (i % per_seq) * tm, 8), 0))
        assert t_pad % 8 == 0 and first % 8 == 0 and tm % 8 == 0
    return pl.pallas_call(
        _ffn_kernel,
        grid=(n_out // tm, nf),
        in_specs=[xs_spec,
                  pl.BlockSpec((1, d), lambda i, f: (0, 0)),
                  pl.BlockSpec((None, d, tf), lambda i, f: (layer, 0, f)),
                  pl.BlockSpec((None, d, tf), lambda i, f: (layer, 0, nf + f)),
                  pl.BlockSpec((None, tf, d), lambda i, f: (layer, f, 0)),
                  pl.BlockSpec((1, d), lambda i, f: (0, 0))],
        out_specs=pl.BlockSpec((tm, d), lambda i, f: (i, 0)),
        out_shape=jax.ShapeDtypeStruct((n_out, d), F32),
        scratch_shapes=[pltpu.VMEM((tm, d), BF16), pltpu.VMEM((tm, d), F32)],
        compiler_params=_cparams(("parallel", "arbitrary")),
        name="swiglu_ffn",
    )(xs, nw_in, w_ffn_in_all, w_ffn_in_all, w_ffn_out_all, nw_out)


def _s5_tables(lam_re, lam_im, b_re, b_im, c_re, c_im, log_step):
    groups = lam_re.shape[0]
    nblk = groups // S5_GB
    lr, li = lam_re.astype(F32), lam_im.astype(F32)
    step = jnp.exp(log_step.astype(F32))[:, None]
    mag = jnp.exp(lr * step)
    ab_re, ab_im = mag * jnp.cos(li * step), mag * jnp.sin(li * step)
    den = lr * lr + li * li
    coef_re = ((ab_re - 1.0) * lr + ab_im * li) / den
    coef_im = (ab_im * lr - (ab_re - 1.0) * li) / den
    br, bi = b_re.astype(F32), b_im.astype(F32)
    bb_re = coef_re[..., None] * br - coef_im[..., None] * bi
    bb_im = coef_re[..., None] * bi + coef_im[..., None] * br
    cr, ci = c_re.astype(F32), c_im.astype(F32)
    eye = jnp.eye(S5_GB, dtype=F32)
    taus = jnp.arange(S5_L)
    lrs = (lr * step).reshape(nblk, 1, S5_HALF)
    lis = (li * step).reshape(nblk, 1, S5_HALF)

    def lam_pow(n):
        nn = n.astype(F32)[None, :, None]
        m = jnp.exp(nn * lrs)
        return m * jnp.cos(nn * lis), m * jnp.sin(nn * lis)

    def blockdiag(t):
        t = t.reshape(nblk, S5_GB, S5_GROUP, S5_STATE)
        return jnp.einsum("ngcp,gh->ngchp", t, eye).reshape(nblk, LANES, S5_HALF)

    bc = jnp.stack([blockdiag(bb_re.transpose(0, 2, 1)), blockdiag(bb_im.transpose(0, 2, 1)),
                    blockdiag(cr), blockdiag(ci)], axis=1)

    n_list = jnp.concatenate([S5_L - 1 - taus, 1 + taus, S5_L * jnp.array([1, 2, 4, 0, 0, 0, 0, 0]),
                              S5_L * (1 + jnp.arange(8))])
    wr, wi = lam_pow(n_list)
    pw = jnp.concatenate([wr, wi], axis=2)
    return bc, pw


def kernel(x, meta, rel_bias, norm_w, w_in, conv_w, conv_b, lru_w_a, lru_b_a, lru_w_x, lru_b_x, lru_lambda, da_lambda, da_subln, s5_lam_re, s5_lam_im, s5_b_re, s5_b_im, s5_c_re, s5_c_im, s5_d, s5_log_step, s5_w_glu, s5_b_glu, b_gate, w_branch, w_out, w_ffn_in, w_ffn_out):
    bsz, seq, d_model = x.shape
    depth = w_in.shape[0]
    lru_w = conv_w.shape[2]
    s5_w = s5_d.shape[1]
    qk_w = DA_HEADS * 2 * DA_HEAD_DIM
    assert lru_w == LRU_BLOCKS * LANES and s5_w % LANES == 0 and d_model % LANES == 0
    col_gate, col_x = 0, lru_w // LANES
    col_q = 2 * lru_w // LANES
    col_k = col_q + qk_w // LANES
    col_v = col_k + qk_w // LANES
    col_u = col_v + DA_HEADS
    col_g = (col_u + s5_w // LANES) * LANES

    t_real = N_META + seq
    tb = 384 if t_real >= 1024 else 128
    t_pad = -(-t_real // tb) * tb
    assert t_pad % SCAN_CH == 0
    n_tok = bsz * t_pad
    tm_in = _pick_tile(n_tok, (1408, 1056, 768, 512, 384, 256, 128))
    tm_merge = _pick_tile(n_tok, (1056, 768, 512, 384, 256, 128))
    tm_ffn = _pick_tile(n_tok, (768, 512, 384, 256, 128))
    tm_out = _pick_tile(n_tok, (512, 384, 256, 128))
    tm_last = _pick_tile(seq, (512, 256, 128))

    xs = jnp.concatenate([jnp.broadcast_to(meta.astype(F32)[None], (bsz, N_META, d_model)), x,
                          jnp.zeros((bsz, t_pad - t_real, d_model), F32)], axis=1)
    xs = xs.reshape(n_tok, d_model)
    btiles = _bias_tiles(rel_bias.astype(F32), tb)

    w_in_b, w_glu_b, w_branch_b = w_in.astype(BF16), s5_w_glu.astype(BF16), w_branch.astype(BF16)
    w_out_b, w_ffn_in_b, w_ffn_out_b = w_out.astype(BF16), w_ffn_in.astype(BF16), w_ffn_out.astype(BF16)

    for l in range(depth):
        lam_init = 0.8 - 0.6 * math.exp(-0.3 * l)
        proj = _inproj(xs, norm_w[l, 0][None], w_in_b, l, tm_in, 1024)
        proj3 = proj.reshape(bsz, t_pad, proj.shape[1])

        lru_p = jnp.concatenate([conv_w[l], conv_b[l][None], lru_b_a[l][None], lru_b_x[l][None],
                                 jax.nn.log_sigmoid(lru_lambda[l].astype(F32))[None]], axis=0)
        w_ax = jnp.concatenate([lru_w_a[l], lru_w_x[l]], axis=2).astype(BF16)
        y_a = _lru(proj3, lru_p, w_ax, col_gate, col_x)

        y_b = _attn(proj3, btiles, da_lambda[l], da_subln[l][None], lam_init, tb, col_q, col_k, col_v)

        bc, pw = _s5_tables(s5_lam_re[l], s5_lam_im[l], s5_b_re[l], s5_b_im[l],
                            s5_c_re[l], s5_c_im[l], s5_log_step[l])
        y_c = _s5(proj3, bc, pw, s5_d[l][None], col_u)

        merged = _merge(y_a.reshape(n_tok, lru_w), y_b.reshape(n_tok, -1), y_c.reshape(n_tok, s5_w),
                        proj, w_glu_b, s5_b_glu[l][None], w_branch_b, b_gate[l], col_g, l, tm_merge, 512)
        xs = _outproj(merged, w_out_b, l, xs, norm_w[l, 1][None], tm_out)
        if l + 1 < depth:
            xs = _ffn(xs, norm_w[l, 2][None], w_ffn_in_b, w_ffn_out_b, norm_w[l, 3][None], l, tm_ffn, 512)
        else:
            out = _ffn(xs, norm_w[l, 2][None], w_ffn_in_b, w_ffn_out_b, norm_w[l, 3][None], l, tm_last,
                       512, real_rows=(t_pad, N_META, seq))

    return out.reshape(bsz, seq, d_model)
```

```python
import functools
import math

import jax
import jax.numpy as jnp
from jax import lax
from jax.experimental import pallas as pl
from jax.experimental.pallas import tpu as pltpu

F32 = jnp.float32
BF16 = jnp.bfloat16

N_META = 16
CONV_W = 4
LRU_C = 8.0
LRU_BLOCKS = 8
DA_HEADS = 8
DA_HEAD_DIM = 64
S5_GROUP = 16
S5_STATE = 64
REL_BUCKETS = 32
REL_MAX_DIST = 128
N_BRANCH = 3

LANES = 128
S5_GB = LANES // S5_GROUP
S5_HALF = S5_GB * S5_STATE
S5_L = 8
SCAN_CH = 128
MASK_NEG = -1e30
LOG2E = math.log2(math.e)
VMEM_LIMIT = 56 * 1024 * 1024


def _cparams(sem):
    return pltpu.CompilerParams(dimension_semantics=sem, vmem_limit_bytes=VMEM_LIMIT)


def _pick_tile(n, candidates):
    for c in candidates:
        if n % c == 0:
            return c
    raise ValueError(f"no tile in {candidates} divides {n}")


def _gelu_tanh(x):
    return 0.5 * x * (1.0 + jnp.tanh(math.sqrt(2.0 / math.pi) * (x + 0.044715 * (x * x * x))))


def _sigmoid(x):
    return 1.0 / (1.0 + jnp.exp(-x))


def _rms(x, w, eps):
    return (x * lax.rsqrt(jnp.mean(x * x, axis=-1, keepdims=True) + eps)) * w


def _inproj_kernel(x_ref, nw_ref, w_ref, o_ref, h_scr):
    @pl.when(pl.program_id(1) == 0)
    def _():
        h_scr[...] = _rms(x_ref[...], nw_ref[...], 1e-6).astype(BF16)

    o_ref[...] = jnp.dot(h_scr[...], w_ref[...], preferred_element_type=F32).astype(o_ref.dtype)


def _inproj(xs, nw, w_all, layer, tm, tn):
    n, d = xs.shape
    n_out = w_all.shape[2]
    return pl.pallas_call(
        _inproj_kernel,
        grid=(n // tm, n_out // tn),
        in_specs=[pl.BlockSpec((tm, d), lambda i, j: (i, 0)),
                  pl.BlockSpec((1, d), lambda i, j: (0, 0)),
                  pl.BlockSpec((None, d, tn), lambda i, j: (layer, 0, j))],
        out_specs=pl.BlockSpec((tm, tn), lambda i, j: (i, j)),
        out_shape=jax.ShapeDtypeStruct((n, n_out), BF16),
        scratch_shapes=[pltpu.VMEM((tm, d), BF16)],
        compiler_params=_cparams(("parallel", "arbitrary")),
        name="inproj",
    )(xs, nw, w_all)


def _lru_kernel(g_ref, x_ref, p_ref, w_ref, o_ref, xs_scr):
    t_len = x_ref.shape[1]
    halo = 8
    xs_scr[0:halo, :] = jnp.zeros((halo, LANES), F32)
    xs_scr[halo:, :] = x_ref[0].astype(F32)
    p = p_ref[...]
    cw = [p[j:j + 1, :] for j in range(CONV_W)]
    cb, ba, bx = p[4:5, :], p[5:6, :], p[6:7, :]
    c_logsig = LRU_C * p[7:8, :]
    w = w_ref[0]
    row = lax.broadcasted_iota(jnp.int32, (SCAN_CH, LANES), 0)

    def local_scan(base):
        xc = cb
        for j in range(CONV_W):
            xc = xc + cw[j] * xs_scr[pl.ds(base + halo - (CONV_W - 1) + j, SCAN_CH), :]
        ri = jnp.dot(xc.astype(BF16), w, preferred_element_type=F32)
        r = _sigmoid(ri[:, :LANES] + ba)
        i = _sigmoid(ri[:, LANES:] + bx)
        a = jnp.exp(r * c_logsig)
        b = jnp.sqrt(1.0 - a * a) * (i * xc)
        s = 1
        while s < SCAN_CH:
            keep = row >= s
            a_sh = jnp.where(keep, pltpu.roll(a, s, 0), 1.0)
            b_sh = jnp.where(keep, pltpu.roll(b, s, 0), 0.0)
            b = a * b_sh + b
            a = a * a_sh
            s *= 2
        return a, b

    n_chunks = t_len // SCAN_CH
    sub = 3 if n_chunks % 3 == 0 else 1

    def chunk(c, h0):
        bases = [pl.multiple_of((c * sub + j) * SCAN_CH, SCAN_CH) for j in range(sub)]
        scans = [local_scan(base) for base in bases]
        for base, (a, b) in zip(bases, scans):
            h = a * h0 + b
            g = g_ref[0, pl.ds(base, SCAN_CH), :].astype(F32)
            o_ref[0, pl.ds(base, SCAN_CH), :] = (h * _gelu_tanh(g)).astype(o_ref.dtype)
            h0 = h[SCAN_CH - 1:SCAN_CH, :]
        return h0

    lax.fori_loop(0, n_chunks // sub, chunk, jnp.zeros((1, LANES), F32))


def _lru(proj3, lru_p, w_ax, col_gate, col_x):
    b, t, _ = proj3.shape
    width = LRU_BLOCKS * LANES
    return pl.pallas_call(
        _lru_kernel,
        grid=(b, LRU_BLOCKS),
        in_specs=[pl.BlockSpec((1, t, LANES), lambda bi, h: (bi, 0, col_gate + h)),
                  pl.BlockSpec((1, t, LANES), lambda bi, h: (bi, 0, col_x + h)),
                  pl.BlockSpec((8, LANES), lambda bi, h: (0, h)),
                  pl.BlockSpec((1, LANES, 2 * LANES), lambda bi, h: (h, 0, 0))],
        out_specs=pl.BlockSpec((1, t, LANES), lambda bi, h: (bi, 0, h)),
        out_shape=jax.ShapeDtypeStruct((b, t, width), BF16),
        scratch_shapes=[pltpu.VMEM((t + 8, LANES), F32)],
        compiler_params=_cparams(("parallel", "parallel")),
        name="rglru",
    )(proj3, proj3, lru_p, w_ax)


def _bias_tile_kernel(rb_ref, o_ref):
    h = pl.program_id(0)
    tb = o_ref.shape[1]
    nb = tb // LANES
    i = lax.broadcasted_iota(jnp.int32, (LANES, LANES), 0)
    j = lax.broadcasted_iota(jnp.int32, (LANES, LANES), 1)
    max_exact = REL_BUCKETS // 2

    def pattern(delta):
        d = i - j + delta * LANES
        n = jnp.maximum(d, 0)
        nf = jnp.maximum(n, 1).astype(F32)
        large = max_exact + (jnp.log(nf / max_exact) / math.log(REL_MAX_DIST / max_exact)
                             * (REL_BUCKETS - max_exact)).astype(jnp.int32)
        large = jnp.minimum(large, REL_BUCKETS - 1)
        bucket = jnp.where(n < max_exact, n, large)
        val = jnp.zeros((LANES, LANES), F32)
        for bkt in range(REL_BUCKETS):
            val = jnp.where(bucket == bkt, rb_ref[bkt, h], val)
        val = val - rb_ref[REL_BUCKETS - 1, h]
        return jnp.where(d >= 0, val, MASK_NEG)

    pieces = {0: pattern(0), 1: pattern(1)}
    for ri in range(nb):
        for cj in range(2 * nb):
            delta = ri - cj + nb
            if delta in pieces:
                piece = pieces[delta]
            else:
                piece = jnp.full((LANES, LANES), 0.0 if delta >= 2 else MASK_NEG, F32)
            o_ref[0, ri * LANES:(ri + 1) * LANES, cj * LANES:(cj + 1) * LANES] = piece


def _bias_tiles(rel_bias, tb):
    return pl.pallas_call(
        _bias_tile_kernel,
        grid=(DA_HEADS,),
        in_specs=[pl.BlockSpec(memory_space=pltpu.SMEM)],
        out_specs=pl.BlockSpec((1, tb, 2 * tb), lambda h: (h, 0, 0)),
        out_shape=jax.ShapeDtypeStruct((DA_HEADS, tb, 2 * tb), F32),
        compiler_params=_cparams(("parallel",)),
        name="t5_bias_tiles",
    )(rel_bias)


def _attn_qblock(qi, q_ref, k_ref, v1_scr, bt_ref, sw_ref, o_ref, m_scr, acc_scr, s_scr, accp_scr, lam,
                 lam_init):
    tb = s_scr.shape[2]
    dn_t = (((1,), (1,)), ((), ()))
    lane = lax.broadcasted_iota(jnp.int32, (tb, LANES), 1)
    qstart = pl.multiple_of(qi * tb, tb)
    qs = q_ref[0, pl.ds(qstart, tb), :] * (DA_HEAD_DIM ** -0.5)
    zero = jnp.zeros_like(qs)
    qcat = jnp.concatenate([jnp.where(lane < DA_HEAD_DIM, qs, zero),
                            jnp.where(lane >= DA_HEAD_DIM, qs, zero)], axis=0)
    n_far = jnp.maximum(qi - 1, 0)
    n_quads = n_far // 4
    n_rest = n_far % 4
    tail_blk = 4 * n_quads

    def scores(blk, nblocks, biased):
        kb = k_ref[0, pl.ds(pl.multiple_of(blk * tb, tb), nblocks * tb), :]
        s_both = lax.dot_general(qcat, kb, dn_t, preferred_element_type=F32)
        out = []
        for c in range(2):
            s = s_both[c * tb:(c + 1) * tb]
            if biased:
                nb = min(nblocks, 2)
                plain = (nblocks - nb) * tb
                tail = s[:, plain:] + bt_ref[0, :, (2 - nb) * tb:]
                s = tail if plain == 0 else jnp.concatenate([s[:, :plain], tail], axis=1)
            s = s * LOG2E
            for i in range(nblocks):
                s_scr[c, blk + i] = s[:, i * tb:(i + 1) * tb]
            out.append(s)
        return out

    def lane_chunks(s):
        return [s[:, j * LANES:(j + 1) * LANES] for j in range(s.shape[1] // LANES)]

    def max_into(ms, ss):
        out = []
        for m, s in zip(ms, ss):
            for ch in lane_chunks(s):
                m = jnp.maximum(m, ch)
            out.append(m)
        return out

    m_init = jnp.full((tb, LANES), MASK_NEG, F32)

    @pl.when(qi == 0)
    def _():
        m_scr[0], m_scr[1] = max_into((m_init, m_init), scores(0, 1, True))

    for rest in range(4):
        @pl.when(jnp.logical_and(qi >= 1, n_rest == rest))
        def _():
            m_scr[0], m_scr[1] = max_into((m_init, m_init), scores(tail_blk, rest + 2, True))
            _attn_finalize(qstart - tb, accp_scr, sw_ref, o_ref, lam, lam_init)

    def quad_max(ki, ms):
        return tuple(max_into(ms, scores(4 * ki, 4, False)))

    ms = lax.fori_loop(0, n_quads, quad_max, (m_scr[0], m_scr[1]))
    def row_max():
        mb = [jnp.broadcast_to(jnp.max(ms[c], axis=-1, keepdims=True), (tb, LANES)) for c in range(2)]
        m_scr[0], m_scr[1] = mb
        return mb

    def accumulate(blk, nblocks, st, mb):
        vb = v1_scr[pl.ds(pl.multiple_of(blk * tb, tb), nblocks * tb), :]
        out = []
        for c in range(2):
            ps = []
            for i in range(nblocks):
                ps += [jnp.exp2(ch - mb[c]) for ch in lane_chunks(s_scr[c, blk + i])]
            p = jnp.concatenate(ps, axis=1).astype(BF16)
            out.append(st[c] + jnp.dot(p, vb, preferred_element_type=F32))
        return tuple(out)

    def store_state(st):
        acc_scr[0], acc_scr[1] = st

    za = jnp.zeros((tb, 2 * LANES), F32)

    @pl.when(qi == 0)
    def _():
        store_state(accumulate(0, 1, (za, za), row_max()))

    for rest in range(4):
        @pl.when(jnp.logical_and(qi >= 1, n_rest == rest))
        def _():
            store_state(accumulate(tail_blk, rest + 2, (za, za), row_max()))

    mb_q = (m_scr[0], m_scr[1])
    a0, a1 = lax.fori_loop(0, n_quads, lambda ki, st: accumulate(4 * ki, 4, st, mb_q),
                           (acc_scr[0], acc_scr[1]))

    accp_scr[0], accp_scr[1] = a0, a1


def _attn_finalize(row_start, accp_scr, sw_ref, o_ref, lam, lam_init):
    tb = accp_scr.shape[1]
    a0, a1 = accp_scr[0], accp_scr[1]
    o = a0[:, :LANES] / a0[:, LANES:] - lam * (a1[:, :LANES] / a1[:, LANES:])
    o = _rms(o, sw_ref[...], 1e-5) * (1.0 - lam_init)
    if not isinstance(row_start, int):
        row_start = pl.multiple_of(row_start, tb)
    o_ref[0, pl.ds(row_start, tb), :] = o.astype(o_ref.dtype)


def _attn_kernel(q_ref, k_ref, v_ref, bt_ref, dl_ref, sw_ref, o_ref, m_scr, acc_scr, s_scr, v1_scr, accp_scr,
                 *, lam_init):
    dl = dl_ref[...]
    lam = (jnp.exp(jnp.sum(dl[0:1, :] * dl[1:2, :], axis=-1, keepdims=True))
           - jnp.exp(jnp.sum(dl[2:3, :] * dl[3:4, :], axis=-1, keepdims=True)) + lam_init)
    v1_scr[:, :LANES] = v_ref[0]
    v1_scr[:, LANES:] = jnp.ones((v_ref.shape[1], LANES), BF16)

    def qblock(qi, carry):
        _attn_qblock(qi, q_ref, k_ref, v1_scr, bt_ref, sw_ref, o_ref, m_scr, acc_scr, s_scr, accp_scr, lam,
                     lam_init)
        return carry

    tb = s_scr.shape[2]
    nq = q_ref.shape[1] // tb
    lax.fori_loop(0, nq, qblock, 0)
    _attn_finalize((nq - 1) * tb, accp_scr, sw_ref, o_ref, lam, lam_init)


def _attn(proj3, btiles, da_lambda, da_subln, lam_init, tb, col_q, col_k, col_v):
    b, t, _ = proj3.shape
    width = DA_HEADS * LANES
    return pl.pallas_call(
        functools.partial(_attn_kernel, lam_init=lam_init),
        grid=(b, DA_HEADS),
        in_specs=[pl.BlockSpec((1, t, LANES), lambda bi, h: (bi, 0, col_q + h)),
                  pl.BlockSpec((1, t, LANES), lambda bi, h: (bi, 0, col_k + h)),
                  pl.BlockSpec((1, t, LANES), lambda bi, h: (bi, 0, col_v + h)),
                  pl.BlockSpec((1, tb, 2 * tb), lambda bi, h: (h, 0, 0)),
                  pl.BlockSpec((4, DA_HEAD_DIM), lambda bi, h: (0, 0)),
                  pl.BlockSpec((1, LANES), lambda bi, h: (0, 0))],
        out_specs=pl.BlockSpec((1, t, LANES), lambda bi, h: (bi, 0, h)),
        out_shape=jax.ShapeDtypeStruct((b, t, width), BF16),
        scratch_shapes=[pltpu.VMEM((2, tb, LANES), F32), pltpu.VMEM((2, tb, 2 * LANES), F32),
                        pltpu.VMEM((2, t // tb, tb, tb), F32), pltpu.VMEM((t, 2 * LANES), BF16),
                        pltpu.VMEM((2, tb, 2 * LANES), F32)],
        compiler_params=_cparams(("parallel", "parallel")),
        name="diff_attn",
    )(proj3, proj3, proj3, btiles, da_lambda, da_subln)


def _s5_kernel(u_ref, bc_ref, pw_ref, d_ref, o_ref, uf_scr, g_scr, hp_scr, y_scr,
               wg_scr, wk_scr, wct_scr):
    t_len = u_ref.shape[1]
    nj = t_len // S5_L
    pw = pw_ref[0]

    @pl.when(pl.program_id(1) == 0)
    def _():
        bd_re, bd_im, ct_re, ct_im = bc_ref[0, 0], bc_ref[0, 1], bc_ref[0, 2], bc_ref[0, 3]
        for s in range(S5_L):
            rows = slice(s * LANES, (s + 1) * LANES)
            qr, qi = pw[s:s + 1, :S5_HALF], pw[s:s + 1, S5_HALF:]
            wg_scr[rows, :S5_HALF] = (qr * bd_re - qi * bd_im).astype(BF16)
            wg_scr[rows, S5_HALF:] = (qr * bd_im + qi * bd_re).astype(BF16)
            er, ei = pw[8 + s:9 + s, :S5_HALF], pw[8 + s:9 + s, S5_HALF:]
            wct_scr[rows, :S5_HALF] = (ct_re * er - ct_im * ei).astype(BF16)
            wct_scr[rows, S5_HALF:] = (-(ct_re * ei + ct_im * er)).astype(BF16)
        dn_t = (((1,), (1,)), ((), ()))
        bcat = jnp.concatenate([bd_re, bd_im], axis=1).astype(BF16)
        crows = [jnp.concatenate([ct_re, -ct_im], axis=1).astype(BF16)]
        crows += [wct_scr[s * LANES:(s + 1) * LANES, :] for s in range(S5_L - 1)]
        kts = [lax.dot_general(bcat, c, dn_t, preferred_element_type=F32).astype(BF16) for c in crows]
        for s in range(S5_L):
            for r in range(S5_L):
                wk_scr[s * LANES:(s + 1) * LANES, r * LANES:(r + 1) * LANES] = (
                    kts[r - s] if r >= s else jnp.zeros((LANES, LANES), BF16))

    uf_scr[...] = u_ref[0].astype(F32)
    ur = jnp.concatenate([uf_scr[pl.ds(s, nj, stride=S5_L), :].astype(BF16) for s in range(S5_L)],
                         axis=1)
    g_scr[...] = jnp.dot(ur, wg_scr[...], preferred_element_type=F32)

    pr, pi = pw[24:32, :S5_HALF], pw[24:32, S5_HALF:]
    row = lax.broadcasted_iota(jnp.int32, (8, S5_HALF), 0)

    def local_scan(base):
        x = g_scr[pl.ds(base, 8), :]
        xr, xi = x[:, :S5_HALF], x[:, S5_HALF:]
        for lvl, s in enumerate((1, 2, 4)):
            ar, ai = pw[16 + lvl:17 + lvl, :S5_HALF], pw[16 + lvl:17 + lvl, S5_HALF:]
            keep = row >= s
            sr = jnp.where(keep, pltpu.roll(xr, s, 0), 0.0)
            si = jnp.where(keep, pltpu.roll(xi, s, 0), 0.0)
            xr, xi = xr + (ar * sr - ai * si), xi + (ar * si + ai * sr)
        return xr, xi

    first = row == 0
    per_iter = 2

    def groups(gi, carry):
        cr, ci = carry
        bases = [pl.multiple_of((gi * per_iter + k) * 8, 8) for k in range(per_iter)]
        scans = [local_scan(base) for base in bases]
        for base, (xr, xi) in zip(bases, scans):
            xr, xi = xr + (pr * cr - pi * ci), xi + (pr * ci + pi * cr)
            hp_scr[pl.ds(base, 8), :] = jnp.concatenate(
                [jnp.where(first, cr, pltpu.roll(xr, 1, 0)), jnp.where(first, ci, pltpu.roll(xi, 1, 0))],
                axis=1)
            cr, ci = xr[7:8, :], xi[7:8, :]
        return cr, ci

    z = jnp.zeros((1, S5_HALF), F32)
    lax.fori_loop(0, nj // (8 * per_iter), groups, (z, z))

    y = (jnp.dot(ur, wk_scr[...], preferred_element_type=F32)
         + lax.dot_general(hp_scr[...].astype(BF16), wct_scr[...], (((1,), (1,)), ((), ())),
                           preferred_element_type=F32))
    for s in range(S5_L):
        y_scr[pl.ds(s, nj, stride=S5_L), :] = y[:, s * LANES:(s + 1) * LANES]
    o_ref[0] = _gelu_tanh(y_scr[...] + d_ref[...] * uf_scr[...]).astype(o_ref.dtype)


def _s5(proj3, bc_all, pw_all, layer, dvec, col_u):
    b, t, _ = proj3.shape
    nblk = bc_all.shape[1]
    nj = t // S5_L
    wide = S5_L * LANES
    assert nj % 16 == 0
    return pl.pallas_call(
        _s5_kernel,
        grid=(nblk, b),
        in_specs=[pl.BlockSpec((1, t, LANES), lambda g, bi: (bi, 0, col_u + g)),
                  pl.BlockSpec((None, 1, 4, LANES, S5_HALF), lambda g, bi: (layer, g, 0, 0, 0)),
                  pl.BlockSpec((None, 1, 32, 2 * S5_HALF), lambda g, bi: (layer, g, 0, 0)),
                  pl.BlockSpec((1, LANES), lambda g, bi: (0, g))],
        out_specs=pl.BlockSpec((1, t, LANES), lambda g, bi: (bi, 0, g)),
        out_shape=jax.ShapeDtypeStruct((b, t, nblk * LANES), BF16),
        scratch_shapes=[pltpu.VMEM((t, LANES), F32), pltpu.VMEM((nj, 2 * S5_HALF), F32),
                        pltpu.VMEM((nj, 2 * S5_HALF), F32), pltpu.VMEM((t, LANES), F32),
                        pltpu.VMEM((wide, 2 * S5_HALF), BF16), pltpu.VMEM((wide, wide), BF16),
                        pltpu.VMEM((wide, 2 * S5_HALF), BF16)],
        compiler_params=_cparams(("parallel", "arbitrary")),
        name="s5",
    )(proj3, bc_all, pw_all, dvec)


def _merge_kernel(ya_ref, yb_ref, yc_ref, g0_ref, g1_ref, g2_ref, wg_ref, bg_ref, wb_ref, bgate_ref,
                  o_ref, yc_scr):
    @pl.when(pl.program_id(1) == 0)
    def _():
        yc = yc_ref[...]
        z = jnp.dot(yc, wg_ref[...], preferred_element_type=F32) + bg_ref[...]
        yc_scr[...] = (yc.astype(F32) * _sigmoid(z)).astype(BF16)

    bgate = bgate_ref[...]
    ys = (ya_ref[...], yb_ref[...], yc_scr[...])
    gs = (g0_ref, g1_ref, g2_ref)
    merged = None
    for br in range(N_BRANCH):
        gate = _sigmoid(gs[br][...].astype(F32) + bgate[br:br + 1, :])
        term = gate * jnp.dot(ys[br], wb_ref[br], preferred_element_type=F32)
        merged = term if merged is None else merged + term
    o_ref[...] = merged.astype(o_ref.dtype)


def _merge(ya, yb, yc, proj, w_glu_all, b_glu, w_branch_all, b_gate, col_g, layer, tm, tn):
    n, wdt = ya.shape
    d = w_branch_all.shape[3]
    gcol = [(col_g + br * d) // tn for br in range(N_BRANCH)]
    yspec = pl.BlockSpec((tm, wdt), lambda i, j: (i, 0))

    def gspec(br):
        return pl.BlockSpec((tm, tn), lambda i, j: (i, gcol[br] + j))

    return pl.pallas_call(
        _merge_kernel,
        grid=(n // tm, d // tn),
        in_specs=[yspec, yspec, yspec, gspec(0), gspec(1), gspec(2),
                  pl.BlockSpec((None, wdt, wdt), lambda i, j: (layer, 0, 0)),
                  pl.BlockSpec((1, wdt), lambda i, j: (0, 0)),
                  pl.BlockSpec((None, N_BRANCH, wdt, tn), lambda i, j: (layer, 0, 0, j)),
                  pl.BlockSpec((N_BRANCH, tn), lambda i, j: (0, j))],
        out_specs=pl.BlockSpec((tm, tn), lambda i, j: (i, j)),
        out_shape=jax.ShapeDtypeStruct((n, d), BF16),
        scratch_shapes=[pltpu.VMEM((tm, wdt), BF16)],
        compiler_params=_cparams(("parallel", "arbitrary")),
        name="merge",
    )(ya, yb, yc, proj, proj, proj, w_glu_all, b_glu, w_branch_all, b_gate)


def _outproj_kernel(m_ref, w_ref, xs_ref, nw_ref, o_ref):
    mix = jnp.dot(m_ref[...], w_ref[...], preferred_element_type=F32)
    o_ref[...] = xs_ref[...] + _rms(mix, nw_ref[...], 1e-6)


def _outproj(merged, w_out_all, layer, xs, nw, tm):
    n, d = xs.shape
    return pl.pallas_call(
        _outproj_kernel,
        grid=(n // tm,),
        in_specs=[pl.BlockSpec((tm, d), lambda i: (i, 0)),
                  pl.BlockSpec((None, d, d), lambda i: (layer, 0, 0)),
                  pl.BlockSpec((tm, d), lambda i: (i, 0)),
                  pl.BlockSpec((1, d), lambda i: (0, 0))],
        out_specs=pl.BlockSpec((tm, d), lambda i: (i, 0)),
        out_shape=jax.ShapeDtypeStruct((n, d), F32),
        compiler_params=_cparams(("parallel",)),
        name="outproj_residual",
    )(merged, w_out_all, xs, nw)


def _ffn_kernel(xs_ref, nw_in_ref, wg_ref, wu_ref, wo_ref, nw_out_ref, o_ref, h_scr, acc_scr):
    f = pl.program_id(1)

    @pl.when(f == 0)
    def _():
        h_scr[...] = _rms(xs_ref[...], nw_in_ref[...], 1e-6).astype(BF16)
        acc_scr[...] = jnp.zeros(acc_scr.shape, F32)

    h = h_scr[...]
    gate = jnp.dot(h, wg_ref[...], preferred_element_type=F32)
    up = jnp.dot(h, wu_ref[...], preferred_element_type=F32)
    act = (gate * _sigmoid(gate) * up).astype(BF16)
    acc_scr[...] += jnp.dot(act, wo_ref[...], preferred_element_type=F32)

    @pl.when(f == pl.num_programs(1) - 1)
    def _():
        o_ref[...] = xs_ref[...] + _rms(acc_scr[...], nw_out_ref[...], 1e-6)


def _ffn(xs, nw_in, w_ffn_in_all, w_ffn_out_all, nw_out, layer, tm, tf, real_rows=None):
    n, d = xs.shape
    d_ff = w_ffn_out_all.shape[1]
    nf = d_ff // tf
    if real_rows is None:
        n_out = n
        xs_spec = pl.BlockSpec((tm, d), lambda i, f: (i, 0))
    else:
        t_pad, first, count = real_rows
        per_seq = count // tm
        n_out = (n // t_pad) * count
        xs_spec = pl.BlockSpec((pl.Element(tm), pl.Element(d)),
                               lambda i, f: (pl.multiple_of(
                                   (i // per_seq) * t_pad + first + (i % per_seq) * tm, 8), 0))
        assert t_pad % 8 == 0 and first % 8 == 0 and tm % 8 == 0
    return pl.pallas_call(
        _ffn_kernel,
        grid=(n_out // tm, nf),
        in_specs=[xs_spec,
                  pl.BlockSpec((1, d), lambda i, f: (0, 0)),
                  pl.BlockSpec((None, d, tf), lambda i, f: (layer, 0, f)),
                  pl.BlockSpec((None, d, tf), lambda i, f: (layer, 0, nf + f)),
                  pl.BlockSpec((None, tf, d), lambda i, f: (layer, f, 0)),
                  pl.BlockSpec((1, d), lambda i, f: (0, 0))],
        out_specs=pl.BlockSpec((tm, d), lambda i, f: (i, 0)),
        out_shape=jax.ShapeDtypeStruct((n_out, d), F32),
        scratch_shapes=[pltpu.VMEM((tm, d), BF16), pltpu.VMEM((tm, d), F32)],
        compiler_params=_cparams(("parallel", "arbitrary")),
        name="swiglu_ffn",
    )(xs, nw_in, w_ffn_in_all, w_ffn_in_all, w_ffn_out_all, nw_out)


def _s5_tables(lam_re, lam_im, b_re, b_im, c_re, c_im, log_step):
    groups = lam_re.shape[0]
    nblk = groups // S5_GB
    lr, li = lam_re.astype(F32), lam_im.astype(F32)
    step = jnp.exp(log_step.astype(F32))[:, None]
    mag = jnp.exp(lr * step)
    ab_re, ab_im = mag * jnp.cos(li * step), mag * jnp.sin(li * step)
    den = lr * lr + li * li
    coef_re = ((ab_re - 1.0) * lr + ab_im * li) / den
    coef_im = (ab_im * lr - (ab_re - 1.0) * li) / den
    br, bi = b_re.astype(F32), b_im.astype(F32)
    bb_re = coef_re[..., None] * br - coef_im[..., None] * bi
    bb_im = coef_re[..., None] * bi + coef_im[..., None] * br
    cr, ci = c_re.astype(F32), c_im.astype(F32)
    eye = jnp.eye(S5_GB, dtype=F32)
    taus = jnp.arange(S5_L)
    lrs = (lr * step).reshape(nblk, 1, S5_HALF)
    lis = (li * step).reshape(nblk, 1, S5_HALF)

    def lam_pow(n):
        nn = n.astype(F32)[None, :, None]
        m = jnp.exp(nn * lrs)
        return m * jnp.cos(nn * lis), m * jnp.sin(nn * lis)

    def blockdiag(t):
        t = t.reshape(nblk, S5_GB, S5_GROUP, S5_STATE)
        return jnp.einsum("ngcp,gh->ngchp", t, eye).reshape(nblk, LANES, S5_HALF)

    bc = jnp.stack([blockdiag(bb_re.transpose(0, 2, 1)), blockdiag(bb_im.transpose(0, 2, 1)),
                    blockdiag(cr), blockdiag(ci)], axis=1)

    n_list = jnp.concatenate([S5_L - 1 - taus, 1 + taus, S5_L * jnp.array([1, 2, 4, 0, 0, 0, 0, 0]),
                              S5_L * (1 + jnp.arange(8))])
    wr, wi = lam_pow(n_list)
    pw = jnp.concatenate([wr, wi], axis=2)
    return bc, pw


def kernel(x, meta, rel_bias, norm_w, w_in, conv_w, conv_b, lru_w_a, lru_b_a, lru_w_x, lru_b_x, lru_lambda, da_lambda, da_subln, s5_lam_re, s5_lam_im, s5_b_re, s5_b_im, s5_c_re, s5_c_im, s5_d, s5_log_step, s5_w_glu, s5_b_glu, b_gate, w_branch, w_out, w_ffn_in, w_ffn_out):
    bsz, seq, d_model = x.shape
    depth = w_in.shape[0]
    lru_w = conv_w.shape[2]
    s5_w = s5_d.shape[1]
    qk_w = DA_HEADS * 2 * DA_HEAD_DIM
    assert lru_w == LRU_BLOCKS * LANES and s5_w % LANES == 0 and d_model % LANES == 0
    col_gate, col_x = 0, lru_w // LANES
    col_q = 2 * lru_w // LANES
    col_k = col_q + qk_w // LANES
    col_v = col_k + qk_w // LANES
    col_u = col_v + DA_HEADS
    col_g = (col_u + s5_w // LANES) * LANES

    t_real = N_META + seq
    tb = 384 if t_real >= 1024 else 128
    t_pad = -(-t_real // tb) * tb
    assert t_pad % SCAN_CH == 0
    n_tok = bsz * t_pad
    tm_in = _pick_tile(n_tok, (1408, 1056, 768, 512, 384, 256, 128))
    tm_merge = _pick_tile(n_tok, (1056, 768, 512, 384, 256, 128))
    tm_ffn = _pick_tile(n_tok, (768, 512, 384, 256, 128))
    tm_out = _pick_tile(n_tok, (512, 384, 256, 128))
    tm_last = _pick_tile(seq, (512, 256, 128))

    xs = jnp.concatenate([jnp.broadcast_to(meta.astype(F32)[None], (bsz, N_META, d_model)), x,
                          jnp.zeros((bsz, t_pad - t_real, d_model), F32)], axis=1)
    xs = xs.reshape(n_tok, d_model)
    btiles = _bias_tiles(rel_bias.astype(F32), tb)

    w_in_b, w_glu_b, w_branch_b = w_in.astype(BF16), s5_w_glu.astype(BF16), w_branch.astype(BF16)
    w_out_b, w_ffn_in_b, w_ffn_out_b = w_out.astype(BF16), w_ffn_in.astype(BF16), w_ffn_out.astype(BF16)

    lru_p_all = jnp.concatenate([conv_w, conv_b[:, None], lru_b_a[:, None], lru_b_x[:, None],
                                 jax.nn.log_sigmoid(lru_lambda.astype(F32))[:, None]], axis=1)
    w_ax_all = jnp.concatenate([lru_w_a, lru_w_x], axis=3).astype(BF16)
    bc_all, pw_all = jax.vmap(_s5_tables)(s5_lam_re, s5_lam_im, s5_b_re, s5_b_im, s5_c_re, s5_c_im,
                                          s5_log_step)

    for l in range(depth):
        lam_init = 0.8 - 0.6 * math.exp(-0.3 * l)
        proj = _inproj(xs, norm_w[l, 0][None], w_in_b, l, tm_in, 1024)
        proj3 = proj.reshape(bsz, t_pad, proj.shape[1])

        y_a = _lru(proj3, lru_p_all[l], w_ax_all[l], col_gate, col_x)

        y_b = _attn(proj3, btiles, da_lambda[l], da_subln[l][None], lam_init, tb, col_q, col_k, col_v)

        y_c = _s5(proj3, bc_all, pw_all, l, s5_d[l][None], col_u)

        merged = _merge(y_a.reshape(n_tok, lru_w), y_b.reshape(n_tok, -1), y_c.reshape(n_tok, s5_w),
                        proj, w_glu_b, s5_b_glu[l][None], w_branch_b, b_gate[l], col_g, l, tm_merge, 512)
        xs = _outproj(merged, w_out_b, l, xs, norm_w[l, 1][None], tm_out)
        if l + 1 < depth:
            xs = _ffn(xs, norm_w[l, 2][None], w_ffn_in_b, w_ffn_out_b, norm_w[l, 3][None], l, tm_ffn, 512)
        else:
            out = _ffn(xs, norm_w[l, 2][None], w_ffn_in_b, w_ffn_out_b, norm_w[l, 3][None], l, tm_last,
                       512, real_rows=(t_pad, N_META, seq))

    return out.reshape(bsz, seq, d_model)
```

```python
import functools
import math

import jax
import jax.numpy as jnp
from jax import lax
from jax.experimental import pallas as pl
from jax.experimental.pallas import tpu as pltpu

F32 = jnp.float32
BF16 = jnp.bfloat16

N_META = 16
CONV_W = 4
LRU_C = 8.0
LRU_BLOCKS = 8
DA_HEADS = 8
DA_HEAD_DIM = 64
S5_GROUP = 16
S5_STATE = 64
REL_BUCKETS = 32
REL_MAX_DIST = 128
N_BRANCH = 3

LANES = 128
S5_GB = LANES // S5_GROUP
S5_HALF = S5_GB * S5_STATE
S5_L = 8
SCAN_CH = 128
MASK_NEG = -1e30
LOG2E = math.log2(math.e)
VMEM_LIMIT = 56 * 1024 * 1024


def _cparams(sem):
    return pltpu.CompilerParams(dimension_semantics=sem, vmem_limit_bytes=VMEM_LIMIT)


def _pick_tile(n, candidates):
    for c in candidates:
        if n % c == 0:
            return c
    raise ValueError(f"no tile in {candidates} divides {n}")


def _gelu_tanh(x):
    k = math.sqrt(2.0 / math.pi)
    half = 0.5 * x
    return half + half * jnp.tanh(x * (k + (k * 0.044715) * (x * x)))


def _sigmoid(x):
    return 1.0 / (1.0 + jnp.exp2(x * (-LOG2E)))


def _rms(x, w, eps):
    return (x * lax.rsqrt(jnp.mean(x * x, axis=-1, keepdims=True) + eps)) * w


def _inproj_kernel(x_ref, nw_ref, w_ref, o_ref, h_scr):
    @pl.when(pl.program_id(1) == 0)
    def _():
        h_scr[...] = _rms(x_ref[...], nw_ref[...], 1e-6).astype(BF16)

    o_ref[...] = jnp.dot(h_scr[...], w_ref[...], preferred_element_type=F32).astype(o_ref.dtype)


def _inproj(xs, nw, w_all, layer, tm, tn):
    n, d = xs.shape
    n_out = w_all.shape[2]
    return pl.pallas_call(
        _inproj_kernel,
        grid=(n // tm, n_out // tn),
        in_specs=[pl.BlockSpec((tm, d), lambda i, j: (i, 0)),
                  pl.BlockSpec((1, d), lambda i, j: (0, 0)),
                  pl.BlockSpec((None, d, tn), lambda i, j: (layer, 0, j))],
        out_specs=pl.BlockSpec((tm, tn), lambda i, j: (i, j)),
        out_shape=jax.ShapeDtypeStruct((n, n_out), BF16),
        scratch_shapes=[pltpu.VMEM((tm, d), BF16)],
        compiler_params=_cparams(("parallel", "arbitrary")),
        name="inproj",
    )(xs, nw, w_all)


def _lru_kernel(g_ref, x_ref, p_ref, w_ref, o_ref, xs_scr):
    t_len = x_ref.shape[1]
    halo = 8
    xs_scr[0:halo, :] = jnp.zeros((halo, LANES), F32)
    xs_scr[halo:, :] = x_ref[0].astype(F32)
    p = p_ref[...]
    cw = [p[j:j + 1, :] for j in range(CONV_W)]
    cb, ba, bx = p[4:5, :], p[5:6, :], p[6:7, :]
    c_log2sig = (LRU_C * LOG2E) * p[7:8, :]
    w = w_ref[0]
    row = lax.broadcasted_iota(jnp.int32, (SCAN_CH, LANES), 0)

    def local_scan(base):
        xc = cb
        for j in range(CONV_W):
            xc = xc + cw[j] * xs_scr[pl.ds(base + halo - (CONV_W - 1) + j, SCAN_CH), :]
        ri = jnp.dot(xc.astype(BF16), w, preferred_element_type=F32)
        r = _sigmoid(ri[:, :LANES] + ba)
        i = _sigmoid(ri[:, LANES:] + bx)
        a = jnp.exp2(r * c_log2sig)
        b = jnp.sqrt(1.0 - a * a) * (i * xc)
        s = 1
        while s < SCAN_CH:
            keep = row >= s
            a_sh = jnp.where(keep, pltpu.roll(a, s, 0), 1.0)
            b_sh = jnp.where(keep, pltpu.roll(b, s, 0), 0.0)
            b = a * b_sh + b
            a = a * a_sh
            s *= 2
        return a, b

    n_chunks = t_len // SCAN_CH
    sub = next(c for c in (11, 3, 1) if n_chunks % c == 0)

    def chunk(c, h0):
        bases = [pl.multiple_of((c * sub + j) * SCAN_CH, SCAN_CH) for j in range(sub)]
        scans = [local_scan(base) for base in bases]
        for base, (a, b) in zip(bases, scans):
            h = a * h0 + b
            g = g_ref[0, pl.ds(base, SCAN_CH), :].astype(F32)
            o_ref[0, pl.ds(base, SCAN_CH), :] = (h * _gelu_tanh(g)).astype(o_ref.dtype)
            h0 = h[SCAN_CH - 1:SCAN_CH, :]
        return h0

    lax.fori_loop(0, n_chunks // sub, chunk, jnp.zeros((1, LANES), F32))


def _lru(proj3, lru_p, w_ax, col_gate, col_x):
    b, t, _ = proj3.shape
    width = LRU_BLOCKS * LANES
    return pl.pallas_call(
        _lru_kernel,
        grid=(b, LRU_BLOCKS),
        in_specs=[pl.BlockSpec((1, t, LANES), lambda bi, h: (bi, 0, col_gate + h)),
                  pl.BlockSpec((1, t, LANES), lambda bi, h: (bi, 0, col_x + h)),
                  pl.BlockSpec((8, LANES), lambda bi, h: (0, h)),
                  pl.BlockSpec((1, LANES, 2 * LANES), lambda bi, h: (h, 0, 0))],
        out_specs=pl.BlockSpec((1, t, LANES), lambda bi, h: (bi, 0, h)),
        out_shape=jax.ShapeDtypeStruct((b, t, width), BF16),
        scratch_shapes=[pltpu.VMEM((t + 8, LANES), F32)],
        compiler_params=_cparams(("parallel", "parallel")),
        name="rglru",
    )(proj3, proj3, lru_p, w_ax)


def _bias_tile_kernel(rb_ref, o_ref):
    h = pl.program_id(0)
    tb = o_ref.shape[1]
    nb = tb // LANES
    i = lax.broadcasted_iota(jnp.int32, (LANES, LANES), 0)
    j = lax.broadcasted_iota(jnp.int32, (LANES, LANES), 1)
    max_exact = REL_BUCKETS // 2

    def pattern(delta):
        d = i - j + delta * LANES
        n = jnp.maximum(d, 0)
        nf = jnp.maximum(n, 1).astype(F32)
        large = max_exact + (jnp.log(nf / max_exact) / math.log(REL_MAX_DIST / max_exact)
                             * (REL_BUCKETS - max_exact)).astype(jnp.int32)
        large = jnp.minimum(large, REL_BUCKETS - 1)
        bucket = jnp.where(n < max_exact, n, large)
        val = jnp.zeros((LANES, LANES), F32)
        for bkt in range(REL_BUCKETS):
            val = jnp.where(bucket == bkt, rb_ref[bkt, h], val)
        val = val - rb_ref[REL_BUCKETS - 1, h]
        return jnp.where(d >= 0, val, MASK_NEG)

    pieces = {0: pattern(0), 1: pattern(1)}
    for ri in range(nb):
        for cj in range(2 * nb):
            delta = ri - cj + nb
            if delta in pieces:
                piece = pieces[delta]
            else:
                piece = jnp.full((LANES, LANES), 0.0 if delta >= 2 else MASK_NEG, F32)
            o_ref[0, ri * LANES:(ri + 1) * LANES, cj * LANES:(cj + 1) * LANES] = piece


def _bias_tiles(rel_bias, tb):
    return pl.pallas_call(
        _bias_tile_kernel,
        grid=(DA_HEADS,),
        in_specs=[pl.BlockSpec(memory_space=pltpu.SMEM)],
        out_specs=pl.BlockSpec((1, tb, 2 * tb), lambda h: (h, 0, 0)),
        out_shape=jax.ShapeDtypeStruct((DA_HEADS, tb, 2 * tb), F32),
        compiler_params=_cparams(("parallel",)),
        name="t5_bias_tiles",
    )(rel_bias)


def _attn_qblock(qi, q_ref, k_ref, v1_scr, bt_ref, sw_ref, o_ref, m_scr, acc_scr, s_scr, accp_scr, lam,
                 lam_init):
    tb = s_scr.shape[2]
    dn_t = (((1,), (1,)), ((), ()))
    lane = lax.broadcasted_iota(jnp.int32, (tb, LANES), 1)
    qstart = pl.multiple_of(qi * tb, tb)
    qs = q_ref[0, pl.ds(qstart, tb), :] * (DA_HEAD_DIM ** -0.5)
    zero = jnp.zeros_like(qs)
    qcat = jnp.concatenate([jnp.where(lane < DA_HEAD_DIM, qs, zero),
                            jnp.where(lane >= DA_HEAD_DIM, qs, zero)], axis=0)
    n_far = jnp.maximum(qi - 1, 0)
    n_quads = n_far // 4
    n_rest = n_far % 4
    tail_blk = 4 * n_quads

    def scores(blk, nblocks, biased):
        kb = k_ref[0, pl.ds(pl.multiple_of(blk * tb, tb), nblocks * tb), :]
        s_both = lax.dot_general(qcat, kb, dn_t, preferred_element_type=F32)
        out = []
        for c in range(2):
            s = s_both[c * tb:(c + 1) * tb]
            if biased:
                nb = min(nblocks, 2)
                plain = (nblocks - nb) * tb
                tail = s[:, plain:] + bt_ref[0, :, (2 - nb) * tb:]
                s = tail if plain == 0 else jnp.concatenate([s[:, :plain], tail], axis=1)
            s = s * LOG2E
            for i in range(nblocks):
                s_scr[c, blk + i] = s[:, i * tb:(i + 1) * tb]
            out.append(s)
        return out

    def lane_chunks(s):
        return [s[:, j * LANES:(j + 1) * LANES] for j in range(s.shape[1] // LANES)]

    def max_into(ms, ss):
        out = []
        for m, s in zip(ms, ss):
            for ch in lane_chunks(s):
                m = jnp.maximum(m, ch)
            out.append(m)
        return out

    m_init = jnp.full((tb, LANES), MASK_NEG, F32)

    @pl.when(qi == 0)
    def _():
        m_scr[0], m_scr[1] = max_into((m_init, m_init), scores(0, 1, True))

    for rest in range(4):
        @pl.when(jnp.logical_and(qi >= 1, n_rest == rest))
        def _():
            m_scr[0], m_scr[1] = max_into((m_init, m_init), scores(tail_blk, rest + 2, True))
            _attn_finalize(qstart - tb, accp_scr, sw_ref, o_ref, lam, lam_init)

    def quad_max(ki, ms):
        return tuple(max_into(ms, scores(4 * ki, 4, False)))

    ms = lax.fori_loop(0, n_quads, quad_max, (m_scr[0], m_scr[1]))
    def row_max():
        mb = [jnp.broadcast_to(jnp.max(ms[c], axis=-1, keepdims=True), (tb, LANES)) for c in range(2)]
        m_scr[0], m_scr[1] = mb
        return mb

    def accumulate(blk, nblocks, st, mb):
        vb = v1_scr[pl.ds(pl.multiple_of(blk * tb, tb), nblocks * tb), :]
        out = []
        for c in range(2):
            ps = []
            for i in range(nblocks):
                ps += [jnp.exp2(ch - mb[c]) for ch in lane_chunks(s_scr[c, blk + i])]
            p = jnp.concatenate(ps, axis=1).astype(BF16)
            out.append(st[c] + jnp.dot(p, vb, preferred_element_type=F32))
        return tuple(out)

    def store_state(st):
        acc_scr[0], acc_scr[1] = st

    za = jnp.zeros((tb, 2 * LANES), F32)

    @pl.when(qi == 0)
    def _():
        store_state(accumulate(0, 1, (za, za), row_max()))

    for rest in range(4):
        @pl.when(jnp.logical_and(qi >= 1, n_rest == rest))
        def _():
            store_state(accumulate(tail_blk, rest + 2, (za, za), row_max()))

    mb_q = (m_scr[0], m_scr[1])
    a0, a1 = lax.fori_loop(0, n_quads, lambda ki, st: accumulate(4 * ki, 4, st, mb_q),
                           (acc_scr[0], acc_scr[1]))

    accp_scr[0], accp_scr[1] = a0, a1


def _attn_finalize(row_start, accp_scr, sw_ref, o_ref, lam, lam_init):
    tb = accp_scr.shape[1]
    a0, a1 = accp_scr[0], accp_scr[1]
    o = a0[:, :LANES] / a0[:, LANES:] - lam * (a1[:, :LANES] / a1[:, LANES:])
    o = _rms(o, sw_ref[...], 1e-5) * (1.0 - lam_init)
    if not isinstance(row_start, int):
        row_start = pl.multiple_of(row_start, tb)
    o_ref[0, pl.ds(row_start, tb), :] = o.astype(o_ref.dtype)


def _attn_kernel(q_ref, k_ref, v_ref, bt_ref, dl_ref, sw_ref, o_ref, m_scr, acc_scr, s_scr, v1_scr, accp_scr,
                 *, lam_init):
    dl = dl_ref[...]
    lam = (jnp.exp(jnp.sum(dl[0:1, :] * dl[1:2, :], axis=-1, keepdims=True))
           - jnp.exp(jnp.sum(dl[2:3, :] * dl[3:4, :], axis=-1, keepdims=True)) + lam_init)
    v1_scr[:, :LANES] = v_ref[0]
    v1_scr[:, LANES:] = jnp.ones((v_ref.shape[1], LANES), BF16)

    def qblock(qi, carry):
        _attn_qblock(qi, q_ref, k_ref, v1_scr, bt_ref, sw_ref, o_ref, m_scr, acc_scr, s_scr, accp_scr, lam,
                     lam_init)
        return carry

    tb = s_scr.shape[2]
    nq = q_ref.shape[1] // tb
    lax.fori_loop(0, nq, qblock, 0)
    _attn_finalize((nq - 1) * tb, accp_scr, sw_ref, o_ref, lam, lam_init)


def _attn(proj3, btiles, da_lambda, da_subln, lam_init, tb, col_q, col_k, col_v):
    b, t, _ = proj3.shape
    width = DA_HEADS * LANES
    return pl.pallas_call(
        functools.partial(_attn_kernel, lam_init=lam_init),
        grid=(b, DA_HEADS),
        in_specs=[pl.BlockSpec((1, t, LANES), lambda bi, h: (bi, 0, col_q + h)),
                  pl.BlockSpec((1, t, LANES), lambda bi, h: (bi, 0, col_k + h)),
                  pl.BlockSpec((1, t, LANES), lambda bi, h: (bi, 0, col_v + h)),
                  pl.BlockSpec((1, tb, 2 * tb), lambda bi, h: (h, 0, 0)),
                  pl.BlockSpec((4, DA_HEAD_DIM), lambda bi, h: (0, 0)),
                  pl.BlockSpec((1, LANES), lambda bi, h: (0, 0))],
        out_specs=pl.BlockSpec((1, t, LANES), lambda bi, h: (bi, 0, h)),
        out_shape=jax.ShapeDtypeStruct((b, t, width), BF16),
        scratch_shapes=[pltpu.VMEM((2, tb, LANES), F32), pltpu.VMEM((2, tb, 2 * LANES), F32),
                        pltpu.VMEM((2, t // tb, tb, tb), F32), pltpu.VMEM((t, 2 * LANES), BF16),
                        pltpu.VMEM((2, tb, 2 * LANES), F32)],
        compiler_params=_cparams(("parallel", "parallel")),
        name="diff_attn",
    )(proj3, proj3, proj3, btiles, da_lambda, da_subln)


def _s5_kernel(u_ref, bc_ref, pw_ref, d_ref, o_ref, uf_scr, g_scr, hp_scr, y_scr,
               wg_scr, wk_scr, wct_scr):
    t_len = u_ref.shape[1]
    nj = t_len // S5_L
    pw = pw_ref[0]

    @pl.when(pl.program_id(1) == 0)
    def _():
        bd_re, bd_im, ct_re, ct_im = bc_ref[0, 0], bc_ref[0, 1], bc_ref[0, 2], bc_ref[0, 3]
        for s in range(S5_L):
            rows = slice(s * LANES, (s + 1) * LANES)
            qr, qi = pw[s:s + 1, :S5_HALF], pw[s:s + 1, S5_HALF:]
            wg_scr[rows, :S5_HALF] = (qr * bd_re - qi * bd_im).astype(BF16)
            wg_scr[rows, S5_HALF:] = (qr * bd_im + qi * bd_re).astype(BF16)
            er, ei = pw[8 + s:9 + s, :S5_HALF], pw[8 + s:9 + s, S5_HALF:]
            wct_scr[rows, :S5_HALF] = (ct_re * er - ct_im * ei).astype(BF16)
            wct_scr[rows, S5_HALF:] = (-(ct_re * ei + ct_im * er)).astype(BF16)
        dn_t = (((1,), (1,)), ((), ()))
        bcat = jnp.concatenate([bd_re, bd_im], axis=1).astype(BF16)
        crows = [jnp.concatenate([ct_re, -ct_im], axis=1).astype(BF16)]
        crows += [wct_scr[s * LANES:(s + 1) * LANES, :] for s in range(S5_L - 1)]
        kts = [lax.dot_general(bcat, c, dn_t, preferred_element_type=F32).astype(BF16) for c in crows]
        for s in range(S5_L):
            for r in range(S5_L):
                wk_scr[s * LANES:(s + 1) * LANES, r * LANES:(r + 1) * LANES] = (
                    kts[r - s] if r >= s else jnp.zeros((LANES, LANES), BF16))

    uf_scr[...] = u_ref[0].astype(F32)
    ur = jnp.concatenate([uf_scr[pl.ds(s, nj, stride=S5_L), :].astype(BF16) for s in range(S5_L)],
                         axis=1)
    g_scr[...] = jnp.dot(ur, wg_scr[...], preferred_element_type=F32)

    pr, pi = pw[24:32, :S5_HALF], pw[24:32, S5_HALF:]
    row = lax.broadcasted_iota(jnp.int32, (8, S5_HALF), 0)

    def local_scan(base):
        x = g_scr[pl.ds(base, 8), :]
        xr, xi = x[:, :S5_HALF], x[:, S5_HALF:]
        for lvl, s in enumerate((1, 2, 4)):
            ar, ai = pw[16 + lvl:17 + lvl, :S5_HALF], pw[16 + lvl:17 + lvl, S5_HALF:]
            keep = row >= s
            sr = jnp.where(keep, pltpu.roll(xr, s, 0), 0.0)
            si = jnp.where(keep, pltpu.roll(xi, s, 0), 0.0)
            xr, xi = xr + (ar * sr - ai * si), xi + (ar * si + ai * sr)
        return xr, xi

    first = row == 0
    per_iter = 2

    def groups(gi, carry):
        cr, ci = carry
        bases = [pl.multiple_of((gi * per_iter + k) * 8, 8) for k in range(per_iter)]
        scans = [local_scan(base) for base in bases]
        for base, (xr, xi) in zip(bases, scans):
            xr, xi = xr + (pr * cr - pi * ci), xi + (pr * ci + pi * cr)
            hp_scr[pl.ds(base, 8), :] = jnp.concatenate(
                [jnp.where(first, cr, pltpu.roll(xr, 1, 0)), jnp.where(first, ci, pltpu.roll(xi, 1, 0))],
                axis=1)
            cr, ci = xr[7:8, :], xi[7:8, :]
        return cr, ci

    z = jnp.zeros((1, S5_HALF), F32)
    lax.fori_loop(0, nj // (8 * per_iter), groups, (z, z))

    y = (jnp.dot(ur, wk_scr[...], preferred_element_type=F32)
         + lax.dot_general(hp_scr[...].astype(BF16), wct_scr[...], (((1,), (1,)), ((), ())),
                           preferred_element_type=F32))
    for s in range(S5_L):
        y_scr[pl.ds(s, nj, stride=S5_L), :] = y[:, s * LANES:(s + 1) * LANES]
    o_ref[0] = _gelu_tanh(y_scr[...] + d_ref[...] * uf_scr[...]).astype(o_ref.dtype)


def _s5(proj3, bc_all, pw_all, layer, dvec, col_u):
    b, t, _ = proj3.shape
    nblk = bc_all.shape[1]
    nj = t // S5_L
    wide = S5_L * LANES
    assert nj % 16 == 0
    return pl.pallas_call(
        _s5_kernel,
        grid=(nblk, b),
        in_specs=[pl.BlockSpec((1, t, LANES), lambda g, bi: (bi, 0, col_u + g)),
                  pl.BlockSpec((None, 1, 4, LANES, S5_HALF), lambda g, bi: (layer, g, 0, 0, 0)),
                  pl.BlockSpec((None, 1, 32, 2 * S5_HALF), lambda g, bi: (layer, g, 0, 0)),
                  pl.BlockSpec((1, LANES), lambda g, bi: (0, g))],
        out_specs=pl.BlockSpec((1, t, LANES), lambda g, bi: (bi, 0, g)),
        out_shape=jax.ShapeDtypeStruct((b, t, nblk * LANES), BF16),
        scratch_shapes=[pltpu.VMEM((t, LANES), F32), pltpu.VMEM((nj, 2 * S5_HALF), F32),
                        pltpu.VMEM((nj, 2 * S5_HALF), F32), pltpu.VMEM((t, LANES), F32),
                        pltpu.VMEM((wide, 2 * S5_HALF), BF16), pltpu.VMEM((wide, wide), BF16),
                        pltpu.VMEM((wide, 2 * S5_HALF), BF16)],
        compiler_params=_cparams(("parallel", "arbitrary")),
        name="s5",
    )(proj3, bc_all, pw_all, dvec)


def _merge_kernel(ya_ref, yb_ref, yc_ref, g0_ref, g1_ref, g2_ref, wg_ref, bg_ref, wb_ref, bgate_ref,
                  o_ref, yc_scr):
    @pl.when(pl.program_id(1) == 0)
    def _():
        yc = yc_ref[...]
        z = jnp.dot(yc, wg_ref[...], preferred_element_type=F32) + bg_ref[...]
        yc_scr[...] = (yc.astype(F32) * _sigmoid(z)).astype(BF16)

    bgate = bgate_ref[...]
    ys = (ya_ref[...], yb_ref[...], yc_scr[...])
    gs = (g0_ref, g1_ref, g2_ref)
    merged = None
    for br in range(N_BRANCH):
        gate = _sigmoid(gs[br][...].astype(F32) + bgate[br:br + 1, :])
        term = gate * jnp.dot(ys[br], wb_ref[br], preferred_element_type=F32)
        merged = term if merged is None else merged + term
    o_ref[...] = merged.astype(o_ref.dtype)


def _merge(ya, yb, yc, proj, w_glu_all, b_glu, w_branch_all, b_gate, col_g, layer, tm, tn):
    n, wdt = ya.shape
    d = w_branch_all.shape[3]
    gcol = [(col_g + br * d) // tn for br in range(N_BRANCH)]
    yspec = pl.BlockSpec((tm, wdt), lambda i, j: (i, 0))

    def gspec(br):
        return pl.BlockSpec((tm, tn), lambda i, j: (i, gcol[br] + j))

    return pl.pallas_call(
        _merge_kernel,
        grid=(n // tm, d // tn),
        in_specs=[yspec, yspec, yspec, gspec(0), gspec(1), gspec(2),
                  pl.BlockSpec((None, wdt, wdt), lambda i, j: (layer, 0, 0)),
                  pl.BlockSpec((1, wdt), lambda i, j: (0, 0)),
                  pl.BlockSpec((None, N_BRANCH, wdt, tn), lambda i, j: (layer, 0, 0, j)),
                  pl.BlockSpec((N_BRANCH, tn), lambda i, j: (0, j))],
        out_specs=pl.BlockSpec((tm, tn), lambda i, j: (i, j)),
        out_shape=jax.ShapeDtypeStruct((n, d), BF16),
        scratch_shapes=[pltpu.VMEM((tm, wdt), BF16)],
        compiler_params=_cparams(("parallel", "arbitrary")),
        name="merge",
    )(ya, yb, yc, proj, proj, proj, w_glu_all, b_glu, w_branch_all, b_gate)


def _outproj_kernel(m_ref, w_ref, xs_ref, nw_ref, o_ref):
    mix = jnp.dot(m_ref[...], w_ref[...], preferred_element_type=F32)
    o_ref[...] = xs_ref[...] + _rms(mix, nw_ref[...], 1e-6)


def _outproj(merged, w_out_all, layer, xs, nw, tm):
    n, d = xs.shape
    return pl.pallas_call(
        _outproj_kernel,
        grid=(n // tm,),
        in_specs=[pl.BlockSpec((tm, d), lambda i: (i, 0)),
                  pl.BlockSpec((None, d, d), lambda i: (layer, 0, 0)),
                  pl.BlockSpec((tm, d), lambda i: (i, 0)),
                  pl.BlockSpec((1, d), lambda i: (0, 0))],
        out_specs=pl.BlockSpec((tm, d), lambda i: (i, 0)),
        out_shape=jax.ShapeDtypeStruct((n, d), F32),
        compiler_params=_cparams(("parallel",)),
        name="outproj_residual",
    )(merged, w_out_all, xs, nw)


def _ffn_kernel(xs_ref, nw_in_ref, wg_ref, wu_ref, wo_ref, nw_out_ref, o_ref, h_scr, acc_scr):
    f = pl.program_id(1)

    @pl.when(f == 0)
    def _():
        h_scr[...] = _rms(xs_ref[...], nw_in_ref[...], 1e-6).astype(BF16)
        acc_scr[...] = jnp.zeros(acc_scr.shape, F32)

    h = h_scr[...]
    gate = jnp.dot(h, wg_ref[...], preferred_element_type=F32)
    up = jnp.dot(h, wu_ref[...], preferred_element_type=F32)
    act = (gate * _sigmoid(gate) * up).astype(BF16)
    acc_scr[...] += jnp.dot(act, wo_ref[...], preferred_element_type=F32)

    @pl.when(f == pl.num_programs(1) - 1)
    def _():
        o_ref[...] = xs_ref[...] + _rms(acc_scr[...], nw_out_ref[...], 1e-6)


def _ffn(xs, nw_in, w_ffn_in_all, w_ffn_out_all, nw_out, layer, tm, tf, real_rows=None):
    n, d = xs.shape
    d_ff = w_ffn_out_all.shape[1]
    nf = d_ff // tf
    if real_rows is None:
        n_out = n
        xs_spec = pl.BlockSpec((tm, d), lambda i, f: (i, 0))
    else:
        t_pad, first, count = real_rows
        per_seq = count // tm
        n_out = (n // t_pad) * count
        xs_spec = pl.BlockSpec((pl.Element(tm), pl.Element(d)),
                               lambda i, f: (pl.multiple_of(
                                   (i // per_seq) * t_pad + first + (i % per_seq) * tm, 8), 0))
        assert t_pad % 8 == 0 and first % 8 == 0 and tm % 8 == 0
    return pl.pallas_call(
        _ffn_kernel,
        grid=(n_out // tm, nf),
        in_specs=[xs_spec,
                  pl.BlockSpec((1, d), lambda i, f: (0, 0)),
                  pl.BlockSpec((None, d, tf), lambda i, f: (layer, 0, f)),
                  pl.BlockSpec((None, d, tf), lambda i, f: (layer, 0, nf + f)),
                  pl.BlockSpec((None, tf, d), lambda i, f: (layer, f, 0)),
                  pl.BlockSpec((1, d), lambda i, f: (0, 0))],
        out_specs=pl.BlockSpec((tm, d), lambda i, f: (i, 0)),
        out_shape=jax.ShapeDtypeStruct((n_out, d), F32),
        scratch_shapes=[pltpu.VMEM((tm, d), BF16), pltpu.VMEM((tm, d), F32)],
        compiler_params=_cparams(("parallel", "arbitrary")),
        name="swiglu_ffn",
    )(xs, nw_in, w_ffn_in_all, w_ffn_in_all, w_ffn_out_all, nw_out)


def _s5_tables(lam_re, lam_im, b_re, b_im, c_re, c_im, log_step):
    groups = lam_re.shape[0]
    nblk = groups // S5_GB
    lr, li = lam_re.astype(F32), lam_im.astype(F32)
    step = jnp.exp(log_step.astype(F32))[:, None]
    mag = jnp.exp(lr * step)
    ab_re, ab_im = mag * jnp.cos(li * step), mag * jnp.sin(li * step)
    den = lr * lr + li * li
    coef_re = ((ab_re - 1.0) * lr + ab_im * li) / den
    coef_im = (ab_im * lr - (ab_re - 1.0) * li) / den
    br, bi = b_re.astype(F32), b_im.astype(F32)
    bb_re = coef_re[..., None] * br - coef_im[..., None] * bi
    bb_im = coef_re[..., None] * bi + coef_im[..., None] * br
    cr, ci = c_re.astype(F32), c_im.astype(F32)
    eye = jnp.eye(S5_GB, dtype=F32)
    taus = jnp.arange(S5_L)
    lrs = (lr * step).reshape(nblk, 1, S5_HALF)
    lis = (li * step).reshape(nblk, 1, S5_HALF)

    def lam_pow(n):
        nn = n.astype(F32)[None, :, None]
        m = jnp.exp(nn * lrs)
        return m * jnp.cos(nn * lis), m * jnp.sin(nn * lis)

    def blockdiag(t):
        t = t.reshape(nblk, S5_GB, S5_GROUP, S5_STATE)
        return jnp.einsum("ngcp,gh->ngchp", t, eye).reshape(nblk, LANES, S5_HALF)

    bc = jnp.stack([blockdiag(bb_re.transpose(0, 2, 1)), blockdiag(bb_im.transpose(0, 2, 1)),
                    blockdiag(cr), blockdiag(ci)], axis=1)

    n_list = jnp.concatenate([S5_L - 1 - taus, 1 + taus, S5_L * jnp.array([1, 2, 4, 0, 0, 0, 0, 0]),
                              S5_L * (1 + jnp.arange(8))])
    wr, wi = lam_pow(n_list)
    pw = jnp.concatenate([wr, wi], axis=2)
    return bc, pw


def kernel(x, meta, rel_bias, norm_w, w_in, conv_w, conv_b, lru_w_a, lru_b_a, lru_w_x, lru_b_x, lru_lambda, da_lambda, da_subln, s5_lam_re, s5_lam_im, s5_b_re, s5_b_im, s5_c_re, s5_c_im, s5_d, s5_log_step, s5_w_glu, s5_b_glu, b_gate, w_branch, w_out, w_ffn_in, w_ffn_out):
    bsz, seq, d_model = x.shape
    depth = w_in.shape[0]
    lru_w = conv_w.shape[2]
    s5_w = s5_d.shape[1]
    qk_w = DA_HEADS * 2 * DA_HEAD_DIM
    assert lru_w == LRU_BLOCKS * LANES and s5_w % LANES == 0 and d_model % LANES == 0
    col_gate, col_x = 0, lru_w // LANES
    col_q = 2 * lru_w // LANES
    col_k = col_q + qk_w // LANES
    col_v = col_k + qk_w // LANES
    col_u = col_v + DA_HEADS
    col_g = (col_u + s5_w // LANES) * LANES

    t_real = N_META + seq
    tb = 384 if t_real >= 1024 else 128
    t_pad = -(-t_real // tb) * tb
    assert t_pad % SCAN_CH == 0
    n_tok = bsz * t_pad
    tm_in = _pick_tile(n_tok, (1408, 1056, 768, 512, 384, 256, 128))
    tm_merge = _pick_tile(n_tok, (1056, 768, 512, 384, 256, 128))
    tm_ffn = _pick_tile(n_tok, (768, 512, 384, 256, 128))
    tm_out = _pick_tile(n_tok, (512, 384, 256, 128))
    tm_last = _pick_tile(seq, (512, 256, 128))

    xs = jnp.concatenate([jnp.broadcast_to(meta.astype(F32)[None], (bsz, N_META, d_model)), x,
                          jnp.zeros((bsz, t_pad - t_real, d_model), F32)], axis=1)
    xs = xs.reshape(n_tok, d_model)
    btiles = _bias_tiles(rel_bias.astype(F32), tb)

    w_in_b, w_glu_b, w_branch_b = w_in.astype(BF16), s5_w_glu.astype(BF16), w_branch.astype(BF16)
    w_out_b, w_ffn_in_b, w_ffn_out_b = w_out.astype(BF16), w_ffn_in.astype(BF16), w_ffn_out.astype(BF16)

    lru_p_all = jnp.concatenate([conv_w, conv_b[:, None], lru_b_a[:, None], lru_b_x[:, None],
                                 jax.nn.log_sigmoid(lru_lambda.astype(F32))[:, None]], axis=1)
    w_ax_all = jnp.concatenate([lru_w_a, lru_w_x], axis=3).astype(BF16)
    bc_all, pw_all = jax.vmap(_s5_tables)(s5_lam_re, s5_lam_im, s5_b_re, s5_b_im, s5_c_re, s5_c_im,
                                          s5_log_step)

    for l in range(depth):
        lam_init = 0.8 - 0.6 * math.exp(-0.3 * l)
        proj = _inproj(xs, norm_w[l, 0][None], w_in_b, l, tm_in, 1024)
        proj3 = proj.reshape(bsz, t_pad, proj.shape[1])

        y_a = _lru(proj3, lru_p_all[l], w_ax_all[l], col_gate, col_x)

        y_b = _attn(proj3, btiles, da_lambda[l], da_subln[l][None], lam_init, tb, col_q, col_k, col_v)

        y_c = _s5(proj3, bc_all, pw_all, l, s5_d[l][None], col_u)

        merged = _merge(y_a.reshape(n_tok, lru_w), y_b.reshape(n_tok, -1), y_c.reshape(n_tok, s5_w),
                        proj, w_glu_b, s5_b_glu[l][None], w_branch_b, b_gate[l], col_g, l, tm_merge, 512)
        xs = _outproj(merged, w_out_b, l, xs, norm_w[l, 1][None], tm_out)
        if l + 1 < depth:
            xs = _ffn(xs, norm_w[l, 2][None], w_ffn_in_b, w_ffn_out_b, norm_w[l, 3][None], l, tm_ffn, 512)
        else:
            out = _ffn(xs, norm_w[l, 2][None], w_ffn_in_b, w_ffn_out_b, norm_w[l, 3][None], l, tm_last,
                       512, real_rows=(t_pad, N_META, seq))

    return out.reshape(bsz, seq, d_model)
```

```python
import functools
import math

import jax
import jax.numpy as jnp
from jax import lax
from jax.experimental import pallas as pl
from jax.experimental.pallas import tpu as pltpu

F32 = jnp.float32
BF16 = jnp.bfloat16

N_META = 16
CONV_W = 4
LRU_C = 8.0
LRU_BLOCKS = 8
DA_HEADS = 8
DA_HEAD_DIM = 64
S5_GROUP = 16
S5_STATE = 64
REL_BUCKETS = 32
REL_MAX_DIST = 128
N_BRANCH = 3

LANES = 128
S5_GB = LANES // S5_GROUP
S5_HALF = S5_GB * S5_STATE
S5_L = 8
SCAN_CH = 128
MASK_NEG = -1e30
LOG2E = math.log2(math.e)
VMEM_LIMIT = 56 * 1024 * 1024


def _cparams(sem):
    return pltpu.CompilerParams(dimension_semantics=sem, vmem_limit_bytes=VMEM_LIMIT)


def _pick_tile(n, candidates):
    for c in candidates:
        if n % c == 0:
            return c
    raise ValueError(f"no tile in {candidates} divides {n}")


def _gelu_tanh(x):
    k = math.sqrt(2.0 / math.pi)
    half = 0.5 * x
    return half + half * jnp.tanh(x * (k + (k * 0.044715) * (x * x)))


def _sigmoid(x):
    return 1.0 / (1.0 + jnp.exp2(x * (-LOG2E)))


def _rms(x, w, eps):
    return (x * lax.rsqrt(jnp.mean(x * x, axis=-1, keepdims=True) + eps)) * w


def _inproj_kernel(x_ref, nw_ref, w_ref, o_ref, h_scr):
    @pl.when(pl.program_id(1) == 0)
    def _():
        h_scr[...] = _rms(x_ref[...], nw_ref[...], 1e-6).astype(BF16)

    o_ref[...] = jnp.dot(h_scr[...], w_ref[...], preferred_element_type=F32).astype(o_ref.dtype)


def _inproj(xs, nw, w_all, layer, tm, tn):
    n, d = xs.shape
    n_out = w_all.shape[2]
    return pl.pallas_call(
        _inproj_kernel,
        grid=(n // tm, n_out // tn),
        in_specs=[pl.BlockSpec((tm, d), lambda i, j: (i, 0)),
                  pl.BlockSpec((1, d), lambda i, j: (0, 0)),
                  pl.BlockSpec((None, d, tn), lambda i, j: (layer, 0, j))],
        out_specs=pl.BlockSpec((tm, tn), lambda i, j: (i, j)),
        out_shape=jax.ShapeDtypeStruct((n, n_out), BF16),
        scratch_shapes=[pltpu.VMEM((tm, d), BF16)],
        compiler_params=_cparams(("parallel", "arbitrary")),
        name="inproj",
    )(xs, nw, w_all)


def _lru_kernel(g_ref, x_ref, p_ref, w_ref, o_ref, xs_scr):
    t_len = x_ref.shape[1]
    halo = 8
    xs_scr[0:halo, :] = jnp.zeros((halo, LANES), F32)
    xs_scr[halo:, :] = x_ref[0].astype(F32)
    p = p_ref[...]
    cw = [p[j:j + 1, :] for j in range(CONV_W)]
    cb, ba, bx = p[4:5, :], p[5:6, :], p[6:7, :]
    c_log2sig = (LRU_C * LOG2E) * p[7:8, :]
    w = w_ref[0]
    row = lax.broadcasted_iota(jnp.int32, (SCAN_CH, LANES), 0)

    def local_scan(base):
        xc = cb
        for j in range(CONV_W):
            xc = xc + cw[j] * xs_scr[pl.ds(base + halo - (CONV_W - 1) + j, SCAN_CH), :]
        ri = jnp.dot(xc.astype(BF16), w, preferred_element_type=F32)
        r = _sigmoid(ri[:, :LANES] + ba)
        i = _sigmoid(ri[:, LANES:] + bx)
        a = jnp.exp2(r * c_log2sig)
        b = jnp.sqrt(1.0 - a * a) * (i * xc)
        s = 1
        while s < SCAN_CH:
            keep = row >= s
            a_sh = jnp.where(keep, pltpu.roll(a, s, 0), 1.0)
            b_sh = jnp.where(keep, pltpu.roll(b, s, 0), 0.0)
            b = a * b_sh + b
            a = a * a_sh
            s *= 2
        return a, b

    n_chunks = t_len // SCAN_CH
    sub = next(c for c in (11, 3, 1) if n_chunks % c == 0)

    def chunk(c, h0):
        bases = [pl.multiple_of((c * sub + j) * SCAN_CH, SCAN_CH) for j in range(sub)]
        scans = [local_scan(base) for base in bases]
        for base, (a, b) in zip(bases, scans):
            h = a * h0 + b
            g = g_ref[0, pl.ds(base, SCAN_CH), :].astype(F32)
            o_ref[0, pl.ds(base, SCAN_CH), :] = (h * _gelu_tanh(g)).astype(o_ref.dtype)
            h0 = h[SCAN_CH - 1:SCAN_CH, :]
        return h0

    lax.fori_loop(0, n_chunks // sub, chunk, jnp.zeros((1, LANES), F32))


def _lru(proj3, lru_p, w_ax, col_gate, col_x):
    b, t, _ = proj3.shape
    width = LRU_BLOCKS * LANES
    return pl.pallas_call(
        _lru_kernel,
        grid=(b, LRU_BLOCKS),
        in_specs=[pl.BlockSpec((1, t, LANES), lambda bi, h: (bi, 0, col_gate + h)),
                  pl.BlockSpec((1, t, LANES), lambda bi, h: (bi, 0, col_x + h)),
                  pl.BlockSpec((8, LANES), lambda bi, h: (0, h)),
                  pl.BlockSpec((1, LANES, 2 * LANES), lambda bi, h: (h, 0, 0))],
        out_specs=pl.BlockSpec((1, t, LANES), lambda bi, h: (bi, 0, h)),
        out_shape=jax.ShapeDtypeStruct((b, t, width), BF16),
        scratch_shapes=[pltpu.VMEM((t + 8, LANES), F32)],
        compiler_params=_cparams(("parallel", "parallel")),
        name="rglru",
    )(proj3, proj3, lru_p, w_ax)


def _bias_tile_kernel(rb_ref, o_ref):
    h = pl.program_id(0)
    tb = o_ref.shape[1]
    nb = tb // LANES
    i = lax.broadcasted_iota(jnp.int32, (LANES, LANES), 0)
    j = lax.broadcasted_iota(jnp.int32, (LANES, LANES), 1)
    max_exact = REL_BUCKETS // 2

    def pattern(delta):
        d = i - j + delta * LANES
        n = jnp.maximum(d, 0)
        nf = jnp.maximum(n, 1).astype(F32)
        large = max_exact + (jnp.log(nf / max_exact) / math.log(REL_MAX_DIST / max_exact)
                             * (REL_BUCKETS - max_exact)).astype(jnp.int32)
        large = jnp.minimum(large, REL_BUCKETS - 1)
        bucket = jnp.where(n < max_exact, n, large)
        val = jnp.zeros((LANES, LANES), F32)
        for bkt in range(REL_BUCKETS):
            val = jnp.where(bucket == bkt, rb_ref[bkt, h], val)
        val = val - rb_ref[REL_BUCKETS - 1, h]
        return jnp.where(d >= 0, val, MASK_NEG)

    pieces = {0: pattern(0), 1: pattern(1)}
    for ri in range(nb):
        for cj in range(2 * nb):
            delta = ri - cj + nb
            if delta in pieces:
                piece = pieces[delta]
            else:
                piece = jnp.full((LANES, LANES), 0.0 if delta >= 2 else MASK_NEG, F32)
            o_ref[0, ri * LANES:(ri + 1) * LANES, cj * LANES:(cj + 1) * LANES] = piece


def _bias_tiles(rel_bias, tb):
    return pl.pallas_call(
        _bias_tile_kernel,
        grid=(DA_HEADS,),
        in_specs=[pl.BlockSpec(memory_space=pltpu.SMEM)],
        out_specs=pl.BlockSpec((1, tb, 2 * tb), lambda h: (h, 0, 0)),
        out_shape=jax.ShapeDtypeStruct((DA_HEADS, tb, 2 * tb), F32),
        compiler_params=_cparams(("parallel",)),
        name="t5_bias_tiles",
    )(rel_bias)


def _attn_qblock(qi, q_ref, k_ref, v1_scr, bt_ref, sw_ref, o_ref, m_scr, acc_scr, s_scr, accp_scr, lam,
                 lam_init):
    tb = s_scr.shape[2]
    dn_t = (((1,), (1,)), ((), ()))
    lane = lax.broadcasted_iota(jnp.int32, (tb, LANES), 1)
    qstart = pl.multiple_of(qi * tb, tb)
    qs = q_ref[0, pl.ds(qstart, tb), :] * (DA_HEAD_DIM ** -0.5)
    zero = jnp.zeros_like(qs)
    qcat = jnp.concatenate([jnp.where(lane < DA_HEAD_DIM, qs, zero),
                            jnp.where(lane >= DA_HEAD_DIM, qs, zero)], axis=0)
    n_far = jnp.maximum(qi - 1, 0)
    n_quads = n_far // 4
    n_rest = n_far % 4
    tail_blk = 4 * n_quads

    def scores(blk, nblocks, biased):
        kb = k_ref[0, pl.ds(pl.multiple_of(blk * tb, tb), nblocks * tb), :]
        s_both = lax.dot_general(qcat, kb, dn_t, preferred_element_type=F32)
        out = []
        for c in range(2):
            s = s_both[c * tb:(c + 1) * tb]
            if biased:
                nb = min(nblocks, 2)
                plain = (nblocks - nb) * tb
                tail = s[:, plain:] + bt_ref[0, :, (2 - nb) * tb:]
                s = tail if plain == 0 else jnp.concatenate([s[:, :plain], tail], axis=1)
            s = s * LOG2E
            for i in range(nblocks):
                s_scr[c, blk + i] = s[:, i * tb:(i + 1) * tb]
            out.append(s)
        return out

    def lane_chunks(s):
        return [s[:, j * LANES:(j + 1) * LANES] for j in range(s.shape[1] // LANES)]

    def max_into(ms, ss):
        out = []
        for m, s in zip(ms, ss):
            for ch in lane_chunks(s):
                m = jnp.maximum(m, ch)
            out.append(m)
        return out

    m_init = jnp.full((tb, LANES), MASK_NEG, F32)

    @pl.when(qi == 0)
    def _():
        m_scr[0], m_scr[1] = max_into((m_init, m_init), scores(0, 1, True))

    for rest in range(4):
        @pl.when(jnp.logical_and(qi >= 1, n_rest == rest))
        def _():
            m_scr[0], m_scr[1] = max_into((m_init, m_init), scores(tail_blk, rest + 2, True))
            _attn_finalize(qstart - tb, accp_scr, sw_ref, o_ref, lam, lam_init)

    def quad_max(ki, ms):
        return tuple(max_into(ms, scores(4 * ki, 4, False)))

    ms = lax.fori_loop(0, n_quads, quad_max, (m_scr[0], m_scr[1]))
    def row_max():
        mb = [jnp.broadcast_to(jnp.max(ms[c], axis=-1, keepdims=True), (tb, LANES)) for c in range(2)]
        m_scr[0], m_scr[1] = mb
        return mb

    def accumulate(blk, nblocks, st, mb):
        vb = v1_scr[pl.ds(pl.multiple_of(blk * tb, tb), nblocks * tb), :]
        out = []
        for c in range(2):
            ps = []
            for i in range(nblocks):
                ps += [jnp.exp2(ch - mb[c]) for ch in lane_chunks(s_scr[c, blk + i])]
            p = jnp.concatenate(ps, axis=1).astype(BF16)
            out.append(st[c] + jnp.dot(p, vb, preferred_element_type=F32))
        return tuple(out)

    def store_state(st):
        acc_scr[0], acc_scr[1] = st

    za = jnp.zeros((tb, 2 * LANES), F32)

    @pl.when(qi == 0)
    def _():
        store_state(accumulate(0, 1, (za, za), row_max()))

    for rest in range(4):
        @pl.when(jnp.logical_and(qi >= 1, n_rest == rest))
        def _():
            store_state(accumulate(tail_blk, rest + 2, (za, za), row_max()))

    mb_q = (m_scr[0], m_scr[1])
    a0, a1 = lax.fori_loop(0, n_quads, lambda ki, st: accumulate(4 * ki, 4, st, mb_q),
                           (acc_scr[0], acc_scr[1]))

    accp_scr[0], accp_scr[1] = a0, a1


def _attn_finalize(row_start, accp_scr, sw_ref, o_ref, lam, lam_init):
    tb = accp_scr.shape[1]
    a0, a1 = accp_scr[0], accp_scr[1]
    o = a0[:, :LANES] / a0[:, LANES:] - lam * (a1[:, :LANES] / a1[:, LANES:])
    o = _rms(o, sw_ref[...], 1e-5) * (1.0 - lam_init)
    if not isinstance(row_start, int):
        row_start = pl.multiple_of(row_start, tb)
    o_ref[0, pl.ds(row_start, tb), :] = o.astype(o_ref.dtype)


def _attn_kernel(q_ref, k_ref, v_ref, bt_ref, dl_ref, sw_ref, o_ref, m_scr, acc_scr, s_scr, v1_scr, accp_scr,
                 *, lam_init):
    dl = dl_ref[...]
    lam = (jnp.exp(jnp.sum(dl[0:1, :] * dl[1:2, :], axis=-1, keepdims=True))
           - jnp.exp(jnp.sum(dl[2:3, :] * dl[3:4, :], axis=-1, keepdims=True)) + lam_init)
    v1_scr[:, :LANES] = v_ref[0]
    v1_scr[:, LANES:] = jnp.ones((v_ref.shape[1], LANES), BF16)

    def qblock(qi, carry):
        _attn_qblock(qi, q_ref, k_ref, v1_scr, bt_ref, sw_ref, o_ref, m_scr, acc_scr, s_scr, accp_scr, lam,
                     lam_init)
        return carry

    tb = s_scr.shape[2]
    nq = q_ref.shape[1] // tb
    lax.fori_loop(0, nq, qblock, 0)
    _attn_finalize((nq - 1) * tb, accp_scr, sw_ref, o_ref, lam, lam_init)


def _attn(proj3, btiles, da_lambda, da_subln, lam_init, tb, col_q, col_k, col_v):
    b, t, _ = proj3.shape
    width = DA_HEADS * LANES
    return pl.pallas_call(
        functools.partial(_attn_kernel, lam_init=lam_init),
        grid=(b, DA_HEADS),
        in_specs=[pl.BlockSpec((1, t, LANES), lambda bi, h: (bi, 0, col_q + h)),
                  pl.BlockSpec((1, t, LANES), lambda bi, h: (bi, 0, col_k + h)),
                  pl.BlockSpec((1, t, LANES), lambda bi, h: (bi, 0, col_v + h)),
                  pl.BlockSpec((1, tb, 2 * tb), lambda bi, h: (h, 0, 0)),
                  pl.BlockSpec((4, DA_HEAD_DIM), lambda bi, h: (0, 0)),
                  pl.BlockSpec((1, LANES), lambda bi, h: (0, 0))],
        out_specs=pl.BlockSpec((1, t, LANES), lambda bi, h: (bi, 0, h)),
        out_shape=jax.ShapeDtypeStruct((b, t, width), BF16),
        scratch_shapes=[pltpu.VMEM((2, tb, LANES), F32), pltpu.VMEM((2, tb, 2 * LANES), F32),
                        pltpu.VMEM((2, t // tb, tb, tb), F32), pltpu.VMEM((t, 2 * LANES), BF16),
                        pltpu.VMEM((2, tb, 2 * LANES), F32)],
        compiler_params=_cparams(("parallel", "parallel")),
        name="diff_attn",
    )(proj3, proj3, proj3, btiles, da_lambda, da_subln)


def _s5_kernel(u_ref, bc_ref, pw_ref, d_ref, o_ref, uf_scr, g_scr, hp_scr, y_scr,
               wg_scr, wk_scr, wct_scr):
    t_len = u_ref.shape[1]
    nj = t_len // S5_L
    pw = pw_ref[0]

    @pl.when(pl.program_id(1) == 0)
    def _():
        bd_re, bd_im, ct_re, ct_im = bc_ref[0, 0], bc_ref[0, 1], bc_ref[0, 2], bc_ref[0, 3]
        for s in range(S5_L):
            rows = slice(s * LANES, (s + 1) * LANES)
            qr, qi = pw[s:s + 1, :S5_HALF], pw[s:s + 1, S5_HALF:]
            wg_scr[rows, :S5_HALF] = (qr * bd_re - qi * bd_im).astype(BF16)
            wg_scr[rows, S5_HALF:] = (qr * bd_im + qi * bd_re).astype(BF16)
            er, ei = pw[8 + s:9 + s, :S5_HALF], pw[8 + s:9 + s, S5_HALF:]
            wct_scr[rows, :S5_HALF] = (ct_re * er - ct_im * ei).astype(BF16)
            wct_scr[rows, S5_HALF:] = (-(ct_re * ei + ct_im * er)).astype(BF16)
        dn_t = (((1,), (1,)), ((), ()))
        bcat = jnp.concatenate([bd_re, bd_im], axis=1).astype(BF16)
        crows = [jnp.concatenate([ct_re, -ct_im], axis=1).astype(BF16)]
        crows += [wct_scr[s * LANES:(s + 1) * LANES, :] for s in range(S5_L - 1)]
        kts = [lax.dot_general(bcat, c, dn_t, preferred_element_type=F32).astype(BF16) for c in crows]
        for s in range(S5_L):
            for r in range(S5_L):
                wk_scr[s * LANES:(s + 1) * LANES, r * LANES:(r + 1) * LANES] = (
                    kts[r - s] if r >= s else jnp.zeros((LANES, LANES), BF16))

    uf_scr[...] = u_ref[0].astype(F32)
    ur = jnp.concatenate([uf_scr[pl.ds(s, nj, stride=S5_L), :].astype(BF16) for s in range(S5_L)],
                         axis=1)
    g_scr[...] = jnp.dot(ur, wg_scr[...], preferred_element_type=F32)

    pr, pi = pw[24:32, :S5_HALF], pw[24:32, S5_HALF:]
    row = lax.broadcasted_iota(jnp.int32, (8, S5_HALF), 0)

    def local_scan(base):
        x = g_scr[pl.ds(base, 8), :]
        xr, xi = x[:, :S5_HALF], x[:, S5_HALF:]
        for lvl, s in enumerate((1, 2, 4)):
            ar, ai = pw[16 + lvl:17 + lvl, :S5_HALF], pw[16 + lvl:17 + lvl, S5_HALF:]
            keep = row >= s
            sr = jnp.where(keep, pltpu.roll(xr, s, 0), 0.0)
            si = jnp.where(keep, pltpu.roll(xi, s, 0), 0.0)
            xr, xi = xr + (ar * sr - ai * si), xi + (ar * si + ai * sr)
        return xr, xi

    first = row == 0
    per_iter = 2

    def groups(gi, carry):
        cr, ci = carry
        bases = [pl.multiple_of((gi * per_iter + k) * 8, 8) for k in range(per_iter)]
        scans = [local_scan(base) for base in bases]
        for base, (xr, xi) in zip(bases, scans):
            xr, xi = xr + (pr * cr - pi * ci), xi + (pr * ci + pi * cr)
            hp_scr[pl.ds(base, 8), :] = jnp.concatenate(
                [jnp.where(first, cr, pltpu.roll(xr, 1, 0)), jnp.where(first, ci, pltpu.roll(xi, 1, 0))],
                axis=1)
            cr, ci = xr[7:8, :], xi[7:8, :]
        return cr, ci

    y_in = jnp.dot(ur, wk_scr[...], preferred_element_type=F32)
    z = jnp.zeros((1, S5_HALF), F32)
    lax.fori_loop(0, nj // (8 * per_iter), groups, (z, z), unroll=True)

    y = y_in + lax.dot_general(hp_scr[...].astype(BF16), wct_scr[...], (((1,), (1,)), ((), ())),
                               preferred_element_type=F32)
    for s in range(S5_L):
        y_scr[pl.ds(s, nj, stride=S5_L), :] = y[:, s * LANES:(s + 1) * LANES]
    o_ref[0] = _gelu_tanh(y_scr[...] + d_ref[...] * uf_scr[...]).astype(o_ref.dtype)


def _s5(proj3, bc_all, pw_all, layer, dvec, col_u):
    b, t, _ = proj3.shape
    nblk = bc_all.shape[1]
    nj = t // S5_L
    wide = S5_L * LANES
    assert nj % 16 == 0
    return pl.pallas_call(
        _s5_kernel,
        grid=(nblk, b),
        in_specs=[pl.BlockSpec((1, t, LANES), lambda g, bi: (bi, 0, col_u + g)),
                  pl.BlockSpec((None, 1, 4, LANES, S5_HALF), lambda g, bi: (layer, g, 0, 0, 0)),
                  pl.BlockSpec((None, 1, 32, 2 * S5_HALF), lambda g, bi: (layer, g, 0, 0)),
                  pl.BlockSpec((1, LANES), lambda g, bi: (0, g))],
        out_specs=pl.BlockSpec((1, t, LANES), lambda g, bi: (bi, 0, g)),
        out_shape=jax.ShapeDtypeStruct((b, t, nblk * LANES), BF16),
        scratch_shapes=[pltpu.VMEM((t, LANES), F32), pltpu.VMEM((nj, 2 * S5_HALF), F32),
                        pltpu.VMEM((nj, 2 * S5_HALF), F32), pltpu.VMEM((t, LANES), F32),
                        pltpu.VMEM((wide, 2 * S5_HALF), BF16), pltpu.VMEM((wide, wide), BF16),
                        pltpu.VMEM((wide, 2 * S5_HALF), BF16)],
        compiler_params=_cparams(("parallel", "arbitrary")),
        name="s5",
    )(proj3, bc_all, pw_all, dvec)


def _merge_kernel(ya_ref, yb_ref, yc_ref, g0_ref, g1_ref, g2_ref, wg_ref, bg_ref, wb_ref, bgate_ref,
                  o_ref, yc_scr):
    @pl.when(pl.program_id(1) == 0)
    def _():
        yc = yc_ref[...]
        z = jnp.dot(yc, wg_ref[...], preferred_element_type=F32) + bg_ref[...]
        yc_scr[...] = (yc.astype(F32) * _sigmoid(z)).astype(BF16)

    bgate = bgate_ref[...]
    ys = (ya_ref[...], yb_ref[...], yc_scr[...])
    gs = (g0_ref, g1_ref, g2_ref)
    merged = None
    for br in range(N_BRANCH):
        gate = _sigmoid(gs[br][...].astype(F32) + bgate[br:br + 1, :])
        term = gate * jnp.dot(ys[br], wb_ref[br], preferred_element_type=F32)
        merged = term if merged is None else merged + term
    o_ref[...] = merged.astype(o_ref.dtype)


def _merge(ya, yb, yc, proj, w_glu_all, b_glu, w_branch_all, b_gate, col_g, layer, tm, tn):
    n, wdt = ya.shape
    d = w_branch_all.shape[3]
    gcol = [(col_g + br * d) // tn for br in range(N_BRANCH)]
    yspec = pl.BlockSpec((tm, wdt), lambda i, j: (i, 0))

    def gspec(br):
        return pl.BlockSpec((tm, tn), lambda i, j: (i, gcol[br] + j))

    return pl.pallas_call(
        _merge_kernel,
        grid=(n // tm, d // tn),
        in_specs=[yspec, yspec, yspec, gspec(0), gspec(1), gspec(2),
                  pl.BlockSpec((None, wdt, wdt), lambda i, j: (layer, 0, 0)),
                  pl.BlockSpec((1, wdt), lambda i, j: (0, 0)),
                  pl.BlockSpec((None, N_BRANCH, wdt, tn), lambda i, j: (layer, 0, 0, j)),
                  pl.BlockSpec((N_BRANCH, tn), lambda i, j: (0, j))],
        out_specs=pl.BlockSpec((tm, tn), lambda i, j: (i, j)),
        out_shape=jax.ShapeDtypeStruct((n, d), BF16),
        scratch_shapes=[pltpu.VMEM((tm, wdt), BF16)],
        compiler_params=_cparams(("parallel", "arbitrary")),
        name="merge",
    )(ya, yb, yc, proj, proj, proj, w_glu_all, b_glu, w_branch_all, b_gate)


def _outproj_kernel(m_ref, w_ref, xs_ref, nw_ref, o_ref):
    mix = jnp.dot(m_ref[...], w_ref[...], preferred_element_type=F32)
    o_ref[...] = xs_ref[...] + _rms(mix, nw_ref[...], 1e-6)


def _outproj(merged, w_out_all, layer, xs, nw, tm):
    n, d = xs.shape
    return pl.pallas_call(
        _outproj_kernel,
        grid=(n // tm,),
        in_specs=[pl.BlockSpec((tm, d), lambda i: (i, 0)),
                  pl.BlockSpec((None, d, d), lambda i: (layer, 0, 0)),
                  pl.BlockSpec((tm, d), lambda i: (i, 0)),
                  pl.BlockSpec((1, d), lambda i: (0, 0))],
        out_specs=pl.BlockSpec((tm, d), lambda i: (i, 0)),
        out_shape=jax.ShapeDtypeStruct((n, d), F32),
        compiler_params=_cparams(("parallel",)),
        name="outproj_residual",
    )(merged, w_out_all, xs, nw)


def _ffn_kernel(xs_ref, nw_in_ref, wg_ref, wu_ref, wo_ref, nw_out_ref, o_ref, h_scr, acc_scr):
    f = pl.program_id(1)

    @pl.when(f == 0)
    def _():
        h_scr[...] = _rms(xs_ref[...], nw_in_ref[...], 1e-6).astype(BF16)
        acc_scr[...] = jnp.zeros(acc_scr.shape, F32)

    h = h_scr[...]
    gate = jnp.dot(h, wg_ref[...], preferred_element_type=F32)
    up = jnp.dot(h, wu_ref[...], preferred_element_type=F32)
    act = (gate * _sigmoid(gate) * up).astype(BF16)
    acc_scr[...] += jnp.dot(act, wo_ref[...], preferred_element_type=F32)

    @pl.when(f == pl.num_programs(1) - 1)
    def _():
        o_ref[...] = xs_ref[...] + _rms(acc_scr[...], nw_out_ref[...], 1e-6)


def _ffn(xs, nw_in, w_ffn_in_all, w_ffn_out_all, nw_out, layer, tm, tf, real_rows=None):
    n, d = xs.shape
    d_ff = w_ffn_out_all.shape[1]
    nf = d_ff // tf
    if real_rows is None:
        n_out = n
        xs_spec = pl.BlockSpec((tm, d), lambda i, f: (i, 0))
    else:
        t_pad, first, count = real_rows
        per_seq = count // tm
        n_out = (n // t_pad) * count
        xs_spec = pl.BlockSpec((pl.Element(tm), pl.Element(d)),
                               lambda i, f: (pl.multiple_of(
                                   (i // per_seq) * t_pad + first + (i % per_seq) * tm, 8), 0))
        assert t_pad % 8 == 0 and first % 8 == 0 and tm % 8 == 0
    return pl.pallas_call(
        _ffn_kernel,
        grid=(n_out // tm, nf),
        in_specs=[xs_spec,
                  pl.BlockSpec((1, d), lambda i, f: (0, 0)),
                  pl.BlockSpec((None, d, tf), lambda i, f: (layer, 0, f)),
                  pl.BlockSpec((None, d, tf), lambda i, f: (layer, 0, nf + f)),
                  pl.BlockSpec((None, tf, d), lambda i, f: (layer, f, 0)),
                  pl.BlockSpec((1, d), lambda i, f: (0, 0))],
        out_specs=pl.BlockSpec((tm, d), lambda i, f: (i, 0)),
        out_shape=jax.ShapeDtypeStruct((n_out, d), F32),
        scratch_shapes=[pltpu.VMEM((tm, d), BF16), pltpu.VMEM((tm, d), F32)],
        compiler_params=_cparams(("parallel", "arbitrary")),
        name="swiglu_ffn",
    )(xs, nw_in, w_ffn_in_all, w_ffn_in_all, w_ffn_out_all, nw_out)


def _s5_tables(lam_re, lam_im, b_re, b_im, c_re, c_im, log_step):
    groups = lam_re.shape[0]
    nblk = groups // S5_GB
    lr, li = lam_re.astype(F32), lam_im.astype(F32)
    step = jnp.exp(log_step.astype(F32))[:, None]
    mag = jnp.exp(lr * step)
    ab_re, ab_im = mag * jnp.cos(li * step), mag * jnp.sin(li * step)
    den = lr * lr + li * li
    coef_re = ((ab_re - 1.0) * lr + ab_im * li) / den
    coef_im = (ab_im * lr - (ab_re - 1.0) * li) / den
    br, bi = b_re.astype(F32), b_im.astype(F32)
    bb_re = coef_re[..., None] * br - coef_im[..., None] * bi
    bb_im = coef_re[..., None] * bi + coef_im[..., None] * br
    cr, ci = c_re.astype(F32), c_im.astype(F32)
    eye = jnp.eye(S5_GB, dtype=F32)
    taus = jnp.arange(S5_L)
    lrs = (lr * step).reshape(nblk, 1, S5_HALF)
    lis = (li * step).reshape(nblk, 1, S5_HALF)

    def lam_pow(n):
        nn = n.astype(F32)[None, :, None]
        m = jnp.exp(nn * lrs)
        return m * jnp.cos(nn * lis), m * jnp.sin(nn * lis)

    def blockdiag(t):
        t = t.reshape(nblk, S5_GB, S5_GROUP, S5_STATE)
        return jnp.einsum("ngcp,gh->ngchp", t, eye).reshape(nblk, LANES, S5_HALF)

    bc = jnp.stack([blockdiag(bb_re.transpose(0, 2, 1)), blockdiag(bb_im.transpose(0, 2, 1)),
                    blockdiag(cr), blockdiag(ci)], axis=1)

    n_list = jnp.concatenate([S5_L - 1 - taus, 1 + taus, S5_L * jnp.array([1, 2, 4, 0, 0, 0, 0, 0]),
                              S5_L * (1 + jnp.arange(8))])
    wr, wi = lam_pow(n_list)
    pw = jnp.concatenate([wr, wi], axis=2)
    return bc, pw


def kernel(x, meta, rel_bias, norm_w, w_in, conv_w, conv_b, lru_w_a, lru_b_a, lru_w_x, lru_b_x, lru_lambda, da_lambda, da_subln, s5_lam_re, s5_lam_im, s5_b_re, s5_b_im, s5_c_re, s5_c_im, s5_d, s5_log_step, s5_w_glu, s5_b_glu, b_gate, w_branch, w_out, w_ffn_in, w_ffn_out):
    bsz, seq, d_model = x.shape
    depth = w_in.shape[0]
    lru_w = conv_w.shape[2]
    s5_w = s5_d.shape[1]
    qk_w = DA_HEADS * 2 * DA_HEAD_DIM
    assert lru_w == LRU_BLOCKS * LANES and s5_w % LANES == 0 and d_model % LANES == 0
    col_gate, col_x = 0, lru_w // LANES
    col_q = 2 * lru_w // LANES
    col_k = col_q + qk_w // LANES
    col_v = col_k + qk_w // LANES
    col_u = col_v + DA_HEADS
    col_g = (col_u + s5_w // LANES) * LANES

    t_real = N_META + seq
    tb = 384 if t_real >= 1024 else 128
    t_pad = -(-t_real // tb) * tb
    assert t_pad % SCAN_CH == 0
    n_tok = bsz * t_pad
    tm_in = _pick_tile(n_tok, (1408, 1056, 768, 512, 384, 256, 128))
    tm_merge = _pick_tile(n_tok, (1056, 768, 512, 384, 256, 128))
    tm_ffn = _pick_tile(n_tok, (768, 512, 384, 256, 128))
    tm_out = _pick_tile(n_tok, (512, 384, 256, 128))
    tm_last = _pick_tile(seq, (512, 256, 128))

    xs = jnp.concatenate([jnp.broadcast_to(meta.astype(F32)[None], (bsz, N_META, d_model)), x,
                          jnp.zeros((bsz, t_pad - t_real, d_model), F32)], axis=1)
    xs = xs.reshape(n_tok, d_model)
    btiles = _bias_tiles(rel_bias.astype(F32), tb)

    w_in_b, w_glu_b, w_branch_b = w_in.astype(BF16), s5_w_glu.astype(BF16), w_branch.astype(BF16)
    w_out_b, w_ffn_in_b, w_ffn_out_b = w_out.astype(BF16), w_ffn_in.astype(BF16), w_ffn_out.astype(BF16)

    lru_p_all = jnp.concatenate([conv_w, conv_b[:, None], lru_b_a[:, None], lru_b_x[:, None],
                                 jax.nn.log_sigmoid(lru_lambda.astype(F32))[:, None]], axis=1)
    w_ax_all = jnp.concatenate([lru_w_a, lru_w_x], axis=3).astype(BF16)
    bc_all, pw_all = jax.vmap(_s5_tables)(s5_lam_re, s5_lam_im, s5_b_re, s5_b_im, s5_c_re, s5_c_im,
                                          s5_log_step)

    for l in range(depth):
        lam_init = 0.8 - 0.6 * math.exp(-0.3 * l)
        proj = _inproj(xs, norm_w[l, 0][None], w_in_b, l, tm_in, 1024)
        proj3 = proj.reshape(bsz, t_pad, proj.shape[1])

        y_a = _lru(proj3, lru_p_all[l], w_ax_all[l], col_gate, col_x)

        y_b = _attn(proj3, btiles, da_lambda[l], da_subln[l][None], lam_init, tb, col_q, col_k, col_v)

        y_c = _s5(proj3, bc_all, pw_all, l, s5_d[l][None], col_u)

        merged = _merge(y_a.reshape(n_tok, lru_w), y_b.reshape(n_tok, -1), y_c.reshape(n_tok, s5_w),
                        proj, w_glu_b, s5_b_glu[l][None], w_branch_b, b_gate[l], col_g, l, tm_merge, 512)
        xs = _outproj(merged, w_out_b, l, xs, norm_w[l, 1][None], tm_out)
        if l + 1 < depth:
            xs = _ffn(xs, norm_w[l, 2][None], w_ffn_in_b, w_ffn_out_b, norm_w[l, 3][None], l, tm_ffn, 512)
        else:
            out = _ffn(xs, norm_w[l, 2][None], w_ffn_in_b, w_ffn_out_b, norm_w[l, 3][None], l, tm_last,
                       512, real_rows=(t_pad, N_META, seq))

    return out.reshape(bsz, seq, d_model)
```

```python
import functools
import math

import jax
import jax.numpy as jnp
from jax import lax
from jax.experimental import pallas as pl
from jax.experimental.pallas import tpu as pltpu

F32 = jnp.float32
BF16 = jnp.bfloat16

N_META = 16
CONV_W = 4
LRU_C = 8.0
LRU_BLOCKS = 8
DA_HEADS = 8
DA_HEAD_DIM = 64
S5_GROUP = 16
S5_STATE = 64
REL_BUCKETS = 32
REL_MAX_DIST = 128
N_BRANCH = 3

LANES = 128
S5_GB = LANES // S5_GROUP
S5_HALF = S5_GB * S5_STATE
S5_L = 8
SCAN_CH = 128
MASK_NEG = -1e30
LOG2E = math.log2(math.e)
VMEM_LIMIT = 56 * 1024 * 1024


def _cparams(sem):
    return pltpu.CompilerParams(dimension_semantics=sem, vmem_limit_bytes=VMEM_LIMIT)


def _pick_tile(n, candidates):
    for c in candidates:
        if n % c == 0:
            return c
    raise ValueError(f"no tile in {candidates} divides {n}")


def _gelu_tanh(x):
    k = math.sqrt(2.0 / math.pi)
    half = 0.5 * x
    return half + half * jnp.tanh(x * (k + (k * 0.044715) * (x * x)))


def _sigmoid(x):
    return 1.0 / (1.0 + jnp.exp2(x * (-LOG2E)))


def _rms(x, w, eps):
    return (x * lax.rsqrt(jnp.mean(x * x, axis=-1, keepdims=True) + eps)) * w


def _inproj_kernel(x_ref, nw_ref, w_ref, o_ref, h_scr):
    @pl.when(pl.program_id(1) == 0)
    def _():
        h_scr[...] = _rms(x_ref[...], nw_ref[...], 1e-6).astype(BF16)

    o_ref[...] = jnp.dot(h_scr[...], w_ref[...], preferred_element_type=F32).astype(o_ref.dtype)


def _inproj(xs, nw, w_all, layer, tm, tn):
    n, d = xs.shape
    n_out = w_all.shape[2]
    return pl.pallas_call(
        _inproj_kernel,
        grid=(n // tm, n_out // tn),
        in_specs=[pl.BlockSpec((tm, d), lambda i, j: (i, 0)),
                  pl.BlockSpec((1, d), lambda i, j: (0, 0)),
                  pl.BlockSpec((None, d, tn), lambda i, j: (layer, 0, j))],
        out_specs=pl.BlockSpec((tm, tn), lambda i, j: (i, j)),
        out_shape=jax.ShapeDtypeStruct((n, n_out), BF16),
        scratch_shapes=[pltpu.VMEM((tm, d), BF16)],
        compiler_params=_cparams(("parallel", "arbitrary")),
        name="inproj",
    )(xs, nw, w_all)


def _lru_kernel(g_ref, x_ref, p_ref, w_ref, o_ref, xs_scr):
    t_len = x_ref.shape[1]
    halo = 8
    xs_scr[0:halo, :] = jnp.zeros((halo, LANES), F32)
    xs_scr[halo:, :] = x_ref[0].astype(F32)
    p = p_ref[...]
    cw = [p[j:j + 1, :] for j in range(CONV_W)]
    cb, ba, bx = p[4:5, :], p[5:6, :], p[6:7, :]
    c_log2sig = (LRU_C * LOG2E) * p[7:8, :]
    w = w_ref[0]
    row = lax.broadcasted_iota(jnp.int32, (SCAN_CH, LANES), 0)

    def local_scan(base):
        xc = cb
        for j in range(CONV_W):
            xc = xc + cw[j] * xs_scr[pl.ds(base + halo - (CONV_W - 1) + j, SCAN_CH), :]
        ri = jnp.dot(xc.astype(BF16), w, preferred_element_type=F32)
        r = _sigmoid(ri[:, :LANES] + ba)
        i = _sigmoid(ri[:, LANES:] + bx)
        a = jnp.exp2(r * c_log2sig)
        b = jnp.sqrt(1.0 - a * a) * (i * xc)
        s = 1
        while s < SCAN_CH:
            keep = row >= s
            a_sh = jnp.where(keep, pltpu.roll(a, s, 0), 1.0)
            b_sh = jnp.where(keep, pltpu.roll(b, s, 0), 0.0)
            b = a * b_sh + b
            a = a * a_sh
            s *= 2
        return a, b

    n_chunks = t_len // SCAN_CH
    sub = next(c for c in (11, 3, 1) if n_chunks % c == 0)

    def chunk(c, h0):
        bases = [pl.multiple_of((c * sub + j) * SCAN_CH, SCAN_CH) for j in range(sub)]
        scans = [local_scan(base) for base in bases]
        for base, (a, b) in zip(bases, scans):
            h = a * h0 + b
            g = g_ref[0, pl.ds(base, SCAN_CH), :].astype(F32)
            o_ref[0, pl.ds(base, SCAN_CH), :] = (h * _gelu_tanh(g)).astype(o_ref.dtype)
            h0 = h[SCAN_CH - 1:SCAN_CH, :]
        return h0

    lax.fori_loop(0, n_chunks // sub, chunk, jnp.zeros((1, LANES), F32))


def _lru(proj3, lru_p, w_ax, col_gate, col_x):
    b, t, _ = proj3.shape
    width = LRU_BLOCKS * LANES
    return pl.pallas_call(
        _lru_kernel,
        grid=(b, LRU_BLOCKS),
        in_specs=[pl.BlockSpec((1, t, LANES), lambda bi, h: (bi, 0, col_gate + h)),
                  pl.BlockSpec((1, t, LANES), lambda bi, h: (bi, 0, col_x + h)),
                  pl.BlockSpec((8, LANES), lambda bi, h: (0, h)),
                  pl.BlockSpec((1, LANES, 2 * LANES), lambda bi, h: (h, 0, 0))],
        out_specs=pl.BlockSpec((1, t, LANES), lambda bi, h: (bi, 0, h)),
        out_shape=jax.ShapeDtypeStruct((b, t, width), BF16),
        scratch_shapes=[pltpu.VMEM((t + 8, LANES), F32)],
        compiler_params=_cparams(("parallel", "parallel")),
        name="rglru",
    )(proj3, proj3, lru_p, w_ax)


def _bias_tile_kernel(rb_ref, o_ref):
    h = pl.program_id(0)
    tb = o_ref.shape[1]
    nb = tb // LANES
    i = lax.broadcasted_iota(jnp.int32, (LANES, LANES), 0)
    j = lax.broadcasted_iota(jnp.int32, (LANES, LANES), 1)
    max_exact = REL_BUCKETS // 2

    def pattern(delta):
        d = i - j + delta * LANES
        n = jnp.maximum(d, 0)
        nf = jnp.maximum(n, 1).astype(F32)
        large = max_exact + (jnp.log(nf / max_exact) / math.log(REL_MAX_DIST / max_exact)
                             * (REL_BUCKETS - max_exact)).astype(jnp.int32)
        large = jnp.minimum(large, REL_BUCKETS - 1)
        bucket = jnp.where(n < max_exact, n, large)
        val = jnp.zeros((LANES, LANES), F32)
        for bkt in range(REL_BUCKETS):
            val = jnp.where(bucket == bkt, rb_ref[bkt, h], val)
        val = val - rb_ref[REL_BUCKETS - 1, h]
        return jnp.where(d >= 0, val, MASK_NEG)

    pieces = {0: pattern(0), 1: pattern(1)}
    for ri in range(nb):
        for cj in range(2 * nb):
            delta = ri - cj + nb
            if delta in pieces:
                piece = pieces[delta]
            else:
                piece = jnp.full((LANES, LANES), 0.0 if delta >= 2 else MASK_NEG, F32)
            o_ref[0, ri * LANES:(ri + 1) * LANES, cj * LANES:(cj + 1) * LANES] = piece


def _bias_tiles(rel_bias, tb):
    return pl.pallas_call(
        _bias_tile_kernel,
        grid=(DA_HEADS,),
        in_specs=[pl.BlockSpec(memory_space=pltpu.SMEM)],
        out_specs=pl.BlockSpec((1, tb, 2 * tb), lambda h: (h, 0, 0)),
        out_shape=jax.ShapeDtypeStruct((DA_HEADS, tb, 2 * tb), F32),
        compiler_params=_cparams(("parallel",)),
        name="t5_bias_tiles",
    )(rel_bias)


def _attn_qblock(qi, q_ref, k_ref, v1_scr, bt_ref, sw_ref, o_ref, m_scr, acc_scr, s_scr, accp_scr, lam,
                 lam_init):
    tb = s_scr.shape[2]
    dn_t = (((1,), (1,)), ((), ()))
    lane = lax.broadcasted_iota(jnp.int32, (tb, LANES), 1)
    qstart = pl.multiple_of(qi * tb, tb)
    qs = q_ref[0, pl.ds(qstart, tb), :] * (DA_HEAD_DIM ** -0.5)
    zero = jnp.zeros_like(qs)
    qcat = jnp.concatenate([jnp.where(lane < DA_HEAD_DIM, qs, zero),
                            jnp.where(lane >= DA_HEAD_DIM, qs, zero)], axis=0)
    n_far = jnp.maximum(qi - 1, 0)
    n_quads = n_far // 4
    n_rest = n_far % 4
    tail_blk = 4 * n_quads

    def scores(blk, nblocks, biased):
        kb = k_ref[0, pl.ds(pl.multiple_of(blk * tb, tb), nblocks * tb), :]
        s_both = lax.dot_general(qcat, kb, dn_t, preferred_element_type=F32)
        out = []
        for c in range(2):
            s = s_both[c * tb:(c + 1) * tb]
            if biased:
                nb = min(nblocks, 2)
                plain = (nblocks - nb) * tb
                tail = s[:, plain:] + bt_ref[0, :, (2 - nb) * tb:]
                s = tail if plain == 0 else jnp.concatenate([s[:, :plain], tail], axis=1)
            s = s * LOG2E
            for i in range(nblocks):
                s_scr[c, blk + i] = s[:, i * tb:(i + 1) * tb]
            out.append(s)
        return out

    def lane_chunks(s):
        return [s[:, j * LANES:(j + 1) * LANES] for j in range(s.shape[1] // LANES)]

    def max_into(ms, ss):
        out = []
        for m, s in zip(ms, ss):
            for ch in lane_chunks(s):
                m = jnp.maximum(m, ch)
            out.append(m)
        return out

    m_init = jnp.full((tb, LANES), MASK_NEG, F32)

    @pl.when(qi == 0)
    def _():
        m_scr[0], m_scr[1] = max_into((m_init, m_init), scores(0, 1, True))

    for rest in range(4):
        @pl.when(jnp.logical_and(qi >= 1, n_rest == rest))
        def _():
            m_scr[0], m_scr[1] = max_into((m_init, m_init), scores(tail_blk, rest + 2, True))
            _attn_finalize(qstart - tb, accp_scr, sw_ref, o_ref, lam, lam_init)

    def quad_max(ki, ms):
        return tuple(max_into(ms, scores(4 * ki, 4, False)))

    ms = lax.fori_loop(0, n_quads, quad_max, (m_scr[0], m_scr[1]))
    def row_max():
        mb = [jnp.broadcast_to(jnp.max(ms[c], axis=-1, keepdims=True), (tb, LANES)) for c in range(2)]
        m_scr[0], m_scr[1] = mb
        return mb

    def accumulate(blk, nblocks, st, mb):
        vb = v1_scr[pl.ds(pl.multiple_of(blk * tb, tb), nblocks * tb), :]
        out = []
        for c in range(2):
            ps = []
            for i in range(nblocks):
                ps += [jnp.exp2(ch - mb[c]) for ch in lane_chunks(s_scr[c, blk + i])]
            p = jnp.concatenate(ps, axis=1).astype(BF16)
            out.append(st[c] + jnp.dot(p, vb, preferred_element_type=F32))
        return tuple(out)

    def store_state(st):
        acc_scr[0], acc_scr[1] = st

    za = jnp.zeros((tb, 2 * LANES), F32)

    @pl.when(qi == 0)
    def _():
        store_state(accumulate(0, 1, (za, za), row_max()))

    for rest in range(4):
        @pl.when(jnp.logical_and(qi >= 1, n_rest == rest))
        def _():
            store_state(accumulate(tail_blk, rest + 2, (za, za), row_max()))

    mb_q = (m_scr[0], m_scr[1])
    a0, a1 = lax.fori_loop(0, n_quads, lambda ki, st: accumulate(4 * ki, 4, st, mb_q),
                           (acc_scr[0], acc_scr[1]))

    accp_scr[0], accp_scr[1] = a0, a1


def _attn_finalize(row_start, accp_scr, sw_ref, o_ref, lam, lam_init):
    tb = accp_scr.shape[1]
    a0, a1 = accp_scr[0], accp_scr[1]
    o = a0[:, :LANES] / a0[:, LANES:] - lam * (a1[:, :LANES] / a1[:, LANES:])
    o = _rms(o, sw_ref[...], 1e-5) * (1.0 - lam_init)
    if not isinstance(row_start, int):
        row_start = pl.multiple_of(row_start, tb)
    o_ref[0, pl.ds(row_start, tb), :] = o.astype(o_ref.dtype)


def _attn_kernel(q_ref, k_ref, v_ref, bt_ref, dl_ref, sw_ref, o_ref, m_scr, acc_scr, s_scr, v1_scr, accp_scr,
                 *, lam_init):
    dl = dl_ref[...]
    lam = (jnp.exp(jnp.sum(dl[0:1, :] * dl[1:2, :], axis=-1, keepdims=True))
           - jnp.exp(jnp.sum(dl[2:3, :] * dl[3:4, :], axis=-1, keepdims=True)) + lam_init)
    v1_scr[:, :LANES] = v_ref[0]
    v1_scr[:, LANES:] = jnp.ones((v_ref.shape[1], LANES), BF16)

    def qblock(qi, carry):
        _attn_qblock(qi, q_ref, k_ref, v1_scr, bt_ref, sw_ref, o_ref, m_scr, acc_scr, s_scr, accp_scr, lam,
                     lam_init)
        return carry

    tb = s_scr.shape[2]
    nq = q_ref.shape[1] // tb
    lax.fori_loop(0, nq, qblock, 0)
    _attn_finalize((nq - 1) * tb, accp_scr, sw_ref, o_ref, lam, lam_init)


def _attn(proj3, btiles, da_lambda, da_subln, lam_init, tb, col_q, col_k, col_v):
    b, t, _ = proj3.shape
    width = DA_HEADS * LANES
    return pl.pallas_call(
        functools.partial(_attn_kernel, lam_init=lam_init),
        grid=(b, DA_HEADS),
        in_specs=[pl.BlockSpec((1, t, LANES), lambda bi, h: (bi, 0, col_q + h)),
                  pl.BlockSpec((1, t, LANES), lambda bi, h: (bi, 0, col_k + h)),
                  pl.BlockSpec((1, t, LANES), lambda bi, h: (bi, 0, col_v + h)),
                  pl.BlockSpec((1, tb, 2 * tb), lambda bi, h: (h, 0, 0)),
                  pl.BlockSpec((4, DA_HEAD_DIM), lambda bi, h: (0, 0)),
                  pl.BlockSpec((1, LANES), lambda bi, h: (0, 0))],
        out_specs=pl.BlockSpec((1, t, LANES), lambda bi, h: (bi, 0, h)),
        out_shape=jax.ShapeDtypeStruct((b, t, width), BF16),
        scratch_shapes=[pltpu.VMEM((2, tb, LANES), F32), pltpu.VMEM((2, tb, 2 * LANES), F32),
                        pltpu.VMEM((2, t // tb, tb, tb), F32), pltpu.VMEM((t, 2 * LANES), BF16),
                        pltpu.VMEM((2, tb, 2 * LANES), F32)],
        compiler_params=_cparams(("parallel", "parallel")),
        name="diff_attn",
    )(proj3, proj3, proj3, btiles, da_lambda, da_subln)


def _s5_kernel(u_ref, bc_ref, pw_ref, d_ref, o_ref, uf_scr, g_scr, hp_scr, y_scr,
               wg_scr, wk_scr, wct_scr):
    t_len = u_ref.shape[1]
    nj = t_len // S5_L
    pw = pw_ref[0]

    @pl.when(pl.program_id(1) == 0)
    def _():
        bd_re, bd_im, ct_re, ct_im = bc_ref[0, 0], bc_ref[0, 1], bc_ref[0, 2], bc_ref[0, 3]
        for s in range(S5_L):
            rows = slice(s * LANES, (s + 1) * LANES)
            qr, qi = pw[s:s + 1, :S5_HALF], pw[s:s + 1, S5_HALF:]
            wg_scr[rows, :S5_HALF] = (qr * bd_re - qi * bd_im).astype(BF16)
            wg_scr[rows, S5_HALF:] = (qr * bd_im + qi * bd_re).astype(BF16)
            er, ei = pw[8 + s:9 + s, :S5_HALF], pw[8 + s:9 + s, S5_HALF:]
            wct_scr[rows, :S5_HALF] = (ct_re * er - ct_im * ei).astype(BF16)
            wct_scr[rows, S5_HALF:] = (-(ct_re * ei + ct_im * er)).astype(BF16)
        dn_t = (((1,), (1,)), ((), ()))
        bcat = jnp.concatenate([bd_re, bd_im], axis=1).astype(BF16)
        crows = [jnp.concatenate([ct_re, -ct_im], axis=1).astype(BF16)]
        crows += [wct_scr[s * LANES:(s + 1) * LANES, :] for s in range(S5_L - 1)]
        kts = [lax.dot_general(bcat, c, dn_t, preferred_element_type=F32).astype(BF16) for c in crows]
        for s in range(S5_L):
            for r in range(S5_L):
                wk_scr[s * LANES:(s + 1) * LANES, r * LANES:(r + 1) * LANES] = (
                    kts[r - s] if r >= s else jnp.zeros((LANES, LANES), BF16))

    uf_scr[...] = u_ref[0].astype(F32)
    ur = jnp.concatenate([uf_scr[pl.ds(s, nj, stride=S5_L), :].astype(BF16) for s in range(S5_L)],
                         axis=1)
    g_scr[...] = jnp.dot(ur, wg_scr[...], preferred_element_type=F32)

    pr, pi = pw[24:32, :S5_HALF], pw[24:32, S5_HALF:]
    row = lax.broadcasted_iota(jnp.int32, (8, S5_HALF), 0)

    def local_scan(base):
        x = g_scr[pl.ds(base, 8), :]
        xr, xi = x[:, :S5_HALF], x[:, S5_HALF:]
        for lvl, s in enumerate((1, 2, 4)):
            ar, ai = pw[16 + lvl:17 + lvl, :S5_HALF], pw[16 + lvl:17 + lvl, S5_HALF:]
            keep = row >= s
            sr = jnp.where(keep, pltpu.roll(xr, s, 0), 0.0)
            si = jnp.where(keep, pltpu.roll(xi, s, 0), 0.0)
            xr, xi = xr + (ar * sr - ai * si), xi + (ar * si + ai * sr)
        return xr, xi

    first = row == 0
    per_iter = 2

    def groups(gi, carry):
        cr, ci = carry
        bases = [pl.multiple_of((gi * per_iter + k) * 8, 8) for k in range(per_iter)]
        scans = [local_scan(base) for base in bases]
        for base, (xr, xi) in zip(bases, scans):
            xr, xi = xr + (pr * cr - pi * ci), xi + (pr * ci + pi * cr)
            hp_scr[pl.ds(base, 8), :] = jnp.concatenate(
                [jnp.where(first, cr, pltpu.roll(xr, 1, 0)), jnp.where(first, ci, pltpu.roll(xi, 1, 0))],
                axis=1)
            cr, ci = xr[7:8, :], xi[7:8, :]
        return cr, ci

    y_in = jnp.dot(ur, wk_scr[...], preferred_element_type=F32)
    z = jnp.zeros((1, S5_HALF), F32)
    lax.fori_loop(0, nj // (8 * per_iter), groups, (z, z), unroll=True)

    y = y_in + lax.dot_general(hp_scr[...].astype(BF16), wct_scr[...], (((1,), (1,)), ((), ())),
                               preferred_element_type=F32)
    for s in range(S5_L):
        y_scr[pl.ds(s, nj, stride=S5_L), :] = y[:, s * LANES:(s + 1) * LANES]
    o_ref[0] = _gelu_tanh(y_scr[...] + d_ref[...] * uf_scr[...]).astype(o_ref.dtype)


def _s5(proj3, bc_all, pw_all, layer, dvec, col_u):
    b, t, _ = proj3.shape
    nblk = bc_all.shape[1]
    nj = t // S5_L
    wide = S5_L * LANES
    assert nj % 16 == 0
    return pl.pallas_call(
        _s5_kernel,
        grid=(nblk, b),
        in_specs=[pl.BlockSpec((1, t, LANES), lambda g, bi: (bi, 0, col_u + g)),
                  pl.BlockSpec((None, 1, 4, LANES, S5_HALF), lambda g, bi: (layer, g, 0, 0, 0)),
                  pl.BlockSpec((None, 1, 32, 2 * S5_HALF), lambda g, bi: (layer, g, 0, 0)),
                  pl.BlockSpec((1, LANES), lambda g, bi: (0, g))],
        out_specs=pl.BlockSpec((1, t, LANES), lambda g, bi: (bi, 0, g)),
        out_shape=jax.ShapeDtypeStruct((b, t, nblk * LANES), BF16),
        scratch_shapes=[pltpu.VMEM((t, LANES), F32), pltpu.VMEM((nj, 2 * S5_HALF), F32),
                        pltpu.VMEM((nj, 2 * S5_HALF), F32), pltpu.VMEM((t, LANES), F32),
                        pltpu.VMEM((wide, 2 * S5_HALF), BF16), pltpu.VMEM((wide, wide), BF16),
                        pltpu.VMEM((wide, 2 * S5_HALF), BF16)],
        compiler_params=_cparams(("parallel", "arbitrary")),
        name="s5",
    )(proj3, bc_all, pw_all, dvec)


def _merge_kernel(ya_ref, yb_ref, yc_ref, g0_ref, g1_ref, g2_ref, wg_ref, bg_ref, wb_ref, bgate_ref,
                  o_ref, yc_scr):
    @pl.when(pl.program_id(1) == 0)
    def _():
        yc = yc_ref[...]
        z = jnp.dot(yc, wg_ref[...], preferred_element_type=F32) + bg_ref[...]
        yc_scr[...] = (yc.astype(F32) * _sigmoid(z)).astype(BF16)

    bgate = bgate_ref[...]
    ys = (ya_ref[...], yb_ref[...], yc_scr[...])
    gs = (g0_ref, g1_ref, g2_ref)
    merged = None
    for br in range(N_BRANCH):
        gate = _sigmoid(gs[br][...].astype(F32) + bgate[br:br + 1, :])
        term = gate * jnp.dot(ys[br], wb_ref[br], preferred_element_type=F32)
        merged = term if merged is None else merged + term
    o_ref[...] = merged.astype(o_ref.dtype)


def _merge(ya, yb, yc, proj, w_glu_all, b_glu, w_branch_all, b_gate, col_g, layer, tm, tn):
    n, wdt = ya.shape
    d = w_branch_all.shape[3]
    gcol = [(col_g + br * d) // tn for br in range(N_BRANCH)]
    yspec = pl.BlockSpec((tm, wdt), lambda i, j: (i, 0))

    def gspec(br):
        return pl.BlockSpec((tm, tn), lambda i, j: (i, gcol[br] + j))

    return pl.pallas_call(
        _merge_kernel,
        grid=(n // tm, d // tn),
        in_specs=[yspec, yspec, yspec, gspec(0), gspec(1), gspec(2),
                  pl.BlockSpec((None, wdt, wdt), lambda i, j: (layer, 0, 0)),
                  pl.BlockSpec((1, wdt), lambda i, j: (0, 0)),
                  pl.BlockSpec((None, N_BRANCH, wdt, tn), lambda i, j: (layer, 0, 0, j)),
                  pl.BlockSpec((N_BRANCH, tn), lambda i, j: (0, j))],
        out_specs=pl.BlockSpec((tm, tn), lambda i, j: (i, j)),
        out_shape=jax.ShapeDtypeStruct((n, d), BF16),
        scratch_shapes=[pltpu.VMEM((tm, wdt), BF16)],
        compiler_params=_cparams(("parallel", "arbitrary")),
        name="merge",
    )(ya, yb, yc, proj, proj, proj, w_glu_all, b_glu, w_branch_all, b_gate)


def _outproj_kernel(m_ref, w_ref, xs_ref, nw_ref, o_ref):
    mix = jnp.dot(m_ref[...], w_ref[...], preferred_element_type=F32)
    o_ref[...] = xs_ref[...] + _rms(mix, nw_ref[...], 1e-6)


def _outproj(merged, w_out_all, layer, xs, nw, tm):
    n, d = xs.shape
    return pl.pallas_call(
        _outproj_kernel,
        grid=(n // tm,),
        in_specs=[pl.BlockSpec((tm, d), lambda i: (i, 0)),
                  pl.BlockSpec((None, d, d), lambda i: (layer, 0, 0)),
                  pl.BlockSpec((tm, d), lambda i: (i, 0)),
                  pl.BlockSpec((1, d), lambda i: (0, 0))],
        out_specs=pl.BlockSpec((tm, d), lambda i: (i, 0)),
        out_shape=jax.ShapeDtypeStruct((n, d), F32),
        compiler_params=_cparams(("parallel",)),
        name="outproj_residual",
    )(merged, w_out_all, xs, nw)


def _ffn_kernel(xs_ref, nw_in_ref, wg_ref, wu_ref, wo_ref, nw_out_ref, o_ref, h_scr, acc_scr):
    f = pl.program_id(1)

    @pl.when(f == 0)
    def _():
        h_scr[...] = _rms(xs_ref[...], nw_in_ref[...], 1e-6).astype(BF16)
        acc_scr[...] = jnp.zeros(acc_scr.shape, F32)

    h = h_scr[...]
    gate = jnp.dot(h, wg_ref[...], preferred_element_type=F32)
    up = jnp.dot(h, wu_ref[...], preferred_element_type=F32)
    act = (gate * _sigmoid(gate) * up).astype(BF16)
    acc_scr[...] += jnp.dot(act, wo_ref[...], preferred_element_type=F32)

    @pl.when(f == pl.num_programs(1) - 1)
    def _():
        o_ref[...] = xs_ref[...] + _rms(acc_scr[...], nw_out_ref[...], 1e-6)


def _ffn(xs, nw_in, w_ffn_in_all, w_ffn_out_all, nw_out, layer, tm, tf, real_rows=None):
    n, d = xs.shape
    d_ff = w_ffn_out_all.shape[1]
    nf = d_ff // tf
    if real_rows is None:
        n_out = n
        xs_spec = pl.BlockSpec((tm, d), lambda i, f: (i, 0))
    else:
        t_pad, first, count = real_rows
        per_seq = count // tm
        n_out = (n // t_pad) * count
        xs_spec = pl.BlockSpec((pl.Element(tm), pl.Element(d)),
                               lambda i, f: (pl.multiple_of(
                                   (i // per_seq) * t_pad + first + (i % per_seq) * tm, 8), 0))
        assert t_pad % 8 == 0 and first % 8 == 0 and tm % 8 == 0
    return pl.pallas_call(
        _ffn_kernel,
        grid=(n_out // tm, nf),
        in_specs=[xs_spec,
                  pl.BlockSpec((1, d), lambda i, f: (0, 0)),
                  pl.BlockSpec((None, d, tf), lambda i, f: (layer, 0, f)),
                  pl.BlockSpec((None, d, tf), lambda i, f: (layer, 0, nf + f)),
                  pl.BlockSpec((None, tf, d), lambda i, f: (layer, f, 0)),
                  pl.BlockSpec((1, d), lambda i, f: (0, 0))],
        out_specs=pl.BlockSpec((tm, d), lambda i, f: (i, 0)),
        out_shape=jax.ShapeDtypeStruct((n_out, d), F32),
        scratch_shapes=[pltpu.VMEM((tm, d), BF16), pltpu.VMEM((tm, d), F32)],
        compiler_params=_cparams(("parallel", "arbitrary")),
        name="swiglu_ffn",
    )(xs, nw_in, w_ffn_in_all, w_ffn_in_all, w_ffn_out_all, nw_out)


def _s5_tables(lam_re, lam_im, b_re, b_im, c_re, c_im, log_step):
    groups = lam_re.shape[0]
    nblk = groups // S5_GB
    lr, li = lam_re.astype(F32), lam_im.astype(F32)
    step = jnp.exp(log_step.astype(F32))[:, None]
    mag = jnp.exp(lr * step)
    ab_re, ab_im = mag * jnp.cos(li * step), mag * jnp.sin(li * step)
    den = lr * lr + li * li
    coef_re = ((ab_re - 1.0) * lr + ab_im * li) / den
    coef_im = (ab_im * lr - (ab_re - 1.0) * li) / den
    br, bi = b_re.astype(F32), b_im.astype(F32)
    bb_re = coef_re[..., None] * br - coef_im[..., None] * bi
    bb_im = coef_re[..., None] * bi + coef_im[..., None] * br
    cr, ci = c_re.astype(F32), c_im.astype(F32)
    eye = jnp.eye(S5_GB, dtype=F32)
    taus = jnp.arange(S5_L)
    lrs = (lr * step).reshape(nblk, 1, S5_HALF)
    lis = (li * step).reshape(nblk, 1, S5_HALF)

    def lam_pow(n):
        nn = n.astype(F32)[None, :, None]
        m = jnp.exp(nn * lrs)
        return m * jnp.cos(nn * lis), m * jnp.sin(nn * lis)

    def blockdiag(t):
        t = t.reshape(nblk, S5_GB, S5_GROUP, S5_STATE)
        return jnp.einsum("ngcp,gh->ngchp", t, eye).reshape(nblk, LANES, S5_HALF)

    bc = jnp.stack([blockdiag(bb_re.transpose(0, 2, 1)), blockdiag(bb_im.transpose(0, 2, 1)),
                    blockdiag(cr), blockdiag(ci)], axis=1)

    n_list = jnp.concatenate([S5_L - 1 - taus, 1 + taus, S5_L * jnp.array([1, 2, 4, 0, 0, 0, 0, 0]),
                              S5_L * (1 + jnp.arange(8))])
    wr, wi = lam_pow(n_list)
    pw = jnp.concatenate([wr, wi], axis=2)
    return bc, pw


def kernel(x, meta, rel_bias, norm_w, w_in, conv_w, conv_b, lru_w_a, lru_b_a, lru_w_x, lru_b_x, lru_lambda, da_lambda, da_subln, s5_lam_re, s5_lam_im, s5_b_re, s5_b_im, s5_c_re, s5_c_im, s5_d, s5_log_step, s5_w_glu, s5_b_glu, b_gate, w_branch, w_out, w_ffn_in, w_ffn_out):
    bsz, seq, d_model = x.shape
    depth = w_in.shape[0]
    lru_w = conv_w.shape[2]
    s5_w = s5_d.shape[1]
    qk_w = DA_HEADS * 2 * DA_HEAD_DIM
    assert lru_w == LRU_BLOCKS * LANES and s5_w % LANES == 0 and d_model % LANES == 0
    col_gate, col_x = 0, lru_w // LANES
    col_q = 2 * lru_w // LANES
    col_k = col_q + qk_w // LANES
    col_v = col_k + qk_w // LANES
    col_u = col_v + DA_HEADS
    col_g = (col_u + s5_w // LANES) * LANES

    t_real = N_META + seq
    tb = 384 if t_real >= 1024 else 128
    t_pad = -(-t_real // tb) * tb
    assert t_pad % SCAN_CH == 0
    n_tok = bsz * t_pad
    tm_in = _pick_tile(n_tok, (1408, 1056, 768, 512, 384, 256, 128))
    tm_merge = _pick_tile(n_tok, (768, 512, 384, 256, 128))
    tm_ffn = _pick_tile(n_tok, (768, 512, 384, 256, 128))
    tm_out = _pick_tile(n_tok, (512, 384, 256, 128))
    tm_last = _pick_tile(seq, (512, 256, 128))

    xs = jnp.concatenate([jnp.broadcast_to(meta.astype(F32)[None], (bsz, N_META, d_model)), x,
                          jnp.zeros((bsz, t_pad - t_real, d_model), F32)], axis=1)
    xs = xs.reshape(n_tok, d_model)
    btiles = _bias_tiles(rel_bias.astype(F32), tb)

    w_in_b, w_glu_b, w_branch_b = w_in.astype(BF16), s5_w_glu.astype(BF16), w_branch.astype(BF16)
    w_out_b, w_ffn_in_b, w_ffn_out_b = w_out.astype(BF16), w_ffn_in.astype(BF16), w_ffn_out.astype(BF16)

    lru_p_all = jnp.concatenate([conv_w, conv_b[:, None], lru_b_a[:, None], lru_b_x[:, None],
                                 jax.nn.log_sigmoid(lru_lambda.astype(F32))[:, None]], axis=1)
    w_ax_all = jnp.concatenate([lru_w_a, lru_w_x], axis=3).astype(BF16)
    bc_all, pw_all = jax.vmap(_s5_tables)(s5_lam_re, s5_lam_im, s5_b_re, s5_b_im, s5_c_re, s5_c_im,
                                          s5_log_step)

    for l in range(depth):
        lam_init = 0.8 - 0.6 * math.exp(-0.3 * l)
        proj = _inproj(xs, norm_w[l, 0][None], w_in_b, l, tm_in, 1024)
        proj3 = proj.reshape(bsz, t_pad, proj.shape[1])

        y_a = _lru(proj3, lru_p_all[l], w_ax_all[l], col_gate, col_x)

        y_b = _attn(proj3, btiles, da_lambda[l], da_subln[l][None], lam_init, tb, col_q, col_k, col_v)

        y_c = _s5(proj3, bc_all, pw_all, l, s5_d[l][None], col_u)

        merged = _merge(y_a.reshape(n_tok, lru_w), y_b.reshape(n_tok, -1), y_c.reshape(n_tok, s5_w),
                        proj, w_glu_b, s5_b_glu[l][None], w_branch_b, b_gate[l], col_g, l, tm_merge, 1024)
        xs = _outproj(merged, w_out_b, l, xs, norm_w[l, 1][None], tm_out)
        if l + 1 < depth:
            xs = _ffn(xs, norm_w[l, 2][None], w_ffn_in_b, w_ffn_out_b, norm_w[l, 3][None], l, tm_ffn, 512)
        else:
            out = _ffn(xs, norm_w[l, 2][None], w_ffn_in_b, w_ffn_out_b, norm_w[l, 3][None], l, tm_last,
                       512, real_rows=(t_pad, N_META, seq))

    return out.reshape(bsz, seq, d_model)
```
